```python
import math
import jax, jax.numpy as jnp
from jax import lax
import numpy as np

D_MODEL = 1024
BATCH = 4
SEQ = 8192
DEPTH = 1
DEC_BATCH = 8
DEC_SEQ = 32
PAST_LEN = 1024

CHUNK = 64
Q_BLOCK = 128
D_MIX = D_MODEL
D_ATTN = D_MIX // 2
N_HEADS = 8
HEAD_DIM = D_ATTN // N_HEADS
D_SSM = D_MIX - D_ATTN
SSM_GROUP = 16
N_SSM_GROUPS = D_SSM // SSM_GROUP
SSM_STATE = 64
DT_MIN = 1e-3
DT_MAX = 1e-1
N_EXPERTS = 256
TOP_K = 8
N_EXP_GROUPS = 8
TOPK_GROUPS = 4
D_EXPERT = 256
D_SHARED = 256
ROUTED_SCALE = 2.5
EXPERT_BLOCK = 128
FORGET_BIAS_INIT = 2.0
EPS = 1e-6
D_IN = 3 * D_ATTN + N_HEADS + D_SSM
SPLITS = (D_ATTN, 2 * D_ATTN, 3 * D_ATTN, 3 * D_ATTN + N_HEADS)

kernel_name = 'fox_s5_moe_streaming_step'

F32 = jnp.float32


def rmsnorm(x, g):
    xf = x.astype(F32)
    y = xf * lax.rsqrt(jnp.mean(xf * xf, axis=-1, keepdims=True) + EPS) * g.astype(F32)
    return y.astype(x.dtype)


def swiglu(x, wg, wu, wd):
    return (jax.nn.silu(x @ wg) * (x @ wu)) @ wd


def fox_block(q, k, v, fq, fk, qpos, kpos):
    s = jnp.einsum('bqhd,bkhd->bhqk', q, k).astype(F32) * (HEAD_DIM ** -0.5)
    s = s + jnp.transpose(fq, (0, 2, 1))[..., :, None] - jnp.transpose(fk, (0, 2, 1))[..., None, :]
    mask = kpos[None, :] <= qpos[:, None]
    s = jnp.where(mask, s, -jnp.inf)
    p = jax.nn.softmax(s, axis=-1)
    return jnp.einsum('bhqk,bkhd->bqhd', p.astype(v.dtype), v)


def fox_prompt(q, k, v, logf):
    Bn, L = q.shape[0], q.shape[1]
    F = jnp.cumsum(logf.astype(F32), axis=1)
    kpos = jnp.arange(L)

    def one(i):
        s0 = i * Q_BLOCK
        qb = lax.dynamic_slice_in_dim(q, s0, Q_BLOCK, axis=1)
        fq = lax.dynamic_slice_in_dim(F, s0, Q_BLOCK, axis=1)
        return fox_block(qb, k, v, fq, F, s0 + jnp.arange(Q_BLOCK), kpos)

    out = lax.map(one, jnp.arange(L // Q_BLOCK))
    return jnp.transpose(out, (1, 0, 2, 3, 4)).reshape(Bn, L, N_HEADS, HEAD_DIM)


def fox_sample(q, k, v, logf, pk, pv, plf):
    P, L = pk.shape[1], q.shape[1]
    k_all = jnp.concatenate([pk, k], axis=1)
    v_all = jnp.concatenate([pv, v], axis=1)
    F = jnp.cumsum(jnp.concatenate([plf.astype(F32), logf], axis=1), axis=1)
    return fox_block(q, k_all, v_all, F[:, P:], F, P + jnp.arange(L), jnp.arange(P + L))


def ssm_discretize(lam_re, lam_im, log_dt, b_re, b_im):
    lam = lax.complex(lam_re.astype(F32), lam_im.astype(F32))
    dt = jnp.exp(log_dt.astype(F32))[:, None]
    abar = jnp.exp(lam * dt)
    bmat = lax.complex(b_re.astype(F32), b_im.astype(F32))
    bbar = ((abar - 1.0) / lam)[..., None] * bmat
    return abar, bbar


def _lin_combine(e1, e2):
    a1, b1 = e1
    a2, b2 = e2
    return a1 * a2, a2 * b1 + b2


def ssm_mix(u, h0, abar, bbar, c_re, c_im, d_skip, w_glu, b_glu):
    Bn, L = u.shape[0], u.shape[1]
    uf = u.astype(F32).reshape(Bn, L, N_SSM_GROUPS, SSM_GROUP)
    bu = jnp.einsum('blgc,gpc->blgp', uf.astype(jnp.complex64), bbar)
    if h0 is not None:
        bu = bu.at[:, 0].add(abar * h0)
    a = jnp.broadcast_to(abar[None, None], (1, L, N_SSM_GROUPS, SSM_STATE))
    _, h = lax.associative_scan(_lin_combine, (a, bu), axis=1)
    cmat = lax.complex(c_re.astype(F32), c_im.astype(F32))
    y = jnp.einsum('gcp,blgp->blgc', cmat, h).real + d_skip.astype(F32).reshape(N_SSM_GROUPS, SSM_GROUP) * uf
    y = y.reshape(Bn, L, D_SSM)
    out = y * jax.nn.sigmoid(jax.nn.gelu(y) @ w_glu.astype(F32) + b_glu.astype(F32))
    return out.astype(u.dtype), h[:, -1]


def route(h, w_router, r_bias):
    T = h.shape[0]
    s = jax.nn.sigmoid((h @ w_router).astype(F32))
    sc = s + r_bias.astype(F32)
    gsc = lax.top_k(sc.reshape(T, N_EXP_GROUPS, N_EXPERTS // N_EXP_GROUPS), 2)[0].sum(-1)
    _, gidx = lax.top_k(gsc, TOPK_GROUPS)
    gmask = jnp.any(gidx[..., None] == jnp.arange(N_EXP_GROUPS), axis=1)
    emask = jnp.repeat(gmask, N_EXPERTS // N_EXP_GROUPS, axis=1)
    _, idx = lax.top_k(jnp.where(emask, sc, -jnp.inf), TOP_K)
    w = jnp.take_along_axis(s, idx, axis=1)
    w = w / jnp.sum(w, axis=-1, keepdims=True) * ROUTED_SCALE
    return idx, w


def routed_experts(h, idx, wts, w_gate, w_up, w_down):
    T, D = h.shape
    M = T * TOP_K
    flat_e = idx.reshape(M)
    order = jnp.argsort(flat_e)
    e_sorted = flat_e[order]
    counts = jnp.bincount(flat_e, length=N_EXPERTS)
    padded = (counts + EXPERT_BLOCK - 1) // EXPERT_BLOCK * EXPERT_BLOCK
    pend = jnp.cumsum(padded)
    pstart = pend - padded
    start = jnp.cumsum(counts) - counts
    dest = pstart[e_sorted] + jnp.arange(M) - start[e_sorted]
    n_blocks = -(-M // EXPERT_BLOCK) + N_EXPERTS
    rows = n_blocks * EXPERT_BLOCK
    row_tok = jnp.full((rows,), T, jnp.int32).at[dest].set((order // TOP_K).astype(jnp.int32))
    row_w = jnp.zeros((rows,), F32).at[dest].set(wts.reshape(M).astype(F32)[order])
    blk_e = jnp.minimum(jnp.searchsorted(pend, jnp.arange(n_blocks) * EXPERT_BLOCK, side='right'), N_EXPERTS - 1)
    h_pad = jnp.concatenate([h, jnp.zeros((1, D), h.dtype)], axis=0)

    def one(args):
        tok, w, e = args
        xb = h_pad[tok]
        yb = swiglu(xb, w_gate[e], w_up[e], w_down[e])
        return yb * w[:, None].astype(yb.dtype)

    ys = lax.map(one, (row_tok.reshape(n_blocks, EXPERT_BLOCK), row_w.reshape(n_blocks, EXPERT_BLOCK), blk_e))
    y = jnp.zeros((T + 1, D), ys.dtype).at[row_tok].add(ys.reshape(rows, D))
    return y[:T]


def layer(x, c, past, lw):
    (w_ada, b_ada, g_mix, w_in, b_f, lam_re, lam_im, log_dt, b_re, b_im, c_re, c_im, d_skip,
     w_glu, b_glu, g_ao, g_so, w_out, g_ffn, w_router, r_bias, w_gate, w_up, w_down,
     ws_gate, ws_up, ws_down) = lw
    Bn, L, D = x.shape
    mod = (jax.nn.silu(c) @ w_ada + b_ada).reshape(Bn, 6, D)
    sh1, sc1, gt1, sh2, sc2, gt2 = [mod[:, i, None, :] for i in range(6)]

    h = rmsnorm(x, g_mix) * (1.0 + sc1) + sh1
    q, k, v, fl, u = jnp.split(h @ w_in, SPLITS, axis=-1)
    q = q.reshape(Bn, L, N_HEADS, HEAD_DIM)
    k = k.reshape(Bn, L, N_HEADS, HEAD_DIM)
    v = v.reshape(Bn, L, N_HEADS, HEAD_DIM)
    logf = jax.nn.log_sigmoid(fl.astype(F32) + b_f.astype(F32))
    abar, bbar = ssm_discretize(lam_re, lam_im, log_dt, b_re, b_im)
    if past is None:
        attn = fox_prompt(q, k, v, logf)
        h0 = None
    else:
        pk, pv, plf, hre, him = past
        attn = fox_sample(q, k, v, logf, pk, pv, plf)
        h0 = lax.complex(hre.astype(F32), him.astype(F32))
    ssm, h_last = ssm_mix(u, h0, abar, bbar, c_re, c_im, d_skip, w_glu, b_glu)
    merged = jnp.concatenate([rmsnorm(attn.reshape(Bn, L, D_ATTN), g_ao), rmsnorm(ssm, g_so)], axis=-1)
    x = x + gt1 * (merged @ w_out)

    h2 = (rmsnorm(x, g_ffn) * (1.0 + sc2) + sh2).reshape(Bn * L, D)
    idx, wts = route(h2, w_router, r_bias)
    ff = routed_experts(h2, idx, wts, w_gate, w_up, w_down) + swiglu(h2, ws_gate, ws_up, ws_down)
    x = x + gt2 * ff.reshape(Bn, L, D).astype(x.dtype)
    return x, (k, v, logf), (h_last.real, h_last.imag)


def setup_inputs(seed: int = 0) -> dict:
    key = jax.random.key(seed)
    ks = iter(jax.random.split(key, 64))

    def nrm(shape, scale=1.0):
        return scale * jax.random.normal(next(ks), shape, F32)

    G, P = N_SSM_GROUPS, SSM_STATE
    return {
        'x_prompt': nrm((BATCH, SEQ, D_MODEL)),
        'x_sample': nrm((DEC_BATCH, DEC_SEQ, D_MODEL)),
        'c_prompt': nrm((BATCH, D_MODEL)),
        'c_sample': nrm((DEC_BATCH, D_MODEL)),
        'cache_k': nrm((DEPTH, DEC_BATCH, PAST_LEN, N_HEADS, HEAD_DIM)),
        'cache_v': nrm((DEPTH, DEC_BATCH, PAST_LEN, N_HEADS, HEAD_DIM)),
        'cache_logf': jax.nn.log_sigmoid(FORGET_BIAS_INIT + nrm((DEPTH, DEC_BATCH, PAST_LEN, N_HEADS))),
        'state_ssm_re': nrm((DEPTH, DEC_BATCH, G, P)),
        'state_ssm_im': nrm((DEPTH, DEC_BATCH, G, P)),
        'w_ada': nrm((DEPTH, D_MODEL, 6 * D_MODEL), 0.25 * D_MODEL ** -0.5),
        'b_ada': nrm((DEPTH, 6 * D_MODEL), 0.01),
        'g_mix': 1.0 + nrm((DEPTH, D_MODEL), 0.01),
        'w_in': nrm((DEPTH, D_MODEL, D_IN), D_MODEL ** -0.5),
        'b_f': FORGET_BIAS_INIT + nrm((DEPTH, N_HEADS), 0.1),
        'lam_re': -0.5 + nrm((DEPTH, G, P), 0.01),
        'lam_im': jnp.pi * jnp.arange(P, dtype=F32) + nrm((DEPTH, G, P), 0.01),
        'log_dt': jax.random.uniform(next(ks), (DEPTH, G), F32, minval=math.log(DT_MIN), maxval=math.log(DT_MAX)),
        'ssm_b_re': nrm((DEPTH, G, P, SSM_GROUP), (2 * SSM_GROUP) ** -0.5),
        'ssm_b_im': nrm((DEPTH, G, P, SSM_GROUP), (2 * SSM_GROUP) ** -0.5),
        'ssm_c_re': nrm((DEPTH, G, SSM_GROUP, P), (2 * P) ** -0.5),
        'ssm_c_im': nrm((DEPTH, G, SSM_GROUP, P), (2 * P) ** -0.5),
        'ssm_d': nrm((DEPTH, D_SSM)),
        'w_glu': nrm((DEPTH, D_SSM, D_SSM), D_SSM ** -0.5),
        'b_glu': nrm((DEPTH, D_SSM), 0.01),
        'g_attn_out': 1.0 + nrm((DEPTH, D_ATTN), 0.01),
        'g_ssm_out': 1.0 + nrm((DEPTH, D_SSM), 0.01),
        'w_out': nrm((DEPTH, D_MIX, D_MODEL), D_MIX ** -0.5),
        'g_ffn': 1.0 + nrm((DEPTH, D_MODEL), 0.01),
        'w_router': nrm((DEPTH, D_MODEL, N_EXPERTS), D_MODEL ** -0.5),
        'router_bias': nrm((DEPTH, N_EXPERTS), 0.01),
        'w_gate': nrm((DEPTH, N_EXPERTS, D_MODEL, D_EXPERT), D_MODEL ** -0.5),
        'w_up': nrm((DEPTH, N_EXPERTS, D_MODEL, D_EXPERT), D_MODEL ** -0.5),
        'w_down': nrm((DEPTH, N_EXPERTS, D_EXPERT, D_MODEL), D_EXPERT ** -0.5),
        'ws_gate': nrm((DEPTH, D_MODEL, D_SHARED), D_MODEL ** -0.5),
        'ws_up': nrm((DEPTH, D_MODEL, D_SHARED), D_MODEL ** -0.5),
        'ws_down': nrm((DEPTH, D_SHARED, D_MODEL), D_SHARED ** -0.5),
        'g_final': 1.0 + nrm((D_MODEL,), 0.01),
    }


def reference(x_prompt, x_sample, c_prompt, c_sample, cache_k, cache_v, cache_logf, state_ssm_re, state_ssm_im,
              w_ada, b_ada, g_mix, w_in, b_f, lam_re, lam_im, log_dt, ssm_b_re, ssm_b_im, ssm_c_re, ssm_c_im,
              ssm_d, w_glu, b_glu, g_attn_out, g_ssm_out, w_out, g_ffn, w_router, router_bias,
              w_gate, w_up, w_down, ws_gate, ws_up, ws_down, g_final):
    assert x_sample.shape[1] <= CHUNK
    yp, ys = x_prompt, x_sample
    kp_l, vp_l, fp_l, rp_l, ip_l = [], [], [], [], []
    ks_l, vs_l, fs_l, rs_l, is_l = [], [], [], [], []
    for l in range(DEPTH):
        lw = (w_ada[l], b_ada[l], g_mix[l], w_in[l], b_f[l], lam_re[l], lam_im[l], log_dt[l],
              ssm_b_re[l], ssm_b_im[l], ssm_c_re[l], ssm_c_im[l], ssm_d[l], w_glu[l], b_glu[l],
              g_attn_out[l], g_ssm_out[l], w_out[l], g_ffn[l], w_router[l], router_bias[l],
              w_gate[l], w_up[l], w_down[l], ws_gate[l], ws_up[l], ws_down[l])
        yp, (kp, vp, fp), (rp, ip) = layer(yp, c_prompt, None, lw)
        past = (cache_k[l], cache_v[l], cache_logf[l], state_ssm_re[l], state_ssm_im[l])
        ys, (kss, vss, fss), (rss, iss) = layer(ys, c_sample, past, lw)
        kp_l.append(kp); vp_l.append(vp); fp_l.append(fp); rp_l.append(rp); ip_l.append(ip)
        ks_l.append(kss); vs_l.append(vss); fs_l.append(fss); rs_l.append(rss); is_l.append(iss)
    y_prompt = rmsnorm(yp, g_final)
    y_sample = rmsnorm(ys, g_final)
    return (y_prompt, y_sample,
            jnp.stack(kp_l), jnp.stack(vp_l), jnp.stack(fp_l), jnp.stack(rp_l), jnp.stack(ip_l),
            jnp.stack(ks_l), jnp.stack(vs_l), jnp.stack(fs_l), jnp.stack(rs_l), jnp.stack(is_l))
```

```python
import functools
import math

import jax
import jax.numpy as jnp
from jax import lax
from jax.experimental import pallas as pl
from jax.experimental.pallas import tpu as pltpu

F32 = jnp.float32
BF16 = jnp.bfloat16
U32 = jnp.uint32
I32 = jnp.int32

N_HEADS = 8
HEAD_DIM = 64
HEAD_PAD = 128
D_ATTN = N_HEADS * HEAD_DIM
SSM_GROUP = 16
N_SSM_GROUPS = 32
SSM_STATE = 64
D_SSM = SSM_GROUP * N_SSM_GROUPS
D_STATE = N_SSM_GROUPS * SSM_STATE
N_EXPERTS = 256
TOP_K = 8
N_EXP_GROUPS = 8
TOPK_GROUPS = 4
ROUTED_SCALE = 2.5
EPS = 1e-6
LANES = 128

SEQ_BLOCK = 512
SSM_CHUNK = 128
EXPERT_ROWS = 256
COMBINE_ROWS = 256
DISPATCH_ROWS = 512
VMEM_LIMIT = 56 * 1024 * 1024

_NT = (((1,), (1,)), ((), ()))


def _cparams(*sem):
    return pltpu.CompilerParams(dimension_semantics=sem, vmem_limit_bytes=VMEM_LIMIT)


def _split3(x):
    hi = x.astype(BF16)
    r = x - hi.astype(F32)
    mid = r.astype(BF16)
    lo = (r - mid.astype(F32)).astype(BF16)
    return hi, mid, lo


def _dot(a, b):
    return jnp.dot(a, b, preferred_element_type=F32)


def _dot_nt(a, b):
    return lax.dot_general(a, b, _NT, preferred_element_type=F32)


def _rms(x, g):
    return x * lax.rsqrt(jnp.mean(x * x, axis=-1, keepdims=True) + EPS) * g


def _sigmoid(x):
    return 1.0 / (1.0 + jnp.exp(-x))


def _pack_bf16_pair(a, b):
    ab = lax.bitcast_convert_type(a.astype(BF16).astype(F32), U32)
    bb = lax.bitcast_convert_type(b.astype(BF16).astype(F32), U32)
    return (ab >> 16) | (bb & jnp.uint32(0xFFFF0000))


def _unpack_bf16_pair(w):
    a = lax.bitcast_convert_type(w << 16, F32)
    b = lax.bitcast_convert_type(w & jnp.uint32(0xFFFF0000), F32)
    return a, b


def _prep_kernel(c_ref, w_ref, b_ref, o_ref):
    c = c_ref[...]
    a = c * _sigmoid(c)
    a_hi = a.astype(BF16)
    a_lo = (a - a_hi.astype(F32)).astype(BF16)
    w = w_ref[...]
    w_hi = w.astype(BF16)
    w_lo = (w - w_hi.astype(F32)).astype(BF16)
    o_ref[...] = _dot(a_hi, w_hi) + _dot(a_lo, w_hi) + _dot(a_hi, w_lo) + b_ref[...]


def _prep(c_all, w_ada, b_ada):
    n, d = c_all.shape
    nout = w_ada.shape[1]
    tn = 1024
    return pl.pallas_call(
        _prep_kernel,
        grid=(nout // tn,),
        in_specs=[pl.BlockSpec((n, d), lambda j: (0, 0)),
                  pl.BlockSpec((d, tn), lambda j: (0, j)),
                  pl.BlockSpec((1, tn), lambda j: (0, j))],
        out_specs=pl.BlockSpec((n, tn), lambda j: (0, j)),
        out_shape=jax.ShapeDtypeStruct((n, nout), F32),
        compiler_params=_cparams("arbitrary"),
        name="prep",
    )(c_all, w_ada, b_ada.reshape(1, nout))


def _disc_kernel(lr_ref, li_ref, ldt_ref, br_ref, bi_ref, ar_ref, ai_ref, bbr_ref, bbi_ref):
    lr = lr_ref[...]
    li = li_ref[...]
    dt = jnp.exp(ldt_ref[...])
    er = jnp.exp(lr * dt)
    ang = li * dt
    ar = er * jnp.cos(ang)
    ai = er * jnp.sin(ang)
    ar_ref[...] = ar
    ai_ref[...] = ai
    den = lr * lr + li * li
    nr = ((ar - 1.0) * lr + ai * li) / den
    ni = (ai * lr - (ar - 1.0) * li) / den
    nr3 = nr[:, None, :]
    ni3 = ni[:, None, :]
    br = br_ref[...]
    bi = bi_ref[...]
    bbr_ref[...] = nr3 * br - ni3 * bi
    bbi_ref[...] = nr3 * bi + ni3 * br


def _disc(lam_re, lam_im, log_dt, bt_re, bt_im):
    g, p = lam_re.shape
    c = bt_re.shape[1]
    return pl.pallas_call(
        _disc_kernel,
        out_shape=(jax.ShapeDtypeStruct((g, p), F32), jax.ShapeDtypeStruct((g, p), F32),
                   jax.ShapeDtypeStruct((g, c, p), F32), jax.ShapeDtypeStruct((g, c, p), F32)),
        name="disc",
    )(lam_re, lam_im, log_dt.reshape(g, 1), bt_re, bt_im)


def _pow_kernel(ar_ref, ai_ref, pr_ref, pi_ref, *, tc):
    n1 = lax.broadcasted_iota(I32, (tc, 1), 0) + 1
    pr = jnp.ones(pr_ref.shape, F32)
    pi = jnp.zeros(pi_ref.shape, F32)
    sr = ar_ref[...]
    si = ai_ref[...]
    for k in range(int(math.log2(tc)) + 1):
        bit = ((n1 >> k) & 1) == 1
        fr = jnp.where(bit, sr, 1.0)
        fi = jnp.where(bit, si, 0.0)
        pr, pi = pr * fr - pi * fi, pr * fi + pi * fr
        sr, si = sr * sr - si * si, 2.0 * sr * si
    pr_ref[...] = pr
    pi_ref[...] = pi


def _pow_table(a_re, a_im, tc):
    n = a_re.shape[1]
    return pl.pallas_call(
        functools.partial(_pow_kernel, tc=tc),
        out_shape=(jax.ShapeDtypeStruct((tc, n), F32), jax.ShapeDtypeStruct((tc, n), F32)),
        name=f"pow{tc}",
    )(a_re, a_im)


def _inproj_kernel(x_ref, sh_ref, sc_ref, g_ref, wm_ref, wf_ref, bf_ref, selq_ref, selk_ref,
                   cq_ref, ck_ref, cv_ref,
                   q_ref, k_ref, v_ref, kf_ref, vf_ref, lf_ref, fb_ref, u_ref, carry_ref, *, tl):
    @pl.when(pl.program_id(1) == 0)
    def _():
        carry_ref[...] = jnp.zeros_like(carry_ref)

    x = x_ref[0]
    h = _rms(x, g_ref[...]) * (1.0 + sc_ref[0]) + sh_ref[0]
    hb = h.astype(BF16)
    main = _dot(hb, wm_ref[...])
    hw = N_HEADS * HEAD_PAD
    qp = main[:, 0:hw] * (HEAD_DIM ** -0.5)
    kp = main[:, hw:2 * hw]
    vp = main[:, 2 * hw:3 * hw]
    o = 3 * hw
    kf_ref[0] = main[:, o:o + D_ATTN]
    vf_ref[0] = main[:, o + D_ATTN:o + 2 * D_ATTN]
    u_ref[0] = main[:, o + 2 * D_ATTN:o + 2 * D_ATTN + D_SSM]

    fl = _dot(hb, wf_ref[...]) + bf_ref[...]
    lf = jnp.minimum(fl, 0.0) - jnp.log1p(jnp.exp(-jnp.abs(fl)))
    lf_ref[0] = lf[:, 0:N_HEADS]

    row = lax.broadcasted_iota(I32, (tl, tl), 0)
    col = lax.broadcasted_iota(I32, (tl, tl), 1)
    tri = jnp.where(row >= col, 1.0, 0.0).astype(BF16)
    hi, mid, lo = _split3(lf)
    frel = _dot(tri, hi) + _dot(tri, mid) + _dot(tri, lo)
    fb_ref[0, 0] = carry_ref[...]
    carry_ref[...] = carry_ref[...] + frel[tl - 1:tl, :]

    fcat = jnp.concatenate(_split3(frel), axis=-1)
    q_aug = (qp + _dot(fcat, selq_ref[...]) + cq_ref[...]).astype(BF16)
    k_aug = (kp + _dot(fcat, selk_ref[...]) + ck_ref[...]).astype(BF16)
    v_aug = (vp + cv_ref[...]).astype(BF16)
    for hd in range(N_HEADS):
        sl = slice(hd * HEAD_PAD, (hd + 1) * HEAD_PAD)
        q_ref[0, hd] = q_aug[:, sl]
        k_ref[0, hd] = k_aug[:, sl]
        v_ref[0, hd] = v_aug[:, sl]


def _inproj(x, sh, sc, g, wts):
    b, l, d = x.shape
    tl = min(l, SEQ_BLOCK)
    nl = l // tl
    wm, wf, bf, selq, selk, cq, ck, cv = wts
    hm = jax.ShapeDtypeStruct((b, N_HEADS, l, HEAD_PAD), BF16)
    hm_spec = pl.BlockSpec((1, N_HEADS, tl, HEAD_PAD), lambda i, j: (i, 0, j, 0))
    tok = lambda w: pl.BlockSpec((1, tl, w), lambda i, j: (i, j, 0))
    full = lambda a: pl.BlockSpec(a.shape, lambda i, j: (0,) * a.ndim)
    row = pl.BlockSpec((1, 1, d), lambda i, j: (i, 0, 0))
    return pl.pallas_call(
        functools.partial(_inproj_kernel, tl=tl),
        grid=(b, nl),
        in_specs=[tok(d), row, row, full(g), full(wm), full(wf), full(bf), full(selq), full(selk),
                  full(cq), full(ck), full(cv)],
        out_specs=[hm_spec, hm_spec, hm_spec, tok(D_ATTN), tok(D_ATTN), tok(N_HEADS),
                   pl.BlockSpec((1, 1, 1, LANES), lambda i, j: (i, j, 0, 0)), tok(D_SSM)],
        out_shape=[hm, hm, hm,
                   jax.ShapeDtypeStruct((b, l, D_ATTN), F32), jax.ShapeDtypeStruct((b, l, D_ATTN), F32),
                   jax.ShapeDtypeStruct((b, l, N_HEADS), F32),
                   jax.ShapeDtypeStruct((b, nl, 1, LANES), F32),
                   jax.ShapeDtypeStruct((b, l, D_SSM), F32)],
        scratch_shapes=[pltpu.VMEM((1, LANES), F32)],
        compiler_params=_cparams("arbitrary", "arbitrary"),
        name=f"inproj{l}",
    )(x, sh, sc, g, wm, wf, bf, selq, selk, cq, ck, cv)


def _inproj_weights(w_in, b_f):
    d = w_in.shape[0]
    wq, wk, wv = (w_in[:, i * D_ATTN:(i + 1) * D_ATTN] for i in range(3))
    wfl = w_in[:, 3 * D_ATTN:3 * D_ATTN + N_HEADS]
    wu = w_in[:, 3 * D_ATTN + N_HEADS:]

    def pad_heads(w):
        w = w.reshape(d, N_HEADS, HEAD_DIM)
        w = jnp.pad(w, ((0, 0), (0, 0), (0, HEAD_PAD - HEAD_DIM)))
        return w.reshape(d, N_HEADS * HEAD_PAD)

    wm = jnp.concatenate([pad_heads(wq), pad_heads(wk), pad_heads(wv), wk, wv, wu], axis=1).astype(BF16)
    wf = jnp.pad(wfl, ((0, 0), (0, LANES - N_HEADS))).astype(BF16)
    bf = jnp.pad(b_f, (0, LANES - N_HEADS)).reshape(1, LANES).astype(F32)

    hw = N_HEADS * HEAD_PAD
    hd = jnp.arange(N_HEADS)
    selq = jnp.zeros((3 * LANES, hw), F32)
    selk = jnp.zeros((3 * LANES, hw), F32)
    cq = jnp.zeros((1, hw), F32)
    ck = jnp.zeros((1, hw), F32)
    cv = jnp.zeros((1, hw), F32)
    for part in range(3):
        selq = selq.at[part * LANES + hd, hd * HEAD_PAD + HEAD_DIM + part].set(1.0)
        selk = selk.at[part * LANES + hd, hd * HEAD_PAD + HEAD_DIM + 3 + part].set(-1.0)
        cq = cq.at[0, hd * HEAD_PAD + HEAD_DIM + 3 + part].set(1.0)
        ck = ck.at[0, hd * HEAD_PAD + HEAD_DIM + part].set(1.0)
    cv = cv.at[0, hd * HEAD_PAD + HEAD_DIM].set(1.0)
    return wm, wf, bf, selq.astype(BF16), selk.astype(BF16), cq, ck, cv


def _attn_kernel(fb_ref, q_ref, k_ref, v_ref, o_ref, *, r, nblk):
    b = pl.program_id(0)
    hd = pl.program_id(1)
    i = pl.program_id(2)
    q = q_ref[0, 0]
    fbase = (b * nblk) * N_HEADS + hd
    fbi = fb_ref[fbase + i * N_HEADS]

    def step(j, carry, masked):
        m, acc = carry
        start = pl.multiple_of(j * r, r)
        kj = k_ref[0, 0, pl.ds(start, r), :]
        vj = v_ref[0, 0, pl.ds(start, r), :]
        s = _dot_nt(q, kj)
        if masked:
            row = lax.broadcasted_iota(I32, (r, r), 0)
            col = lax.broadcasted_iota(I32, (r, r), 1)
            s = jnp.where(row >= col, s, -jnp.inf)
        dlt = fbi - fb_ref[fbase + j * N_HEADS]
        mnew = jnp.maximum(m, jnp.max(s, axis=-1, keepdims=True) + dlt)
        p = jnp.exp(s - (mnew - dlt))
        acc = jnp.exp(m - mnew) * acc + _dot(p.astype(BF16), vj)
        return mnew, acc

    init = (jnp.full((r, 1), -jnp.inf, F32), jnp.zeros((r, HEAD_PAD), F32))
    carry = lax.fori_loop(0, i, lambda j, c: step(j, c, False), init)
    _, acc = step(i, carry, True)
    lane = lax.broadcasted_iota(I32, (r, HEAD_PAD), 1)
    out = acc / acc[:, HEAD_DIM:HEAD_DIM + 1]
    o_ref[0, 0] = jnp.where(lane < HEAD_DIM, out, 0.0).astype(BF16)


def _attn(q, k, v, fb):
    b, h, l, _ = q.shape
    r = min(l, SEQ_BLOCK)
    nblk = l // r
    return pl.pallas_call(
        functools.partial(_attn_kernel, r=r, nblk=nblk),
        grid=(b, h, nblk),
        in_specs=[pl.BlockSpec(memory_space=pltpu.SMEM),
                  pl.BlockSpec((1, 1, r, HEAD_PAD), lambda bi, hi, i: (bi, hi, i, 0)),
                  pl.BlockSpec((1, 1, l, HEAD_PAD), lambda bi, hi, i: (bi, hi, 0, 0)),
                  pl.BlockSpec((1, 1, l, HEAD_PAD), lambda bi, hi, i: (bi, hi, 0, 0))],
        out_specs=pl.BlockSpec((1, 1, r, HEAD_PAD), lambda bi, hi, i: (bi, hi, i, 0)),
        out_shape=jax.ShapeDtypeStruct((b, h, l, HEAD_PAD), BF16),
        compiler_params=_cparams("arbitrary", "arbitrary", "arbitrary"),
        name="attn",
    )(fb, q, k, v)


def _attn_dec_kernel(q_ref, k_ref, v_ref, lfc_ref, lfr_ref, o_ref, *, lq, past, lk):
    row = lax.broadcasted_iota(I32, (lk, lk), 0)
    col = lax.broadcasted_iota(I32, (lk, lk), 1)
    tri = jnp.where(row >= col, 1.0, 0.0).astype(BF16)
    upper = jnp.where(row <= col, 1.0, 0.0).astype(BF16)
    c_hi, c_mid, c_lo = _split3(lfc_ref[0])
    fcol = _dot(tri, c_hi) + _dot(tri, c_mid) + _dot(tri, c_lo)
    r_hi, r_mid, r_lo = _split3(lfr_ref[0])
    frow = _dot(r_hi, upper) + _dot(r_mid, upper) + _dot(r_lo, upper)
    fq = fcol[past:past + lq, :]
    qpos = past + lax.broadcasted_iota(I32, (lq, lk), 0)
    kpos = lax.broadcasted_iota(I32, (lq, lk), 1)
    ok = kpos <= qpos
    qlane = lax.broadcasted_iota(I32, (lq, HEAD_PAD), 1)
    klane = lax.broadcasted_iota(I32, (lk, HEAD_PAD), 1)
    for hd in range(N_HEADS):
        odd = hd % 2 == 1
        pair = slice((hd // 2) * HEAD_PAD, (hd // 2 + 1) * HEAD_PAD)
        mine = (klane >= HEAD_DIM) if odd else (klane < HEAD_DIM)
        kh = jnp.where(mine, k_ref[0, :, pair], 0.0).astype(BF16)
        vh = jnp.where(mine, v_ref[0, :, pair], 0.0).astype(BF16)
        qh = jnp.where(qlane < HEAD_DIM, q_ref[0, hd].astype(F32), 0.0)
        if odd:
            qh = pltpu.roll(qh, HEAD_DIM, axis=1)
        s = _dot_nt(qh.astype(BF16), kh) + fq[:, hd:hd + 1] - frow[hd:hd + 1, :]
        s = jnp.where(ok, s, -jnp.inf)
        m = jnp.max(s, axis=-1, keepdims=True)
        pb = jnp.exp(s - m).astype(BF16)
        den = jnp.sum(pb.astype(F32), axis=-1, keepdims=True)
        out = _dot(pb, vh) / den
        if odd:
            out = pltpu.roll(out, HEAD_DIM, axis=1)
        o_ref[0, hd] = out.astype(BF16)


def _attn_dec(q, k_all, v_all, lf_cols, lf_rows, past, lq):
    b = q.shape[0]
    lk = k_all.shape[1]
    return pl.pallas_call(
        functools.partial(_attn_dec_kernel, lq=lq, past=past, lk=lk),
        grid=(b,),
        in_specs=[pl.BlockSpec((1, N_HEADS, lq, HEAD_PAD), lambda i: (i, 0, 0, 0)),
                  pl.BlockSpec((1, lk, D_ATTN), lambda i: (i, 0, 0)),
                  pl.BlockSpec((1, lk, D_ATTN), lambda i: (i, 0, 0)),
                  pl.BlockSpec((1, lk, LANES), lambda i: (i, 0, 0)),
                  pl.BlockSpec((1, N_HEADS, lk), lambda i: (i, 0, 0))],
        out_specs=pl.BlockSpec((1, N_HEADS, lq, HEAD_PAD), lambda i: (i, 0, 0, 0)),
        out_shape=jax.ShapeDtypeStruct((b, N_HEADS, lq, HEAD_PAD), BF16),
        compiler_params=_cparams("arbitrary"),
        name="attn_dec",
    )(q, k_all, v_all, lf_cols, lf_rows)


def _ssm_kernel(u_ref, h0r_ref, h0i_ref, br_ref, bi_ref, cc_ref, pr_ref, pi_ref, d_ref, wg_ref, bg_ref,
                gso_ref, y_ref, hr_out, hi_out, cr_ref, ci_ref, *, tc):
    @pl.when(pl.program_id(1) == 0)
    def _():
        cr_ref[...] = h0r_ref[0]
        ci_ref[...] = h0i_ref[0]

    u = u_ref[0]
    ub = u.astype(BF16)
    hr = _dot(ub, br_ref[...])
    hi = _dot(ub, bi_ref[...])
    t = lax.broadcasted_iota(I32, (tc, 1), 0)
    for k in range(int(math.log2(tc))):
        n = 1 << k
        ar = pr_ref[n - 1:n, :]
        ai = pi_ref[n - 1:n, :]
        keep = t >= n
        sr = jnp.where(keep, pltpu.roll(hr, n, axis=0), 0.0)
        si = jnp.where(keep, pltpu.roll(hi, n, axis=0), 0.0)
        hr, hi = hr + ar * sr - ai * si, hi + ar * si + ai * sr
    pr = pr_ref[...]
    pi = pi_ref[...]
    cr = cr_ref[...]
    ci = ci_ref[...]
    hr, hi = hr + pr * cr - pi * ci, hi + pr * ci + pi * cr
    cr_ref[...] = hr[tc - 1:tc, :]
    ci_ref[...] = hi[tc - 1:tc, :]
    hr_out[0] = hr[tc - 1:tc, :]
    hi_out[0] = hi[tc - 1:tc, :]

    hcat = jnp.concatenate([hr, hi], axis=-1).astype(BF16)
    y = _dot(hcat, cc_ref[...]) + d_ref[...] * u
    gl = 0.5 * y * (1.0 + jnp.tanh(math.sqrt(2.0 / math.pi) * (y + 0.044715 * (y * y * y))))
    z = _dot(gl.astype(BF16), wg_ref[...]) + bg_ref[...]
    out = y * _sigmoid(z)
    y_ref[0] = _rms(out, gso_ref[...]).astype(BF16)


def _ssm(u, h0r, h0i, consts):
    b, l, _ = u.shape
    br, bi, cc, pr, pi, dsk, wg, bg, gso = consts
    tc = pr.shape[0]
    full = lambda a: pl.BlockSpec(a.shape, lambda i, j: (0,) * a.ndim)
    st = pl.BlockSpec((1, 1, D_STATE), lambda i, j: (i, 0, 0))
    return pl.pallas_call(
        functools.partial(_ssm_kernel, tc=tc),
        grid=(b, l // tc),
        in_specs=[pl.BlockSpec((1, tc, D_SSM), lambda i, j: (i, j, 0)), st, st,
                  full(br), full(bi), full(cc), full(pr), full(pi), full(dsk), full(wg), full(bg), full(gso)],
        out_specs=[pl.BlockSpec((1, tc, D_SSM), lambda i, j: (i, j, 0)), st, st],
        out_shape=[jax.ShapeDtypeStruct((b, l, D_SSM), BF16),
                   jax.ShapeDtypeStruct((b, 1, D_STATE), F32), jax.ShapeDtypeStruct((b, 1, D_STATE), F32)],
        scratch_shapes=[pltpu.VMEM((1, D_STATE), F32), pltpu.VMEM((1, D_STATE), F32)],
        compiler_params=_cparams("arbitrary", "arbitrary"),
        name=f"ssm{l}",
    )(u, h0r, h0i, br, bi, cc, pr, pi, dsk, wg, bg, gso)


def _post_kernel(x_ref, a_ref, s_ref, gt1_ref, sh2_ref, sc2_ref, gt2_ref, cnt0_ref, gao_ref, wo_ref, gffn_ref,
                 wrh_ref, wrl_ref, rb_ref, wsgu_ref, wsd_ref,
                 xm_ref, hp_ref, idx_ref, pos_ref, w_ref, cnt_ref, carry_ref, *, tl, first):
    step = pl.program_id(0) * pl.num_programs(1) + pl.program_id(1)

    @pl.when(step == 0)
    def _():
        carry_ref[...] = cnt0_ref[...] if not first else jnp.zeros_like(carry_ref)

    x = x_ref[0]
    attn = jnp.concatenate([a_ref[0, hd] for hd in range(N_HEADS)], axis=-1).astype(F32)
    ms = jnp.sum(attn * attn, axis=-1, keepdims=True) * (1.0 / D_ATTN)
    attn_n = (attn * lax.rsqrt(ms + EPS) * gao_ref[...]).astype(BF16)
    merged = jnp.concatenate([attn_n, s_ref[0]], axis=-1)
    x1 = x + gt1_ref[0] * _dot(merged, wo_ref[...])
    h2 = _rms(x1, gffn_ref[...]) * (1.0 + sc2_ref[0]) + sh2_ref[0]
    d = h2.shape[-1]
    hp_ref[0] = _pack_bf16_pair(h2[:, :d // 2], h2[:, d // 2:])
    h_hi = h2.astype(BF16)
    h_lo = (h2 - h_hi.astype(F32)).astype(BF16)

    gu = _dot(h_hi, wsgu_ref[...])
    ds = gu.shape[-1] // 2
    g = gu[:, :ds]
    act = (g * _sigmoid(g) * gu[:, ds:]).astype(BF16)
    xm_ref[0] = x1 + gt2_ref[0] * _dot(act, wsd_ref[...])

    wrh = wrh_ref[...]
    logits = _dot_nt(wrh, h_hi) + _dot_nt(wrh, h_lo) + _dot_nt(wrl_ref[...], h_hi)
    s = _sigmoid(logits)
    sc = s + rb_ref[...]
    ge = N_EXPERTS // N_EXP_GROUPS
    neg = -jnp.inf
    gi = lax.broadcasted_iota(I32, (ge, tl), 0)
    gsc = []
    for gidx in range(N_EXP_GROUPS):
        blk = sc[gidx * ge:(gidx + 1) * ge, :]
        m1 = jnp.max(blk, axis=0, keepdims=True)
        f1 = jnp.min(jnp.where(blk == m1, gi, ge), axis=0, keepdims=True)
        m2 = jnp.max(jnp.where(gi == f1, neg, blk), axis=0, keepdims=True)
        gsc.append(m1 + m2)
    gwork = jnp.concatenate(gsc, axis=0)
    ni = lax.broadcasted_iota(I32, (N_EXP_GROUPS, tl), 0)
    gsel = jnp.zeros((N_EXP_GROUPS, tl), F32)
    for _ in range(TOPK_GROUPS):
        mx = jnp.max(gwork, axis=0, keepdims=True)
        fi = jnp.min(jnp.where(gwork == mx, ni, N_EXP_GROUPS), axis=0, keepdims=True)
        hit = ni == fi
        gsel = jnp.where(hit, 1.0, gsel)
        gwork = jnp.where(hit, neg, gwork)
    work = jnp.concatenate(
        [jnp.where(gsel[gidx:gidx + 1, :] > 0.0, sc[gidx * ge:(gidx + 1) * ge, :], neg)
         for gidx in range(N_EXP_GROUPS)], axis=0)
    ei = lax.broadcasted_iota(I32, (N_EXPERTS, tl), 0)
    chosen = jnp.zeros((N_EXPERTS, tl), F32)
    idx_rows, w_rows = [], []
    for _ in range(TOP_K):
        mx = jnp.max(work, axis=0, keepdims=True)
        fi = jnp.min(jnp.where(work == mx, ei, N_EXPERTS), axis=0, keepdims=True)
        hit = ei == fi
        w_rows.append(jnp.sum(jnp.where(hit, s, 0.0), axis=0, keepdims=True))
        idx_rows.append(fi)
        chosen = jnp.where(hit, 1.0, chosen)
        work = jnp.where(hit, neg, work)
    wt = jnp.concatenate(w_rows, axis=0)
    wt = wt / jnp.sum(wt, axis=0, keepdims=True) * ROUTED_SCALE
    idx_ref[0] = jnp.concatenate(idx_rows, axis=0)

    trow = lax.broadcasted_iota(I32, (tl, tl), 0)
    tcol = lax.broadcasted_iota(I32, (tl, tl), 1)
    before = jnp.where(trow < tcol, 1.0, 0.0).astype(BF16)
    rank = _dot(chosen.astype(BF16), before) + carry_ref[...]
    pos_rows = [jnp.sum(jnp.where(ei == idx_rows[k], rank, 0.0), axis=0, keepdims=True) for k in range(TOP_K)]
    pos_ref[0] = jnp.concatenate(pos_rows, axis=0).astype(I32)
    carry_ref[...] = carry_ref[...] + jnp.sum(chosen, axis=1, keepdims=True)
    cnt_ref[...] = carry_ref[...]

    eye = jnp.where(trow == tcol, 1.0, 0.0).astype(BF16)
    t_hi, t_mid, t_lo = _split3(wt)
    w_ref[0] = _dot_nt(eye, t_hi) + _dot_nt(eye, t_mid) + _dot_nt(eye, t_lo)


def _post(x, attn, ssm, mods, cnt0, consts, first):
    b, l, d = x.shape
    tl = min(l, SEQ_BLOCK)
    gt1, sh2, sc2, gt2 = mods
    gao, wo, gffn, wrh, wrl, rb, wsgu, wsd = consts
    tok = lambda w: pl.BlockSpec((1, tl, w), lambda i, j: (i, j, 0))
    full = lambda a: pl.BlockSpec(a.shape, lambda i, j: (0,) * a.ndim)
    row = pl.BlockSpec((1, 1, d), lambda i, j: (i, 0, 0))
    tk = pl.BlockSpec((1, TOP_K, tl), lambda i, j: (i, 0, j))
    return pl.pallas_call(
        functools.partial(_post_kernel, tl=tl, first=first),
        grid=(b, l // tl),
        in_specs=[tok(d), pl.BlockSpec((1, N_HEADS, tl, HEAD_PAD), lambda i, j: (i, 0, j, 0)), tok(D_SSM),
                  row, row, row, row, full(cnt0), full(gao), full(wo), full(gffn), full(wrh), full(wrl),
                  full(rb), full(wsgu), full(wsd)],
        out_specs=[tok(d), tok(d // 2), tk, tk, tok(TOP_K), full(cnt0)],
        out_shape=[jax.ShapeDtypeStruct((b, l, d), F32), jax.ShapeDtypeStruct((b, l, d // 2), U32),
                   jax.ShapeDtypeStruct((b, TOP_K, l), I32), jax.ShapeDtypeStruct((b, TOP_K, l), I32),
                   jax.ShapeDtypeStruct((b, l, TOP_K), F32), jax.ShapeDtypeStruct(cnt0.shape, F32)],
        scratch_shapes=[pltpu.VMEM(cnt0.shape, F32)],
        compiler_params=_cparams("arbitrary", "arbitrary"),
        name=f"post{l}",
    )(x, attn, ssm, gt1, sh2, sc2, gt2, cnt0, gao, wo, gffn, wrh, wrl, rb, wsgu, wsd)


def _dispatch_kernel(dest_ref, h_ref, xin_ref, xs_ref, sem, *, td):
    del xin_ref
    base = pl.program_id(0) * td

    def issue(t, c):
        for k in range(TOP_K):
            pltpu.make_async_copy(h_ref.at[pl.ds(base + t, 1)], xs_ref.at[pl.ds(dest_ref[k, t], 1)], sem).start()
        return c

    lax.fori_loop(0, td, issue, 0)

    def drain(t, c):
        for k in range(TOP_K):
            pltpu.make_async_copy(h_ref.at[pl.ds(0, 1)], xs_ref.at[pl.ds(0, 1)], sem).wait()
        return c

    lax.fori_loop(0, td, drain, 0)


def _dispatch(dest, hp, xs0):
    t, w = hp.shape
    td = min(t, DISPATCH_ROWS)
    return pl.pallas_call(
        functools.partial(_dispatch_kernel, td=td),
        grid=(t // td,),
        in_specs=[pl.BlockSpec((TOP_K, td), lambda i: (0, i), memory_space=pltpu.SMEM),
                  pl.BlockSpec(memory_space=pl.ANY), pl.BlockSpec(memory_space=pl.ANY)],
        out_specs=pl.BlockSpec(memory_space=pl.ANY),
        out_shape=jax.ShapeDtypeStruct(xs0.shape, xs0.dtype),
        scratch_shapes=[pltpu.SemaphoreType.DMA(())],
        input_output_aliases={2: 0},
        compiler_params=_cparams("arbitrary"),
        name="dispatch",
    )(dest, hp, xs0)


def _experts_kernel(be_ref, nb_ref, x_ref, wg_ref, wu_ref, wd_ref, y_ref, wgu_s, wd_s):
    i = pl.program_id(0)
    e = be_ref[i]
    prev = be_ref[jnp.maximum(i - 1, 0)]
    dm = wg_ref.shape[1]
    de = wg_ref.shape[2]

    @pl.when((i == 0) | (e != prev))
    def _():
        wgu_s[:, 0:de] = wg_ref[0].astype(BF16)
        wgu_s[:, de:2 * de] = wu_ref[0].astype(BF16)
        wd_s[...] = wd_ref[0].astype(BF16)

    @pl.when(i < nb_ref[0])
    def _():
        xa, xb = _unpack_bf16_pair(x_ref[...])
        gu = _dot(xa.astype(BF16), wgu_s[0:dm // 2, :]) + _dot(xb.astype(BF16), wgu_s[dm // 2:dm, :])
        g = gu[:, :de]
        act = (g * _sigmoid(g) * gu[:, de:]).astype(BF16)
        y = _dot(act, wd_s[...])
        y_ref[...] = _pack_bf16_pair(y[:, :dm // 2], y[:, dm // 2:])

    @pl.when(i >= nb_ref[0])
    def _():
        y_ref[...] = jnp.zeros_like(y_ref)


def _experts(blk_e, nb_used, xs, w_gate, w_up, w_down):
    rows, half = xs.shape
    ne, dm, de = w_gate.shape
    nb = rows // EXPERT_ROWS
    grid_spec = pltpu.PrefetchScalarGridSpec(
        num_scalar_prefetch=2,
        grid=(nb,),
        in_specs=[pl.BlockSpec((EXPERT_ROWS, half), lambda i, be, n: (i, 0)),
                  pl.BlockSpec((1, dm, de), lambda i, be, n: (be[i], 0, 0)),
                  pl.BlockSpec((1, dm, de), lambda i, be, n: (be[i], 0, 0)),
                  pl.BlockSpec((1, de, dm), lambda i, be, n: (be[i], 0, 0))],
        out_specs=pl.BlockSpec((EXPERT_ROWS, half), lambda i, be, n: (i, 0)),
        scratch_shapes=[pltpu.VMEM((dm, 2 * de), BF16), pltpu.VMEM((de, dm), BF16)],
    )
    return pl.pallas_call(
        _experts_kernel,
        grid_spec=grid_spec,
        out_shape=jax.ShapeDtypeStruct((rows, half), U32),
        compiler_params=_cparams("arbitrary"),
        name="experts",
    )(blk_e, nb_used, xs, w_gate, w_up, w_down)


def _combine_kernel(dest_ref, ys_ref, w_ref, xm_ref, gt2_ref, gfin_ref, o_ref, buf, sem, *, tl):
    def issue(t, c):
        for k in range(TOP_K):
            pltpu.make_async_copy(ys_ref.at[pl.ds(dest_ref[0, k, t], 1)], buf.at[k, pl.ds(t, 1)], sem).start()
        return c

    lax.fori_loop(0, tl, issue, 0)

    def drain(t, c):
        for k in range(TOP_K):
            pltpu.make_async_copy(ys_ref.at[pl.ds(0, 1)], buf.at[0, pl.ds(0, 1)], sem).wait()
        return c

    lax.fori_loop(0, tl, drain, 0)

    w = w_ref[0]
    half = buf.shape[-1]
    acc_a = jnp.zeros((tl, half), F32)
    acc_b = jnp.zeros((tl, half), F32)
    for k in range(TOP_K):
        ya, yb = _unpack_bf16_pair(buf[k])
        wk = w[:, k:k + 1]
        acc_a = acc_a + wk * ya
        acc_b = acc_b + wk * yb
    routed = jnp.concatenate([acc_a, acc_b], axis=-1)
    x2 = xm_ref[0] + gt2_ref[0] * routed
    o_ref[0] = _rms(x2, gfin_ref[...])


def _combine(dest, ys, w, xm, gt2, gfin):
    b, l, d = xm.shape
    tl = min(l, COMBINE_ROWS)
    half = ys.shape[1]
    return pl.pallas_call(
        functools.partial(_combine_kernel, tl=tl),
        grid=(b, l // tl),
        in_specs=[pl.BlockSpec((1, TOP_K, tl), lambda i, j: (i, 0, j), memory_space=pltpu.SMEM),
                  pl.BlockSpec(memory_space=pl.ANY),
                  pl.BlockSpec((1, tl, TOP_K), lambda i, j: (i, j, 0)),
                  pl.BlockSpec((1, tl, d), lambda i, j: (i, j, 0)),
                  pl.BlockSpec((1, 1, d), lambda i, j: (i, 0, 0)),
                  pl.BlockSpec(gfin.shape, lambda i, j: (0, 0))],
        out_specs=pl.BlockSpec((1, tl, d), lambda i, j: (i, j, 0)),
        out_shape=jax.ShapeDtypeStruct((b, l, d), F32),
        scratch_shapes=[pltpu.VMEM((TOP_K, tl, half), U32), pltpu.SemaphoreType.DMA(())],
        compiler_params=_cparams("arbitrary", "arbitrary"),
        name=f"combine{l}",
    )(dest, ys, w, xm, gt2, gfin)


def _block_diag(m):
    g, a, b = m.shape
    eye = jnp.eye(g, dtype=m.dtype)
    return jnp.einsum("gab,gh->gahb", m, eye).reshape(g * a, g * b)


def kernel(x_prompt, x_sample, c_prompt, c_sample, cache_k, cache_v, cache_logf, state_ssm_re, state_ssm_im,
           w_ada, b_ada, g_mix, w_in, b_f, lam_re, lam_im, log_dt, ssm_b_re, ssm_b_im, ssm_c_re, ssm_c_im,
           ssm_d, w_glu, b_glu, g_attn_out, g_ssm_out, w_out, g_ffn, w_router, router_bias,
           w_gate, w_up, w_down, ws_gate, ws_up, ws_down, g_final):
    assert w_ada.shape[0] == 1, "single layer"
    bp, lp, d = x_prompt.shape
    bs, ls, _ = x_sample.shape
    past = cache_k.shape[2]

    nb = bp + bs
    nbp = -(-nb // 8) * 8
    c_all = jnp.concatenate([c_prompt, c_sample, jnp.zeros((nbp - nb, d), F32)], axis=0)
    mod = _prep(c_all, w_ada[0], b_ada[0]).reshape(nbp, 6, 1, d)
    mod_p = [mod[:bp, i] for i in range(6)]
    mod_s = [mod[bp:nb, i] for i in range(6)]

    a_re, a_im, bb_re, bb_im = _disc(lam_re[0], lam_im[0], log_dt[0],
                                     jnp.swapaxes(ssm_b_re[0], 1, 2), jnp.swapaxes(ssm_b_im[0], 1, 2))
    a_re = a_re.reshape(1, D_STATE)
    a_im = a_im.reshape(1, D_STATE)
    bd_re = _block_diag(bb_re).astype(BF16)
    bd_im = _block_diag(bb_im).astype(BF16)
    cc = jnp.concatenate([_block_diag(jnp.swapaxes(ssm_c_re[0], 1, 2)),
                          -_block_diag(jnp.swapaxes(ssm_c_im[0], 1, 2))], axis=0).astype(BF16)
    ssm_tail = (ssm_d[0].reshape(1, D_SSM), w_glu[0].astype(BF16), b_glu[0].reshape(1, D_SSM),
                g_ssm_out[0].reshape(1, D_SSM))

    def ssm_consts(l):
        tc = min(l, SSM_CHUNK)
        pr, pi = _pow_table(a_re, a_im, tc)
        return (bd_re, bd_im, cc, pr, pi) + ssm_tail

    inw = _inproj_weights(w_in[0], b_f[0])
    g_mix2 = g_mix[0].reshape(1, d)

    wo = w_out[0]
    wo_attn = jnp.pad(wo[:D_ATTN].reshape(N_HEADS, HEAD_DIM, d), ((0, 0), (0, HEAD_PAD - HEAD_DIM), (0, 0)))
    wo_pad = jnp.concatenate([wo_attn.reshape(N_HEADS * HEAD_PAD, d), wo[D_ATTN:]], axis=0).astype(BF16)
    gao_pad = jnp.pad(g_attn_out[0].reshape(N_HEADS, HEAD_DIM),
                      ((0, 0), (0, HEAD_PAD - HEAD_DIM))).reshape(1, N_HEADS * HEAD_PAD)
    wr_t = w_router[0].T
    wr_hi = wr_t.astype(BF16)
    wr_lo = (wr_t - wr_hi.astype(F32)).astype(BF16)
    post_consts = (gao_pad, wo_pad, g_ffn[0].reshape(1, d), wr_hi, wr_lo, router_bias[0].reshape(N_EXPERTS, 1),
                   jnp.concatenate([ws_gate[0], ws_up[0]], axis=1).astype(BF16), ws_down[0].astype(BF16))

    def mixer(x, modv, attn_fn, h0r, h0i):
        sh1, sc1, gt1, sh2, sc2, gt2 = modv
        q, k, v, kf, vf, lf, fb, u = _inproj(x, sh1, sc1, g_mix2, inw)
        attn = attn_fn(q, k, v, kf, vf, lf, fb)
        ssm, hr, hi = _ssm(u, h0r, h0i, ssm_consts(x.shape[1]))
        return attn, ssm, (kf, vf, lf, hr, hi), (gt1, sh2, sc2, gt2)

    def attn_prompt(q, k, v, kf, vf, lf, fb):
        return _attn(q, k, v, fb[:, :, 0, :N_HEADS].reshape(-1))

    zeros_p = jnp.zeros((bp, 1, D_STATE), F32)
    attn_p, ssm_p, new_p, m_p = mixer(x_prompt, mod_p, attn_prompt, zeros_p, zeros_p)

    lk = -(-(past + ls) // LANES) * LANES
    padk = lk - past - ls

    def attn_sample(q, k, v, kf, vf, lf, fb):
        k_all = jnp.concatenate([cache_k[0].reshape(bs, past, D_ATTN), kf,
                                 jnp.zeros((bs, padk, D_ATTN), F32)], axis=1)
        v_all = jnp.concatenate([cache_v[0].reshape(bs, past, D_ATTN), vf,
                                 jnp.zeros((bs, padk, D_ATTN), F32)], axis=1)
        lf_all = jnp.concatenate([cache_logf[0], lf, jnp.zeros((bs, padk, N_HEADS), F32)], axis=1)
        lf_cols = jnp.pad(lf_all, ((0, 0), (0, 0), (0, LANES - N_HEADS)))
        lf_rows = jnp.swapaxes(lf_all, 1, 2)
        return _attn_dec(q, k_all, v_all, lf_cols, lf_rows, past, ls)

    attn_s, ssm_s, new_s, m_s = mixer(x_sample, mod_s, attn_sample,
                                      state_ssm_re[0].reshape(bs, 1, D_STATE), state_ssm_im[0].reshape(bs, 1, D_STATE))

    cnt0 = jnp.zeros((N_EXPERTS, 1), F32)
    xm_p, hp_p, idx_p, pos_p, w_p, cnt1 = _post(x_prompt, attn_p, ssm_p, m_p, cnt0, post_consts, True)
    xm_s, hp_s, idx_s, pos_s, w_s, cnt2 = _post(x_sample, attn_s, ssm_s, m_s, cnt1, post_consts, False)

    counts = cnt2[:, 0].astype(I32)
    padded = (counts + EXPERT_ROWS - 1) // EXPERT_ROWS * EXPERT_ROWS
    pend = jnp.cumsum(padded)
    pstart = pend - padded
    tp, ts = bp * lp, bs * ls
    n_blocks = -(-((tp + ts) * TOP_K) // EXPERT_ROWS) + N_EXPERTS
    rows = n_blocks * EXPERT_ROWS
    dest_p = pstart[idx_p] + pos_p
    dest_s = pstart[idx_s] + pos_s
    blk_e = jnp.minimum(jnp.searchsorted(pend, jnp.arange(n_blocks, dtype=I32) * EXPERT_ROWS, side="right"),
                        N_EXPERTS - 1).astype(I32)
    nb_used = (pend[-1] // EXPERT_ROWS).astype(I32).reshape(1)
    blk_e = jnp.where(jnp.arange(n_blocks) < nb_used[0], blk_e, blk_e[jnp.maximum(nb_used[0] - 1, 0)])

    half = d // 2
    xs = jnp.zeros((rows, half), U32)
    flat = lambda dd: jnp.swapaxes(dd, 0, 1).reshape(TOP_K, -1)
    xs = _dispatch(flat(dest_p), hp_p.reshape(tp, half), xs)
    xs = _dispatch(flat(dest_s), hp_s.reshape(ts, half), xs)
    ys = _experts(blk_e, nb_used, xs, w_gate[0], w_up[0], w_down[0])

    gfin = g_final.reshape(1, d)
    y_p = _combine(dest_p, ys, w_p, xm_p, m_p[3], gfin)
    y_s = _combine(dest_s, ys, w_s, xm_s, m_s[3], gfin)

    def pack(new, b, l):
        kf, vf, lf, hr, hi = new
        return (kf.reshape(1, b, l, N_HEADS, HEAD_DIM), vf.reshape(1, b, l, N_HEADS, HEAD_DIM),
                lf.reshape(1, b, l, N_HEADS),
                hr.reshape(1, b, N_SSM_GROUPS, SSM_STATE), hi.reshape(1, b, N_SSM_GROUPS, SSM_STATE))

    return (y_p, y_s) + pack(new_p, bp, lp) + pack(new_s, bs, ls)
```

```python
import functools
import math

import jax
import jax.numpy as jnp
from jax import lax
from jax.experimental import pallas as pl
from jax.experimental.pallas import tpu as pltpu

F32 = jnp.float32
BF16 = jnp.bfloat16
U32 = jnp.uint32
I32 = jnp.int32

N_HEADS = 8
HEAD_DIM = 64
HEAD_PAD = 128
D_ATTN = N_HEADS * HEAD_DIM
SSM_GROUP = 16
N_SSM_GROUPS = 32
SSM_STATE = 64
D_SSM = SSM_GROUP * N_SSM_GROUPS
D_STATE = N_SSM_GROUPS * SSM_STATE
N_EXPERTS = 256
TOP_K = 8
N_EXP_GROUPS = 8
TOPK_GROUPS = 4
ROUTED_SCALE = 2.5
EPS = 1e-6
LANES = 128

SEQ_BLOCK = 512
SSM_CHUNK = 128
EXPERT_ROWS = 256
COMBINE_ROWS = 256
DISPATCH_ROWS = 512
VMEM_LIMIT = 56 * 1024 * 1024

_NT = (((1,), (1,)), ((), ()))


def _cparams(*sem):
    return pltpu.CompilerParams(dimension_semantics=sem, vmem_limit_bytes=VMEM_LIMIT)


def _split3(x):
    hi = x.astype(BF16)
    r = x - hi.astype(F32)
    mid = r.astype(BF16)
    lo = (r - mid.astype(F32)).astype(BF16)
    return hi, mid, lo


def _dot(a, b):
    return jnp.dot(a, b, preferred_element_type=F32)


def _dot_nt(a, b):
    return lax.dot_general(a, b, _NT, preferred_element_type=F32)


def _rms(x, g):
    return x * lax.rsqrt(jnp.mean(x * x, axis=-1, keepdims=True) + EPS) * g


def _sigmoid(x):
    return 1.0 / (1.0 + jnp.exp(-x))


def _pack_bf16_pair(a, b):
    ab = lax.bitcast_convert_type(a.astype(BF16).astype(F32), U32)
    bb = lax.bitcast_convert_type(b.astype(BF16).astype(F32), U32)
    return (ab >> 16) | (bb & jnp.uint32(0xFFFF0000))


def _unpack_bf16_pair(w):
    a = lax.bitcast_convert_type(w << 16, F32)
    b = lax.bitcast_convert_type(w & jnp.uint32(0xFFFF0000), F32)
    return a, b


def _prep_kernel(c_ref, w_ref, b_ref, o_ref):
    c = c_ref[...]
    a = c * _sigmoid(c)
    a_hi = a.astype(BF16)
    a_lo = (a - a_hi.astype(F32)).astype(BF16)
    w = w_ref[...]
    w_hi = w.astype(BF16)
    w_lo = (w - w_hi.astype(F32)).astype(BF16)
    o_ref[...] = _dot(a_hi, w_hi) + _dot(a_lo, w_hi) + _dot(a_hi, w_lo) + b_ref[...]


def _prep(c_all, w_ada, b_ada):
    n, d = c_all.shape
    nout = w_ada.shape[1]
    tn = 1024
    return pl.pallas_call(
        _prep_kernel,
        grid=(nout // tn,),
        in_specs=[pl.BlockSpec((n, d), lambda j: (0, 0)),
                  pl.BlockSpec((d, tn), lambda j: (0, j)),
                  pl.BlockSpec((1, tn), lambda j: (0, j))],
        out_specs=pl.BlockSpec((n, tn), lambda j: (0, j)),
        out_shape=jax.ShapeDtypeStruct((n, nout), F32),
        compiler_params=_cparams("arbitrary"),
        name="prep",
    )(c_all, w_ada, b_ada.reshape(1, nout))


def _disc_kernel(lr_ref, li_ref, ldt_ref, br_ref, bi_ref, ar_ref, ai_ref, bbr_ref, bbi_ref):
    lr = lr_ref[...]
    li = li_ref[...]
    dt = jnp.exp(ldt_ref[...])
    er = jnp.exp(lr * dt)
    ang = li * dt
    ar = er * jnp.cos(ang)
    ai = er * jnp.sin(ang)
    ar_ref[...] = ar
    ai_ref[...] = ai
    den = lr * lr + li * li
    nr = ((ar - 1.0) * lr + ai * li) / den
    ni = (ai * lr - (ar - 1.0) * li) / den
    nr3 = nr[:, None, :]
    ni3 = ni[:, None, :]
    br = br_ref[...]
    bi = bi_ref[...]
    bbr_ref[...] = nr3 * br - ni3 * bi
    bbi_ref[...] = nr3 * bi + ni3 * br


def _disc(lam_re, lam_im, log_dt, bt_re, bt_im):
    g, p = lam_re.shape
    c = bt_re.shape[1]
    return pl.pallas_call(
        _disc_kernel,
        out_shape=(jax.ShapeDtypeStruct((g, p), F32), jax.ShapeDtypeStruct((g, p), F32),
                   jax.ShapeDtypeStruct((g, c, p), F32), jax.ShapeDtypeStruct((g, c, p), F32)),
        name="disc",
    )(lam_re, lam_im, log_dt.reshape(g, 1), bt_re, bt_im)


def _pow_kernel(ar_ref, ai_ref, pr_ref, pi_ref, *, tc):
    n1 = lax.broadcasted_iota(I32, (tc, 1), 0) + 1
    pr = jnp.ones(pr_ref.shape, F32)
    pi = jnp.zeros(pi_ref.shape, F32)
    sr = ar_ref[...]
    si = ai_ref[...]
    for k in range(int(math.log2(tc)) + 1):
        bit = ((n1 >> k) & 1) == 1
        fr = jnp.where(bit, sr, 1.0)
        fi = jnp.where(bit, si, 0.0)
        pr, pi = pr * fr - pi * fi, pr * fi + pi * fr
        sr, si = sr * sr - si * si, 2.0 * sr * si
    pr_ref[...] = pr
    pi_ref[...] = pi


def _pow_table(a_re, a_im, tc):
    n = a_re.shape[1]
    return pl.pallas_call(
        functools.partial(_pow_kernel, tc=tc),
        out_shape=(jax.ShapeDtypeStruct((tc, n), F32), jax.ShapeDtypeStruct((tc, n), F32)),
        name=f"pow{tc}",
    )(a_re, a_im)


def _inproj_kernel(x_ref, sh_ref, sc_ref, g_ref, wm_ref, wf_ref, bf_ref, selq_ref, selk_ref,
                   cq_ref, ck_ref, cv_ref,
                   q_ref, k_ref, v_ref, kf_ref, vf_ref, lf_ref, fb_ref, u_ref, carry_ref, *, tl):
    @pl.when(pl.program_id(1) == 0)
    def _():
        carry_ref[...] = jnp.zeros_like(carry_ref)

    x = x_ref[0]
    h = _rms(x, g_ref[...]) * (1.0 + sc_ref[0]) + sh_ref[0]
    hb = h.astype(BF16)
    main = _dot(hb, wm_ref[...])
    hw = N_HEADS * HEAD_PAD
    qp = main[:, 0:hw] * (HEAD_DIM ** -0.5)
    kp = main[:, hw:2 * hw]
    vp = main[:, 2 * hw:3 * hw]
    o = 3 * hw
    kf_ref[0] = main[:, o:o + D_ATTN]
    vf_ref[0] = main[:, o + D_ATTN:o + 2 * D_ATTN]
    u_ref[0] = main[:, o + 2 * D_ATTN:o + 2 * D_ATTN + D_SSM]

    fl = _dot(hb, wf_ref[...]) + bf_ref[...]
    lf = jnp.minimum(fl, 0.0) - jnp.log1p(jnp.exp(-jnp.abs(fl)))
    lf_ref[0] = lf[:, 0:N_HEADS]

    row = lax.broadcasted_iota(I32, (tl, tl), 0)
    col = lax.broadcasted_iota(I32, (tl, tl), 1)
    tri = jnp.where(row >= col, 1.0, 0.0).astype(BF16)
    hi, mid, lo = _split3(lf)
    frel = _dot(tri, hi) + _dot(tri, mid) + _dot(tri, lo)
    fb_ref[0, 0] = carry_ref[...]
    carry_ref[...] = carry_ref[...] + frel[tl - 1:tl, :]

    fcat = jnp.concatenate(_split3(frel), axis=-1)
    q_aug = (qp + _dot(fcat, selq_ref[...]) + cq_ref[...]).astype(BF16)
    k_aug = (kp + _dot(fcat, selk_ref[...]) + ck_ref[...]).astype(BF16)
    v_aug = (vp + cv_ref[...]).astype(BF16)
    for hd in range(N_HEADS):
        sl = slice(hd * HEAD_PAD, (hd + 1) * HEAD_PAD)
        q_ref[0, hd] = q_aug[:, sl]
        k_ref[0, hd] = k_aug[:, sl]
        v_ref[0, hd] = v_aug[:, sl]


def _inproj(x, sh, sc, g, wts):
    b, l, d = x.shape
    tl = min(l, SEQ_BLOCK)
    nl = l // tl
    wm, wf, bf, selq, selk, cq, ck, cv = wts
    hm = jax.ShapeDtypeStruct((b, N_HEADS, l, HEAD_PAD), BF16)
    hm_spec = pl.BlockSpec((1, N_HEADS, tl, HEAD_PAD), lambda i, j: (i, 0, j, 0))
    tok = lambda w: pl.BlockSpec((1, tl, w), lambda i, j: (i, j, 0))
    full = lambda a: pl.BlockSpec(a.shape, lambda i, j: (0,) * a.ndim)
    row = pl.BlockSpec((1, 1, d), lambda i, j: (i, 0, 0))
    return pl.pallas_call(
        functools.partial(_inproj_kernel, tl=tl),
        grid=(b, nl),
        in_specs=[tok(d), row, row, full(g), full(wm), full(wf), full(bf), full(selq), full(selk),
                  full(cq), full(ck), full(cv)],
        out_specs=[hm_spec, hm_spec, hm_spec, tok(D_ATTN), tok(D_ATTN), tok(N_HEADS),
                   pl.BlockSpec((1, 1, 1, LANES), lambda i, j: (i, j, 0, 0)), tok(D_SSM)],
        out_shape=[hm, hm, hm,
                   jax.ShapeDtypeStruct((b, l, D_ATTN), F32), jax.ShapeDtypeStruct((b, l, D_ATTN), F32),
                   jax.ShapeDtypeStruct((b, l, N_HEADS), F32),
                   jax.ShapeDtypeStruct((b, nl, 1, LANES), F32),
                   jax.ShapeDtypeStruct((b, l, D_SSM), F32)],
        scratch_shapes=[pltpu.VMEM((1, LANES), F32)],
        compiler_params=_cparams("arbitrary", "arbitrary"),
        name=f"inproj{l}",
    )(x, sh, sc, g, wm, wf, bf, selq, selk, cq, ck, cv)


def _inproj_weights(w_in, b_f):
    d = w_in.shape[0]
    wq, wk, wv = (w_in[:, i * D_ATTN:(i + 1) * D_ATTN] for i in range(3))
    wfl = w_in[:, 3 * D_ATTN:3 * D_ATTN + N_HEADS]
    wu = w_in[:, 3 * D_ATTN + N_HEADS:]

    def pad_heads(w):
        w = w.reshape(d, N_HEADS, HEAD_DIM)
        w = jnp.pad(w, ((0, 0), (0, 0), (0, HEAD_PAD - HEAD_DIM)))
        return w.reshape(d, N_HEADS * HEAD_PAD)

    wm = jnp.concatenate([pad_heads(wq), pad_heads(wk), pad_heads(wv), wk, wv, wu], axis=1).astype(BF16)
    wf = jnp.pad(wfl, ((0, 0), (0, LANES - N_HEADS))).astype(BF16)
    bf = jnp.pad(b_f, (0, LANES - N_HEADS)).reshape(1, LANES).astype(F32)

    hw = N_HEADS * HEAD_PAD
    hd = jnp.arange(N_HEADS)
    selq = jnp.zeros((3 * LANES, hw), F32)
    selk = jnp.zeros((3 * LANES, hw), F32)
    cq = jnp.zeros((1, hw), F32)
    ck = jnp.zeros((1, hw), F32)
    cv = jnp.zeros((1, hw), F32)
    for part in range(3):
        selq = selq.at[part * LANES + hd, hd * HEAD_PAD + HEAD_DIM + part].set(1.0)
        selk = selk.at[part * LANES + hd, hd * HEAD_PAD + HEAD_DIM + 3 + part].set(-1.0)
        cq = cq.at[0, hd * HEAD_PAD + HEAD_DIM + 3 + part].set(1.0)
        ck = ck.at[0, hd * HEAD_PAD + HEAD_DIM + part].set(1.0)
    cv = cv.at[0, hd * HEAD_PAD + HEAD_DIM].set(1.0)
    return wm, wf, bf, selq.astype(BF16), selk.astype(BF16), cq, ck, cv


def _attn_kernel(fb_ref, q_ref, k_ref, v_ref, o_ref, *, r, nblk):
    b = pl.program_id(0)
    hd = pl.program_id(1)
    i = pl.program_id(2)
    q = q_ref[0, 0]
    fbase = (b * nblk) * N_HEADS + hd
    fbi = fb_ref[fbase + i * N_HEADS]

    def step(j, carry, masked):
        m, acc = carry
        start = pl.multiple_of(j * r, r)
        kj = k_ref[0, 0, pl.ds(start, r), :]
        vj = v_ref[0, 0, pl.ds(start, r), :]
        s = _dot_nt(q, kj)
        if masked:
            row = lax.broadcasted_iota(I32, (r, r), 0)
            col = lax.broadcasted_iota(I32, (r, r), 1)
            s = jnp.where(row >= col, s, -jnp.inf)
        dlt = fbi - fb_ref[fbase + j * N_HEADS]
        mnew = jnp.maximum(m, jnp.max(s, axis=-1, keepdims=True) + dlt)
        p = jnp.exp(s - (mnew - dlt))
        acc = jnp.exp(m - mnew) * acc + _dot(p.astype(BF16), vj)
        return mnew, acc

    init = (jnp.full((r, 1), -jnp.inf, F32), jnp.zeros((r, HEAD_PAD), F32))
    carry = lax.fori_loop(0, i, lambda j, c: step(j, c, False), init)
    _, acc = step(i, carry, True)
    lane = lax.broadcasted_iota(I32, (r, HEAD_PAD), 1)
    out = acc / acc[:, HEAD_DIM:HEAD_DIM + 1]
    o_ref[0, 0] = jnp.where(lane < HEAD_DIM, out, 0.0).astype(BF16)


def _attn(q, k, v, fb):
    b, h, l, _ = q.shape
    r = min(l, SEQ_BLOCK)
    nblk = l // r
    return pl.pallas_call(
        functools.partial(_attn_kernel, r=r, nblk=nblk),
        grid=(b, h, nblk),
        in_specs=[pl.BlockSpec(memory_space=pltpu.SMEM),
                  pl.BlockSpec((1, 1, r, HEAD_PAD), lambda bi, hi, i: (bi, hi, i, 0)),
                  pl.BlockSpec((1, 1, l, HEAD_PAD), lambda bi, hi, i: (bi, hi, 0, 0)),
                  pl.BlockSpec((1, 1, l, HEAD_PAD), lambda bi, hi, i: (bi, hi, 0, 0))],
        out_specs=pl.BlockSpec((1, 1, r, HEAD_PAD), lambda bi, hi, i: (bi, hi, i, 0)),
        out_shape=jax.ShapeDtypeStruct((b, h, l, HEAD_PAD), BF16),
        compiler_params=_cparams("arbitrary", "arbitrary", "arbitrary"),
        name="attn",
    )(fb, q, k, v)


def _attn_dec_kernel(q_ref, k_ref, v_ref, lfc_ref, lfr_ref, o_ref, *, lq, past, lk):
    row = lax.broadcasted_iota(I32, (lk, lk), 0)
    col = lax.broadcasted_iota(I32, (lk, lk), 1)
    tri = jnp.where(row >= col, 1.0, 0.0).astype(BF16)
    upper = jnp.where(row <= col, 1.0, 0.0).astype(BF16)
    c_hi, c_mid, c_lo = _split3(lfc_ref[0])
    fcol = _dot(tri, c_hi) + _dot(tri, c_mid) + _dot(tri, c_lo)
    r_hi, r_mid, r_lo = _split3(lfr_ref[0])
    frow = _dot(r_hi, upper) + _dot(r_mid, upper) + _dot(r_lo, upper)
    fq = fcol[past:past + lq, :]
    qpos = past + lax.broadcasted_iota(I32, (lq, lk), 0)
    kpos = lax.broadcasted_iota(I32, (lq, lk), 1)
    ok = kpos <= qpos
    qlane = lax.broadcasted_iota(I32, (lq, HEAD_PAD), 1)
    klane = lax.broadcasted_iota(I32, (lk, HEAD_PAD), 1)
    for hd in range(N_HEADS):
        odd = hd % 2 == 1
        pair = slice((hd // 2) * HEAD_PAD, (hd // 2 + 1) * HEAD_PAD)
        mine = (klane >= HEAD_DIM) if odd else (klane < HEAD_DIM)
        kh = jnp.where(mine, k_ref[0, :, pair], 0.0).astype(BF16)
        vh = jnp.where(mine, v_ref[0, :, pair], 0.0).astype(BF16)
        qh = jnp.where(qlane < HEAD_DIM, q_ref[0, hd].astype(F32), 0.0)
        if odd:
            qh = pltpu.roll(qh, HEAD_DIM, axis=1)
        s = _dot_nt(qh.astype(BF16), kh) + fq[:, hd:hd + 1] - frow[hd:hd + 1, :]
        s = jnp.where(ok, s, -jnp.inf)
        m = jnp.max(s, axis=-1, keepdims=True)
        pb = jnp.exp(s - m).astype(BF16)
        den = jnp.sum(pb.astype(F32), axis=-1, keepdims=True)
        out = _dot(pb, vh) / den
        if odd:
            out = pltpu.roll(out, HEAD_DIM, axis=1)
        o_ref[0, hd] = out.astype(BF16)


def _attn_dec(q, k_all, v_all, lf_cols, lf_rows, past, lq):
    b = q.shape[0]
    lk = k_all.shape[1]
    return pl.pallas_call(
        functools.partial(_attn_dec_kernel, lq=lq, past=past, lk=lk),
        grid=(b,),
        in_specs=[pl.BlockSpec((1, N_HEADS, lq, HEAD_PAD), lambda i: (i, 0, 0, 0)),
                  pl.BlockSpec((1, lk, D_ATTN), lambda i: (i, 0, 0)),
                  pl.BlockSpec((1, lk, D_ATTN), lambda i: (i, 0, 0)),
                  pl.BlockSpec((1, lk, LANES), lambda i: (i, 0, 0)),
                  pl.BlockSpec((1, N_HEADS, lk), lambda i: (i, 0, 0))],
        out_specs=pl.BlockSpec((1, N_HEADS, lq, HEAD_PAD), lambda i: (i, 0, 0, 0)),
        out_shape=jax.ShapeDtypeStruct((b, N_HEADS, lq, HEAD_PAD), BF16),
        compiler_params=_cparams("arbitrary"),
        name="attn_dec",
    )(q, k_all, v_all, lf_cols, lf_rows)


def _ssm_kernel(u_ref, h0r_ref, h0i_ref, br_ref, bi_ref, cc_ref, pr_ref, pi_ref, d_ref, wg_ref, bg_ref,
                gso_ref, y_ref, hr_out, hi_out, cr_ref, ci_ref, *, tc):
    @pl.when(pl.program_id(1) == 0)
    def _():
        cr_ref[...] = h0r_ref[0]
        ci_ref[...] = h0i_ref[0]

    u = u_ref[0]
    ub = u.astype(BF16)
    hr = _dot(ub, br_ref[...])
    hi = _dot(ub, bi_ref[...])
    t = lax.broadcasted_iota(I32, (tc, 1), 0)
    for k in range(int(math.log2(tc))):
        n = 1 << k
        ar = pr_ref[n - 1:n, :]
        ai = pi_ref[n - 1:n, :]
        keep = t >= n
        sr = jnp.where(keep, pltpu.roll(hr, n, axis=0), 0.0)
        si = jnp.where(keep, pltpu.roll(hi, n, axis=0), 0.0)
        hr, hi = hr + ar * sr - ai * si, hi + ar * si + ai * sr
    pr = pr_ref[...]
    pi = pi_ref[...]
    cr = cr_ref[...]
    ci = ci_ref[...]
    hr, hi = hr + pr * cr - pi * ci, hi + pr * ci + pi * cr
    cr_ref[...] = hr[tc - 1:tc, :]
    ci_ref[...] = hi[tc - 1:tc, :]
    hr_out[0] = hr[tc - 1:tc, :]
    hi_out[0] = hi[tc - 1:tc, :]

    hcat = jnp.concatenate([hr, hi], axis=-1).astype(BF16)
    y = _dot(hcat, cc_ref[...]) + d_ref[...] * u
    gl = 0.5 * y * (1.0 + jnp.tanh(math.sqrt(2.0 / math.pi) * (y + 0.044715 * (y * y * y))))
    z = _dot(gl.astype(BF16), wg_ref[...]) + bg_ref[...]
    out = y * _sigmoid(z)
    y_ref[0] = _rms(out, gso_ref[...]).astype(BF16)


def _ssm(u, h0r, h0i, consts):
    b, l, _ = u.shape
    br, bi, cc, pr, pi, dsk, wg, bg, gso = consts
    tc = pr.shape[0]
    full = lambda a: pl.BlockSpec(a.shape, lambda i, j: (0,) * a.ndim)
    st = pl.BlockSpec((1, 1, D_STATE), lambda i, j: (i, 0, 0))
    return pl.pallas_call(
        functools.partial(_ssm_kernel, tc=tc),
        grid=(b, l // tc),
        in_specs=[pl.BlockSpec((1, tc, D_SSM), lambda i, j: (i, j, 0)), st, st,
                  full(br), full(bi), full(cc), full(pr), full(pi), full(dsk), full(wg), full(bg), full(gso)],
        out_specs=[pl.BlockSpec((1, tc, D_SSM), lambda i, j: (i, j, 0)), st, st],
        out_shape=[jax.ShapeDtypeStruct((b, l, D_SSM), BF16),
                   jax.ShapeDtypeStruct((b, 1, D_STATE), F32), jax.ShapeDtypeStruct((b, 1, D_STATE), F32)],
        scratch_shapes=[pltpu.VMEM((1, D_STATE), F32), pltpu.VMEM((1, D_STATE), F32)],
        compiler_params=_cparams("arbitrary", "arbitrary"),
        name=f"ssm{l}",
    )(u, h0r, h0i, br, bi, cc, pr, pi, dsk, wg, bg, gso)


def _post_kernel(x_ref, a_ref, s_ref, gt1_ref, sh2_ref, sc2_ref, gt2_ref, cnt0_ref, gao_ref, wo_ref, gffn_ref,
                 wrh_ref, wrl_ref, rb_ref, wsgu_ref, wsd_ref,
                 xm_ref, hp_ref, idx_ref, pos_ref, w_ref, cnt_ref, carry_ref, *, tl, first):
    step = pl.program_id(0) * pl.num_programs(1) + pl.program_id(1)

    @pl.when(step == 0)
    def _():
        carry_ref[...] = cnt0_ref[...] if not first else jnp.zeros_like(carry_ref)

    x = x_ref[0]
    attn = jnp.concatenate([a_ref[0, hd] for hd in range(N_HEADS)], axis=-1).astype(F32)
    ms = jnp.sum(attn * attn, axis=-1, keepdims=True) * (1.0 / D_ATTN)
    attn_n = (attn * lax.rsqrt(ms + EPS) * gao_ref[...]).astype(BF16)
    merged = jnp.concatenate([attn_n, s_ref[0]], axis=-1)
    x1 = x + gt1_ref[0] * _dot(merged, wo_ref[...])
    h2 = _rms(x1, gffn_ref[...]) * (1.0 + sc2_ref[0]) + sh2_ref[0]
    d = h2.shape[-1]
    hp_ref[0] = _pack_bf16_pair(h2[:, :d // 2], h2[:, d // 2:])
    h_hi = h2.astype(BF16)
    h_lo = (h2 - h_hi.astype(F32)).astype(BF16)

    gu = _dot(h_hi, wsgu_ref[...])
    ds = gu.shape[-1] // 2
    g = gu[:, :ds]
    act = (g * _sigmoid(g) * gu[:, ds:]).astype(BF16)
    xm_ref[0] = x1 + gt2_ref[0] * _dot(act, wsd_ref[...])

    wrh = wrh_ref[...]
    logits = _dot_nt(wrh, h_hi) + _dot_nt(wrh, h_lo) + _dot_nt(wrl_ref[...], h_hi)
    s = _sigmoid(logits)
    sc = s + rb_ref[...]
    ge = N_EXPERTS // N_EXP_GROUPS
    neg = -jnp.inf
    gi = lax.broadcasted_iota(I32, (ge, tl), 0)
    gsc = []
    for gidx in range(N_EXP_GROUPS):
        blk = sc[gidx * ge:(gidx + 1) * ge, :]
        m1 = jnp.max(blk, axis=0, keepdims=True)
        f1 = jnp.min(jnp.where(blk == m1, gi, ge), axis=0, keepdims=True)
        m2 = jnp.max(jnp.where(gi == f1, neg, blk), axis=0, keepdims=True)
        gsc.append(m1 + m2)
    gwork = jnp.concatenate(gsc, axis=0)
    ni = lax.broadcasted_iota(I32, (N_EXP_GROUPS, tl), 0)
    gsel = jnp.zeros((N_EXP_GROUPS, tl), F32)
    for _ in range(TOPK_GROUPS):
        mx = jnp.max(gwork, axis=0, keepdims=True)
        fi = jnp.min(jnp.where(gwork == mx, ni, N_EXP_GROUPS), axis=0, keepdims=True)
        hit = ni == fi
        gsel = jnp.where(hit, 1.0, gsel)
        gwork = jnp.where(hit, neg, gwork)
    work = jnp.concatenate(
        [jnp.where(gsel[gidx:gidx + 1, :] > 0.0, sc[gidx * ge:(gidx + 1) * ge, :], neg)
         for gidx in range(N_EXP_GROUPS)], axis=0)
    ei = lax.broadcasted_iota(I32, (N_EXPERTS, tl), 0)
    chosen = jnp.zeros((N_EXPERTS, tl), F32)
    idx_rows, w_rows = [], []
    for _ in range(TOP_K):
        mx = jnp.max(work, axis=0, keepdims=True)
        fi = jnp.min(jnp.where(work == mx, ei, N_EXPERTS), axis=0, keepdims=True)
        hit = ei == fi
        w_rows.append(jnp.sum(jnp.where(hit, s, 0.0), axis=0, keepdims=True))
        idx_rows.append(fi)
        chosen = jnp.where(hit, 1.0, chosen)
        work = jnp.where(hit, neg, work)
    wt = jnp.concatenate(w_rows, axis=0)
    wt = wt / jnp.sum(wt, axis=0, keepdims=True) * ROUTED_SCALE
    idx_ref[0] = jnp.concatenate(idx_rows, axis=0)

    trow = lax.broadcasted_iota(I32, (tl, tl), 0)
    tcol = lax.broadcasted_iota(I32, (tl, tl), 1)
    before = jnp.where(trow < tcol, 1.0, 0.0).astype(BF16)
    rank = _dot(chosen.astype(BF16), before) + carry_ref[...]
    pos_rows = [jnp.sum(jnp.where(ei == idx_rows[k], rank, 0.0), axis=0, keepdims=True) for k in range(TOP_K)]
    pos_ref[0] = jnp.concatenate(pos_rows, axis=0).astype(I32)
    carry_ref[...] = carry_ref[...] + jnp.sum(chosen, axis=1, keepdims=True)
    cnt_ref[...] = carry_ref[...]

    eye = jnp.where(trow == tcol, 1.0, 0.0).astype(BF16)
    t_hi, t_mid, t_lo = _split3(wt)
    w_ref[0] = _dot_nt(eye, t_hi) + _dot_nt(eye, t_mid) + _dot_nt(eye, t_lo)


def _post(x, attn, ssm, mods, cnt0, consts, first):
    b, l, d = x.shape
    tl = min(l, SEQ_BLOCK)
    gt1, sh2, sc2, gt2 = mods
    gao, wo, gffn, wrh, wrl, rb, wsgu, wsd = consts
    tok = lambda w: pl.BlockSpec((1, tl, w), lambda i, j: (i, j, 0))
    full = lambda a: pl.BlockSpec(a.shape, lambda i, j: (0,) * a.ndim)
    row = pl.BlockSpec((1, 1, d), lambda i, j: (i, 0, 0))
    tk = pl.BlockSpec((1, TOP_K, tl), lambda i, j: (i, 0, j))
    return pl.pallas_call(
        functools.partial(_post_kernel, tl=tl, first=first),
        grid=(b, l // tl),
        in_specs=[tok(d), pl.BlockSpec((1, N_HEADS, tl, HEAD_PAD), lambda i, j: (i, 0, j, 0)), tok(D_SSM),
                  row, row, row, row, full(cnt0), full(gao), full(wo), full(gffn), full(wrh), full(wrl),
                  full(rb), full(wsgu), full(wsd)],
        out_specs=[tok(d), tok(d // 2), tk, tk, tok(TOP_K), full(cnt0)],
        out_shape=[jax.ShapeDtypeStruct((b, l, d), F32), jax.ShapeDtypeStruct((b, l, d // 2), U32),
                   jax.ShapeDtypeStruct((b, TOP_K, l), I32), jax.ShapeDtypeStruct((b, TOP_K, l), I32),
                   jax.ShapeDtypeStruct((b, l, TOP_K), F32), jax.ShapeDtypeStruct(cnt0.shape, F32)],
        scratch_shapes=[pltpu.VMEM(cnt0.shape, F32)],
        compiler_params=_cparams("arbitrary", "arbitrary"),
        name=f"post{l}",
    )(x, attn, ssm, gt1, sh2, sc2, gt2, cnt0, gao, wo, gffn, wrh, wrl, rb, wsgu, wsd)


def _slots_kernel(pstart_ref, idx_ref, pos_ref, dest_ref):
    idx = idx_ref[0]

    def body(e, base):
        return jnp.where(idx == e, pstart_ref[e], base)

    dest_ref[0] = lax.fori_loop(0, N_EXPERTS, body, jnp.zeros_like(idx)) + pos_ref[0]


def _slots(pstart, idx, pos):
    b, k, l = idx.shape
    tl = min(l, 2048)
    spec = pl.BlockSpec((1, k, tl), lambda i, j: (i, 0, j))
    return pl.pallas_call(
        _slots_kernel,
        grid=(b, l // tl),
        in_specs=[pl.BlockSpec(memory_space=pltpu.SMEM), spec, spec],
        out_specs=spec,
        out_shape=jax.ShapeDtypeStruct(idx.shape, I32),
        compiler_params=_cparams("arbitrary", "arbitrary"),
        name=f"slots{l}",
    )(pstart, idx, pos)


def _dispatch_kernel(dest_ref, h_ref, xin_ref, xs_ref, sem, *, td):
    del xin_ref

    def issue(t, c):
        for k in range(TOP_K):
            pltpu.make_async_copy(h_ref.at[pl.ds(t, 1)], xs_ref.at[pl.ds(dest_ref[k, t], 1)], sem).start()
        return c

    lax.fori_loop(0, td, issue, 0)

    def drain(t, c):
        for k in range(TOP_K):
            pltpu.make_async_copy(h_ref.at[pl.ds(0, 1)], xs_ref.at[pl.ds(0, 1)], sem).wait()
        return c

    lax.fori_loop(0, td, drain, 0)


def _dispatch(dest, hp, xs0):
    t, w = hp.shape
    td = min(t, DISPATCH_ROWS)
    return pl.pallas_call(
        functools.partial(_dispatch_kernel, td=td),
        grid=(t // td,),
        in_specs=[pl.BlockSpec((TOP_K, td), lambda i: (0, i), memory_space=pltpu.SMEM),
                  pl.BlockSpec((td, w), lambda i: (i, 0)), pl.BlockSpec(memory_space=pl.ANY)],
        out_specs=pl.BlockSpec(memory_space=pl.ANY),
        out_shape=jax.ShapeDtypeStruct(xs0.shape, xs0.dtype),
        scratch_shapes=[pltpu.SemaphoreType.DMA(())],
        input_output_aliases={2: 0},
        compiler_params=_cparams("arbitrary"),
        name="dispatch",
    )(dest, hp, xs0)


def _experts_kernel(be_ref, nb_ref, x_ref, wg_ref, wu_ref, wd_ref, y_ref, wgu_s, wd_s):
    i = pl.program_id(0)
    e = be_ref[i]
    prev = be_ref[jnp.maximum(i - 1, 0)]
    dm = wg_ref.shape[1]
    de = wg_ref.shape[2]

    @pl.when((i == 0) | (e != prev))
    def _():
        wgu_s[:, 0:de] = wg_ref[0].astype(BF16)
        wgu_s[:, de:2 * de] = wu_ref[0].astype(BF16)
        wd_s[...] = wd_ref[0].astype(BF16)

    @pl.when(i < nb_ref[0])
    def _():
        xa, xb = _unpack_bf16_pair(x_ref[...])
        gu = _dot(xa.astype(BF16), wgu_s[0:dm // 2, :]) + _dot(xb.astype(BF16), wgu_s[dm // 2:dm, :])
        g = gu[:, :de]
        act = (g * _sigmoid(g) * gu[:, de:]).astype(BF16)
        y = _dot(act, wd_s[...])
        y_ref[...] = _pack_bf16_pair(y[:, :dm // 2], y[:, dm // 2:])

    @pl.when(i >= nb_ref[0])
    def _():
        y_ref[...] = jnp.zeros_like(y_ref)


def _experts(blk_e, nb_used, xs, w_gate, w_up, w_down):
    rows, half = xs.shape
    ne, dm, de = w_gate.shape
    nb = rows // EXPERT_ROWS
    grid_spec = pltpu.PrefetchScalarGridSpec(
        num_scalar_prefetch=2,
        grid=(nb,),
        in_specs=[pl.BlockSpec((EXPERT_ROWS, half), lambda i, be, n: (i, 0)),
                  pl.BlockSpec((1, dm, de), lambda i, be, n: (be[i], 0, 0)),
                  pl.BlockSpec((1, dm, de), lambda i, be, n: (be[i], 0, 0)),
                  pl.BlockSpec((1, de, dm), lambda i, be, n: (be[i], 0, 0))],
        out_specs=pl.BlockSpec((EXPERT_ROWS, half), lambda i, be, n: (i, 0)),
        scratch_shapes=[pltpu.VMEM((dm, 2 * de), BF16), pltpu.VMEM((de, dm), BF16)],
    )
    return pl.pallas_call(
        _experts_kernel,
        grid_spec=grid_spec,
        out_shape=jax.ShapeDtypeStruct((rows, half), U32),
        compiler_params=_cparams("arbitrary"),
        name="experts",
    )(blk_e, nb_used, xs, w_gate, w_up, w_down)


def _combine_kernel(dest_ref, ys_ref, w_ref, xm_ref, gt2_ref, gfin_ref, o_ref, buf, sem, *, tl):
    def issue(t, c):
        for k in range(TOP_K):
            pltpu.make_async_copy(ys_ref.at[pl.ds(dest_ref[0, k, t], 1)], buf.at[k, pl.ds(t, 1)], sem).start()
        return c

    lax.fori_loop(0, tl, issue, 0)

    def drain(t, c):
        for k in range(TOP_K):
            pltpu.make_async_copy(ys_ref.at[pl.ds(0, 1)], buf.at[0, pl.ds(0, 1)], sem).wait()
        return c

    lax.fori_loop(0, tl, drain, 0)

    w = w_ref[0]
    half = buf.shape[-1]
    acc_a = jnp.zeros((tl, half), F32)
    acc_b = jnp.zeros((tl, half), F32)
    for k in range(TOP_K):
        ya, yb = _unpack_bf16_pair(buf[k])
        wk = w[:, k:k + 1]
        acc_a = acc_a + wk * ya
        acc_b = acc_b + wk * yb
    routed = jnp.concatenate([acc_a, acc_b], axis=-1)
    x2 = xm_ref[0] + gt2_ref[0] * routed
    o_ref[0] = _rms(x2, gfin_ref[...])


def _combine(dest, ys, w, xm, gt2, gfin):
    b, l, d = xm.shape
    tl = min(l, COMBINE_ROWS)
    half = ys.shape[1]
    return pl.pallas_call(
        functools.partial(_combine_kernel, tl=tl),
        grid=(b, l // tl),
        in_specs=[pl.BlockSpec((1, TOP_K, tl), lambda i, j: (i, 0, j), memory_space=pltpu.SMEM),
                  pl.BlockSpec(memory_space=pl.ANY),
                  pl.BlockSpec((1, tl, TOP_K), lambda i, j: (i, j, 0)),
                  pl.BlockSpec((1, tl, d), lambda i, j: (i, j, 0)),
                  pl.BlockSpec((1, 1, d), lambda i, j: (i, 0, 0)),
                  pl.BlockSpec(gfin.shape, lambda i, j: (0, 0))],
        out_specs=pl.BlockSpec((1, tl, d), lambda i, j: (i, j, 0)),
        out_shape=jax.ShapeDtypeStruct((b, l, d), F32),
        scratch_shapes=[pltpu.VMEM((TOP_K, tl, half), U32), pltpu.SemaphoreType.DMA(())],
        compiler_params=_cparams("arbitrary", "arbitrary"),
        name=f"combine{l}",
    )(dest, ys, w, xm, gt2, gfin)


def _block_diag(m):
    g, a, b = m.shape
    eye = jnp.eye(g, dtype=m.dtype)
    return jnp.einsum("gab,gh->gahb", m, eye).reshape(g * a, g * b)


def kernel(x_prompt, x_sample, c_prompt, c_sample, cache_k, cache_v, cache_logf, state_ssm_re, state_ssm_im,
           w_ada, b_ada, g_mix, w_in, b_f, lam_re, lam_im, log_dt, ssm_b_re, ssm_b_im, ssm_c_re, ssm_c_im,
           ssm_d, w_glu, b_glu, g_attn_out, g_ssm_out, w_out, g_ffn, w_router, router_bias,
           w_gate, w_up, w_down, ws_gate, ws_up, ws_down, g_final):
    assert w_ada.shape[0] == 1, "single layer"
    bp, lp, d = x_prompt.shape
    bs, ls, _ = x_sample.shape
    past = cache_k.shape[2]

    nb = bp + bs
    nbp = -(-nb // 8) * 8
    c_all = jnp.concatenate([c_prompt, c_sample, jnp.zeros((nbp - nb, d), F32)], axis=0)
    mod = _prep(c_all, w_ada[0], b_ada[0]).reshape(nbp, 6, 1, d)
    mod_p = [mod[:bp, i] for i in range(6)]
    mod_s = [mod[bp:nb, i] for i in range(6)]

    a_re, a_im, bb_re, bb_im = _disc(lam_re[0], lam_im[0], log_dt[0],
                                     jnp.swapaxes(ssm_b_re[0], 1, 2), jnp.swapaxes(ssm_b_im[0], 1, 2))
    a_re = a_re.reshape(1, D_STATE)
    a_im = a_im.reshape(1, D_STATE)
    bd_re = _block_diag(bb_re).astype(BF16)
    bd_im = _block_diag(bb_im).astype(BF16)
    cc = jnp.concatenate([_block_diag(jnp.swapaxes(ssm_c_re[0], 1, 2)),
                          -_block_diag(jnp.swapaxes(ssm_c_im[0], 1, 2))], axis=0).astype(BF16)
    ssm_tail = (ssm_d[0].reshape(1, D_SSM), w_glu[0].astype(BF16), b_glu[0].reshape(1, D_SSM),
                g_ssm_out[0].reshape(1, D_SSM))

    def ssm_consts(l):
        tc = min(l, SSM_CHUNK)
        pr, pi = _pow_table(a_re, a_im, tc)
        return (bd_re, bd_im, cc, pr, pi) + ssm_tail

    inw = _inproj_weights(w_in[0], b_f[0])
    g_mix2 = g_mix[0].reshape(1, d)

    wo = w_out[0]
    wo_attn = jnp.pad(wo[:D_ATTN].reshape(N_HEADS, HEAD_DIM, d), ((0, 0), (0, HEAD_PAD - HEAD_DIM), (0, 0)))
    wo_pad = jnp.concatenate([wo_attn.reshape(N_HEADS * HEAD_PAD, d), wo[D_ATTN:]], axis=0).astype(BF16)
    gao_pad = jnp.pad(g_attn_out[0].reshape(N_HEADS, HEAD_DIM),
                      ((0, 0), (0, HEAD_PAD - HEAD_DIM))).reshape(1, N_HEADS * HEAD_PAD)
    wr_t = w_router[0].T
    wr_hi = wr_t.astype(BF16)
    wr_lo = (wr_t - wr_hi.astype(F32)).astype(BF16)
    post_consts = (gao_pad, wo_pad, g_ffn[0].reshape(1, d), wr_hi, wr_lo, router_bias[0].reshape(N_EXPERTS, 1),
                   jnp.concatenate([ws_gate[0], ws_up[0]], axis=1).astype(BF16), ws_down[0].astype(BF16))

    def mixer(x, modv, attn_fn, h0r, h0i):
        sh1, sc1, gt1, sh2, sc2, gt2 = modv
        q, k, v, kf, vf, lf, fb, u = _inproj(x, sh1, sc1, g_mix2, inw)
        attn = attn_fn(q, k, v, kf, vf, lf, fb)
        ssm, hr, hi = _ssm(u, h0r, h0i, ssm_consts(x.shape[1]))
        return attn, ssm, (kf, vf, lf, hr, hi), (gt1, sh2, sc2, gt2)

    def attn_prompt(q, k, v, kf, vf, lf, fb):
        return _attn(q, k, v, fb[:, :, 0, :N_HEADS].reshape(-1))

    zeros_p = jnp.zeros((bp, 1, D_STATE), F32)
    attn_p, ssm_p, new_p, m_p = mixer(x_prompt, mod_p, attn_prompt, zeros_p, zeros_p)

    lk = -(-(past + ls) // LANES) * LANES
    padk = lk - past - ls

    def attn_sample(q, k, v, kf, vf, lf, fb):
        k_all = jnp.concatenate([cache_k[0].reshape(bs, past, D_ATTN), kf,
                                 jnp.zeros((bs, padk, D_ATTN), F32)], axis=1)
        v_all = jnp.concatenate([cache_v[0].reshape(bs, past, D_ATTN), vf,
                                 jnp.zeros((bs, padk, D_ATTN), F32)], axis=1)
        lf_all = jnp.concatenate([cache_logf[0], lf, jnp.zeros((bs, padk, N_HEADS), F32)], axis=1)
        lf_cols = jnp.pad(lf_all, ((0, 0), (0, 0), (0, LANES - N_HEADS)))
        lf_rows = jnp.swapaxes(lf_all, 1, 2)
        return _attn_dec(q, k_all, v_all, lf_cols, lf_rows, past, ls)

    attn_s, ssm_s, new_s, m_s = mixer(x_sample, mod_s, attn_sample,
                                      state_ssm_re[0].reshape(bs, 1, D_STATE), state_ssm_im[0].reshape(bs, 1, D_STATE))

    cnt0 = jnp.zeros((N_EXPERTS, 1), F32)
    xm_p, hp_p, idx_p, pos_p, w_p, cnt1 = _post(x_prompt, attn_p, ssm_p, m_p, cnt0, post_consts, True)
    xm_s, hp_s, idx_s, pos_s, w_s, cnt2 = _post(x_sample, attn_s, ssm_s, m_s, cnt1, post_consts, False)

    counts = cnt2[:, 0].astype(I32)
    padded = (counts + EXPERT_ROWS - 1) // EXPERT_ROWS * EXPERT_ROWS
    pend = jnp.cumsum(padded)
    pstart = pend - padded
    tp, ts = bp * lp, bs * ls
    n_blocks = -(-((tp + ts) * TOP_K) // EXPERT_ROWS) + N_EXPERTS
    rows = n_blocks * EXPERT_ROWS
    dest_p = _slots(pstart, idx_p, pos_p)
    dest_s = _slots(pstart, idx_s, pos_s)
    nb_used = (pend[-1] // EXPERT_ROWS).astype(I32).reshape(1)
    blk_row = jnp.minimum(jnp.arange(n_blocks, dtype=I32), nb_used[0] - 1) * EXPERT_ROWS
    blk_e = jnp.minimum(jnp.sum((pend[None, :] <= blk_row[:, None]).astype(I32), axis=1), N_EXPERTS - 1)

    half = d // 2
    xs = jnp.zeros((rows, half), U32)
    flat = lambda dd: jnp.swapaxes(dd, 0, 1).reshape(TOP_K, -1)
    xs = _dispatch(flat(dest_p), hp_p.reshape(tp, half), xs)
    xs = _dispatch(flat(dest_s), hp_s.reshape(ts, half), xs)
    ys = _experts(blk_e, nb_used, xs, w_gate[0], w_up[0], w_down[0])

    gfin = g_final.reshape(1, d)
    y_p = _combine(dest_p, ys, w_p, xm_p, m_p[3], gfin)
    y_s = _combine(dest_s, ys, w_s, xm_s, m_s[3], gfin)

    def pack(new, b, l):
        kf, vf, lf, hr, hi = new
        return (kf.reshape(1, b, l, N_HEADS, HEAD_DIM), vf.reshape(1, b, l, N_HEADS, HEAD_DIM),
                lf.reshape(1, b, l, N_HEADS),
                hr.reshape(1, b, N_SSM_GROUPS, SSM_STATE), hi.reshape(1, b, N_SSM_GROUPS, SSM_STATE))

    return (y_p, y_s) + pack(new_p, bp, lp) + pack(new_s, bs, ls)
```

```python
import functools
import math

import jax
import jax.numpy as jnp
from jax import lax
from jax.experimental import pallas as pl
from jax.experimental.pallas import tpu as pltpu

F32 = jnp.float32
BF16 = jnp.bfloat16
U32 = jnp.uint32
I32 = jnp.int32

N_HEADS = 8
HEAD_DIM = 64
HEAD_PAD = 128
D_ATTN = N_HEADS * HEAD_DIM
SSM_GROUP = 16
N_SSM_GROUPS = 32
SSM_STATE = 64
D_SSM = SSM_GROUP * N_SSM_GROUPS
D_STATE = N_SSM_GROUPS * SSM_STATE
N_EXPERTS = 256
TOP_K = 8
N_EXP_GROUPS = 8
TOPK_GROUPS = 4
ROUTED_SCALE = 2.5
EPS = 1e-6
LANES = 128

SEQ_BLOCK = 512
SSM_CHUNK = 128
EXPERT_ROWS = 512
ATTN_HEADS_PER_STEP = 2
LOG2E = math.log2(math.e)
COMBINE_ROWS = 256
DISPATCH_ROWS = 512
VMEM_LIMIT = 56 * 1024 * 1024

_NT = (((1,), (1,)), ((), ()))


def _cparams(*sem):
    return pltpu.CompilerParams(dimension_semantics=sem, vmem_limit_bytes=VMEM_LIMIT)


def _split3(x):
    hi = x.astype(BF16)
    r = x - hi.astype(F32)
    mid = r.astype(BF16)
    lo = (r - mid.astype(F32)).astype(BF16)
    return hi, mid, lo


def _dot(a, b):
    return jnp.dot(a, b, preferred_element_type=F32)


def _dot_nt(a, b):
    return lax.dot_general(a, b, _NT, preferred_element_type=F32)


def _rms(x, g):
    return x * lax.rsqrt(jnp.mean(x * x, axis=-1, keepdims=True) + EPS) * g


def _sigmoid(x):
    return 1.0 / (1.0 + jnp.exp(-x))


def _pack_bf16_pair(a, b):
    ab = lax.bitcast_convert_type(a.astype(BF16).astype(F32), U32)
    bb = lax.bitcast_convert_type(b.astype(BF16).astype(F32), U32)
    return (ab >> 16) | (bb & jnp.uint32(0xFFFF0000))


def _unpack_bf16_pair(w):
    a = lax.bitcast_convert_type(w << 16, F32)
    b = lax.bitcast_convert_type(w & jnp.uint32(0xFFFF0000), F32)
    return a, b


def _prep_kernel(c_ref, w_ref, b_ref, o_ref):
    c = c_ref[...]
    a = c * _sigmoid(c)
    a_hi = a.astype(BF16)
    a_lo = (a - a_hi.astype(F32)).astype(BF16)
    w = w_ref[...]
    w_hi = w.astype(BF16)
    w_lo = (w - w_hi.astype(F32)).astype(BF16)
    o_ref[...] = _dot(a_hi, w_hi) + _dot(a_lo, w_hi) + _dot(a_hi, w_lo) + b_ref[...]


def _prep(c_all, w_ada, b_ada):
    n, d = c_all.shape
    nout = w_ada.shape[1]
    tn = 1024
    return pl.pallas_call(
        _prep_kernel,
        grid=(nout // tn,),
        in_specs=[pl.BlockSpec((n, d), lambda j: (0, 0)),
                  pl.BlockSpec((d, tn), lambda j: (0, j)),
                  pl.BlockSpec((1, tn), lambda j: (0, j))],
        out_specs=pl.BlockSpec((n, tn), lambda j: (0, j)),
        out_shape=jax.ShapeDtypeStruct((n, nout), F32),
        compiler_params=_cparams("arbitrary"),
        name="prep",
    )(c_all, w_ada, b_ada.reshape(1, nout))


def _disc_kernel(lr_ref, li_ref, ldt_ref, br_ref, bi_ref, ar_ref, ai_ref, bbr_ref, bbi_ref):
    lr = lr_ref[...]
    li = li_ref[...]
    dt = jnp.exp(ldt_ref[...])
    er = jnp.exp(lr * dt)
    ang = li * dt
    ar = er * jnp.cos(ang)
    ai = er * jnp.sin(ang)
    ar_ref[...] = ar
    ai_ref[...] = ai
    den = lr * lr + li * li
    nr = ((ar - 1.0) * lr + ai * li) / den
    ni = (ai * lr - (ar - 1.0) * li) / den
    nr3 = nr[:, None, :]
    ni3 = ni[:, None, :]
    br = br_ref[...]
    bi = bi_ref[...]
    bbr_ref[...] = nr3 * br - ni3 * bi
    bbi_ref[...] = nr3 * bi + ni3 * br


def _disc(lam_re, lam_im, log_dt, bt_re, bt_im):
    g, p = lam_re.shape
    c = bt_re.shape[1]
    return pl.pallas_call(
        _disc_kernel,
        out_shape=(jax.ShapeDtypeStruct((g, p), F32), jax.ShapeDtypeStruct((g, p), F32),
                   jax.ShapeDtypeStruct((g, c, p), F32), jax.ShapeDtypeStruct((g, c, p), F32)),
        name="disc",
    )(lam_re, lam_im, log_dt.reshape(g, 1), bt_re, bt_im)


SCAN_ROWS = 8
SCAN_LEVELS = 3


def _pow_kernel(ar_ref, ai_ref, tr_ref, ti_ref):
    n = ar_ref.shape[1]
    t = lax.broadcasted_iota(I32, (SCAN_ROWS, 1), 0)
    pr = jnp.ones((SCAN_ROWS, n), F32)
    pi = jnp.zeros((SCAN_ROWS, n), F32)
    sr = ar_ref[...]
    si = ai_ref[...]
    for k in range(SCAN_LEVELS + 1):
        bit = (((t + 1) >> k) & 1) == 1
        fr = jnp.where(bit, sr, 1.0)
        fi = jnp.where(bit, si, 0.0)
        pr, pi = pr * fr - pi * fi, pr * fi + pi * fr
        if k < SCAN_LEVELS:
            keep = t >= (1 << k)
            tr_ref[k] = jnp.where(keep, sr, 0.0)
            ti_ref[k] = jnp.where(keep, si, 0.0)
        sr, si = sr * sr - si * si, 2.0 * sr * si
    tr_ref[SCAN_LEVELS] = pr
    ti_ref[SCAN_LEVELS] = pi


def _pow_table(a_re, a_im):
    n = a_re.shape[1]
    shape = jax.ShapeDtypeStruct((SCAN_LEVELS + 1, SCAN_ROWS, n), F32)
    return pl.pallas_call(_pow_kernel, out_shape=(shape, shape), name="pow")(a_re, a_im)


def _inproj_kernel(x_ref, sh_ref, sc_ref, g_ref, wm_ref, wf_ref, bf_ref, selq_ref, selk_ref,
                   cq_ref, ck_ref, cv_ref,
                   q_ref, k_ref, v_ref, kf_ref, vf_ref, lf_ref, fb_ref, u_ref, carry_ref, *, tl):
    @pl.when(pl.program_id(1) == 0)
    def _():
        carry_ref[...] = jnp.zeros_like(carry_ref)

    x = x_ref[0]
    h = _rms(x, g_ref[...]) * (1.0 + sc_ref[0]) + sh_ref[0]
    hb = h.astype(BF16)
    main = _dot(hb, wm_ref[...])
    hw = N_HEADS * HEAD_PAD
    qp = main[:, 0:hw] * (HEAD_DIM ** -0.5 * LOG2E)
    kp = main[:, hw:2 * hw]
    vp = main[:, 2 * hw:3 * hw]
    o = 3 * hw
    kf_ref[0] = main[:, o:o + D_ATTN]
    vf_ref[0] = main[:, o + D_ATTN:o + 2 * D_ATTN]
    u_ref[0] = main[:, o + 2 * D_ATTN:o + 2 * D_ATTN + D_SSM]

    fl = _dot(hb, wf_ref[...]) + bf_ref[...]
    lf = jnp.minimum(fl, 0.0) - jnp.log1p(jnp.exp(-jnp.abs(fl)))
    lf_ref[0] = lf[:, 0:N_HEADS]

    row = lax.broadcasted_iota(I32, (tl, tl), 0)
    col = lax.broadcasted_iota(I32, (tl, tl), 1)
    tri = jnp.where(row >= col, 1.0, 0.0).astype(BF16)
    hi, mid, lo = _split3(lf)
    frel = _dot(tri, hi) + _dot(tri, mid) + _dot(tri, lo)
    fb_ref[0, 0] = carry_ref[...] * LOG2E
    carry_ref[...] = carry_ref[...] + frel[tl - 1:tl, :]

    fcat = jnp.concatenate(_split3(frel * LOG2E), axis=-1)
    q_aug = (qp + _dot(fcat, selq_ref[...]) + cq_ref[...]).astype(BF16)
    k_aug = (kp + _dot(fcat, selk_ref[...]) + ck_ref[...]).astype(BF16)
    v_aug = (vp + cv_ref[...]).astype(BF16)
    for hd in range(N_HEADS):
        sl = slice(hd * HEAD_PAD, (hd + 1) * HEAD_PAD)
        q_ref[0, hd] = q_aug[:, sl]
        k_ref[0, hd] = k_aug[:, sl]
        v_ref[0, hd] = v_aug[:, sl]


def _inproj(x, sh, sc, g, wts):
    b, l, d = x.shape
    tl = min(l, SEQ_BLOCK)
    nl = l // tl
    wm, wf, bf, selq, selk, cq, ck, cv = wts
    hm = jax.ShapeDtypeStruct((b, N_HEADS, l, HEAD_PAD), BF16)
    hm_spec = pl.BlockSpec((1, N_HEADS, tl, HEAD_PAD), lambda i, j: (i, 0, j, 0))
    tok = lambda w: pl.BlockSpec((1, tl, w), lambda i, j: (i, j, 0))
    full = lambda a: pl.BlockSpec(a.shape, lambda i, j: (0,) * a.ndim)
    row = pl.BlockSpec((1, 1, d), lambda i, j: (i, 0, 0))
    return pl.pallas_call(
        functools.partial(_inproj_kernel, tl=tl),
        grid=(b, nl),
        in_specs=[tok(d), row, row, full(g), full(wm), full(wf), full(bf), full(selq), full(selk),
                  full(cq), full(ck), full(cv)],
        out_specs=[hm_spec, hm_spec, hm_spec, tok(D_ATTN), tok(D_ATTN), tok(N_HEADS),
                   pl.BlockSpec((1, 1, 1, LANES), lambda i, j: (i, j, 0, 0)), tok(D_SSM)],
        out_shape=[hm, hm, hm,
                   jax.ShapeDtypeStruct((b, l, D_ATTN), F32), jax.ShapeDtypeStruct((b, l, D_ATTN), F32),
                   jax.ShapeDtypeStruct((b, l, N_HEADS), F32),
                   jax.ShapeDtypeStruct((b, nl, 1, LANES), F32),
                   jax.ShapeDtypeStruct((b, l, D_SSM), F32)],
        scratch_shapes=[pltpu.VMEM((1, LANES), F32)],
        compiler_params=_cparams("arbitrary", "arbitrary"),
        name=f"inproj{l}",
    )(x, sh, sc, g, wm, wf, bf, selq, selk, cq, ck, cv)


def _inproj_weights(w_in, b_f):
    d = w_in.shape[0]
    wq, wk, wv = (w_in[:, i * D_ATTN:(i + 1) * D_ATTN] for i in range(3))
    wfl = w_in[:, 3 * D_ATTN:3 * D_ATTN + N_HEADS]
    wu = w_in[:, 3 * D_ATTN + N_HEADS:]

    def pad_heads(w):
        w = w.reshape(d, N_HEADS, HEAD_DIM)
        w = jnp.pad(w, ((0, 0), (0, 0), (0, HEAD_PAD - HEAD_DIM)))
        return w.reshape(d, N_HEADS * HEAD_PAD)

    wm = jnp.concatenate([pad_heads(wq), pad_heads(wk), pad_heads(wv), wk, wv, wu], axis=1).astype(BF16)
    wf = jnp.pad(wfl, ((0, 0), (0, LANES - N_HEADS))).astype(BF16)
    bf = jnp.pad(b_f, (0, LANES - N_HEADS)).reshape(1, LANES).astype(F32)

    hw = N_HEADS * HEAD_PAD
    hd = jnp.arange(N_HEADS)
    selq = jnp.zeros((3 * LANES, hw), F32)
    selk = jnp.zeros((3 * LANES, hw), F32)
    cq = jnp.zeros((1, hw), F32)
    ck = jnp.zeros((1, hw), F32)
    cv = jnp.zeros((1, hw), F32)
    for part in range(3):
        selq = selq.at[part * LANES + hd, hd * HEAD_PAD + HEAD_DIM + part].set(1.0)
        selk = selk.at[part * LANES + hd, hd * HEAD_PAD + HEAD_DIM + 3 + part].set(-1.0)
        cq = cq.at[0, hd * HEAD_PAD + HEAD_DIM + 3 + part].set(1.0)
        ck = ck.at[0, hd * HEAD_PAD + HEAD_DIM + part].set(1.0)
    cv = cv.at[0, hd * HEAD_PAD + HEAD_DIM].set(1.0)
    return wm, wf, bf, selq.astype(BF16), selk.astype(BF16), cq, ck, cv


def _attn_kernel(fb_ref, q_ref, k_ref, v_ref, o_ref, *, r, nblk, hp):
    b = pl.program_id(0)
    g = pl.program_id(1)
    i = pl.program_id(2)
    fbase = (b * nblk) * N_HEADS + g * hp

    def step(j, carry, masked):
        start = pl.multiple_of(j * r, r)
        out = []
        for hh in range(hp):
            m, acc = carry[hh]
            s = _dot_nt(q_ref[0, hh], k_ref[0, hh, pl.ds(start, r), :])
            if masked:
                row = lax.broadcasted_iota(I32, (r, r), 0)
                col = lax.broadcasted_iota(I32, (r, r), 1)
                s = jnp.where(row >= col, s, -jnp.inf)
            dlt = fb_ref[fbase + hh + i * N_HEADS] - fb_ref[fbase + hh + j * N_HEADS]
            mnew = jnp.maximum(m, jnp.max(s, axis=-1, keepdims=True) + dlt)
            p = jnp.exp2(s - (mnew - dlt))
            acc = jnp.exp2(m - mnew) * acc + _dot(p.astype(BF16), v_ref[0, hh, pl.ds(start, r), :])
            out.append((mnew, acc))
        return tuple(out)

    init = tuple((jnp.full((r, 1), -jnp.inf, F32), jnp.zeros((r, HEAD_PAD), F32)) for _ in range(hp))
    carry = lax.fori_loop(0, i, lambda j, c: step(j, c, False), init)
    carry = step(i, carry, True)
    lane = lax.broadcasted_iota(I32, (r, HEAD_PAD), 1)
    for hh in range(hp):
        acc = carry[hh][1]
        out = acc / acc[:, HEAD_DIM:HEAD_DIM + 1]
        o_ref[0, hh] = jnp.where(lane < HEAD_DIM, out, 0.0).astype(BF16)


def _attn(q, k, v, fb):
    b, h, l, _ = q.shape
    r = min(l, SEQ_BLOCK)
    nblk = l // r
    hp = ATTN_HEADS_PER_STEP
    return pl.pallas_call(
        functools.partial(_attn_kernel, r=r, nblk=nblk, hp=hp),
        grid=(b, h // hp, nblk),
        in_specs=[pl.BlockSpec(memory_space=pltpu.SMEM),
                  pl.BlockSpec((1, hp, r, HEAD_PAD), lambda bi, gi, i: (bi, gi, i, 0)),
                  pl.BlockSpec((1, hp, l, HEAD_PAD), lambda bi, gi, i: (bi, gi, 0, 0)),
                  pl.BlockSpec((1, hp, l, HEAD_PAD), lambda bi, gi, i: (bi, gi, 0, 0))],
        out_specs=pl.BlockSpec((1, hp, r, HEAD_PAD), lambda bi, gi, i: (bi, gi, i, 0)),
        out_shape=jax.ShapeDtypeStruct((b, h, l, HEAD_PAD), BF16),
        compiler_params=_cparams("arbitrary", "arbitrary", "arbitrary"),
        name="attn",
    )(fb, q, k, v)


def _attn_dec_kernel(q_ref, k_ref, v_ref, lfc_ref, lfr_ref, o_ref, *, lq, past, lk):
    row = lax.broadcasted_iota(I32, (lk, lk), 0)
    col = lax.broadcasted_iota(I32, (lk, lk), 1)
    tri = jnp.where(row >= col, 1.0, 0.0).astype(BF16)
    upper = jnp.where(row <= col, 1.0, 0.0).astype(BF16)
    c_hi, c_mid, c_lo = _split3(lfc_ref[0])
    fcol = _dot(tri, c_hi) + _dot(tri, c_mid) + _dot(tri, c_lo)
    r_hi, r_mid, r_lo = _split3(lfr_ref[0])
    frow = _dot(r_hi, upper) + _dot(r_mid, upper) + _dot(r_lo, upper)
    fq = fcol[past:past + lq, :]
    qpos = past + lax.broadcasted_iota(I32, (lq, lk), 0)
    kpos = lax.broadcasted_iota(I32, (lq, lk), 1)
    ok = kpos <= qpos
    qlane = lax.broadcasted_iota(I32, (lq, HEAD_PAD), 1)
    klane = lax.broadcasted_iota(I32, (lk, HEAD_PAD), 1)
    for hd in range(N_HEADS):
        odd = hd % 2 == 1
        pair = slice((hd // 2) * HEAD_PAD, (hd // 2 + 1) * HEAD_PAD)
        mine = (klane >= HEAD_DIM) if odd else (klane < HEAD_DIM)
        kh = jnp.where(mine, k_ref[0, :, pair], 0.0).astype(BF16)
        vh = jnp.where(mine, v_ref[0, :, pair], 0.0).astype(BF16)
        qh = jnp.where(qlane < HEAD_DIM, q_ref[0, hd].astype(F32), 0.0)
        if odd:
            qh = pltpu.roll(qh, HEAD_DIM, axis=1)
        s = _dot_nt(qh.astype(BF16), kh) + LOG2E * (fq[:, hd:hd + 1] - frow[hd:hd + 1, :])
        s = jnp.where(ok, s, -jnp.inf)
        m = jnp.max(s, axis=-1, keepdims=True)
        pb = jnp.exp2(s - m).astype(BF16)
        den = jnp.sum(pb.astype(F32), axis=-1, keepdims=True)
        out = _dot(pb, vh) / den
        if odd:
            out = pltpu.roll(out, HEAD_DIM, axis=1)
        o_ref[0, hd] = out.astype(BF16)


def _attn_dec(q, k_all, v_all, lf_cols, lf_rows, past, lq):
    b = q.shape[0]
    lk = k_all.shape[1]
    return pl.pallas_call(
        functools.partial(_attn_dec_kernel, lq=lq, past=past, lk=lk),
        grid=(b,),
        in_specs=[pl.BlockSpec((1, N_HEADS, lq, HEAD_PAD), lambda i: (i, 0, 0, 0)),
                  pl.BlockSpec((1, lk, D_ATTN), lambda i: (i, 0, 0)),
                  pl.BlockSpec((1, lk, D_ATTN), lambda i: (i, 0, 0)),
                  pl.BlockSpec((1, lk, LANES), lambda i: (i, 0, 0)),
                  pl.BlockSpec((1, N_HEADS, lk), lambda i: (i, 0, 0))],
        out_specs=pl.BlockSpec((1, N_HEADS, lq, HEAD_PAD), lambda i: (i, 0, 0, 0)),
        out_shape=jax.ShapeDtypeStruct((b, N_HEADS, lq, HEAD_PAD), BF16),
        compiler_params=_cparams("arbitrary"),
        name="attn_dec",
    )(q, k_all, v_all, lf_cols, lf_rows)


def _ssm_kernel(u_ref, h0r_ref, h0i_ref, br_ref, bi_ref, cc_ref, pr_ref, pi_ref, d_ref, wg_ref, bg_ref,
                gso_ref, y_ref, hr_out, hi_out, cr_ref, ci_ref, *, tc):
    @pl.when(pl.program_id(1) == 0)
    def _():
        cr_ref[...] = h0r_ref[0]
        ci_ref[...] = h0i_ref[0]

    u = u_ref[0]
    ub = u.astype(BF16)
    hr = _dot(ub, br_ref[...])
    hi = _dot(ub, bi_ref[...])
    nt = tc // SCAN_ROWS
    tiled = (nt, SCAN_ROWS, D_STATE)
    for k in range(SCAN_LEVELS):
        ar = pr_ref[k]
        ai = pi_ref[k]
        sr = pltpu.roll(hr, 1 << k, axis=0).reshape(tiled)
        si = pltpu.roll(hi, 1 << k, axis=0).reshape(tiled)
        hr = (hr.reshape(tiled) + ar * sr - ai * si).reshape(tc, D_STATE)
        hi = (hi.reshape(tiled) + ar * si + ai * sr).reshape(tc, D_STATE)
    pr = pr_ref[SCAN_LEVELS]
    pi = pi_ref[SCAN_LEVELS]
    cr = cr_ref[...]
    ci = ci_ref[...]
    rows_r, rows_i = [], []
    for j in range(nt):
        sl = slice(j * SCAN_ROWS, (j + 1) * SCAN_ROWS)
        xr = hr[sl] + pr * cr - pi * ci
        xi = hi[sl] + pr * ci + pi * cr
        cr = xr[SCAN_ROWS - 1:SCAN_ROWS, :]
        ci = xi[SCAN_ROWS - 1:SCAN_ROWS, :]
        rows_r.append(xr)
        rows_i.append(xi)
    hr = jnp.concatenate(rows_r, axis=0)
    hi = jnp.concatenate(rows_i, axis=0)
    cr_ref[...] = cr
    ci_ref[...] = ci
    hr_out[0] = cr
    hi_out[0] = ci

    hcat = jnp.concatenate([hr, hi], axis=-1).astype(BF16)
    y = _dot(hcat, cc_ref[...]) + d_ref[...] * u
    gl = 0.5 * y * (1.0 + jnp.tanh(math.sqrt(2.0 / math.pi) * (y + 0.044715 * (y * y * y))))
    z = _dot(gl.astype(BF16), wg_ref[...]) + bg_ref[...]
    out = y * _sigmoid(z)
    y_ref[0] = _rms(out, gso_ref[...]).astype(BF16)


def _ssm(u, h0r, h0i, consts):
    b, l, _ = u.shape
    br, bi, cc, pr, pi, dsk, wg, bg, gso = consts
    tc = min(l, SSM_CHUNK)
    full = lambda a: pl.BlockSpec(a.shape, lambda i, j: (0,) * a.ndim)
    st = pl.BlockSpec((1, 1, D_STATE), lambda i, j: (i, 0, 0))
    return pl.pallas_call(
        functools.partial(_ssm_kernel, tc=tc),
        grid=(b, l // tc),
        in_specs=[pl.BlockSpec((1, tc, D_SSM), lambda i, j: (i, j, 0)), st, st,
                  full(br), full(bi), full(cc), full(pr), full(pi), full(dsk), full(wg), full(bg), full(gso)],
        out_specs=[pl.BlockSpec((1, tc, D_SSM), lambda i, j: (i, j, 0)), st, st],
        out_shape=[jax.ShapeDtypeStruct((b, l, D_SSM), BF16),
                   jax.ShapeDtypeStruct((b, 1, D_STATE), F32), jax.ShapeDtypeStruct((b, 1, D_STATE), F32)],
        scratch_shapes=[pltpu.VMEM((1, D_STATE), F32), pltpu.VMEM((1, D_STATE), F32)],
        compiler_params=_cparams("arbitrary", "arbitrary"),
        name=f"ssm{l}",
    )(u, h0r, h0i, br, bi, cc, pr, pi, dsk, wg, bg, gso)


def _post_kernel(x_ref, a_ref, s_ref, gt1_ref, sh2_ref, sc2_ref, gt2_ref, cnt0_ref, gao_ref, wo_ref, gffn_ref,
                 wrh_ref, wrl_ref, rb_ref, wsgu_ref, wsd_ref,
                 xm_ref, hp_ref, idx_ref, pos_ref, w_ref, cnt_ref, carry_ref, *, tl, first):
    step = pl.program_id(0) * pl.num_programs(1) + pl.program_id(1)

    @pl.when(step == 0)
    def _():
        carry_ref[...] = cnt0_ref[...] if not first else jnp.zeros_like(carry_ref)

    x = x_ref[0]
    attn = jnp.concatenate([a_ref[0, hd] for hd in range(N_HEADS)], axis=-1).astype(F32)
    ms = jnp.sum(attn * attn, axis=-1, keepdims=True) * (1.0 / D_ATTN)
    attn_n = (attn * lax.rsqrt(ms + EPS) * gao_ref[...]).astype(BF16)
    merged = jnp.concatenate([attn_n, s_ref[0]], axis=-1)
    x1 = x + gt1_ref[0] * _dot(merged, wo_ref[...])
    h2 = _rms(x1, gffn_ref[...]) * (1.0 + sc2_ref[0]) + sh2_ref[0]
    d = h2.shape[-1]
    hp_ref[0] = _pack_bf16_pair(h2[:, :d // 2], h2[:, d // 2:])
    h_hi = h2.astype(BF16)
    h_lo = (h2 - h_hi.astype(F32)).astype(BF16)

    gu = _dot(h_hi, wsgu_ref[...])
    ds = gu.shape[-1] // 2
    g = gu[:, :ds]
    act = (g * _sigmoid(g) * gu[:, ds:]).astype(BF16)
    xm_ref[0] = x1 + gt2_ref[0] * _dot(act, wsd_ref[...])

    wrh = wrh_ref[...]
    logits = _dot_nt(wrh, h_hi) + _dot_nt(wrh, h_lo) + _dot_nt(wrl_ref[...], h_hi)
    s = _sigmoid(logits)
    sc = s + rb_ref[...]
    ge = N_EXPERTS // N_EXP_GROUPS
    neg = -jnp.inf
    gi = lax.broadcasted_iota(I32, (ge, tl), 0)
    gsc = []
    for gidx in range(N_EXP_GROUPS):
        blk = sc[gidx * ge:(gidx + 1) * ge, :]
        m1 = jnp.max(blk, axis=0, keepdims=True)
        f1 = jnp.min(jnp.where(blk == m1, gi, ge), axis=0, keepdims=True)
        m2 = jnp.max(jnp.where(gi == f1, neg, blk), axis=0, keepdims=True)
        gsc.append(m1 + m2)
    gwork = jnp.concatenate(gsc, axis=0)
    ni = lax.broadcasted_iota(I32, (N_EXP_GROUPS, tl), 0)
    gsel = jnp.zeros((N_EXP_GROUPS, tl), F32)
    for _ in range(TOPK_GROUPS):
        mx = jnp.max(gwork, axis=0, keepdims=True)
        fi = jnp.min(jnp.where(gwork == mx, ni, N_EXP_GROUPS), axis=0, keepdims=True)
        hit = ni == fi
        gsel = jnp.where(hit, 1.0, gsel)
        gwork = jnp.where(hit, neg, gwork)
    work = jnp.concatenate(
        [jnp.where(gsel[gidx:gidx + 1, :] > 0.0, sc[gidx * ge:(gidx + 1) * ge, :], neg)
         for gidx in range(N_EXP_GROUPS)], axis=0)
    ei = lax.broadcasted_iota(I32, (N_EXPERTS, tl), 0)
    chosen = jnp.zeros((N_EXPERTS, tl), F32)
    idx_rows, w_rows = [], []
    for _ in range(TOP_K):
        mx = jnp.max(work, axis=0, keepdims=True)
        fi = jnp.min(jnp.where(work == mx, ei, N_EXPERTS), axis=0, keepdims=True)
        hit = ei == fi
        w_rows.append(jnp.sum(jnp.where(hit, s, 0.0), axis=0, keepdims=True))
        idx_rows.append(fi)
        chosen = jnp.where(hit, 1.0, chosen)
        work = jnp.where(hit, neg, work)
    wt = jnp.concatenate(w_rows, axis=0)
    wt = wt / jnp.sum(wt, axis=0, keepdims=True) * ROUTED_SCALE
    idx_ref[0] = jnp.concatenate(idx_rows, axis=0)

    trow = lax.broadcasted_iota(I32, (tl, tl), 0)
    tcol = lax.broadcasted_iota(I32, (tl, tl), 1)
    before = jnp.where(trow < tcol, 1.0, 0.0).astype(BF16)
    rank = _dot(chosen.astype(BF16), before) + carry_ref[...]
    pos_rows = [jnp.sum(jnp.where(ei == idx_rows[k], rank, 0.0), axis=0, keepdims=True) for k in range(TOP_K)]
    pos_ref[0] = jnp.concatenate(pos_rows, axis=0).astype(I32)
    carry_ref[...] = carry_ref[...] + jnp.sum(chosen, axis=1, keepdims=True)
    cnt_ref[...] = carry_ref[...]

    eye = jnp.where(trow == tcol, 1.0, 0.0).astype(BF16)
    t_hi, t_mid, t_lo = _split3(wt)
    w_ref[0] = _dot_nt(eye, t_hi) + _dot_nt(eye, t_mid) + _dot_nt(eye, t_lo)


def _post(x, attn, ssm, mods, cnt0, consts, first):
    b, l, d = x.shape
    tl = min(l, SEQ_BLOCK)
    gt1, sh2, sc2, gt2 = mods
    gao, wo, gffn, wrh, wrl, rb, wsgu, wsd = consts
    tok = lambda w: pl.BlockSpec((1, tl, w), lambda i, j: (i, j, 0))
    full = lambda a: pl.BlockSpec(a.shape, lambda i, j: (0,) * a.ndim)
    row = pl.BlockSpec((1, 1, d), lambda i, j: (i, 0, 0))
    tk = pl.BlockSpec((1, TOP_K, tl), lambda i, j: (i, 0, j))
    return pl.pallas_call(
        functools.partial(_post_kernel, tl=tl, first=first),
        grid=(b, l // tl),
        in_specs=[tok(d), pl.BlockSpec((1, N_HEADS, tl, HEAD_PAD), lambda i, j: (i, 0, j, 0)), tok(D_SSM),
                  row, row, row, row, full(cnt0), full(gao), full(wo), full(gffn), full(wrh), full(wrl),
                  full(rb), full(wsgu), full(wsd)],
        out_specs=[tok(d), tok(d // 2), tk, tk, tok(TOP_K), full(cnt0)],
        out_shape=[jax.ShapeDtypeStruct((b, l, d), F32), jax.ShapeDtypeStruct((b, l, d // 2), U32),
                   jax.ShapeDtypeStruct((b, TOP_K, l), I32), jax.ShapeDtypeStruct((b, TOP_K, l), I32),
                   jax.ShapeDtypeStruct((b, l, TOP_K), F32), jax.ShapeDtypeStruct(cnt0.shape, F32)],
        scratch_shapes=[pltpu.VMEM(cnt0.shape, F32)],
        compiler_params=_cparams("arbitrary", "arbitrary"),
        name=f"post{l}",
    )(x, attn, ssm, gt1, sh2, sc2, gt2, cnt0, gao, wo, gffn, wrh, wrl, rb, wsgu, wsd)


def _slots_kernel(pstart_ref, idx_ref, pos_ref, dest_ref):
    idx = idx_ref[0]

    def body(e, base):
        return jnp.where(idx == e, pstart_ref[e], base)

    dest_ref[0] = lax.fori_loop(0, N_EXPERTS, body, jnp.zeros_like(idx)) + pos_ref[0]


def _slots(pstart, idx, pos):
    b, k, l = idx.shape
    tl = min(l, 2048)
    spec = pl.BlockSpec((1, k, tl), lambda i, j: (i, 0, j))
    return pl.pallas_call(
        _slots_kernel,
        grid=(b, l // tl),
        in_specs=[pl.BlockSpec(memory_space=pltpu.SMEM), spec, spec],
        out_specs=spec,
        out_shape=jax.ShapeDtypeStruct(idx.shape, I32),
        compiler_params=_cparams("arbitrary", "arbitrary"),
        name=f"slots{l}",
    )(pstart, idx, pos)


def _dispatch_kernel(dest_ref, h_ref, xin_ref, xs_ref, sem, *, td):
    del xin_ref

    def issue(t, c):
        for k in range(TOP_K):
            pltpu.make_async_copy(h_ref.at[pl.ds(t, 1)], xs_ref.at[pl.ds(dest_ref[k, t], 1)], sem).start()
        return c

    lax.fori_loop(0, td, issue, 0)

    def drain(t, c):
        for k in range(TOP_K):
            pltpu.make_async_copy(h_ref.at[pl.ds(0, 1)], xs_ref.at[pl.ds(0, 1)], sem).wait()
        return c

    lax.fori_loop(0, td, drain, 0)


def _dispatch(dest, hp, xs0):
    t, w = hp.shape
    td = min(t, DISPATCH_ROWS)
    return pl.pallas_call(
        functools.partial(_dispatch_kernel, td=td),
        grid=(t // td,),
        in_specs=[pl.BlockSpec((TOP_K, td), lambda i: (0, i), memory_space=pltpu.SMEM),
                  pl.BlockSpec((td, w), lambda i: (i, 0)), pl.BlockSpec(memory_space=pl.ANY)],
        out_specs=pl.BlockSpec(memory_space=pl.ANY),
        out_shape=jax.ShapeDtypeStruct(xs0.shape, xs0.dtype),
        scratch_shapes=[pltpu.SemaphoreType.DMA(())],
        input_output_aliases={2: 0},
        compiler_params=_cparams("arbitrary"),
        name="dispatch",
    )(dest, hp, xs0)


def _experts_kernel(be_ref, nb_ref, x_ref, wg_ref, wu_ref, wd_ref, y_ref, wgu_s, wd_s):
    i = pl.program_id(0)
    e = be_ref[i]
    prev = be_ref[jnp.maximum(i - 1, 0)]
    dm = wg_ref.shape[1]
    de = wg_ref.shape[2]

    @pl.when((i == 0) | (e != prev))
    def _():
        wgu_s[:, 0:de] = wg_ref[0].astype(BF16)
        wgu_s[:, de:2 * de] = wu_ref[0].astype(BF16)
        wd_s[...] = wd_ref[0].astype(BF16)

    @pl.when(i < nb_ref[0])
    def _():
        xa, xb = _unpack_bf16_pair(x_ref[...])
        gu = _dot(xa.astype(BF16), wgu_s[0:dm // 2, :]) + _dot(xb.astype(BF16), wgu_s[dm // 2:dm, :])
        g = gu[:, :de]
        act = (g * _sigmoid(g) * gu[:, de:]).astype(BF16)
        y = _dot(act, wd_s[...])
        y_ref[...] = _pack_bf16_pair(y[:, :dm // 2], y[:, dm // 2:])

    @pl.when(i >= nb_ref[0])
    def _():
        y_ref[...] = jnp.zeros_like(y_ref)


def _experts(blk_e, nb_used, xs, w_gate, w_up, w_down):
    rows, half = xs.shape
    ne, dm, de = w_gate.shape
    nb = rows // EXPERT_ROWS
    grid_spec = pltpu.PrefetchScalarGridSpec(
        num_scalar_prefetch=2,
        grid=(nb,),
        in_specs=[pl.BlockSpec((EXPERT_ROWS, half), lambda i, be, n: (i, 0)),
                  pl.BlockSpec((1, dm, de), lambda i, be, n: (be[i], 0, 0)),
                  pl.BlockSpec((1, dm, de), lambda i, be, n: (be[i], 0, 0)),
                  pl.BlockSpec((1, de, dm), lambda i, be, n: (be[i], 0, 0))],
        out_specs=pl.BlockSpec((EXPERT_ROWS, half), lambda i, be, n: (i, 0)),
        scratch_shapes=[pltpu.VMEM((dm, 2 * de), BF16), pltpu.VMEM((de, dm), BF16)],
    )
    return pl.pallas_call(
        _experts_kernel,
        grid_spec=grid_spec,
        out_shape=jax.ShapeDtypeStruct((rows, half), U32),
        compiler_params=_cparams("arbitrary"),
        name="experts",
    )(blk_e, nb_used, xs, w_gate, w_up, w_down)


def _combine_kernel(dest_ref, ys_ref, w_ref, xm_ref, gt2_ref, gfin_ref, o_ref, buf, sem, *, tl):
    def issue(t, c):
        for k in range(TOP_K):
            pltpu.make_async_copy(ys_ref.at[pl.ds(dest_ref[0, k, t], 1)], buf.at[k, pl.ds(t, 1)], sem).start()
        return c

    lax.fori_loop(0, tl, issue, 0)

    def drain(t, c):
        for k in range(TOP_K):
            pltpu.make_async_copy(ys_ref.at[pl.ds(0, 1)], buf.at[0, pl.ds(0, 1)], sem).wait()
        return c

    lax.fori_loop(0, tl, drain, 0)

    w = w_ref[0]
    half = buf.shape[-1]
    acc_a = jnp.zeros((tl, half), F32)
    acc_b = jnp.zeros((tl, half), F32)
    for k in range(TOP_K):
        ya, yb = _unpack_bf16_pair(buf[k])
        wk = w[:, k:k + 1]
        acc_a = acc_a + wk * ya
        acc_b = acc_b + wk * yb
    routed = jnp.concatenate([acc_a, acc_b], axis=-1)
    x2 = xm_ref[0] + gt2_ref[0] * routed
    o_ref[0] = _rms(x2, gfin_ref[...])


def _combine(dest, ys, w, xm, gt2, gfin):
    b, l, d = xm.shape
    tl = min(l, COMBINE_ROWS)
    half = ys.shape[1]
    return pl.pallas_call(
        functools.partial(_combine_kernel, tl=tl),
        grid=(b, l // tl),
        in_specs=[pl.BlockSpec((1, TOP_K, tl), lambda i, j: (i, 0, j), memory_space=pltpu.SMEM),
                  pl.BlockSpec(memory_space=pl.ANY),
                  pl.BlockSpec((1, tl, TOP_K), lambda i, j: (i, j, 0)),
                  pl.BlockSpec((1, tl, d), lambda i, j: (i, j, 0)),
                  pl.BlockSpec((1, 1, d), lambda i, j: (i, 0, 0)),
                  pl.BlockSpec(gfin.shape, lambda i, j: (0, 0))],
        out_specs=pl.BlockSpec((1, tl, d), lambda i, j: (i, j, 0)),
        out_shape=jax.ShapeDtypeStruct((b, l, d), F32),
        scratch_shapes=[pltpu.VMEM((TOP_K, tl, half), U32), pltpu.SemaphoreType.DMA(())],
        compiler_params=_cparams("arbitrary", "arbitrary"),
        name=f"combine{l}",
    )(dest, ys, w, xm, gt2, gfin)


def _block_diag(m):
    g, a, b = m.shape
    eye = jnp.eye(g, dtype=m.dtype)
    return jnp.einsum("gab,gh->gahb", m, eye).reshape(g * a, g * b)


def kernel(x_prompt, x_sample, c_prompt, c_sample, cache_k, cache_v, cache_logf, state_ssm_re, state_ssm_im,
           w_ada, b_ada, g_mix, w_in, b_f, lam_re, lam_im, log_dt, ssm_b_re, ssm_b_im, ssm_c_re, ssm_c_im,
           ssm_d, w_glu, b_glu, g_attn_out, g_ssm_out, w_out, g_ffn, w_router, router_bias,
           w_gate, w_up, w_down, ws_gate, ws_up, ws_down, g_final):
    assert w_ada.shape[0] == 1, "single layer"
    bp, lp, d = x_prompt.shape
    bs, ls, _ = x_sample.shape
    past = cache_k.shape[2]

    nb = bp + bs
    nbp = -(-nb // 8) * 8
    c_all = jnp.concatenate([c_prompt, c_sample, jnp.zeros((nbp - nb, d), F32)], axis=0)
    mod = _prep(c_all, w_ada[0], b_ada[0]).reshape(nbp, 6, 1, d)
    mod_p = [mod[:bp, i] for i in range(6)]
    mod_s = [mod[bp:nb, i] for i in range(6)]

    a_re, a_im, bb_re, bb_im = _disc(lam_re[0], lam_im[0], log_dt[0],
                                     jnp.swapaxes(ssm_b_re[0], 1, 2), jnp.swapaxes(ssm_b_im[0], 1, 2))
    a_re = a_re.reshape(1, D_STATE)
    a_im = a_im.reshape(1, D_STATE)
    bd_re = _block_diag(bb_re).astype(BF16)
    bd_im = _block_diag(bb_im).astype(BF16)
    cc = jnp.concatenate([_block_diag(jnp.swapaxes(ssm_c_re[0], 1, 2)),
                          -_block_diag(jnp.swapaxes(ssm_c_im[0], 1, 2))], axis=0).astype(BF16)
    ssm_tail = (ssm_d[0].reshape(1, D_SSM), w_glu[0].astype(BF16), b_glu[0].reshape(1, D_SSM),
                g_ssm_out[0].reshape(1, D_SSM))

    ssm_consts = (bd_re, bd_im, cc) + tuple(_pow_table(a_re, a_im)) + ssm_tail

    inw = _inproj_weights(w_in[0], b_f[0])
    g_mix2 = g_mix[0].reshape(1, d)

    wo = w_out[0]
    wo_attn = jnp.pad(wo[:D_ATTN].reshape(N_HEADS, HEAD_DIM, d), ((0, 0), (0, HEAD_PAD - HEAD_DIM), (0, 0)))
    wo_pad = jnp.concatenate([wo_attn.reshape(N_HEADS * HEAD_PAD, d), wo[D_ATTN:]], axis=0).astype(BF16)
    gao_pad = jnp.pad(g_attn_out[0].reshape(N_HEADS, HEAD_DIM),
                      ((0, 0), (0, HEAD_PAD - HEAD_DIM))).reshape(1, N_HEADS * HEAD_PAD)
    wr_t = w_router[0].T
    wr_hi = wr_t.astype(BF16)
    wr_lo = (wr_t - wr_hi.astype(F32)).astype(BF16)
    post_consts = (gao_pad, wo_pad, g_ffn[0].reshape(1, d), wr_hi, wr_lo, router_bias[0].reshape(N_EXPERTS, 1),
                   jnp.concatenate([ws_gate[0], ws_up[0]], axis=1).astype(BF16), ws_down[0].astype(BF16))

    def mixer(x, modv, attn_fn, h0r, h0i):
        sh1, sc1, gt1, sh2, sc2, gt2 = modv
        q, k, v, kf, vf, lf, fb, u = _inproj(x, sh1, sc1, g_mix2, inw)
        attn = attn_fn(q, k, v, kf, vf, lf, fb)
        ssm, hr, hi = _ssm(u, h0r, h0i, ssm_consts)
        return attn, ssm, (kf, vf, lf, hr, hi), (gt1, sh2, sc2, gt2)

    def attn_prompt(q, k, v, kf, vf, lf, fb):
        return _attn(q, k, v, fb[:, :, 0, :N_HEADS].reshape(-1))

    zeros_p = jnp.zeros((bp, 1, D_STATE), F32)
    attn_p, ssm_p, new_p, m_p = mixer(x_prompt, mod_p, attn_prompt, zeros_p, zeros_p)

    lk = -(-(past + ls) // LANES) * LANES
    padk = lk - past - ls

    def attn_sample(q, k, v, kf, vf, lf, fb):
        k_all = jnp.concatenate([cache_k[0].reshape(bs, past, D_ATTN), kf,
                                 jnp.zeros((bs, padk, D_ATTN), F32)], axis=1)
        v_all = jnp.concatenate([cache_v[0].reshape(bs, past, D_ATTN), vf,
                                 jnp.zeros((bs, padk, D_ATTN), F32)], axis=1)
        lf_all = jnp.concatenate([cache_logf[0], lf, jnp.zeros((bs, padk, N_HEADS), F32)], axis=1)
        lf_cols = jnp.pad(lf_all, ((0, 0), (0, 0), (0, LANES - N_HEADS)))
        lf_rows = jnp.swapaxes(lf_all, 1, 2)
        return _attn_dec(q, k_all, v_all, lf_cols, lf_rows, past, ls)

    attn_s, ssm_s, new_s, m_s = mixer(x_sample, mod_s, attn_sample,
                                      state_ssm_re[0].reshape(bs, 1, D_STATE), state_ssm_im[0].reshape(bs, 1, D_STATE))

    cnt0 = jnp.zeros((N_EXPERTS, 1), F32)
    xm_p, hp_p, idx_p, pos_p, w_p, cnt1 = _post(x_prompt, attn_p, ssm_p, m_p, cnt0, post_consts, True)
    xm_s, hp_s, idx_s, pos_s, w_s, cnt2 = _post(x_sample, attn_s, ssm_s, m_s, cnt1, post_consts, False)

    counts = cnt2[:, 0].astype(I32)
    padded = (counts + EXPERT_ROWS - 1) // EXPERT_ROWS * EXPERT_ROWS
    pend = jnp.cumsum(padded)
    pstart = pend - padded
    tp, ts = bp * lp, bs * ls
    n_blocks = -(-((tp + ts) * TOP_K) // EXPERT_ROWS) + N_EXPERTS
    rows = n_blocks * EXPERT_ROWS
    dest_p = _slots(pstart, idx_p, pos_p)
    dest_s = _slots(pstart, idx_s, pos_s)
    nb_used = (pend[-1] // EXPERT_ROWS).astype(I32).reshape(1)
    blk_row = jnp.minimum(jnp.arange(n_blocks, dtype=I32), nb_used[0] - 1) * EXPERT_ROWS
    blk_e = jnp.minimum(jnp.sum((pend[None, :] <= blk_row[:, None]).astype(I32), axis=1), N_EXPERTS - 1)

    half = d // 2
    xs = jnp.zeros((rows, half), U32)
    flat = lambda dd: jnp.swapaxes(dd, 0, 1).reshape(TOP_K, -1)
    xs = _dispatch(flat(dest_p), hp_p.reshape(tp, half), xs)
    xs = _dispatch(flat(dest_s), hp_s.reshape(ts, half), xs)
    ys = _experts(blk_e, nb_used, xs, w_gate[0], w_up[0], w_down[0])

    gfin = g_final.reshape(1, d)
    y_p = _combine(dest_p, ys, w_p, xm_p, m_p[3], gfin)
    y_s = _combine(dest_s, ys, w_s, xm_s, m_s[3], gfin)

    def pack(new, b, l):
        kf, vf, lf, hr, hi = new
        return (kf.reshape(1, b, l, N_HEADS, HEAD_DIM), vf.reshape(1, b, l, N_HEADS, HEAD_DIM),
                lf.reshape(1, b, l, N_HEADS),
                hr.reshape(1, b, N_SSM_GROUPS, SSM_STATE), hi.reshape(1, b, N_SSM_GROUPS, SSM_STATE))

    return (y_p, y_s) + pack(new_p, bp, lp) + pack(new_s, bs, ls)
```

```python
import functools
import math

import jax
import jax.numpy as jnp
from jax import lax
from jax.experimental import pallas as pl
from jax.experimental.pallas import tpu as pltpu

F32 = jnp.float32
BF16 = jnp.bfloat16
U32 = jnp.uint32
I32 = jnp.int32

N_HEADS = 8
HEAD_DIM = 64
HEAD_PAD = 128
D_ATTN = N_HEADS * HEAD_DIM
SSM_GROUP = 16
N_SSM_GROUPS = 32
SSM_STATE = 64
D_SSM = SSM_GROUP * N_SSM_GROUPS
D_STATE = N_SSM_GROUPS * SSM_STATE
N_EXPERTS = 256
TOP_K = 8
N_EXP_GROUPS = 8
TOPK_GROUPS = 4
ROUTED_SCALE = 2.5
EPS = 1e-6
LANES = 128

SEQ_BLOCK = 512
SSM_CHUNK = 128
EXPERT_ROWS = 512
ATTN_HEADS_PER_STEP = 2
LOG2E = math.log2(math.e)
COMBINE_ROWS = 256
DISPATCH_ROWS = 512
VMEM_LIMIT = 56 * 1024 * 1024

_NT = (((1,), (1,)), ((), ()))


def _cparams(*sem):
    return pltpu.CompilerParams(dimension_semantics=sem, vmem_limit_bytes=VMEM_LIMIT)


def _split3(x):
    hi = x.astype(BF16)
    r = x - hi.astype(F32)
    mid = r.astype(BF16)
    lo = (r - mid.astype(F32)).astype(BF16)
    return hi, mid, lo


def _dot(a, b):
    return jnp.dot(a, b, preferred_element_type=F32)


def _dot_nt(a, b):
    return lax.dot_general(a, b, _NT, preferred_element_type=F32)


def _rms(x, g):
    return x * lax.rsqrt(jnp.mean(x * x, axis=-1, keepdims=True) + EPS) * g


def _sigmoid(x):
    return 1.0 / (1.0 + jnp.exp(-x))


def _pack_bf16_pair(a, b):
    ab = lax.bitcast_convert_type(a.astype(BF16).astype(F32), U32)
    bb = lax.bitcast_convert_type(b.astype(BF16).astype(F32), U32)
    return (ab >> 16) | (bb & jnp.uint32(0xFFFF0000))


def _unpack_bf16_pair(w):
    a = lax.bitcast_convert_type(w << 16, F32)
    b = lax.bitcast_convert_type(w & jnp.uint32(0xFFFF0000), F32)
    return a, b


def _prep_kernel(c_ref, w_ref, b_ref, o_ref):
    c = c_ref[...]
    a = c * _sigmoid(c)
    a_hi = a.astype(BF16)
    a_lo = (a - a_hi.astype(F32)).astype(BF16)
    w = w_ref[...]
    w_hi = w.astype(BF16)
    w_lo = (w - w_hi.astype(F32)).astype(BF16)
    o_ref[...] = _dot(a_hi, w_hi) + _dot(a_lo, w_hi) + _dot(a_hi, w_lo) + b_ref[...]


def _prep(c_all, w_ada, b_ada):
    n, d = c_all.shape
    nout = w_ada.shape[1]
    tn = 1024
    return pl.pallas_call(
        _prep_kernel,
        grid=(nout // tn,),
        in_specs=[pl.BlockSpec((n, d), lambda j: (0, 0)),
                  pl.BlockSpec((d, tn), lambda j: (0, j)),
                  pl.BlockSpec((1, tn), lambda j: (0, j))],
        out_specs=pl.BlockSpec((n, tn), lambda j: (0, j)),
        out_shape=jax.ShapeDtypeStruct((n, nout), F32),
        compiler_params=_cparams("arbitrary"),
        name="prep",
    )(c_all, w_ada, b_ada.reshape(1, nout))


def _disc_kernel(lr_ref, li_ref, ldt_ref, br_ref, bi_ref, ar_ref, ai_ref, bbr_ref, bbi_ref):
    lr = lr_ref[...]
    li = li_ref[...]
    dt = jnp.exp(ldt_ref[...])
    er = jnp.exp(lr * dt)
    ang = li * dt
    ar = er * jnp.cos(ang)
    ai = er * jnp.sin(ang)
    ar_ref[...] = ar
    ai_ref[...] = ai
    den = lr * lr + li * li
    nr = ((ar - 1.0) * lr + ai * li) / den
    ni = (ai * lr - (ar - 1.0) * li) / den
    nr3 = nr[:, None, :]
    ni3 = ni[:, None, :]
    br = br_ref[...]
    bi = bi_ref[...]
    bbr_ref[...] = nr3 * br - ni3 * bi
    bbi_ref[...] = nr3 * bi + ni3 * br


def _disc(lam_re, lam_im, log_dt, bt_re, bt_im):
    g, p = lam_re.shape
    c = bt_re.shape[1]
    return pl.pallas_call(
        _disc_kernel,
        out_shape=(jax.ShapeDtypeStruct((g, p), F32), jax.ShapeDtypeStruct((g, p), F32),
                   jax.ShapeDtypeStruct((g, c, p), F32), jax.ShapeDtypeStruct((g, c, p), F32)),
        name="disc",
    )(lam_re, lam_im, log_dt.reshape(g, 1), bt_re, bt_im)


SCAN_ROWS = 8
SCAN_LEVELS = 3


def _pow_kernel(ar_ref, ai_ref, tr_ref, ti_ref):
    n = ar_ref.shape[1]
    t = lax.broadcasted_iota(I32, (SCAN_ROWS, 1), 0)
    pr = jnp.ones((SCAN_ROWS, n), F32)
    pi = jnp.zeros((SCAN_ROWS, n), F32)
    sr = ar_ref[...]
    si = ai_ref[...]
    for k in range(SCAN_LEVELS + 1):
        bit = (((t + 1) >> k) & 1) == 1
        fr = jnp.where(bit, sr, 1.0)
        fi = jnp.where(bit, si, 0.0)
        pr, pi = pr * fr - pi * fi, pr * fi + pi * fr
        if k < SCAN_LEVELS:
            keep = t >= (1 << k)
            tr_ref[k] = jnp.where(keep, sr, 0.0)
            ti_ref[k] = jnp.where(keep, si, 0.0)
        sr, si = sr * sr - si * si, 2.0 * sr * si
    tr_ref[SCAN_LEVELS] = pr
    ti_ref[SCAN_LEVELS] = pi


def _pow_table(a_re, a_im):
    n = a_re.shape[1]
    shape = jax.ShapeDtypeStruct((SCAN_LEVELS + 1, SCAN_ROWS, n), F32)
    return pl.pallas_call(_pow_kernel, out_shape=(shape, shape), name="pow")(a_re, a_im)


def _inproj_kernel(x_ref, sh_ref, sc_ref, g_ref, wm_ref, wf_ref, bf_ref, selq_ref, selk_ref,
                   cq_ref, ck_ref, cv_ref,
                   q_ref, k_ref, v_ref, kf_ref, vf_ref, lf_ref, fb_ref, u_ref, carry_ref, *, tl):
    @pl.when(pl.program_id(1) == 0)
    def _():
        carry_ref[...] = jnp.zeros_like(carry_ref)

    x = x_ref[0]
    h = _rms(x, g_ref[...]) * (1.0 + sc_ref[0]) + sh_ref[0]
    hb = h.astype(BF16)
    main = _dot(hb, wm_ref[...])
    hw = N_HEADS * HEAD_PAD
    qp = main[:, 0:hw] * (HEAD_DIM ** -0.5 * LOG2E)
    kp = main[:, hw:2 * hw]
    vp = main[:, 2 * hw:3 * hw]
    o = 3 * hw
    kf_ref[0] = main[:, o:o + D_ATTN]
    vf_ref[0] = main[:, o + D_ATTN:o + 2 * D_ATTN]
    u_ref[0] = main[:, o + 2 * D_ATTN:o + 2 * D_ATTN + D_SSM]

    fl = _dot(hb, wf_ref[...]) + bf_ref[...]
    lf = jnp.minimum(fl, 0.0) - jnp.log1p(jnp.exp(-jnp.abs(fl)))
    lf_ref[0] = lf[:, 0:N_HEADS]

    row = lax.broadcasted_iota(I32, (tl, tl), 0)
    col = lax.broadcasted_iota(I32, (tl, tl), 1)
    tri = jnp.where(row >= col, 1.0, 0.0).astype(BF16)
    hi, mid, lo = _split3(lf)
    frel = _dot(tri, hi) + _dot(tri, mid) + _dot(tri, lo)
    fb_ref[0, 0] = carry_ref[...] * LOG2E
    carry_ref[...] = carry_ref[...] + frel[tl - 1:tl, :]

    fcat = jnp.concatenate(_split3(frel * LOG2E), axis=-1)
    q_aug = (qp + _dot(fcat, selq_ref[...]) + cq_ref[...]).astype(BF16)
    k_aug = (kp + _dot(fcat, selk_ref[...]) + ck_ref[...]).astype(BF16)
    v_aug = (vp + cv_ref[...]).astype(BF16)
    for hd in range(N_HEADS):
        sl = slice(hd * HEAD_PAD, (hd + 1) * HEAD_PAD)
        q_ref[0, hd] = q_aug[:, sl]
        k_ref[0, hd] = k_aug[:, sl]
        v_ref[0, hd] = v_aug[:, sl]


def _inproj(x, sh, sc, g, wts):
    b, l, d = x.shape
    tl = min(l, SEQ_BLOCK)
    nl = l // tl
    wm, wf, bf, selq, selk, cq, ck, cv = wts
    hm = jax.ShapeDtypeStruct((b, N_HEADS, l, HEAD_PAD), BF16)
    hm_spec = pl.BlockSpec((1, N_HEADS, tl, HEAD_PAD), lambda i, j: (i, 0, j, 0))
    tok = lambda w: pl.BlockSpec((1, tl, w), lambda i, j: (i, j, 0))
    full = lambda a: pl.BlockSpec(a.shape, lambda i, j: (0,) * a.ndim)
    row = pl.BlockSpec((1, 1, d), lambda i, j: (i, 0, 0))
    return pl.pallas_call(
        functools.partial(_inproj_kernel, tl=tl),
        grid=(b, nl),
        in_specs=[tok(d), row, row, full(g), full(wm), full(wf), full(bf), full(selq), full(selk),
                  full(cq), full(ck), full(cv)],
        out_specs=[hm_spec, hm_spec, hm_spec, tok(D_ATTN), tok(D_ATTN), tok(N_HEADS),
                   pl.BlockSpec((1, 1, 1, LANES), lambda i, j: (i, j, 0, 0)), tok(D_SSM)],
        out_shape=[hm, hm, hm,
                   jax.ShapeDtypeStruct((b, l, D_ATTN), F32), jax.ShapeDtypeStruct((b, l, D_ATTN), F32),
                   jax.ShapeDtypeStruct((b, l, N_HEADS), F32),
                   jax.ShapeDtypeStruct((b, nl, 1, LANES), F32),
                   jax.ShapeDtypeStruct((b, l, D_SSM), F32)],
        scratch_shapes=[pltpu.VMEM((1, LANES), F32)],
        compiler_params=_cparams("arbitrary", "arbitrary"),
        name=f"inproj{l}",
    )(x, sh, sc, g, wm, wf, bf, selq, selk, cq, ck, cv)


def _inproj_weights(w_in, b_f):
    d = w_in.shape[0]
    wq, wk, wv = (w_in[:, i * D_ATTN:(i + 1) * D_ATTN] for i in range(3))
    wfl = w_in[:, 3 * D_ATTN:3 * D_ATTN + N_HEADS]
    wu = w_in[:, 3 * D_ATTN + N_HEADS:]

    def pad_heads(w):
        w = w.reshape(d, N_HEADS, HEAD_DIM)
        w = jnp.pad(w, ((0, 0), (0, 0), (0, HEAD_PAD - HEAD_DIM)))
        return w.reshape(d, N_HEADS * HEAD_PAD)

    wm = jnp.concatenate([pad_heads(wq), pad_heads(wk), pad_heads(wv), wk, wv, wu], axis=1).astype(BF16)
    wf = jnp.pad(wfl, ((0, 0), (0, LANES - N_HEADS))).astype(BF16)
    bf = jnp.pad(b_f, (0, LANES - N_HEADS)).reshape(1, LANES).astype(F32)

    hw = N_HEADS * HEAD_PAD
    hd = jnp.arange(N_HEADS)
    selq = jnp.zeros((3 * LANES, hw), F32)
    selk = jnp.zeros((3 * LANES, hw), F32)
    cq = jnp.zeros((1, hw), F32)
    ck = jnp.zeros((1, hw), F32)
    cv = jnp.zeros((1, hw), F32)
    for part in range(3):
        selq = selq.at[part * LANES + hd, hd * HEAD_PAD + HEAD_DIM + part].set(1.0)
        selk = selk.at[part * LANES + hd, hd * HEAD_PAD + HEAD_DIM + 3 + part].set(-1.0)
        cq = cq.at[0, hd * HEAD_PAD + HEAD_DIM + 3 + part].set(1.0)
        ck = ck.at[0, hd * HEAD_PAD + HEAD_DIM + part].set(1.0)
    cv = cv.at[0, hd * HEAD_PAD + HEAD_DIM].set(1.0)
    return wm, wf, bf, selq.astype(BF16), selk.astype(BF16), cq, ck, cv


def _attn_kernel(fb_ref, q_ref, k_ref, v_ref, o_ref, *, r, nblk, hp):
    b = pl.program_id(0)
    g = pl.program_id(1)
    i = pl.program_id(2)
    fbase = (b * nblk) * N_HEADS + g * hp

    def scores(j):
        start = pl.multiple_of(j * r, r)
        return tuple(_dot_nt(q_ref[0, hh], k_ref[0, hh, pl.ds(start, r), :]) for hh in range(hp))

    def absorb(j, s_all, state, masked):
        start = pl.multiple_of(j * r, r)
        out = []
        for hh in range(hp):
            m, acc = state[hh]
            s = s_all[hh]
            if masked:
                row = lax.broadcasted_iota(I32, (r, r), 0)
                col = lax.broadcasted_iota(I32, (r, r), 1)
                s = jnp.where(row >= col, s, -jnp.inf)
            dlt = fb_ref[fbase + hh + i * N_HEADS] - fb_ref[fbase + hh + j * N_HEADS]
            mnew = jnp.maximum(m, jnp.max(s, axis=-1, keepdims=True) + dlt)
            p = jnp.exp2(s - (mnew - dlt))
            acc = jnp.exp2(m - mnew) * acc + _dot(p.astype(BF16), v_ref[0, hh, pl.ds(start, r), :])
            out.append((mnew, acc))
        return tuple(out)

    def pair(jj, state):
        j = 2 * jj
        s0, s1 = scores(j), scores(j + 1)
        return absorb(j + 1, s1, absorb(j, s0, state, False), False)

    def single(j, state):
        return absorb(j, scores(j), state, False)

    state = tuple((jnp.full((r, 1), -jnp.inf, F32), jnp.zeros((r, HEAD_PAD), F32)) for _ in range(hp))
    state = lax.fori_loop(0, i // 2, pair, state)
    state = lax.fori_loop(2 * (i // 2), i, single, state)
    state = absorb(i, scores(i), state, True)
    lane = lax.broadcasted_iota(I32, (r, HEAD_PAD), 1)
    for hh in range(hp):
        acc = state[hh][1]
        out = acc / acc[:, HEAD_DIM:HEAD_DIM + 1]
        o_ref[0, hh] = jnp.where(lane < HEAD_DIM, out, 0.0).astype(BF16)


def _attn(q, k, v, fb):
    b, h, l, _ = q.shape
    r = min(l, SEQ_BLOCK)
    nblk = l // r
    hp = ATTN_HEADS_PER_STEP
    return pl.pallas_call(
        functools.partial(_attn_kernel, r=r, nblk=nblk, hp=hp),
        grid=(b, h // hp, nblk),
        in_specs=[pl.BlockSpec(memory_space=pltpu.SMEM),
                  pl.BlockSpec((1, hp, r, HEAD_PAD), lambda bi, gi, i: (bi, gi, i, 0)),
                  pl.BlockSpec((1, hp, l, HEAD_PAD), lambda bi, gi, i: (bi, gi, 0, 0)),
                  pl.BlockSpec((1, hp, l, HEAD_PAD), lambda bi, gi, i: (bi, gi, 0, 0))],
        out_specs=pl.BlockSpec((1, hp, r, HEAD_PAD), lambda bi, gi, i: (bi, gi, i, 0)),
        out_shape=jax.ShapeDtypeStruct((b, h, l, HEAD_PAD), BF16),
        compiler_params=_cparams("arbitrary", "arbitrary", "arbitrary"),
        name="attn",
    )(fb, q, k, v)


def _attn_dec_kernel(q_ref, k_ref, v_ref, lfc_ref, lfr_ref, o_ref, *, lq, past, lk):
    row = lax.broadcasted_iota(I32, (lk, lk), 0)
    col = lax.broadcasted_iota(I32, (lk, lk), 1)
    tri = jnp.where(row >= col, 1.0, 0.0).astype(BF16)
    upper = jnp.where(row <= col, 1.0, 0.0).astype(BF16)
    c_hi, c_mid, c_lo = _split3(lfc_ref[0])
    fcol = _dot(tri, c_hi) + _dot(tri, c_mid) + _dot(tri, c_lo)
    r_hi, r_mid, r_lo = _split3(lfr_ref[0])
    frow = _dot(r_hi, upper) + _dot(r_mid, upper) + _dot(r_lo, upper)
    fq = fcol[past:past + lq, :]
    qpos = past + lax.broadcasted_iota(I32, (lq, lk), 0)
    kpos = lax.broadcasted_iota(I32, (lq, lk), 1)
    ok = kpos <= qpos
    qlane = lax.broadcasted_iota(I32, (lq, HEAD_PAD), 1)
    klane = lax.broadcasted_iota(I32, (lk, HEAD_PAD), 1)
    for hd in range(N_HEADS):
        odd = hd % 2 == 1
        pair = slice((hd // 2) * HEAD_PAD, (hd // 2 + 1) * HEAD_PAD)
        mine = (klane >= HEAD_DIM) if odd else (klane < HEAD_DIM)
        kh = jnp.where(mine, k_ref[0, :, pair], 0.0).astype(BF16)
        vh = jnp.where(mine, v_ref[0, :, pair], 0.0).astype(BF16)
        qh = jnp.where(qlane < HEAD_DIM, q_ref[0, hd].astype(F32), 0.0)
        if odd:
            qh = pltpu.roll(qh, HEAD_DIM, axis=1)
        s = _dot_nt(qh.astype(BF16), kh) + LOG2E * (fq[:, hd:hd + 1] - frow[hd:hd + 1, :])
        s = jnp.where(ok, s, -jnp.inf)
        m = jnp.max(s, axis=-1, keepdims=True)
        pb = jnp.exp2(s - m).astype(BF16)
        den = jnp.sum(pb.astype(F32), axis=-1, keepdims=True)
        out = _dot(pb, vh) / den
        if odd:
            out = pltpu.roll(out, HEAD_DIM, axis=1)
        o_ref[0, hd] = out.astype(BF16)


def _attn_dec(q, k_all, v_all, lf_cols, lf_rows, past, lq):
    b = q.shape[0]
    lk = k_all.shape[1]
    return pl.pallas_call(
        functools.partial(_attn_dec_kernel, lq=lq, past=past, lk=lk),
        grid=(b,),
        in_specs=[pl.BlockSpec((1, N_HEADS, lq, HEAD_PAD), lambda i: (i, 0, 0, 0)),
                  pl.BlockSpec((1, lk, D_ATTN), lambda i: (i, 0, 0)),
                  pl.BlockSpec((1, lk, D_ATTN), lambda i: (i, 0, 0)),
                  pl.BlockSpec((1, lk, LANES), lambda i: (i, 0, 0)),
                  pl.BlockSpec((1, N_HEADS, lk), lambda i: (i, 0, 0))],
        out_specs=pl.BlockSpec((1, N_HEADS, lq, HEAD_PAD), lambda i: (i, 0, 0, 0)),
        out_shape=jax.ShapeDtypeStruct((b, N_HEADS, lq, HEAD_PAD), BF16),
        compiler_params=_cparams("arbitrary"),
        name="attn_dec",
    )(q, k_all, v_all, lf_cols, lf_rows)


def _ssm_kernel(u_ref, h0r_ref, h0i_ref, br_ref, bi_ref, cc_ref, pr_ref, pi_ref, d_ref, wg_ref, bg_ref,
                gso_ref, y_ref, hr_out, hi_out, cr_ref, ci_ref, *, tc):
    @pl.when(pl.program_id(1) == 0)
    def _():
        cr_ref[...] = h0r_ref[0]
        ci_ref[...] = h0i_ref[0]

    u = u_ref[0]
    ub = u.astype(BF16)
    hr = _dot(ub, br_ref[...])
    hi = _dot(ub, bi_ref[...])
    nt = tc // SCAN_ROWS
    tiled = (nt, SCAN_ROWS, D_STATE)
    for k in range(SCAN_LEVELS):
        ar = pr_ref[k]
        ai = pi_ref[k]
        sr = pltpu.roll(hr, 1 << k, axis=0).reshape(tiled)
        si = pltpu.roll(hi, 1 << k, axis=0).reshape(tiled)
        hr = (hr.reshape(tiled) + ar * sr - ai * si).reshape(tc, D_STATE)
        hi = (hi.reshape(tiled) + ar * si + ai * sr).reshape(tc, D_STATE)
    pr = pr_ref[SCAN_LEVELS]
    pi = pi_ref[SCAN_LEVELS]
    cr = cr_ref[...]
    ci = ci_ref[...]
    rows_r, rows_i = [], []
    for j in range(nt):
        sl = slice(j * SCAN_ROWS, (j + 1) * SCAN_ROWS)
        xr = hr[sl] + pr * cr - pi * ci
        xi = hi[sl] + pr * ci + pi * cr
        cr = xr[SCAN_ROWS - 1:SCAN_ROWS, :]
        ci = xi[SCAN_ROWS - 1:SCAN_ROWS, :]
        rows_r.append(xr)
        rows_i.append(xi)
    hr = jnp.concatenate(rows_r, axis=0)
    hi = jnp.concatenate(rows_i, axis=0)
    cr_ref[...] = cr
    ci_ref[...] = ci
    hr_out[0] = cr
    hi_out[0] = ci

    hcat = jnp.concatenate([hr, hi], axis=-1).astype(BF16)
    y = _dot(hcat, cc_ref[...]) + d_ref[...] * u
    gl = 0.5 * y * (1.0 + jnp.tanh(math.sqrt(2.0 / math.pi) * (y + 0.044715 * (y * y * y))))
    z = _dot(gl.astype(BF16), wg_ref[...]) + bg_ref[...]
    out = y * _sigmoid(z)
    y_ref[0] = _rms(out, gso_ref[...]).astype(BF16)


def _ssm(u, h0r, h0i, consts):
    b, l, _ = u.shape
    br, bi, cc, pr, pi, dsk, wg, bg, gso = consts
    tc = min(l, SSM_CHUNK)
    full = lambda a: pl.BlockSpec(a.shape, lambda i, j: (0,) * a.ndim)
    st = pl.BlockSpec((1, 1, D_STATE), lambda i, j: (i, 0, 0))
    return pl.pallas_call(
        functools.partial(_ssm_kernel, tc=tc),
        grid=(b, l // tc),
        in_specs=[pl.BlockSpec((1, tc, D_SSM), lambda i, j: (i, j, 0)), st, st,
                  full(br), full(bi), full(cc), full(pr), full(pi), full(dsk), full(wg), full(bg), full(gso)],
        out_specs=[pl.BlockSpec((1, tc, D_SSM), lambda i, j: (i, j, 0)), st, st],
        out_shape=[jax.ShapeDtypeStruct((b, l, D_SSM), BF16),
                   jax.ShapeDtypeStruct((b, 1, D_STATE), F32), jax.ShapeDtypeStruct((b, 1, D_STATE), F32)],
        scratch_shapes=[pltpu.VMEM((1, D_STATE), F32), pltpu.VMEM((1, D_STATE), F32)],
        compiler_params=_cparams("arbitrary", "arbitrary"),
        name=f"ssm{l}",
    )(u, h0r, h0i, br, bi, cc, pr, pi, dsk, wg, bg, gso)


def _post_kernel(x_ref, a_ref, s_ref, gt1_ref, sh2_ref, sc2_ref, gt2_ref, cnt0_ref, gao_ref, wo_ref, gffn_ref,
                 wrh_ref, wrl_ref, rb_ref, wsgu_ref, wsd_ref,
                 xm_ref, hp_ref, idx_ref, pos_ref, w_ref, cnt_ref, carry_ref, *, tl, first):
    step = pl.program_id(0) * pl.num_programs(1) + pl.program_id(1)

    @pl.when(step == 0)
    def _():
        carry_ref[...] = cnt0_ref[...] if not first else jnp.zeros_like(carry_ref)

    x = x_ref[0]
    attn = jnp.concatenate([a_ref[0, hd] for hd in range(N_HEADS)], axis=-1).astype(F32)
    ms = jnp.sum(attn * attn, axis=-1, keepdims=True) * (1.0 / D_ATTN)
    attn_n = (attn * lax.rsqrt(ms + EPS) * gao_ref[...]).astype(BF16)
    merged = jnp.concatenate([attn_n, s_ref[0]], axis=-1)
    x1 = x + gt1_ref[0] * _dot(merged, wo_ref[...])
    h2 = _rms(x1, gffn_ref[...]) * (1.0 + sc2_ref[0]) + sh2_ref[0]
    d = h2.shape[-1]
    hp_ref[0] = _pack_bf16_pair(h2[:, :d // 2], h2[:, d // 2:])
    h_hi = h2.astype(BF16)
    h_lo = (h2 - h_hi.astype(F32)).astype(BF16)

    gu = _dot(h_hi, wsgu_ref[...])
    ds = gu.shape[-1] // 2
    g = gu[:, :ds]
    act = (g * _sigmoid(g) * gu[:, ds:]).astype(BF16)
    xm_ref[0] = x1 + gt2_ref[0] * _dot(act, wsd_ref[...])

    wrh = wrh_ref[...]
    logits = _dot_nt(wrh, h_hi) + _dot_nt(wrh, h_lo) + _dot_nt(wrl_ref[...], h_hi)
    s = _sigmoid(logits)
    sc = s + rb_ref[...]
    ge = N_EXPERTS // N_EXP_GROUPS
    neg = -jnp.inf
    gi = lax.broadcasted_iota(I32, (ge, tl), 0)
    gsc = []
    for gidx in range(N_EXP_GROUPS):
        blk = sc[gidx * ge:(gidx + 1) * ge, :]
        m1 = jnp.max(blk, axis=0, keepdims=True)
        f1 = jnp.min(jnp.where(blk == m1, gi, ge), axis=0, keepdims=True)
        m2 = jnp.max(jnp.where(gi == f1, neg, blk), axis=0, keepdims=True)
        gsc.append(m1 + m2)
    gwork = jnp.concatenate(gsc, axis=0)
    ni = lax.broadcasted_iota(I32, (N_EXP_GROUPS, tl), 0)
    gsel = jnp.zeros((N_EXP_GROUPS, tl), F32)
    for _ in range(TOPK_GROUPS):
        mx = jnp.max(gwork, axis=0, keepdims=True)
        fi = jnp.min(jnp.where(gwork == mx, ni, N_EXP_GROUPS), axis=0, keepdims=True)
        hit = ni == fi
        gsel = jnp.where(hit, 1.0, gsel)
        gwork = jnp.where(hit, neg, gwork)
    work = jnp.concatenate(
        [jnp.where(gsel[gidx:gidx + 1, :] > 0.0, sc[gidx * ge:(gidx + 1) * ge, :], neg)
         for gidx in range(N_EXP_GROUPS)], axis=0)
    ei = lax.broadcasted_iota(I32, (N_EXPERTS, tl), 0)
    chosen = jnp.zeros((N_EXPERTS, tl), F32)
    idx_rows, w_rows = [], []
    for _ in range(TOP_K):
        mx = jnp.max(work, axis=0, keepdims=True)
        fi = jnp.min(jnp.where(work == mx, ei, N_EXPERTS), axis=0, keepdims=True)
        hit = ei == fi
        w_rows.append(jnp.sum(jnp.where(hit, s, 0.0), axis=0, keepdims=True))
        idx_rows.append(fi)
        chosen = jnp.where(hit, 1.0, chosen)
        work = jnp.where(hit, neg, work)
    wt = jnp.concatenate(w_rows, axis=0)
    wt = wt / jnp.sum(wt, axis=0, keepdims=True) * ROUTED_SCALE
    idx_ref[0] = jnp.concatenate(idx_rows, axis=0)

    trow = lax.broadcasted_iota(I32, (tl, tl), 0)
    tcol = lax.broadcasted_iota(I32, (tl, tl), 1)
    before = jnp.where(trow < tcol, 1.0, 0.0).astype(BF16)
    rank = _dot(chosen.astype(BF16), before) + carry_ref[...]
    pos_rows = [jnp.sum(jnp.where(ei == idx_rows[k], rank, 0.0), axis=0, keepdims=True) for k in range(TOP_K)]
    pos_ref[0] = jnp.concatenate(pos_rows, axis=0).astype(I32)
    carry_ref[...] = carry_ref[...] + jnp.sum(chosen, axis=1, keepdims=True)
    cnt_ref[...] = carry_ref[...]

    eye = jnp.where(trow == tcol, 1.0, 0.0).astype(BF16)
    t_hi, t_mid, t_lo = _split3(wt)
    w_ref[0] = _dot_nt(eye, t_hi) + _dot_nt(eye, t_mid) + _dot_nt(eye, t_lo)


def _post(x, attn, ssm, mods, cnt0, consts, first):
    b, l, d = x.shape
    tl = min(l, SEQ_BLOCK)
    gt1, sh2, sc2, gt2 = mods
    gao, wo, gffn, wrh, wrl, rb, wsgu, wsd = consts
    tok = lambda w: pl.BlockSpec((1, tl, w), lambda i, j: (i, j, 0))
    full = lambda a: pl.BlockSpec(a.shape, lambda i, j: (0,) * a.ndim)
    row = pl.BlockSpec((1, 1, d), lambda i, j: (i, 0, 0))
    tk = pl.BlockSpec((1, TOP_K, tl), lambda i, j: (i, 0, j))
    return pl.pallas_call(
        functools.partial(_post_kernel, tl=tl, first=first),
        grid=(b, l // tl),
        in_specs=[tok(d), pl.BlockSpec((1, N_HEADS, tl, HEAD_PAD), lambda i, j: (i, 0, j, 0)), tok(D_SSM),
                  row, row, row, row, full(cnt0), full(gao), full(wo), full(gffn), full(wrh), full(wrl),
                  full(rb), full(wsgu), full(wsd)],
        out_specs=[tok(d), tok(d // 2), tk, tk, tok(TOP_K), full(cnt0)],
        out_shape=[jax.ShapeDtypeStruct((b, l, d), F32), jax.ShapeDtypeStruct((b, l, d // 2), U32),
                   jax.ShapeDtypeStruct((b, TOP_K, l), I32), jax.ShapeDtypeStruct((b, TOP_K, l), I32),
                   jax.ShapeDtypeStruct((b, l, TOP_K), F32), jax.ShapeDtypeStruct(cnt0.shape, F32)],
        scratch_shapes=[pltpu.VMEM(cnt0.shape, F32)],
        compiler_params=_cparams("arbitrary", "arbitrary"),
        name=f"post{l}",
    )(x, attn, ssm, gt1, sh2, sc2, gt2, cnt0, gao, wo, gffn, wrh, wrl, rb, wsgu, wsd)


def _slots_kernel(pstart_ref, idx_ref, pos_ref, dest_ref):
    idx = idx_ref[0]

    def body(e, base):
        return jnp.where(idx == e, pstart_ref[e], base)

    dest_ref[0] = lax.fori_loop(0, N_EXPERTS, body, jnp.zeros_like(idx)) + pos_ref[0]


def _slots(pstart, idx, pos):
    b, k, l = idx.shape
    tl = min(l, 2048)
    spec = pl.BlockSpec((1, k, tl), lambda i, j: (i, 0, j))
    return pl.pallas_call(
        _slots_kernel,
        grid=(b, l // tl),
        in_specs=[pl.BlockSpec(memory_space=pltpu.SMEM), spec, spec],
        out_specs=spec,
        out_shape=jax.ShapeDtypeStruct(idx.shape, I32),
        compiler_params=_cparams("arbitrary", "arbitrary"),
        name=f"slots{l}",
    )(pstart, idx, pos)


def _dispatch_kernel(dest_ref, h_ref, xin_ref, xs_ref, sem, *, td):
    del xin_ref

    def issue(t, c):
        for k in range(TOP_K):
            pltpu.make_async_copy(h_ref.at[pl.ds(t, 1)], xs_ref.at[pl.ds(dest_ref[t * TOP_K + k], 1)], sem).start()
        return c

    lax.fori_loop(0, td, issue, 0)

    def drain(t, c):
        for k in range(TOP_K):
            pltpu.make_async_copy(h_ref.at[pl.ds(0, 1)], xs_ref.at[pl.ds(0, 1)], sem).wait()
        return c

    lax.fori_loop(0, td, drain, 0)


def _dispatch(dest, hp, xs0):
    t, w = hp.shape
    td = min(t, DISPATCH_ROWS)
    return pl.pallas_call(
        functools.partial(_dispatch_kernel, td=td),
        grid=(t // td,),
        in_specs=[pl.BlockSpec((td * TOP_K,), lambda i: (i,), memory_space=pltpu.SMEM),
                  pl.BlockSpec((td, w), lambda i: (i, 0)), pl.BlockSpec(memory_space=pl.ANY)],
        out_specs=pl.BlockSpec(memory_space=pl.ANY),
        out_shape=jax.ShapeDtypeStruct(xs0.shape, xs0.dtype),
        scratch_shapes=[pltpu.SemaphoreType.DMA(())],
        input_output_aliases={2: 0},
        compiler_params=_cparams("arbitrary"),
        name="dispatch",
    )(dest, hp, xs0)


def _experts_kernel(be_ref, nb_ref, x_ref, wg_ref, wu_ref, wd_ref, y_ref, wgu_s, wd_s):
    i = pl.program_id(0)
    e = be_ref[i]
    prev = be_ref[jnp.maximum(i - 1, 0)]
    dm = wg_ref.shape[1]
    de = wg_ref.shape[2]

    @pl.when((i == 0) | (e != prev))
    def _():
        wgu_s[:, 0:de] = wg_ref[0].astype(BF16)
        wgu_s[:, de:2 * de] = wu_ref[0].astype(BF16)
        wd_s[...] = wd_ref[0].astype(BF16)

    @pl.when(i < nb_ref[0])
    def _():
        xa, xb = _unpack_bf16_pair(x_ref[...])
        gu = _dot(xa.astype(BF16), wgu_s[0:dm // 2, :]) + _dot(xb.astype(BF16), wgu_s[dm // 2:dm, :])
        g = gu[:, :de]
        act = (g * _sigmoid(g) * gu[:, de:]).astype(BF16)
        y = _dot(act, wd_s[...])
        y_ref[...] = _pack_bf16_pair(y[:, :dm // 2], y[:, dm // 2:])

    @pl.when(i >= nb_ref[0])
    def _():
        y_ref[...] = jnp.zeros_like(y_ref)


def _experts(blk_e, nb_used, xs, w_gate, w_up, w_down):
    rows, half = xs.shape
    ne, dm, de = w_gate.shape
    nb = rows // EXPERT_ROWS
    grid_spec = pltpu.PrefetchScalarGridSpec(
        num_scalar_prefetch=2,
        grid=(nb,),
        in_specs=[pl.BlockSpec((EXPERT_ROWS, half), lambda i, be, n: (i, 0)),
                  pl.BlockSpec((1, dm, de), lambda i, be, n: (be[i], 0, 0)),
                  pl.BlockSpec((1, dm, de), lambda i, be, n: (be[i], 0, 0)),
                  pl.BlockSpec((1, de, dm), lambda i, be, n: (be[i], 0, 0))],
        out_specs=pl.BlockSpec((EXPERT_ROWS, half), lambda i, be, n: (i, 0)),
        scratch_shapes=[pltpu.VMEM((dm, 2 * de), BF16), pltpu.VMEM((de, dm), BF16)],
    )
    return pl.pallas_call(
        _experts_kernel,
        grid_spec=grid_spec,
        out_shape=jax.ShapeDtypeStruct((rows, half), U32),
        compiler_params=_cparams("arbitrary"),
        name="experts",
    )(blk_e, nb_used, xs, w_gate, w_up, w_down)


def _combine_kernel(dcur_ref, dnext_ref, ys_ref, w_ref, xm_ref, gt2_ref, gfin_ref, o_ref, buf0, buf1, sem, *, tl):
    bufs = (buf0, buf1)
    half = buf0.shape[-1]
    s = pl.program_id(0)
    last = pl.num_programs(0) - 1
    group = SCAN_ROWS
    ngroups = tl // group

    def gather(dref, g, to):
        for tt in range(group):
            for k in range(TOP_K):
                row = dref[(g * group + tt) * TOP_K + k]
                pltpu.make_async_copy(ys_ref.at[pl.ds(row, 1)], bufs[to].at[k, g, pl.ds(tt, 1)], sem.at[to]).start()

    def drain(which):
        def body(t, c):
            for k in range(TOP_K):
                pltpu.make_async_copy(ys_ref.at[pl.ds(0, 1)], bufs[which].at[0, 0, pl.ds(0, 1)], sem.at[which]).wait()
            return c
        lax.fori_loop(0, tl, body, 0)

    @pl.when(s == 0)
    def _():
        def body(g, c):
            gather(dcur_ref, g, 0)
            return c
        lax.fori_loop(0, ngroups, body, 0)

    gt2 = gt2_ref[0]
    gfin = gfin_ref[...]

    def run(slot):
        drain(slot)

        def body(g, c):
            gather(dnext_ref, g, 1 - slot)
            t0 = pl.multiple_of(g * group, group)
            w = w_ref[pl.ds(t0, group), :]
            acc_a = jnp.zeros((group, half), F32)
            acc_b = jnp.zeros((group, half), F32)
            for k in range(TOP_K):
                ya, yb = _unpack_bf16_pair(bufs[slot][k, g])
                wk = w[:, k:k + 1]
                acc_a = acc_a + wk * ya
                acc_b = acc_b + wk * yb
            routed = jnp.concatenate([acc_a, acc_b], axis=-1)
            x2 = xm_ref[pl.ds(t0, group), :] + gt2 * routed
            o_ref[pl.ds(t0, group), :] = _rms(x2, gfin)
            return c

        lax.fori_loop(0, ngroups, body, 0)

        @pl.when(s == last)
        def _():
            drain(1 - slot)

    for slot in range(2):
        pl.when(s % 2 == slot)(functools.partial(run, slot))


def _combine(dest, ys, w, xm, gt2, gfin):
    b, l, d = xm.shape
    tl = min(l, COMBINE_ROWS)
    nl = l // tl
    n = b * nl
    half = ys.shape[1]
    out = pl.pallas_call(
        functools.partial(_combine_kernel, tl=tl),
        grid=(n,),
        in_specs=[pl.BlockSpec((tl * TOP_K,), lambda s: (s,), memory_space=pltpu.SMEM),
                  pl.BlockSpec((tl * TOP_K,), lambda s: (jnp.minimum(s + 1, n - 1),), memory_space=pltpu.SMEM),
                  pl.BlockSpec(memory_space=pl.ANY),
                  pl.BlockSpec((tl, TOP_K), lambda s: (s, 0)),
                  pl.BlockSpec((tl, d), lambda s: (s, 0)),
                  pl.BlockSpec((1, 1, d), lambda s: (s // nl, 0, 0)),
                  pl.BlockSpec(gfin.shape, lambda s: (0, 0))],
        out_specs=pl.BlockSpec((tl, d), lambda s: (s, 0)),
        out_shape=jax.ShapeDtypeStruct((b * l, d), F32),
        scratch_shapes=[pltpu.VMEM((TOP_K, tl // SCAN_ROWS, SCAN_ROWS, half), U32),
                        pltpu.VMEM((TOP_K, tl // SCAN_ROWS, SCAN_ROWS, half), U32),
                        pltpu.SemaphoreType.DMA((2,))],
        compiler_params=_cparams("arbitrary"),
        name=f"combine{l}",
    )(dest, dest, ys, w.reshape(b * l, TOP_K), xm.reshape(b * l, d), gt2, gfin)
    return out.reshape(b, l, d)


def _block_diag(m):
    g, a, b = m.shape
    eye = jnp.eye(g, dtype=m.dtype)
    return jnp.einsum("gab,gh->gahb", m, eye).reshape(g * a, g * b)


def kernel(x_prompt, x_sample, c_prompt, c_sample, cache_k, cache_v, cache_logf, state_ssm_re, state_ssm_im,
           w_ada, b_ada, g_mix, w_in, b_f, lam_re, lam_im, log_dt, ssm_b_re, ssm_b_im, ssm_c_re, ssm_c_im,
           ssm_d, w_glu, b_glu, g_attn_out, g_ssm_out, w_out, g_ffn, w_router, router_bias,
           w_gate, w_up, w_down, ws_gate, ws_up, ws_down, g_final):
    assert w_ada.shape[0] == 1, "single layer"
    bp, lp, d = x_prompt.shape
    bs, ls, _ = x_sample.shape
    past = cache_k.shape[2]

    nb = bp + bs
    nbp = -(-nb // 8) * 8
    c_all = jnp.concatenate([c_prompt, c_sample, jnp.zeros((nbp - nb, d), F32)], axis=0)
    mod = _prep(c_all, w_ada[0], b_ada[0]).reshape(nbp, 6, 1, d)
    mod_p = [mod[:bp, i] for i in range(6)]
    mod_s = [mod[bp:nb, i] for i in range(6)]

    a_re, a_im, bb_re, bb_im = _disc(lam_re[0], lam_im[0], log_dt[0],
                                     jnp.swapaxes(ssm_b_re[0], 1, 2), jnp.swapaxes(ssm_b_im[0], 1, 2))
    a_re = a_re.reshape(1, D_STATE)
    a_im = a_im.reshape(1, D_STATE)
    bd_re = _block_diag(bb_re).astype(BF16)
    bd_im = _block_diag(bb_im).astype(BF16)
    cc = jnp.concatenate([_block_diag(jnp.swapaxes(ssm_c_re[0], 1, 2)),
                          -_block_diag(jnp.swapaxes(ssm_c_im[0], 1, 2))], axis=0).astype(BF16)
    ssm_tail = (ssm_d[0].reshape(1, D_SSM), w_glu[0].astype(BF16), b_glu[0].reshape(1, D_SSM),
                g_ssm_out[0].reshape(1, D_SSM))

    ssm_consts = (bd_re, bd_im, cc) + tuple(_pow_table(a_re, a_im)) + ssm_tail

    inw = _inproj_weights(w_in[0], b_f[0])
    g_mix2 = g_mix[0].reshape(1, d)

    wo = w_out[0]
    wo_attn = jnp.pad(wo[:D_ATTN].reshape(N_HEADS, HEAD_DIM, d), ((0, 0), (0, HEAD_PAD - HEAD_DIM), (0, 0)))
    wo_pad = jnp.concatenate([wo_attn.reshape(N_HEADS * HEAD_PAD, d), wo[D_ATTN:]], axis=0).astype(BF16)
    gao_pad = jnp.pad(g_attn_out[0].reshape(N_HEADS, HEAD_DIM),
                      ((0, 0), (0, HEAD_PAD - HEAD_DIM))).reshape(1, N_HEADS * HEAD_PAD)
    wr_t = w_router[0].T
    wr_hi = wr_t.astype(BF16)
    wr_lo = (wr_t - wr_hi.astype(F32)).astype(BF16)
    post_consts = (gao_pad, wo_pad, g_ffn[0].reshape(1, d), wr_hi, wr_lo, router_bias[0].reshape(N_EXPERTS, 1),
                   jnp.concatenate([ws_gate[0], ws_up[0]], axis=1).astype(BF16), ws_down[0].astype(BF16))

    def mixer(x, modv, attn_fn, h0r, h0i):
        sh1, sc1, gt1, sh2, sc2, gt2 = modv
        q, k, v, kf, vf, lf, fb, u = _inproj(x, sh1, sc1, g_mix2, inw)
        attn = attn_fn(q, k, v, kf, vf, lf, fb)
        ssm, hr, hi = _ssm(u, h0r, h0i, ssm_consts)
        return attn, ssm, (kf, vf, lf, hr, hi), (gt1, sh2, sc2, gt2)

    def attn_prompt(q, k, v, kf, vf, lf, fb):
        return _attn(q, k, v, fb[:, :, 0, :N_HEADS].reshape(-1))

    zeros_p = jnp.zeros((bp, 1, D_STATE), F32)
    attn_p, ssm_p, new_p, m_p = mixer(x_prompt, mod_p, attn_prompt, zeros_p, zeros_p)

    lk = -(-(past + ls) // LANES) * LANES
    padk = lk - past - ls

    def attn_sample(q, k, v, kf, vf, lf, fb):
        k_all = jnp.concatenate([cache_k[0].reshape(bs, past, D_ATTN), kf,
                                 jnp.zeros((bs, padk, D_ATTN), F32)], axis=1)
        v_all = jnp.concatenate([cache_v[0].reshape(bs, past, D_ATTN), vf,
                                 jnp.zeros((bs, padk, D_ATTN), F32)], axis=1)
        lf_all = jnp.concatenate([cache_logf[0], lf, jnp.zeros((bs, padk, N_HEADS), F32)], axis=1)
        lf_cols = jnp.pad(lf_all, ((0, 0), (0, 0), (0, LANES - N_HEADS)))
        lf_rows = jnp.swapaxes(lf_all, 1, 2)
        return _attn_dec(q, k_all, v_all, lf_cols, lf_rows, past, ls)

    attn_s, ssm_s, new_s, m_s = mixer(x_sample, mod_s, attn_sample,
                                      state_ssm_re[0].reshape(bs, 1, D_STATE), state_ssm_im[0].reshape(bs, 1, D_STATE))

    cnt0 = jnp.zeros((N_EXPERTS, 1), F32)
    xm_p, hp_p, idx_p, pos_p, w_p, cnt1 = _post(x_prompt, attn_p, ssm_p, m_p, cnt0, post_consts, True)
    xm_s, hp_s, idx_s, pos_s, w_s, cnt2 = _post(x_sample, attn_s, ssm_s, m_s, cnt1, post_consts, False)

    counts = cnt2[:, 0].astype(I32)
    padded = (counts + EXPERT_ROWS - 1) // EXPERT_ROWS * EXPERT_ROWS
    pend = jnp.cumsum(padded)
    pstart = pend - padded
    tp, ts = bp * lp, bs * ls
    n_blocks = -(-((tp + ts) * TOP_K) // EXPERT_ROWS) + N_EXPERTS
    rows = n_blocks * EXPERT_ROWS
    dest_p = _slots(pstart, idx_p, pos_p)
    dest_s = _slots(pstart, idx_s, pos_s)
    nb_used = (pend[-1] // EXPERT_ROWS).astype(I32).reshape(1)
    blk_row = jnp.minimum(jnp.arange(n_blocks, dtype=I32), nb_used[0] - 1) * EXPERT_ROWS
    blk_e = jnp.minimum(jnp.sum((pend[None, :] <= blk_row[:, None]).astype(I32), axis=1), N_EXPERTS - 1)

    half = d // 2
    xs = jnp.zeros((rows, half), U32)
    dflat_p = jnp.swapaxes(dest_p, 1, 2).reshape(-1)
    dflat_s = jnp.swapaxes(dest_s, 1, 2).reshape(-1)
    xs = _dispatch(dflat_p, hp_p.reshape(tp, half), xs)
    xs = _dispatch(dflat_s, hp_s.reshape(ts, half), xs)
    ys = _experts(blk_e, nb_used, xs, w_gate[0], w_up[0], w_down[0])

    gfin = g_final.reshape(1, d)
    y_p = _combine(dflat_p, ys, w_p, xm_p, m_p[3], gfin)
    y_s = _combine(dflat_s, ys, w_s, xm_s, m_s[3], gfin)

    def pack(new, b, l):
        kf, vf, lf, hr, hi = new
        return (kf.reshape(1, b, l, N_HEADS, HEAD_DIM), vf.reshape(1, b, l, N_HEADS, HEAD_DIM),
                lf.reshape(1, b, l, N_HEADS),
                hr.reshape(1, b, N_SSM_GROUPS, SSM_STATE), hi.reshape(1, b, N_SSM_GROUPS, SSM_STATE))

    return (y_p, y_s) + pack(new_p, bp, lp) + pack(new_s, bs, ls)
```

```python
import functools
import math

import jax
import jax.numpy as jnp
from jax import lax
from jax.experimental import pallas as pl
from jax.experimental.pallas import tpu as pltpu

F32 = jnp.float32
BF16 = jnp.bfloat16
U32 = jnp.uint32
I32 = jnp.int32

N_HEADS = 8
HEAD_DIM = 64
HEAD_PAD = 128
D_ATTN = N_HEADS * HEAD_DIM
SSM_GROUP = 16
N_SSM_GROUPS = 32
SSM_STATE = 64
D_SSM = SSM_GROUP * N_SSM_GROUPS
D_STATE = N_SSM_GROUPS * SSM_STATE
N_EXPERTS = 256
TOP_K = 8
N_EXP_GROUPS = 8
TOPK_GROUPS = 4
ROUTED_SCALE = 2.5
EPS = 1e-6
LANES = 128

SEQ_BLOCK = 512
SSM_CHUNK = 512
EXPERT_ROWS = 512
ATTN_HEADS_PER_STEP = 2
ATTN_BLOCKS_PER_TRIP = 4
LOG2E = math.log2(math.e)
COMBINE_ROWS = 256
DISPATCH_ROWS = 512
VMEM_LIMIT = 56 * 1024 * 1024

_NT = (((1,), (1,)), ((), ()))


def _cparams(*sem):
    return pltpu.CompilerParams(dimension_semantics=sem, vmem_limit_bytes=VMEM_LIMIT)


def _split3(x):
    hi = x.astype(BF16)
    r = x - hi.astype(F32)
    mid = r.astype(BF16)
    lo = (r - mid.astype(F32)).astype(BF16)
    return hi, mid, lo


def _dot(a, b):
    return jnp.dot(a, b, preferred_element_type=F32)


def _dot_nt(a, b):
    return lax.dot_general(a, b, _NT, preferred_element_type=F32)


def _rms(x, g):
    return x * lax.rsqrt(jnp.mean(x * x, axis=-1, keepdims=True) + EPS) * g


def _sigmoid(x):
    return 1.0 / (1.0 + jnp.exp(-x))


def _pack_bf16_pair(a, b):
    ab = lax.bitcast_convert_type(a.astype(BF16).astype(F32), U32)
    bb = lax.bitcast_convert_type(b.astype(BF16).astype(F32), U32)
    return (ab >> 16) | (bb & jnp.uint32(0xFFFF0000))


def _unpack_bf16_pair(w):
    a = lax.bitcast_convert_type(w << 16, F32)
    b = lax.bitcast_convert_type(w & jnp.uint32(0xFFFF0000), F32)
    return a, b


def _prep_kernel(c_ref, w_ref, b_ref, o_ref):
    c = c_ref[...]
    a = c * _sigmoid(c)
    a_hi = a.astype(BF16)
    a_lo = (a - a_hi.astype(F32)).astype(BF16)
    w = w_ref[...]
    w_hi = w.astype(BF16)
    w_lo = (w - w_hi.astype(F32)).astype(BF16)
    o_ref[...] = _dot(a_hi, w_hi) + _dot(a_lo, w_hi) + _dot(a_hi, w_lo) + b_ref[...]


def _prep(c_all, w_ada, b_ada):
    n, d = c_all.shape
    nout = w_ada.shape[1]
    tn = 1024
    return pl.pallas_call(
        _prep_kernel,
        grid=(nout // tn,),
        in_specs=[pl.BlockSpec((n, d), lambda j: (0, 0)),
                  pl.BlockSpec((d, tn), lambda j: (0, j)),
                  pl.BlockSpec((1, tn), lambda j: (0, j))],
        out_specs=pl.BlockSpec((n, tn), lambda j: (0, j)),
        out_shape=jax.ShapeDtypeStruct((n, nout), F32),
        compiler_params=_cparams("arbitrary"),
        name="prep",
    )(c_all, w_ada, b_ada.reshape(1, nout))


def _disc_kernel(lr_ref, li_ref, ldt_ref, br_ref, bi_ref, ar_ref, ai_ref, bbr_ref, bbi_ref):
    lr = lr_ref[...]
    li = li_ref[...]
    dt = jnp.exp(ldt_ref[...])
    er = jnp.exp(lr * dt)
    ang = li * dt
    ar = er * jnp.cos(ang)
    ai = er * jnp.sin(ang)
    ar_ref[...] = ar
    ai_ref[...] = ai
    den = lr * lr + li * li
    nr = ((ar - 1.0) * lr + ai * li) / den
    ni = (ai * lr - (ar - 1.0) * li) / den
    nr3 = nr[:, None, :]
    ni3 = ni[:, None, :]
    br = br_ref[...]
    bi = bi_ref[...]
    bbr_ref[...] = nr3 * br - ni3 * bi
    bbi_ref[...] = nr3 * bi + ni3 * br


def _disc(lam_re, lam_im, log_dt, bt_re, bt_im):
    g, p = lam_re.shape
    c = bt_re.shape[1]
    return pl.pallas_call(
        _disc_kernel,
        out_shape=(jax.ShapeDtypeStruct((g, p), F32), jax.ShapeDtypeStruct((g, p), F32),
                   jax.ShapeDtypeStruct((g, c, p), F32), jax.ShapeDtypeStruct((g, c, p), F32)),
        name="disc",
    )(lam_re, lam_im, log_dt.reshape(g, 1), bt_re, bt_im)


SCAN_ROWS = 8
STATE_TILES = D_STATE // LANES


def _inproj_kernel(x_ref, sh_ref, sc_ref, g_ref, wm_ref, wf_ref, bf_ref, selq_ref, selk_ref,
                   cq_ref, ck_ref, cv_ref,
                   q_ref, k_ref, v_ref, kf_ref, vf_ref, lf_ref, fb_ref, u_ref, carry_ref, *, tl):
    @pl.when(pl.program_id(1) == 0)
    def _():
        carry_ref[...] = jnp.zeros_like(carry_ref)

    x = x_ref[0]
    h = _rms(x, g_ref[...]) * (1.0 + sc_ref[0]) + sh_ref[0]
    hb = h.astype(BF16)
    main = _dot(hb, wm_ref[...])
    hw = N_HEADS * HEAD_PAD
    qp = main[:, 0:hw] * (HEAD_DIM ** -0.5 * LOG2E)
    kp = main[:, hw:2 * hw]
    vp = main[:, 2 * hw:3 * hw]
    o = 3 * hw
    kf_ref[0] = main[:, o:o + D_ATTN]
    vf_ref[0] = main[:, o + D_ATTN:o + 2 * D_ATTN]
    u_ref[0] = main[:, o + 2 * D_ATTN:o + 2 * D_ATTN + D_SSM]

    fl = _dot(hb, wf_ref[...]) + bf_ref[...]
    lf = jnp.minimum(fl, 0.0) - jnp.log1p(jnp.exp(-jnp.abs(fl)))
    lf_ref[0] = lf[:, 0:N_HEADS]

    row = lax.broadcasted_iota(I32, (tl, tl), 0)
    col = lax.broadcasted_iota(I32, (tl, tl), 1)
    tri = jnp.where(row >= col, 1.0, 0.0).astype(BF16)
    hi, mid, lo = _split3(lf)
    frel = _dot(tri, hi) + _dot(tri, mid) + _dot(tri, lo)
    fb_ref[0, 0] = carry_ref[...] * LOG2E
    carry_ref[...] = carry_ref[...] + frel[tl - 1:tl, :]

    fcat = jnp.concatenate(_split3(frel * LOG2E), axis=-1)
    q_aug = (qp + _dot(fcat, selq_ref[...]) + cq_ref[...]).astype(BF16)
    k_aug = (kp + _dot(fcat, selk_ref[...]) + ck_ref[...]).astype(BF16)
    v_aug = (vp + cv_ref[...]).astype(BF16)
    for hd in range(N_HEADS):
        sl = slice(hd * HEAD_PAD, (hd + 1) * HEAD_PAD)
        q_ref[0, hd] = q_aug[:, sl]
        k_ref[0, hd] = k_aug[:, sl]
        v_ref[0, hd] = v_aug[:, sl]


def _inproj(x, sh, sc, g, wts):
    b, l, d = x.shape
    tl = min(l, SEQ_BLOCK)
    nl = l // tl
    wm, wf, bf, selq, selk, cq, ck, cv = wts
    hm = jax.ShapeDtypeStruct((b, N_HEADS, l, HEAD_PAD), BF16)
    hm_spec = pl.BlockSpec((1, N_HEADS, tl, HEAD_PAD), lambda i, j: (i, 0, j, 0))
    tok = lambda w: pl.BlockSpec((1, tl, w), lambda i, j: (i, j, 0))
    full = lambda a: pl.BlockSpec(a.shape, lambda i, j: (0,) * a.ndim)
    row = pl.BlockSpec((1, 1, d), lambda i, j: (i, 0, 0))
    return pl.pallas_call(
        functools.partial(_inproj_kernel, tl=tl),
        grid=(b, nl),
        in_specs=[tok(d), row, row, full(g), full(wm), full(wf), full(bf), full(selq), full(selk),
                  full(cq), full(ck), full(cv)],
        out_specs=[hm_spec, hm_spec, hm_spec, tok(D_ATTN), tok(D_ATTN), tok(N_HEADS),
                   pl.BlockSpec((1, 1, 1, LANES), lambda i, j: (i, j, 0, 0)), tok(D_SSM)],
        out_shape=[hm, hm, hm,
                   jax.ShapeDtypeStruct((b, l, D_ATTN), F32), jax.ShapeDtypeStruct((b, l, D_ATTN), F32),
                   jax.ShapeDtypeStruct((b, l, N_HEADS), F32),
                   jax.ShapeDtypeStruct((b, nl, 1, LANES), F32),
                   jax.ShapeDtypeStruct((b, l, D_SSM), F32)],
        scratch_shapes=[pltpu.VMEM((1, LANES), F32)],
        compiler_params=_cparams("arbitrary", "arbitrary"),
        name=f"inproj{l}",
    )(x, sh, sc, g, wm, wf, bf, selq, selk, cq, ck, cv)


def _inproj_weights(w_in, b_f):
    d = w_in.shape[0]
    wq, wk, wv = (w_in[:, i * D_ATTN:(i + 1) * D_ATTN] for i in range(3))
    wfl = w_in[:, 3 * D_ATTN:3 * D_ATTN + N_HEADS]
    wu = w_in[:, 3 * D_ATTN + N_HEADS:]

    def pad_heads(w):
        w = w.reshape(d, N_HEADS, HEAD_DIM)
        w = jnp.pad(w, ((0, 0), (0, 0), (0, HEAD_PAD - HEAD_DIM)))
        return w.reshape(d, N_HEADS * HEAD_PAD)

    wm = jnp.concatenate([pad_heads(wq), pad_heads(wk), pad_heads(wv), wk, wv, wu], axis=1).astype(BF16)
    wf = jnp.pad(wfl, ((0, 0), (0, LANES - N_HEADS))).astype(BF16)
    bf = jnp.pad(b_f, (0, LANES - N_HEADS)).reshape(1, LANES).astype(F32)

    hw = N_HEADS * HEAD_PAD
    hd = jnp.arange(N_HEADS)
    selq = jnp.zeros((3 * LANES, hw), F32)
    selk = jnp.zeros((3 * LANES, hw), F32)
    cq = jnp.zeros((1, hw), F32)
    ck = jnp.zeros((1, hw), F32)
    cv = jnp.zeros((1, hw), F32)
    for part in range(3):
        selq = selq.at[part * LANES + hd, hd * HEAD_PAD + HEAD_DIM + part].set(1.0)
        selk = selk.at[part * LANES + hd, hd * HEAD_PAD + HEAD_DIM + 3 + part].set(-1.0)
        cq = cq.at[0, hd * HEAD_PAD + HEAD_DIM + 3 + part].set(1.0)
        ck = ck.at[0, hd * HEAD_PAD + HEAD_DIM + part].set(1.0)
    cv = cv.at[0, hd * HEAD_PAD + HEAD_DIM].set(1.0)
    return wm, wf, bf, selq.astype(BF16), selk.astype(BF16), cq, ck, cv


def _attn_kernel(fb_ref, q_ref, k_ref, v_ref, o_ref, *, r, nblk, hp):
    b = pl.program_id(0)
    g = pl.program_id(1)
    i = pl.program_id(2)
    fbase = (b * nblk) * N_HEADS + g * hp

    def scores(j):
        start = pl.multiple_of(j * r, r)
        return tuple(_dot_nt(q_ref[0, hh], k_ref[0, hh, pl.ds(start, r), :]) for hh in range(hp))

    def absorb(j, s_all, state, masked):
        start = pl.multiple_of(j * r, r)
        out = []
        for hh in range(hp):
            m, acc = state[hh]
            s = s_all[hh]
            if masked:
                row = lax.broadcasted_iota(I32, (r, r), 0)
                col = lax.broadcasted_iota(I32, (r, r), 1)
                s = jnp.where(row >= col, s, -jnp.inf)
            dlt = fb_ref[fbase + hh + i * N_HEADS] - fb_ref[fbase + hh + j * N_HEADS]
            mnew = jnp.maximum(m, jnp.max(s, axis=-1, keepdims=True) + dlt)
            p = jnp.exp2(s - (mnew - dlt))
            acc = jnp.exp2(m - mnew) * acc + _dot(p.astype(BF16), v_ref[0, hh, pl.ds(start, r), :])
            out.append((mnew, acc))
        return tuple(out)

    nu = ATTN_BLOCKS_PER_TRIP

    def several(jj, state):
        js = [nu * jj + u for u in range(nu)]
        ss = [scores(j) for j in js]
        for j, s_all in zip(js, ss):
            state = absorb(j, s_all, state, False)
        return state

    def single(j, state):
        return absorb(j, scores(j), state, False)

    state = tuple((jnp.full((r, 1), -jnp.inf, F32), jnp.zeros((r, HEAD_PAD), F32)) for _ in range(hp))
    state = lax.fori_loop(0, i // nu, several, state)
    state = lax.fori_loop(nu * (i // nu), i, single, state)
    state = absorb(i, scores(i), state, True)
    lane = lax.broadcasted_iota(I32, (r, HEAD_PAD), 1)
    for hh in range(hp):
        acc = state[hh][1]
        out = acc / acc[:, HEAD_DIM:HEAD_DIM + 1]
        o_ref[0, hh] = jnp.where(lane < HEAD_DIM, out, 0.0).astype(BF16)


def _attn(q, k, v, fb):
    b, h, l, _ = q.shape
    r = min(l, SEQ_BLOCK)
    nblk = l // r
    hp = ATTN_HEADS_PER_STEP
    return pl.pallas_call(
        functools.partial(_attn_kernel, r=r, nblk=nblk, hp=hp),
        grid=(b, h // hp, nblk),
        in_specs=[pl.BlockSpec(memory_space=pltpu.SMEM),
                  pl.BlockSpec((1, hp, r, HEAD_PAD), lambda bi, gi, i: (bi, gi, i, 0)),
                  pl.BlockSpec((1, hp, l, HEAD_PAD), lambda bi, gi, i: (bi, gi, 0, 0)),
                  pl.BlockSpec((1, hp, l, HEAD_PAD), lambda bi, gi, i: (bi, gi, 0, 0))],
        out_specs=pl.BlockSpec((1, hp, r, HEAD_PAD), lambda bi, gi, i: (bi, gi, i, 0)),
        out_shape=jax.ShapeDtypeStruct((b, h, l, HEAD_PAD), BF16),
        compiler_params=_cparams("arbitrary", "arbitrary", "arbitrary"),
        name="attn",
    )(fb, q, k, v)


def _attn_dec_kernel(q_ref, k_ref, v_ref, lfc_ref, lfr_ref, o_ref, *, lq, past, lk):
    row = lax.broadcasted_iota(I32, (lk, lk), 0)
    col = lax.broadcasted_iota(I32, (lk, lk), 1)
    tri = jnp.where(row >= col, 1.0, 0.0).astype(BF16)
    upper = jnp.where(row <= col, 1.0, 0.0).astype(BF16)
    c_hi, c_mid, c_lo = _split3(lfc_ref[0])
    fcol = _dot(tri, c_hi) + _dot(tri, c_mid) + _dot(tri, c_lo)
    r_hi, r_mid, r_lo = _split3(lfr_ref[0])
    frow = _dot(r_hi, upper) + _dot(r_mid, upper) + _dot(r_lo, upper)
    fq = fcol[past:past + lq, :]
    qpos = past + lax.broadcasted_iota(I32, (lq, lk), 0)
    kpos = lax.broadcasted_iota(I32, (lq, lk), 1)
    ok = kpos <= qpos
    qlane = lax.broadcasted_iota(I32, (lq, HEAD_PAD), 1)
    klane = lax.broadcasted_iota(I32, (lk, HEAD_PAD), 1)
    for hd in range(N_HEADS):
        odd = hd % 2 == 1
        pair = slice((hd // 2) * HEAD_PAD, (hd // 2 + 1) * HEAD_PAD)
        mine = (klane >= HEAD_DIM) if odd else (klane < HEAD_DIM)
        kh = jnp.where(mine, k_ref[0, :, pair], 0.0).astype(BF16)
        vh = jnp.where(mine, v_ref[0, :, pair], 0.0).astype(BF16)
        qh = jnp.where(qlane < HEAD_DIM, q_ref[0, hd].astype(F32), 0.0)
        if odd:
            qh = pltpu.roll(qh, HEAD_DIM, axis=1)
        s = _dot_nt(qh.astype(BF16), kh) + LOG2E * (fq[:, hd:hd + 1] - frow[hd:hd + 1, :])
        s = jnp.where(ok, s, -jnp.inf)
        m = jnp.max(s, axis=-1, keepdims=True)
        pb = jnp.exp2(s - m).astype(BF16)
        den = jnp.sum(pb.astype(F32), axis=-1, keepdims=True)
        out = _dot(pb, vh) / den
        if odd:
            out = pltpu.roll(out, HEAD_DIM, axis=1)
        o_ref[0, hd] = out.astype(BF16)


def _attn_dec(q, k_all, v_all, lf_cols, lf_rows, past, lq):
    b = q.shape[0]
    lk = k_all.shape[1]
    return pl.pallas_call(
        functools.partial(_attn_dec_kernel, lq=lq, past=past, lk=lk),
        grid=(b,),
        in_specs=[pl.BlockSpec((1, N_HEADS, lq, HEAD_PAD), lambda i: (i, 0, 0, 0)),
                  pl.BlockSpec((1, lk, D_ATTN), lambda i: (i, 0, 0)),
                  pl.BlockSpec((1, lk, D_ATTN), lambda i: (i, 0, 0)),
                  pl.BlockSpec((1, lk, LANES), lambda i: (i, 0, 0)),
                  pl.BlockSpec((1, N_HEADS, lk), lambda i: (i, 0, 0))],
        out_specs=pl.BlockSpec((1, N_HEADS, lq, HEAD_PAD), lambda i: (i, 0, 0, 0)),
        out_shape=jax.ShapeDtypeStruct((b, N_HEADS, lq, HEAD_PAD), BF16),
        compiler_params=_cparams("arbitrary"),
        name="attn_dec",
    )(q, k_all, v_all, lf_cols, lf_rows)


def _ssm_kernel(u_ref, h0r_ref, h0i_ref, br_ref, bi_ref, cc_ref, ar_ref, ai_ref, d_ref, wg_ref, bg_ref,
                gso_ref, y_ref, hr_out, hi_out, cr_ref, ci_ref, xr_s, xi_s, or_s, oi_s, *, tc):
    @pl.when(pl.program_id(1) == 0)
    def _():
        cr_ref[...] = h0r_ref[0]
        ci_ref[...] = h0i_ref[0]

    u = u_ref[0]
    ub = u.astype(BF16)
    hc = D_SSM // 2
    hs = D_STATE // 2

    def drive(b_ref):
        return jnp.concatenate([_dot(ub[:, :hc], b_ref[0:hc, 0:hs]), _dot(ub[:, hc:], b_ref[hc:, hs:])], axis=-1)

    bur = drive(br_ref)
    bui = drive(bi_ref)
    nt = tc // SCAN_ROWS
    nc = STATE_TILES

    def tile_rows(tau, c):
        r0 = (tau * nc + c) * SCAN_ROWS
        return slice(r0, r0 + SCAN_ROWS)

    for tau in range(nt):
        rows = slice(tau * SCAN_ROWS, (tau + 1) * SCAN_ROWS)
        for c in range(nc):
            xr_s[tile_rows(tau, c), :] = bur[rows, c * LANES:(c + 1) * LANES]
            xi_s[tile_rows(tau, c), :] = bui[rows, c * LANES:(c + 1) * LANES]

    ar = ar_ref[...]
    ai = ai_ref[...]
    hr = cr_ref[...]
    hi = ci_ref[...]
    for t in range(tc):
        step = pl.ds((t // SCAN_ROWS) * nc * SCAN_ROWS + t % SCAN_ROWS, nc, stride=SCAN_ROWS)
        hr, hi = ar * hr - ai * hi + xr_s[step, :], ar * hi + ai * hr + xi_s[step, :]
        or_s[step, :] = hr
        oi_s[step, :] = hi
    cr_ref[...] = hr
    ci_ref[...] = hi
    hr_out[0] = hr
    hi_out[0] = hi

    def states(o_s, c0):
        return jnp.concatenate(
            [jnp.concatenate([o_s[tile_rows(tau, c), :] for c in range(c0, c0 + nc // 2)], axis=-1)
             for tau in range(nt)], axis=0).astype(BF16)

    y = jnp.concatenate(
        [_dot(states(or_s, 0), cc_ref[0:hs, 0:hc]) + _dot(states(oi_s, 0), cc_ref[D_STATE:D_STATE + hs, 0:hc]),
         _dot(states(or_s, nc // 2), cc_ref[hs:D_STATE, hc:]) + _dot(states(oi_s, nc // 2), cc_ref[D_STATE + hs:, hc:])],
        axis=-1) + d_ref[...] * u
    gl = 0.5 * y * (1.0 + jnp.tanh(math.sqrt(2.0 / math.pi) * (y + 0.044715 * (y * y * y))))
    z = _dot(gl.astype(BF16), wg_ref[...]) + bg_ref[...]
    out = y * _sigmoid(z)
    y_ref[0] = _rms(out, gso_ref[...]).astype(BF16)


def _ssm(u, h0r, h0i, consts):
    b, l, _ = u.shape
    br, bi, cc, ar, ai, dsk, wg, bg, gso = consts
    tc = min(l, SSM_CHUNK)
    full = lambda a: pl.BlockSpec(a.shape, lambda i, j: (0,) * a.ndim)
    st = pl.BlockSpec((1, STATE_TILES, LANES), lambda i, j: (i, 0, 0))
    state = jax.ShapeDtypeStruct((b, STATE_TILES, LANES), F32)
    carry = pltpu.VMEM((STATE_TILES, LANES), F32)
    tiles = pltpu.VMEM((tc * STATE_TILES, LANES), F32)
    return pl.pallas_call(
        functools.partial(_ssm_kernel, tc=tc),
        grid=(b, l // tc),
        in_specs=[pl.BlockSpec((1, tc, D_SSM), lambda i, j: (i, j, 0)), st, st,
                  full(br), full(bi), full(cc), full(ar), full(ai), full(dsk), full(wg), full(bg), full(gso)],
        out_specs=[pl.BlockSpec((1, tc, D_SSM), lambda i, j: (i, j, 0)), st, st],
        out_shape=[jax.ShapeDtypeStruct((b, l, D_SSM), BF16), state, state],
        scratch_shapes=[carry, carry, tiles, tiles, tiles, tiles],
        compiler_params=_cparams("arbitrary", "arbitrary"),
        name=f"ssm{l}",
    )(u, h0r, h0i, br, bi, cc, ar, ai, dsk, wg, bg, gso)


def _post_kernel(x_ref, a_ref, s_ref, gt1_ref, sh2_ref, sc2_ref, gt2_ref, cnt0_ref, gao_ref, wo_ref, gffn_ref,
                 wrh_ref, wrl_ref, rb_ref, wsgu_ref, wsd_ref,
                 xm_ref, hp_ref, idx_ref, pos_ref, w_ref, cnt_ref, carry_ref, *, tl, first):
    step = pl.program_id(0) * pl.num_programs(1) + pl.program_id(1)

    @pl.when(step == 0)
    def _():
        carry_ref[...] = cnt0_ref[...] if not first else jnp.zeros_like(carry_ref)

    x = x_ref[0]
    attn = jnp.concatenate([a_ref[0, hd] for hd in range(N_HEADS)], axis=-1).astype(F32)
    ms = jnp.sum(attn * attn, axis=-1, keepdims=True) * (1.0 / D_ATTN)
    attn_n = (attn * lax.rsqrt(ms + EPS) * gao_ref[...]).astype(BF16)
    merged = jnp.concatenate([attn_n, s_ref[0]], axis=-1)
    x1 = x + gt1_ref[0] * _dot(merged, wo_ref[...])
    h2 = _rms(x1, gffn_ref[...]) * (1.0 + sc2_ref[0]) + sh2_ref[0]
    d = h2.shape[-1]
    hp_ref[0] = _pack_bf16_pair(h2[:, :d // 2], h2[:, d // 2:])
    h_hi = h2.astype(BF16)
    h_lo = (h2 - h_hi.astype(F32)).astype(BF16)

    gu = _dot(h_hi, wsgu_ref[...])
    ds = gu.shape[-1] // 2
    g = gu[:, :ds]
    act = (g * _sigmoid(g) * gu[:, ds:]).astype(BF16)
    xm_ref[0] = x1 + gt2_ref[0] * _dot(act, wsd_ref[...])

    wrh = wrh_ref[...]
    logits = _dot_nt(wrh, h_hi) + _dot_nt(wrh, h_lo) + _dot_nt(wrl_ref[...], h_hi)
    s = _sigmoid(logits)
    sc = s + rb_ref[...]
    ge = N_EXPERTS // N_EXP_GROUPS
    neg = -jnp.inf
    gi = lax.broadcasted_iota(I32, (ge, tl), 0)
    gsc = []
    for gidx in range(N_EXP_GROUPS):
        blk = sc[gidx * ge:(gidx + 1) * ge, :]
        m1 = jnp.max(blk, axis=0, keepdims=True)
        f1 = jnp.min(jnp.where(blk == m1, gi, ge), axis=0, keepdims=True)
        m2 = jnp.max(jnp.where(gi == f1, neg, blk), axis=0, keepdims=True)
        gsc.append(m1 + m2)
    gwork = jnp.concatenate(gsc, axis=0)
    ni = lax.broadcasted_iota(I32, (N_EXP_GROUPS, tl), 0)
    gsel = jnp.zeros((N_EXP_GROUPS, tl), F32)
    for _ in range(TOPK_GROUPS):
        mx = jnp.max(gwork, axis=0, keepdims=True)
        fi = jnp.min(jnp.where(gwork == mx, ni, N_EXP_GROUPS), axis=0, keepdims=True)
        hit = ni == fi
        gsel = jnp.where(hit, 1.0, gsel)
        gwork = jnp.where(hit, neg, gwork)
    work = jnp.concatenate(
        [jnp.where(gsel[gidx:gidx + 1, :] > 0.0, sc[gidx * ge:(gidx + 1) * ge, :], neg)
         for gidx in range(N_EXP_GROUPS)], axis=0)
    ei = lax.broadcasted_iota(I32, (N_EXPERTS, tl), 0)
    chosen = jnp.zeros((N_EXPERTS, tl), F32)
    idx_rows, w_rows = [], []
    for _ in range(TOP_K):
        mx = jnp.max(work, axis=0, keepdims=True)
        fi = jnp.min(jnp.where(work == mx, ei, N_EXPERTS), axis=0, keepdims=True)
        hit = ei == fi
        w_rows.append(jnp.sum(jnp.where(hit, s, 0.0), axis=0, keepdims=True))
        idx_rows.append(fi)
        chosen = jnp.where(hit, 1.0, chosen)
        work = jnp.where(hit, neg, work)
    wt = jnp.concatenate(w_rows, axis=0)
    wt = wt / jnp.sum(wt, axis=0, keepdims=True) * ROUTED_SCALE
    idx_ref[0] = jnp.concatenate(idx_rows, axis=0)

    trow = lax.broadcasted_iota(I32, (tl, tl), 0)
    tcol = lax.broadcasted_iota(I32, (tl, tl), 1)
    before = jnp.where(trow < tcol, 1.0, 0.0).astype(BF16)
    rank = _dot(chosen.astype(BF16), before) + carry_ref[...]
    pos_rows = [jnp.sum(jnp.where(ei == idx_rows[k], rank, 0.0), axis=0, keepdims=True) for k in range(TOP_K)]
    pos_ref[0] = jnp.concatenate(pos_rows, axis=0).astype(I32)
    carry_ref[...] = carry_ref[...] + jnp.sum(chosen, axis=1, keepdims=True)
    cnt_ref[...] = carry_ref[...]

    eye = jnp.where(trow == tcol, 1.0, 0.0).astype(BF16)
    t_hi, t_mid, t_lo = _split3(wt)
    w_ref[0] = _dot_nt(eye, t_hi) + _dot_nt(eye, t_mid) + _dot_nt(eye, t_lo)


def _post(x, attn, ssm, mods, cnt0, consts, first):
    b, l, d = x.shape
    tl = min(l, SEQ_BLOCK)
    gt1, sh2, sc2, gt2 = mods
    gao, wo, gffn, wrh, wrl, rb, wsgu, wsd = consts
    tok = lambda w: pl.BlockSpec((1, tl, w), lambda i, j: (i, j, 0))
    full = lambda a: pl.BlockSpec(a.shape, lambda i, j: (0,) * a.ndim)
    row = pl.BlockSpec((1, 1, d), lambda i, j: (i, 0, 0))
    tk = pl.BlockSpec((1, TOP_K, tl), lambda i, j: (i, 0, j))
    return pl.pallas_call(
        functools.partial(_post_kernel, tl=tl, first=first),
        grid=(b, l // tl),
        in_specs=[tok(d), pl.BlockSpec((1, N_HEADS, tl, HEAD_PAD), lambda i, j: (i, 0, j, 0)), tok(D_SSM),
                  row, row, row, row, full(cnt0), full(gao), full(wo), full(gffn), full(wrh), full(wrl),
                  full(rb), full(wsgu), full(wsd)],
        out_specs=[tok(d), tok(d // 2), tk, tk, tok(TOP_K), full(cnt0)],
        out_shape=[jax.ShapeDtypeStruct((b, l, d), F32), jax.ShapeDtypeStruct((b, l, d // 2), U32),
                   jax.ShapeDtypeStruct((b, TOP_K, l), I32), jax.ShapeDtypeStruct((b, TOP_K, l), I32),
                   jax.ShapeDtypeStruct((b, l, TOP_K), F32), jax.ShapeDtypeStruct(cnt0.shape, F32)],
        scratch_shapes=[pltpu.VMEM(cnt0.shape, F32)],
        compiler_params=_cparams("arbitrary", "arbitrary"),
        name=f"post{l}",
    )(x, attn, ssm, gt1, sh2, sc2, gt2, cnt0, gao, wo, gffn, wrh, wrl, rb, wsgu, wsd)


def _slots_kernel(pstart_ref, idx_ref, pos_ref, dest_ref):
    idx = idx_ref[0]

    def body(e, base):
        return jnp.where(idx == e, pstart_ref[e], base)

    dest_ref[0] = lax.fori_loop(0, N_EXPERTS, body, jnp.zeros_like(idx)) + pos_ref[0]


def _slots(pstart, idx, pos):
    b, k, l = idx.shape
    tl = min(l, 2048)
    spec = pl.BlockSpec((1, k, tl), lambda i, j: (i, 0, j))
    return pl.pallas_call(
        _slots_kernel,
        grid=(b, l // tl),
        in_specs=[pl.BlockSpec(memory_space=pltpu.SMEM), spec, spec],
        out_specs=spec,
        out_shape=jax.ShapeDtypeStruct(idx.shape, I32),
        compiler_params=_cparams("arbitrary", "arbitrary"),
        name=f"slots{l}",
    )(pstart, idx, pos)


def _zero_block_kernel(lb_ref, o_ref):
    del lb_ref
    o_ref[...] = jnp.zeros_like(o_ref)


def _zero_tail_blocks(last_blk, rows, half):
    grid_spec = pltpu.PrefetchScalarGridSpec(
        num_scalar_prefetch=1,
        grid=(N_EXPERTS,),
        in_specs=[],
        out_specs=pl.BlockSpec((EXPERT_ROWS, half), lambda e, lb: (lb[e], 0)),
    )
    return pl.pallas_call(
        _zero_block_kernel,
        grid_spec=grid_spec,
        out_shape=jax.ShapeDtypeStruct((rows, half), U32),
        compiler_params=_cparams("arbitrary"),
        name="padzero",
    )(last_blk)


def _dispatch_kernel(dest_ref, h_ref, xin_ref, xs_ref, sem, *, td):
    del xin_ref

    def issue(t, c):
        for k in range(TOP_K):
            pltpu.make_async_copy(h_ref.at[pl.ds(t, 1)], xs_ref.at[pl.ds(dest_ref[t * TOP_K + k], 1)], sem).start()
        return c

    lax.fori_loop(0, td, issue, 0)

    def drain(t, c):
        for k in range(TOP_K):
            pltpu.make_async_copy(h_ref.at[pl.ds(0, 1)], xs_ref.at[pl.ds(0, 1)], sem).wait()
        return c

    lax.fori_loop(0, td, drain, 0)


def _dispatch(dest, hp, xs0):
    t, w = hp.shape
    td = min(t, DISPATCH_ROWS)
    return pl.pallas_call(
        functools.partial(_dispatch_kernel, td=td),
        grid=(t // td,),
        in_specs=[pl.BlockSpec((td * TOP_K,), lambda i: (i,), memory_space=pltpu.SMEM),
                  pl.BlockSpec((td, w), lambda i: (i, 0)), pl.BlockSpec(memory_space=pl.ANY)],
        out_specs=pl.BlockSpec(memory_space=pl.ANY),
        out_shape=jax.ShapeDtypeStruct(xs0.shape, xs0.dtype),
        scratch_shapes=[pltpu.SemaphoreType.DMA(())],
        input_output_aliases={2: 0},
        compiler_params=_cparams("arbitrary"),
        name="dispatch",
    )(dest, hp, xs0)


def _experts_kernel(be_ref, nb_ref, x_ref, wg_ref, wu_ref, wd_ref, y_ref, wgu_s, wd_s):
    i = pl.program_id(0)
    e = be_ref[i]
    prev = be_ref[jnp.maximum(i - 1, 0)]
    dm = wg_ref.shape[1]
    de = wg_ref.shape[2]

    @pl.when((i == 0) | (e != prev))
    def _():
        wgu_s[:, 0:de] = wg_ref[0].astype(BF16)
        wgu_s[:, de:2 * de] = wu_ref[0].astype(BF16)
        wd_s[...] = wd_ref[0].astype(BF16)

    @pl.when(i < nb_ref[0])
    def _():
        xa, xb = _unpack_bf16_pair(x_ref[...])
        gu = _dot(xa.astype(BF16), wgu_s[0:dm // 2, :]) + _dot(xb.astype(BF16), wgu_s[dm // 2:dm, :])
        g = gu[:, :de]
        act = (g * _sigmoid(g) * gu[:, de:]).astype(BF16)
        y = _dot(act, wd_s[...])
        y_ref[...] = _pack_bf16_pair(y[:, :dm // 2], y[:, dm // 2:])

    @pl.when(i >= nb_ref[0])
    def _():
        y_ref[...] = jnp.zeros_like(y_ref)


def _experts(blk_e, nb_used, xs, w_gate, w_up, w_down):
    rows, half = xs.shape
    ne, dm, de = w_gate.shape
    nb = rows // EXPERT_ROWS
    grid_spec = pltpu.PrefetchScalarGridSpec(
        num_scalar_prefetch=2,
        grid=(nb,),
        in_specs=[pl.BlockSpec((EXPERT_ROWS, half), lambda i, be, n: (jnp.minimum(i, n[0] - 1), 0)),
                  pl.BlockSpec((1, dm, de), lambda i, be, n: (be[i], 0, 0)),
                  pl.BlockSpec((1, dm, de), lambda i, be, n: (be[i], 0, 0)),
                  pl.BlockSpec((1, de, dm), lambda i, be, n: (be[i], 0, 0))],
        out_specs=pl.BlockSpec((EXPERT_ROWS, half), lambda i, be, n: (i, 0)),
        scratch_shapes=[pltpu.VMEM((dm, 2 * de), BF16), pltpu.VMEM((de, dm), BF16)],
    )
    return pl.pallas_call(
        _experts_kernel,
        grid_spec=grid_spec,
        out_shape=jax.ShapeDtypeStruct((rows, half), U32),
        compiler_params=_cparams("arbitrary"),
        name="experts",
    )(blk_e, nb_used, xs, w_gate, w_up, w_down)


def _combine_kernel(dcur_ref, dnext_ref, ys_ref, w_ref, xm_ref, gt2_ref, gfin_ref, o_ref, buf0, buf1, sem, *, tl):
    bufs = (buf0, buf1)
    half = buf0.shape[-1]
    s = pl.program_id(0)
    last = pl.num_programs(0) - 1
    group = SCAN_ROWS
    ngroups = tl // group

    def gather(dref, g, to):
        for tt in range(group):
            for k in range(TOP_K):
                row = dref[(g * group + tt) * TOP_K + k]
                pltpu.make_async_copy(ys_ref.at[pl.ds(row, 1)], bufs[to].at[k, g, pl.ds(tt, 1)], sem.at[to]).start()

    def drain(which):
        def body(t, c):
            for k in range(TOP_K):
                pltpu.make_async_copy(ys_ref.at[pl.ds(0, 1)], bufs[which].at[0, 0, pl.ds(0, 1)], sem.at[which]).wait()
            return c
        lax.fori_loop(0, tl, body, 0)

    @pl.when(s == 0)
    def _():
        def body(g, c):
            gather(dcur_ref, g, 0)
            return c
        lax.fori_loop(0, ngroups, body, 0)

    gt2 = gt2_ref[0]
    gfin = gfin_ref[...]

    def run(slot):
        drain(slot)

        def body(g, c):
            gather(dnext_ref, g, 1 - slot)
            t0 = pl.multiple_of(g * group, group)
            w = w_ref[pl.ds(t0, group), :]
            acc_a = jnp.zeros((group, half), F32)
            acc_b = jnp.zeros((group, half), F32)
            for k in range(TOP_K):
                ya, yb = _unpack_bf16_pair(bufs[slot][k, g])
                wk = w[:, k:k + 1]
                acc_a = acc_a + wk * ya
                acc_b = acc_b + wk * yb
            routed = jnp.concatenate([acc_a, acc_b], axis=-1)
            x2 = xm_ref[pl.ds(t0, group), :] + gt2 * routed
            o_ref[pl.ds(t0, group), :] = _rms(x2, gfin)
            return c

        lax.fori_loop(0, ngroups, body, 0)

        @pl.when(s == last)
        def _():
            drain(1 - slot)

    for slot in range(2):
        pl.when(s % 2 == slot)(functools.partial(run, slot))


def _combine(dest, ys, w, xm, gt2, gfin):
    b, l, d = xm.shape
    tl = min(l, COMBINE_ROWS)
    nl = l // tl
    n = b * nl
    half = ys.shape[1]
    out = pl.pallas_call(
        functools.partial(_combine_kernel, tl=tl),
        grid=(n,),
        in_specs=[pl.BlockSpec((tl * TOP_K,), lambda s: (s,), memory_space=pltpu.SMEM),
                  pl.BlockSpec((tl * TOP_K,), lambda s: (jnp.minimum(s + 1, n - 1),), memory_space=pltpu.SMEM),
                  pl.BlockSpec(memory_space=pl.ANY),
                  pl.BlockSpec((tl, TOP_K), lambda s: (s, 0)),
                  pl.BlockSpec((tl, d), lambda s: (s, 0)),
                  pl.BlockSpec((1, 1, d), lambda s: (s // nl, 0, 0)),
                  pl.BlockSpec(gfin.shape, lambda s: (0, 0))],
        out_specs=pl.BlockSpec((tl, d), lambda s: (s, 0)),
        out_shape=jax.ShapeDtypeStruct((b * l, d), F32),
        scratch_shapes=[pltpu.VMEM((TOP_K, tl // SCAN_ROWS, SCAN_ROWS, half), U32),
                        pltpu.VMEM((TOP_K, tl // SCAN_ROWS, SCAN_ROWS, half), U32),
                        pltpu.SemaphoreType.DMA((2,))],
        compiler_params=_cparams("arbitrary"),
        name=f"combine{l}",
    )(dest, dest, ys, w.reshape(b * l, TOP_K), xm.reshape(b * l, d), gt2, gfin)
    return out.reshape(b, l, d)


def _block_diag(m):
    g, a, b = m.shape
    eye = jnp.eye(g, dtype=m.dtype)
    return jnp.einsum("gab,gh->gahb", m, eye).reshape(g * a, g * b)


def kernel(x_prompt, x_sample, c_prompt, c_sample, cache_k, cache_v, cache_logf, state_ssm_re, state_ssm_im,
           w_ada, b_ada, g_mix, w_in, b_f, lam_re, lam_im, log_dt, ssm_b_re, ssm_b_im, ssm_c_re, ssm_c_im,
           ssm_d, w_glu, b_glu, g_attn_out, g_ssm_out, w_out, g_ffn, w_router, router_bias,
           w_gate, w_up, w_down, ws_gate, ws_up, ws_down, g_final):
    assert w_ada.shape[0] == 1, "single layer"
    bp, lp, d = x_prompt.shape
    bs, ls, _ = x_sample.shape
    past = cache_k.shape[2]

    nb = bp + bs
    nbp = -(-nb // 8) * 8
    c_all = jnp.concatenate([c_prompt, c_sample, jnp.zeros((nbp - nb, d), F32)], axis=0)
    mod = _prep(c_all, w_ada[0], b_ada[0]).reshape(nbp, 6, 1, d)
    mod_p = [mod[:bp, i] for i in range(6)]
    mod_s = [mod[bp:nb, i] for i in range(6)]

    a_re, a_im, bb_re, bb_im = _disc(lam_re[0], lam_im[0], log_dt[0],
                                     jnp.swapaxes(ssm_b_re[0], 1, 2), jnp.swapaxes(ssm_b_im[0], 1, 2))
    a_re = a_re.reshape(STATE_TILES, LANES)
    a_im = a_im.reshape(STATE_TILES, LANES)
    bd_re = _block_diag(bb_re).astype(BF16)
    bd_im = _block_diag(bb_im).astype(BF16)
    cc = jnp.concatenate([_block_diag(jnp.swapaxes(ssm_c_re[0], 1, 2)),
                          -_block_diag(jnp.swapaxes(ssm_c_im[0], 1, 2))], axis=0).astype(BF16)
    ssm_tail = (ssm_d[0].reshape(1, D_SSM), w_glu[0].astype(BF16), b_glu[0].reshape(1, D_SSM),
                g_ssm_out[0].reshape(1, D_SSM))

    ssm_consts = (bd_re, bd_im, cc, a_re, a_im) + ssm_tail

    inw = _inproj_weights(w_in[0], b_f[0])
    g_mix2 = g_mix[0].reshape(1, d)

    wo = w_out[0]
    wo_attn = jnp.pad(wo[:D_ATTN].reshape(N_HEADS, HEAD_DIM, d), ((0, 0), (0, HEAD_PAD - HEAD_DIM), (0, 0)))
    wo_pad = jnp.concatenate([wo_attn.reshape(N_HEADS * HEAD_PAD, d), wo[D_ATTN:]], axis=0).astype(BF16)
    gao_pad = jnp.pad(g_attn_out[0].reshape(N_HEADS, HEAD_DIM),
                      ((0, 0), (0, HEAD_PAD - HEAD_DIM))).reshape(1, N_HEADS * HEAD_PAD)
    wr_t = w_router[0].T
    wr_hi = wr_t.astype(BF16)
    wr_lo = (wr_t - wr_hi.astype(F32)).astype(BF16)
    post_consts = (gao_pad, wo_pad, g_ffn[0].reshape(1, d), wr_hi, wr_lo, router_bias[0].reshape(N_EXPERTS, 1),
                   jnp.concatenate([ws_gate[0], ws_up[0]], axis=1).astype(BF16), ws_down[0].astype(BF16))

    def mixer(x, modv, attn_fn, h0r, h0i):
        sh1, sc1, gt1, sh2, sc2, gt2 = modv
        q, k, v, kf, vf, lf, fb, u = _inproj(x, sh1, sc1, g_mix2, inw)
        attn = attn_fn(q, k, v, kf, vf, lf, fb)
        ssm, hr, hi = _ssm(u, h0r, h0i, ssm_consts)
        return attn, ssm, (kf, vf, lf, hr, hi), (gt1, sh2, sc2, gt2)

    def attn_prompt(q, k, v, kf, vf, lf, fb):
        return _attn(q, k, v, fb[:, :, 0, :N_HEADS].reshape(-1))

    zeros_p = jnp.zeros((bp, STATE_TILES, LANES), F32)
    attn_p, ssm_p, new_p, m_p = mixer(x_prompt, mod_p, attn_prompt, zeros_p, zeros_p)

    lk = -(-(past + ls) // LANES) * LANES
    padk = lk - past - ls

    def attn_sample(q, k, v, kf, vf, lf, fb):
        k_all = jnp.concatenate([cache_k[0].reshape(bs, past, D_ATTN), kf,
                                 jnp.zeros((bs, padk, D_ATTN), F32)], axis=1)
        v_all = jnp.concatenate([cache_v[0].reshape(bs, past, D_ATTN), vf,
                                 jnp.zeros((bs, padk, D_ATTN), F32)], axis=1)
        lf_all = jnp.concatenate([cache_logf[0], lf, jnp.zeros((bs, padk, N_HEADS), F32)], axis=1)
        lf_cols = jnp.pad(lf_all, ((0, 0), (0, 0), (0, LANES - N_HEADS)))
        lf_rows = jnp.swapaxes(lf_all, 1, 2)
        return _attn_dec(q, k_all, v_all, lf_cols, lf_rows, past, ls)

    attn_s, ssm_s, new_s, m_s = mixer(x_sample, mod_s, attn_sample,
                                      state_ssm_re[0].reshape(bs, STATE_TILES, LANES),
                                      state_ssm_im[0].reshape(bs, STATE_TILES, LANES))

    cnt0 = jnp.zeros((N_EXPERTS, 1), F32)
    xm_p, hp_p, idx_p, pos_p, w_p, cnt1 = _post(x_prompt, attn_p, ssm_p, m_p, cnt0, post_consts, True)
    xm_s, hp_s, idx_s, pos_s, w_s, cnt2 = _post(x_sample, attn_s, ssm_s, m_s, cnt1, post_consts, False)

    counts = cnt2[:, 0].astype(I32)
    padded = (counts + EXPERT_ROWS - 1) // EXPERT_ROWS * EXPERT_ROWS
    pend = jnp.cumsum(padded)
    pstart = pend - padded
    tp, ts = bp * lp, bs * ls
    n_blocks = -(-((tp + ts) * TOP_K) // EXPERT_ROWS) + N_EXPERTS
    rows = n_blocks * EXPERT_ROWS
    dest_p = _slots(pstart, idx_p, pos_p)
    dest_s = _slots(pstart, idx_s, pos_s)
    nb_used = (pend[-1] // EXPERT_ROWS).astype(I32).reshape(1)
    blk_row = jnp.minimum(jnp.arange(n_blocks, dtype=I32), nb_used[0] - 1) * EXPERT_ROWS
    blk_e = jnp.minimum(jnp.sum((pend[None, :] <= blk_row[:, None]).astype(I32), axis=1), N_EXPERTS - 1)

    half = d // 2
    xs = _zero_tail_blocks(jnp.maximum(pend // EXPERT_ROWS - 1, 0).astype(I32), rows, half)
    dflat_p = jnp.swapaxes(dest_p, 1, 2).reshape(-1)
    dflat_s = jnp.swapaxes(dest_s, 1, 2).reshape(-1)
    xs = _dispatch(dflat_p, hp_p.reshape(tp, half), xs)
    xs = _dispatch(dflat_s, hp_s.reshape(ts, half), xs)
    ys = _experts(blk_e, nb_used, xs, w_gate[0], w_up[0], w_down[0])

    gfin = g_final.reshape(1, d)
    y_p = _combine(dflat_p, ys, w_p, xm_p, m_p[3], gfin)
    y_s = _combine(dflat_s, ys, w_s, xm_s, m_s[3], gfin)

    def pack(new, b, l):
        kf, vf, lf, hr, hi = new
        return (kf.reshape(1, b, l, N_HEADS, HEAD_DIM), vf.reshape(1, b, l, N_HEADS, HEAD_DIM),
                lf.reshape(1, b, l, N_HEADS),
                hr.reshape(1, b, N_SSM_GROUPS, SSM_STATE), hi.reshape(1, b, N_SSM_GROUPS, SSM_STATE))

    return (y_p, y_s) + pack(new_p, bp, lp) + pack(new_s, bs, ls)
```

```python
import functools
import math

import jax
import jax.numpy as jnp
from jax import lax
from jax.experimental import pallas as pl
from jax.experimental.pallas import tpu as pltpu

F32 = jnp.float32
BF16 = jnp.bfloat16
U32 = jnp.uint32
I32 = jnp.int32

N_HEADS = 8
HEAD_DIM = 64
HEAD_PAD = 128
D_ATTN = N_HEADS * HEAD_DIM
SSM_GROUP = 16
N_SSM_GROUPS = 32
SSM_STATE = 64
D_SSM = SSM_GROUP * N_SSM_GROUPS
D_STATE = N_SSM_GROUPS * SSM_STATE
N_EXPERTS = 256
TOP_K = 8
N_EXP_GROUPS = 8
TOPK_GROUPS = 4
ROUTED_SCALE = 2.5
EPS = 1e-6
LANES = 128

SEQ_BLOCK = 512
SSM_CHUNK = 512
EXPERT_ROWS = 512
ATTN_HEADS_PER_STEP = 2
ATTN_BLOCKS_PER_TRIP = 4
LOG2E = math.log2(math.e)
COMBINE_ROWS = 256
DISPATCH_ROWS = 512
VMEM_LIMIT = 56 * 1024 * 1024

_NT = (((1,), (1,)), ((), ()))


def _cparams(*sem):
    return pltpu.CompilerParams(dimension_semantics=sem, vmem_limit_bytes=VMEM_LIMIT)


def _split3(x):
    hi = x.astype(BF16)
    r = x - hi.astype(F32)
    mid = r.astype(BF16)
    lo = (r - mid.astype(F32)).astype(BF16)
    return hi, mid, lo


def _dot(a, b):
    return jnp.dot(a, b, preferred_element_type=F32)


def _dot_nt(a, b):
    return lax.dot_general(a, b, _NT, preferred_element_type=F32)


def _rms(x, g):
    return x * lax.rsqrt(jnp.mean(x * x, axis=-1, keepdims=True) + EPS) * g


def _sigmoid(x):
    return 1.0 / (1.0 + jnp.exp(-x))


def _pack_bf16_pair(a, b):
    ab = lax.bitcast_convert_type(a.astype(BF16).astype(F32), U32)
    bb = lax.bitcast_convert_type(b.astype(BF16).astype(F32), U32)
    return (ab >> 16) | (bb & jnp.uint32(0xFFFF0000))


def _unpack_bf16_pair(w):
    a = lax.bitcast_convert_type(w << 16, F32)
    b = lax.bitcast_convert_type(w & jnp.uint32(0xFFFF0000), F32)
    return a, b


def _prep_kernel(c_ref, w_ref, b_ref, o_ref):
    c = c_ref[...]
    a = c * _sigmoid(c)
    a_hi = a.astype(BF16)
    a_lo = (a - a_hi.astype(F32)).astype(BF16)
    w = w_ref[...]
    w_hi = w.astype(BF16)
    w_lo = (w - w_hi.astype(F32)).astype(BF16)
    o_ref[...] = _dot(a_hi, w_hi) + _dot(a_lo, w_hi) + _dot(a_hi, w_lo) + b_ref[...]


def _prep(c_all, w_ada, b_ada):
    n, d = c_all.shape
    nout = w_ada.shape[1]
    tn = 1024
    return pl.pallas_call(
        _prep_kernel,
        grid=(nout // tn,),
        in_specs=[pl.BlockSpec((n, d), lambda j: (0, 0)),
                  pl.BlockSpec((d, tn), lambda j: (0, j)),
                  pl.BlockSpec((1, tn), lambda j: (0, j))],
        out_specs=pl.BlockSpec((n, tn), lambda j: (0, j)),
        out_shape=jax.ShapeDtypeStruct((n, nout), F32),
        compiler_params=_cparams("arbitrary"),
        name="prep",
    )(c_all, w_ada, b_ada.reshape(1, nout))


def _disc_kernel(lr_ref, li_ref, ldt_ref, br_ref, bi_ref, ar_ref, ai_ref, bbr_ref, bbi_ref):
    lr = lr_ref[...]
    li = li_ref[...]
    dt = jnp.exp(ldt_ref[...])
    er = jnp.exp(lr * dt)
    ang = li * dt
    ar = er * jnp.cos(ang)
    ai = er * jnp.sin(ang)
    ar_ref[...] = ar
    ai_ref[...] = ai
    den = lr * lr + li * li
    nr = ((ar - 1.0) * lr + ai * li) / den
    ni = (ai * lr - (ar - 1.0) * li) / den
    nr3 = nr[:, None, :]
    ni3 = ni[:, None, :]
    br = br_ref[...]
    bi = bi_ref[...]
    bbr_ref[...] = nr3 * br - ni3 * bi
    bbi_ref[...] = nr3 * bi + ni3 * br


def _disc(lam_re, lam_im, log_dt, bt_re, bt_im):
    g, p = lam_re.shape
    c = bt_re.shape[1]
    return pl.pallas_call(
        _disc_kernel,
        out_shape=(jax.ShapeDtypeStruct((g, p), F32), jax.ShapeDtypeStruct((g, p), F32),
                   jax.ShapeDtypeStruct((g, c, p), F32), jax.ShapeDtypeStruct((g, c, p), F32)),
        name="disc",
    )(lam_re, lam_im, log_dt.reshape(g, 1), bt_re, bt_im)


SCAN_ROWS = 8
STATE_TILES = D_STATE // LANES


def _inproj_kernel(x_ref, sh_ref, sc_ref, g_ref, wm_ref, wf_ref, bf_ref, selq_ref, selk_ref,
                   cq_ref, ck_ref, cv_ref,
                   q_ref, k_ref, v_ref, kf_ref, vf_ref, lf_ref, fb_ref, u_ref, carry_ref, *, tl):
    @pl.when(pl.program_id(1) == 0)
    def _():
        carry_ref[...] = jnp.zeros_like(carry_ref)

    x = x_ref[0]
    h = _rms(x, g_ref[...]) * (1.0 + sc_ref[0]) + sh_ref[0]
    hb = h.astype(BF16)
    main = _dot(hb, wm_ref[...])
    hw = N_HEADS * HEAD_PAD
    qp = main[:, 0:hw] * (HEAD_DIM ** -0.5 * LOG2E)
    kp = main[:, hw:2 * hw]
    vp = main[:, 2 * hw:3 * hw]
    o = 3 * hw
    kf_ref[0] = main[:, o:o + D_ATTN]
    vf_ref[0] = main[:, o + D_ATTN:o + 2 * D_ATTN]
    u_ref[0] = main[:, o + 2 * D_ATTN:o + 2 * D_ATTN + D_SSM]

    fl = _dot(hb, wf_ref[...]) + bf_ref[...]
    lf = jnp.minimum(fl, 0.0) - jnp.log1p(jnp.exp(-jnp.abs(fl)))
    lf_ref[0] = lf[:, 0:N_HEADS]

    row = lax.broadcasted_iota(I32, (tl, tl), 0)
    col = lax.broadcasted_iota(I32, (tl, tl), 1)
    tri = jnp.where(row >= col, 1.0, 0.0).astype(BF16)
    hi, mid, lo = _split3(lf)
    frel = _dot(tri, hi) + _dot(tri, mid) + _dot(tri, lo)
    fb_ref[0, 0] = carry_ref[...] * LOG2E
    carry_ref[...] = carry_ref[...] + frel[tl - 1:tl, :]

    fcat = jnp.concatenate(_split3(frel * LOG2E), axis=-1)
    q_aug = (qp + _dot(fcat, selq_ref[...]) + cq_ref[...]).astype(BF16)
    k_aug = (kp + _dot(fcat, selk_ref[...]) + ck_ref[...]).astype(BF16)
    v_aug = (vp + cv_ref[...]).astype(BF16)
    for hd in range(N_HEADS):
        sl = slice(hd * HEAD_PAD, (hd + 1) * HEAD_PAD)
        q_ref[0, hd] = q_aug[:, sl]
        k_ref[0, hd] = k_aug[:, sl]
        v_ref[0, hd] = v_aug[:, sl]


def _inproj(x, sh, sc, g, wts):
    b, l, d = x.shape
    tl = min(l, SEQ_BLOCK)
    nl = l // tl
    wm, wf, bf, selq, selk, cq, ck, cv = wts
    hm = jax.ShapeDtypeStruct((b, N_HEADS, l, HEAD_PAD), BF16)
    hm_spec = pl.BlockSpec((1, N_HEADS, tl, HEAD_PAD), lambda i, j: (i, 0, j, 0))
    tok = lambda w: pl.BlockSpec((1, tl, w), lambda i, j: (i, j, 0))
    full = lambda a: pl.BlockSpec(a.shape, lambda i, j: (0,) * a.ndim)
    row = pl.BlockSpec((1, 1, d), lambda i, j: (i, 0, 0))
    return pl.pallas_call(
        functools.partial(_inproj_kernel, tl=tl),
        grid=(b, nl),
        in_specs=[tok(d), row, row, full(g), full(wm), full(wf), full(bf), full(selq), full(selk),
                  full(cq), full(ck), full(cv)],
        out_specs=[hm_spec, hm_spec, hm_spec, tok(D_ATTN), tok(D_ATTN), tok(N_HEADS),
                   pl.BlockSpec((1, 1, 1, LANES), lambda i, j: (i, j, 0, 0)), tok(D_SSM)],
        out_shape=[hm, hm, hm,
                   jax.ShapeDtypeStruct((b, l, D_ATTN), F32), jax.ShapeDtypeStruct((b, l, D_ATTN), F32),
                   jax.ShapeDtypeStruct((b, l, N_HEADS), F32),
                   jax.ShapeDtypeStruct((b, nl, 1, LANES), F32),
                   jax.ShapeDtypeStruct((b, l, D_SSM), F32)],
        scratch_shapes=[pltpu.VMEM((1, LANES), F32)],
        compiler_params=_cparams("arbitrary", "arbitrary"),
        name=f"inproj{l}",
    )(x, sh, sc, g, wm, wf, bf, selq, selk, cq, ck, cv)


def _inproj_weights(w_in, b_f):
    d = w_in.shape[0]
    wq, wk, wv = (w_in[:, i * D_ATTN:(i + 1) * D_ATTN] for i in range(3))
    wfl = w_in[:, 3 * D_ATTN:3 * D_ATTN + N_HEADS]
    wu = w_in[:, 3 * D_ATTN + N_HEADS:]

    def pad_heads(w):
        w = w.reshape(d, N_HEADS, HEAD_DIM)
        w = jnp.pad(w, ((0, 0), (0, 0), (0, HEAD_PAD - HEAD_DIM)))
        return w.reshape(d, N_HEADS * HEAD_PAD)

    wm = jnp.concatenate([pad_heads(wq), pad_heads(wk), pad_heads(wv), wk, wv, wu], axis=1).astype(BF16)
    wf = jnp.pad(wfl, ((0, 0), (0, LANES - N_HEADS))).astype(BF16)
    bf = jnp.pad(b_f, (0, LANES - N_HEADS)).reshape(1, LANES).astype(F32)

    hw = N_HEADS * HEAD_PAD
    hd = jnp.arange(N_HEADS)
    selq = jnp.zeros((3 * LANES, hw), F32)
    selk = jnp.zeros((3 * LANES, hw), F32)
    cq = jnp.zeros((1, hw), F32)
    ck = jnp.zeros((1, hw), F32)
    cv = jnp.zeros((1, hw), F32)
    for part in range(3):
        selq = selq.at[part * LANES + hd, hd * HEAD_PAD + HEAD_DIM + part].set(1.0)
        selk = selk.at[part * LANES + hd, hd * HEAD_PAD + HEAD_DIM + 3 + part].set(-1.0)
        cq = cq.at[0, hd * HEAD_PAD + HEAD_DIM + 3 + part].set(1.0)
        ck = ck.at[0, hd * HEAD_PAD + HEAD_DIM + part].set(1.0)
    cv = cv.at[0, hd * HEAD_PAD + HEAD_DIM].set(1.0)
    return wm, wf, bf, selq.astype(BF16), selk.astype(BF16), cq, ck, cv


def _attn_kernel(fb_ref, q_ref, k_ref, v_ref, o_ref, *, r, nblk, hp):
    b = pl.program_id(0)
    g = pl.program_id(1)
    i = pl.program_id(2)
    fbase = (b * nblk) * N_HEADS + g * hp

    def scores(j):
        start = pl.multiple_of(j * r, r)
        return tuple(_dot_nt(q_ref[0, hh], k_ref[0, hh, pl.ds(start, r), :]) for hh in range(hp))

    def absorb(j, s_all, state, masked):
        start = pl.multiple_of(j * r, r)
        out = []
        for hh in range(hp):
            m, acc = state[hh]
            s = s_all[hh]
            if masked:
                row = lax.broadcasted_iota(I32, (r, r), 0)
                col = lax.broadcasted_iota(I32, (r, r), 1)
                s = jnp.where(row >= col, s, -jnp.inf)
            dlt = fb_ref[fbase + hh + i * N_HEADS] - fb_ref[fbase + hh + j * N_HEADS]
            mnew = jnp.maximum(m, jnp.max(s, axis=-1, keepdims=True) + dlt)
            p = jnp.exp2(s - (mnew - dlt))
            acc = jnp.exp2(m - mnew) * acc + _dot(p.astype(BF16), v_ref[0, hh, pl.ds(start, r), :])
            out.append((mnew, acc))
        return tuple(out)

    nu = ATTN_BLOCKS_PER_TRIP

    def several(jj, state):
        js = [nu * jj + u for u in range(nu)]
        ss = [scores(j) for j in js]
        for j, s_all in zip(js, ss):
            state = absorb(j, s_all, state, False)
        return state

    def single(j, state):
        return absorb(j, scores(j), state, False)

    state = tuple((jnp.full((r, 1), -jnp.inf, F32), jnp.zeros((r, HEAD_PAD), F32)) for _ in range(hp))
    state = lax.fori_loop(0, i // nu, several, state)
    state = lax.fori_loop(nu * (i // nu), i, single, state)
    state = absorb(i, scores(i), state, True)
    lane = lax.broadcasted_iota(I32, (r, HEAD_PAD), 1)
    for hh in range(hp):
        acc = state[hh][1]
        out = acc / acc[:, HEAD_DIM:HEAD_DIM + 1]
        o_ref[0, hh] = jnp.where(lane < HEAD_DIM, out, 0.0).astype(BF16)


def _attn(q, k, v, fb):
    b, h, l, _ = q.shape
    r = min(l, SEQ_BLOCK)
    nblk = l // r
    hp = ATTN_HEADS_PER_STEP
    return pl.pallas_call(
        functools.partial(_attn_kernel, r=r, nblk=nblk, hp=hp),
        grid=(b, h // hp, nblk),
        in_specs=[pl.BlockSpec(memory_space=pltpu.SMEM),
                  pl.BlockSpec((1, hp, r, HEAD_PAD), lambda bi, gi, i: (bi, gi, i, 0)),
                  pl.BlockSpec((1, hp, l, HEAD_PAD), lambda bi, gi, i: (bi, gi, 0, 0)),
                  pl.BlockSpec((1, hp, l, HEAD_PAD), lambda bi, gi, i: (bi, gi, 0, 0))],
        out_specs=pl.BlockSpec((1, hp, r, HEAD_PAD), lambda bi, gi, i: (bi, gi, i, 0)),
        out_shape=jax.ShapeDtypeStruct((b, h, l, HEAD_PAD), BF16),
        compiler_params=_cparams("arbitrary", "arbitrary", "arbitrary"),
        name="attn",
    )(fb, q, k, v)


def _attn_dec_kernel(q_ref, k_ref, v_ref, lfc_ref, lfr_ref, o_ref, *, lq, past, lk):
    row = lax.broadcasted_iota(I32, (lk, lk), 0)
    col = lax.broadcasted_iota(I32, (lk, lk), 1)
    tri = jnp.where(row >= col, 1.0, 0.0).astype(BF16)
    upper = jnp.where(row <= col, 1.0, 0.0).astype(BF16)
    c_hi, c_mid, c_lo = _split3(lfc_ref[0])
    fcol = _dot(tri, c_hi) + _dot(tri, c_mid) + _dot(tri, c_lo)
    r_hi, r_mid, r_lo = _split3(lfr_ref[0])
    frow = _dot(r_hi, upper) + _dot(r_mid, upper) + _dot(r_lo, upper)
    fq = fcol[past:past + lq, :]
    qpos = past + lax.broadcasted_iota(I32, (lq, lk), 0)
    kpos = lax.broadcasted_iota(I32, (lq, lk), 1)
    ok = kpos <= qpos
    qlane = lax.broadcasted_iota(I32, (lq, HEAD_PAD), 1)
    klane = lax.broadcasted_iota(I32, (lk, HEAD_PAD), 1)
    for hd in range(N_HEADS):
        odd = hd % 2 == 1
        pair = slice((hd // 2) * HEAD_PAD, (hd // 2 + 1) * HEAD_PAD)
        mine = (klane >= HEAD_DIM) if odd else (klane < HEAD_DIM)
        kh = jnp.where(mine, k_ref[0, :, pair], 0.0).astype(BF16)
        vh = jnp.where(mine, v_ref[0, :, pair], 0.0).astype(BF16)
        qh = jnp.where(qlane < HEAD_DIM, q_ref[0, hd].astype(F32), 0.0)
        if odd:
            qh = pltpu.roll(qh, HEAD_DIM, axis=1)
        s = _dot_nt(qh.astype(BF16), kh) + LOG2E * (fq[:, hd:hd + 1] - frow[hd:hd + 1, :])
        s = jnp.where(ok, s, -jnp.inf)
        m = jnp.max(s, axis=-1, keepdims=True)
        pb = jnp.exp2(s - m).astype(BF16)
        den = jnp.sum(pb.astype(F32), axis=-1, keepdims=True)
        out = _dot(pb, vh) / den
        if odd:
            out = pltpu.roll(out, HEAD_DIM, axis=1)
        o_ref[0, hd] = out.astype(BF16)


def _attn_dec(q, k_all, v_all, lf_cols, lf_rows, past, lq):
    b = q.shape[0]
    lk = k_all.shape[1]
    return pl.pallas_call(
        functools.partial(_attn_dec_kernel, lq=lq, past=past, lk=lk),
        grid=(b,),
        in_specs=[pl.BlockSpec((1, N_HEADS, lq, HEAD_PAD), lambda i: (i, 0, 0, 0)),
                  pl.BlockSpec((1, lk, D_ATTN), lambda i: (i, 0, 0)),
                  pl.BlockSpec((1, lk, D_ATTN), lambda i: (i, 0, 0)),
                  pl.BlockSpec((1, lk, LANES), lambda i: (i, 0, 0)),
                  pl.BlockSpec((1, N_HEADS, lk), lambda i: (i, 0, 0))],
        out_specs=pl.BlockSpec((1, N_HEADS, lq, HEAD_PAD), lambda i: (i, 0, 0, 0)),
        out_shape=jax.ShapeDtypeStruct((b, N_HEADS, lq, HEAD_PAD), BF16),
        compiler_params=_cparams("arbitrary"),
        name="attn_dec",
    )(q, k_all, v_all, lf_cols, lf_rows)


def _ssm_kernel(u_ref, h0r_ref, h0i_ref, br_ref, bi_ref, cc_ref, ar_ref, ai_ref, d_ref, wg_ref, bg_ref,
                gso_ref, y_ref, hr_out, hi_out, cr_ref, ci_ref, xr_s, xi_s, or_s, oi_s, *, tc):
    @pl.when(pl.program_id(1) == 0)
    def _():
        cr_ref[...] = h0r_ref[0]
        ci_ref[...] = h0i_ref[0]

    u = u_ref[0]
    ub = u.astype(BF16)
    hc = D_SSM // 2
    hs = D_STATE // 2

    def drive(b_ref):
        return jnp.concatenate([_dot(ub[:, :hc], b_ref[0:hc, 0:hs]), _dot(ub[:, hc:], b_ref[hc:, hs:])], axis=-1)

    bur = drive(br_ref)
    bui = drive(bi_ref)
    nt = tc // SCAN_ROWS
    nc = STATE_TILES

    def tile_rows(tau, c):
        r0 = (tau * nc + c) * SCAN_ROWS
        return slice(r0, r0 + SCAN_ROWS)

    for tau in range(nt):
        rows = slice(tau * SCAN_ROWS, (tau + 1) * SCAN_ROWS)
        for c in range(nc):
            xr_s[tile_rows(tau, c), :] = bur[rows, c * LANES:(c + 1) * LANES]
            xi_s[tile_rows(tau, c), :] = bui[rows, c * LANES:(c + 1) * LANES]

    ar = ar_ref[...]
    ai = ai_ref[...]
    hr = cr_ref[...]
    hi = ci_ref[...]
    for t in range(tc):
        step = pl.ds((t // SCAN_ROWS) * nc * SCAN_ROWS + t % SCAN_ROWS, nc, stride=SCAN_ROWS)
        hr, hi = ar * hr - ai * hi + xr_s[step, :], ar * hi + ai * hr + xi_s[step, :]
        or_s[step, :] = hr
        oi_s[step, :] = hi
    cr_ref[...] = hr
    ci_ref[...] = hi
    hr_out[0] = hr
    hi_out[0] = hi

    def states(o_s, c0):
        return jnp.concatenate(
            [jnp.concatenate([o_s[tile_rows(tau, c), :] for c in range(c0, c0 + nc // 2)], axis=-1)
             for tau in range(nt)], axis=0).astype(BF16)

    y = jnp.concatenate(
        [_dot(states(or_s, 0), cc_ref[0:hs, 0:hc]) + _dot(states(oi_s, 0), cc_ref[D_STATE:D_STATE + hs, 0:hc]),
         _dot(states(or_s, nc // 2), cc_ref[hs:D_STATE, hc:]) + _dot(states(oi_s, nc // 2), cc_ref[D_STATE + hs:, hc:])],
        axis=-1) + d_ref[...] * u
    gl = 0.5 * y * (1.0 + jnp.tanh(math.sqrt(2.0 / math.pi) * (y + 0.044715 * (y * y * y))))
    z = _dot(gl.astype(BF16), wg_ref[...]) + bg_ref[...]
    out = y * _sigmoid(z)
    y_ref[0] = _rms(out, gso_ref[...]).astype(BF16)


def _ssm(u, h0r, h0i, consts):
    b, l, _ = u.shape
    br, bi, cc, ar, ai, dsk, wg, bg, gso = consts
    tc = min(l, SSM_CHUNK)
    full = lambda a: pl.BlockSpec(a.shape, lambda i, j: (0,) * a.ndim)
    st = pl.BlockSpec((1, STATE_TILES, LANES), lambda i, j: (i, 0, 0))
    state = jax.ShapeDtypeStruct((b, STATE_TILES, LANES), F32)
    carry = pltpu.VMEM((STATE_TILES, LANES), F32)
    tiles = pltpu.VMEM((tc * STATE_TILES, LANES), F32)
    return pl.pallas_call(
        functools.partial(_ssm_kernel, tc=tc),
        grid=(b, l // tc),
        in_specs=[pl.BlockSpec((1, tc, D_SSM), lambda i, j: (i, j, 0)), st, st,
                  full(br), full(bi), full(cc), full(ar), full(ai), full(dsk), full(wg), full(bg), full(gso)],
        out_specs=[pl.BlockSpec((1, tc, D_SSM), lambda i, j: (i, j, 0)), st, st],
        out_shape=[jax.ShapeDtypeStruct((b, l, D_SSM), BF16), state, state],
        scratch_shapes=[carry, carry, tiles, tiles, tiles, tiles],
        compiler_params=_cparams("arbitrary", "arbitrary"),
        name=f"ssm{l}",
    )(u, h0r, h0i, br, bi, cc, ar, ai, dsk, wg, bg, gso)


def _post_kernel(x_ref, a_ref, s_ref, gt1_ref, sh2_ref, sc2_ref, gt2_ref, cnt0_ref, gao_ref, wo_ref, gffn_ref,
                 wrh_ref, wrl_ref, rb_ref, wsgu_ref, wsd_ref,
                 xm_ref, hp_ref, idx_ref, pos_ref, w_ref, cnt_ref, carry_ref, *, tl, first):
    step = pl.program_id(0) * pl.num_programs(1) + pl.program_id(1)

    @pl.when(step == 0)
    def _():
        carry_ref[...] = cnt0_ref[...] if not first else jnp.zeros_like(carry_ref)

    x = x_ref[0]
    attn = jnp.concatenate([a_ref[0, hd] for hd in range(N_HEADS)], axis=-1).astype(F32)
    ms = jnp.sum(attn * attn, axis=-1, keepdims=True) * (1.0 / D_ATTN)
    attn_n = (attn * lax.rsqrt(ms + EPS) * gao_ref[...]).astype(BF16)
    merged = jnp.concatenate([attn_n, s_ref[0]], axis=-1)
    x1 = x + gt1_ref[0] * _dot(merged, wo_ref[...])
    h2 = _rms(x1, gffn_ref[...]) * (1.0 + sc2_ref[0]) + sh2_ref[0]
    d = h2.shape[-1]
    hp_ref[0] = _pack_bf16_pair(h2[:, :d // 2], h2[:, d // 2:])
    h_hi = h2.astype(BF16)
    h_lo = (h2 - h_hi.astype(F32)).astype(BF16)

    gu = _dot(h_hi, wsgu_ref[...])
    ds = gu.shape[-1] // 2
    g = gu[:, :ds]
    act = (g * _sigmoid(g) * gu[:, ds:]).astype(BF16)
    xm_ref[0] = x1 + gt2_ref[0] * _dot(act, wsd_ref[...])

    wrh = wrh_ref[...]
    logits = _dot_nt(wrh, h_hi) + _dot_nt(wrh, h_lo) + _dot_nt(wrl_ref[...], h_hi)
    s = _sigmoid(logits)
    sc = s + rb_ref[...]
    ge = N_EXPERTS // N_EXP_GROUPS
    neg = -jnp.inf
    gi = lax.broadcasted_iota(I32, (ge, tl), 0)
    gsc = []
    for gidx in range(N_EXP_GROUPS):
        blk = sc[gidx * ge:(gidx + 1) * ge, :]
        m1 = jnp.max(blk, axis=0, keepdims=True)
        f1 = jnp.min(jnp.where(blk == m1, gi, ge), axis=0, keepdims=True)
        m2 = jnp.max(jnp.where(gi == f1, neg, blk), axis=0, keepdims=True)
        gsc.append(m1 + m2)
    gwork = jnp.concatenate(gsc, axis=0)
    ni = lax.broadcasted_iota(I32, (N_EXP_GROUPS, tl), 0)
    gsel = jnp.zeros((N_EXP_GROUPS, tl), F32)
    for _ in range(TOPK_GROUPS):
        mx = jnp.max(gwork, axis=0, keepdims=True)
        fi = jnp.min(jnp.where(gwork == mx, ni, N_EXP_GROUPS), axis=0, keepdims=True)
        hit = ni == fi
        gsel = jnp.where(hit, 1.0, gsel)
        gwork = jnp.where(hit, neg, gwork)
    work = jnp.concatenate(
        [jnp.where(gsel[gidx:gidx + 1, :] > 0.0, sc[gidx * ge:(gidx + 1) * ge, :], neg)
         for gidx in range(N_EXP_GROUPS)], axis=0)
    ei = lax.broadcasted_iota(I32, (N_EXPERTS, tl), 0)
    chosen = jnp.zeros((N_EXPERTS, tl), F32)
    idx_rows, w_rows = [], []
    for _ in range(TOP_K):
        mx = jnp.max(work, axis=0, keepdims=True)
        fi = jnp.min(jnp.where(work == mx, ei, N_EXPERTS), axis=0, keepdims=True)
        hit = ei == fi
        w_rows.append(jnp.sum(jnp.where(hit, s, 0.0), axis=0, keepdims=True))
        idx_rows.append(fi)
        chosen = jnp.where(hit, 1.0, chosen)
        work = jnp.where(hit, neg, work)
    wt = jnp.concatenate(w_rows, axis=0)
    wt = wt / jnp.sum(wt, axis=0, keepdims=True) * ROUTED_SCALE
    idx_ref[0] = jnp.concatenate(idx_rows, axis=0)

    trow = lax.broadcasted_iota(I32, (tl, tl), 0)
    tcol = lax.broadcasted_iota(I32, (tl, tl), 1)
    before = jnp.where(trow < tcol, 1.0, 0.0).astype(BF16)
    rank = _dot(chosen.astype(BF16), before) + carry_ref[...]
    pos_rows = [jnp.sum(jnp.where(ei == idx_rows[k], rank, 0.0), axis=0, keepdims=True) for k in range(TOP_K)]
    pos_ref[0] = jnp.concatenate(pos_rows, axis=0).astype(I32)
    carry_ref[...] = carry_ref[...] + jnp.sum(chosen, axis=1, keepdims=True)
    cnt_ref[...] = carry_ref[...]

    eye = jnp.where(trow == tcol, 1.0, 0.0).astype(BF16)
    t_hi, t_mid, t_lo = _split3(wt)
    w_ref[0] = _dot_nt(eye, t_hi) + _dot_nt(eye, t_mid) + _dot_nt(eye, t_lo)


def _post(x, attn, ssm, mods, cnt0, consts, first):
    b, l, d = x.shape
    tl = min(l, SEQ_BLOCK)
    gt1, sh2, sc2, gt2 = mods
    gao, wo, gffn, wrh, wrl, rb, wsgu, wsd = consts
    tok = lambda w: pl.BlockSpec((1, tl, w), lambda i, j: (i, j, 0))
    full = lambda a: pl.BlockSpec(a.shape, lambda i, j: (0,) * a.ndim)
    row = pl.BlockSpec((1, 1, d), lambda i, j: (i, 0, 0))
    tk = pl.BlockSpec((1, TOP_K, tl), lambda i, j: (i, 0, j))
    return pl.pallas_call(
        functools.partial(_post_kernel, tl=tl, first=first),
        grid=(b, l // tl),
        in_specs=[tok(d), pl.BlockSpec((1, N_HEADS, tl, HEAD_PAD), lambda i, j: (i, 0, j, 0)), tok(D_SSM),
                  row, row, row, row, full(cnt0), full(gao), full(wo), full(gffn), full(wrh), full(wrl),
                  full(rb), full(wsgu), full(wsd)],
        out_specs=[tok(d), tok(d // 2), tk, tk, tok(TOP_K), full(cnt0)],
        out_shape=[jax.ShapeDtypeStruct((b, l, d), F32), jax.ShapeDtypeStruct((b, l, d // 2), U32),
                   jax.ShapeDtypeStruct((b, TOP_K, l), I32), jax.ShapeDtypeStruct((b, TOP_K, l), I32),
                   jax.ShapeDtypeStruct((b, l, TOP_K), F32), jax.ShapeDtypeStruct(cnt0.shape, F32)],
        scratch_shapes=[pltpu.VMEM(cnt0.shape, F32)],
        compiler_params=_cparams("arbitrary", "arbitrary"),
        name=f"post{l}",
    )(x, attn, ssm, gt1, sh2, sc2, gt2, cnt0, gao, wo, gffn, wrh, wrl, rb, wsgu, wsd)


def _slots_kernel(pstart_ref, idx_ref, pos_ref, dest_ref):
    idx = idx_ref[0]

    def body(e, base):
        return jnp.where(idx == e, pstart_ref[e], base)

    dest_ref[0] = lax.fori_loop(0, N_EXPERTS, body, jnp.zeros_like(idx)) + pos_ref[0]


def _slots(pstart, idx, pos):
    b, k, l = idx.shape
    tl = min(l, 2048)
    spec = pl.BlockSpec((1, k, tl), lambda i, j: (i, 0, j))
    return pl.pallas_call(
        _slots_kernel,
        grid=(b, l // tl),
        in_specs=[pl.BlockSpec(memory_space=pltpu.SMEM), spec, spec],
        out_specs=spec,
        out_shape=jax.ShapeDtypeStruct(idx.shape, I32),
        compiler_params=_cparams("arbitrary", "arbitrary"),
        name=f"slots{l}",
    )(pstart, idx, pos)


INDEX_WINDOW = 2048
INDEX_ALIGN = 1024


def _experts_kernel(be_ref, nb_ref, jb_ref, tok_ref, h_ref, wg_ref, wu_ref, wd_ref, y_ref,
                    wgu_s, wd_s, x0, x1, idx0, idx1, xsem, isem):
    i = pl.program_id(0)
    nbu = nb_ref[0]
    e = be_ref[i]
    prev = be_ref[jnp.maximum(i - 1, 0)]
    dm = wg_ref.shape[1]
    de = wg_ref.shape[2]
    xbufs = (x0, x1)
    ibufs = (idx0, idx1)

    def window(blk, slot):
        start = pl.multiple_of((jb_ref[blk] >> 10) << 10, INDEX_ALIGN)
        return pltpu.make_async_copy(tok_ref.at[pl.ds(start, INDEX_WINDOW)], ibufs[slot], isem.at[slot])

    def gather_rows(blk, slot):
        base = jb_ref[blk] & (INDEX_ALIGN - 1)
        for r in range(EXPERT_ROWS):
            pltpu.make_async_copy(h_ref.at[pl.ds(ibufs[slot][base + r], 1)], xbufs[slot].at[pl.ds(r, 1)],
                                  xsem.at[slot]).start()

    def drain_rows(slot):
        def body(r, c):
            for _ in range(SCAN_ROWS):
                pltpu.make_async_copy(h_ref.at[pl.ds(0, 1)], xbufs[slot].at[pl.ds(0, 1)], xsem.at[slot]).wait()
            return c
        lax.fori_loop(0, EXPERT_ROWS // SCAN_ROWS, body, 0)

    @pl.when(i >= nbu)
    def _():
        y_ref[...] = jnp.zeros_like(y_ref)

    @pl.when(i < nbu)
    def _():
        last = nbu - 1

        @pl.when(i == 0)
        def _():
            first = window(0, 0)
            first.start()
            first.wait()
            gather_rows(0, 0)
            window(jnp.minimum(1, last), 1).start()

        @pl.when((i == 0) | (e != prev))
        def _():
            wgu_s[:, 0:de] = wg_ref[0].astype(BF16)
            wgu_s[:, de:2 * de] = wu_ref[0].astype(BF16)
            wd_s[...] = wd_ref[0].astype(BF16)

        def run(slot):
            nxt = jnp.minimum(i + 1, last)
            window(nxt, 1 - slot).wait()
            drain_rows(slot)
            gather_rows(nxt, 1 - slot)
            xa, xb = _unpack_bf16_pair(xbufs[slot][...])
            gu = _dot(xa.astype(BF16), wgu_s[0:dm // 2, :]) + _dot(xb.astype(BF16), wgu_s[dm // 2:dm, :])
            g = gu[:, :de]
            act = (g * _sigmoid(g) * gu[:, de:]).astype(BF16)
            y = _dot(act, wd_s[...])
            y_ref[...] = _pack_bf16_pair(y[:, :dm // 2], y[:, dm // 2:])

            @pl.when(i < last)
            def _():
                window(jnp.minimum(i + 2, last), slot).start()

            @pl.when(i == last)
            def _():
                drain_rows(1 - slot)

        for slot in range(2):
            pl.when(i % 2 == slot)(functools.partial(run, slot))


def _experts(blk_e, nb_used, jbase, tok, hp, w_gate, w_up, w_down, rows):
    half = hp.shape[1]
    ne, dm, de = w_gate.shape
    nb = rows // EXPERT_ROWS
    grid_spec = pltpu.PrefetchScalarGridSpec(
        num_scalar_prefetch=3,
        grid=(nb,),
        in_specs=[pl.BlockSpec(memory_space=pl.ANY), pl.BlockSpec(memory_space=pl.ANY),
                  pl.BlockSpec((1, dm, de), lambda i, be, n, jb: (be[i], 0, 0)),
                  pl.BlockSpec((1, dm, de), lambda i, be, n, jb: (be[i], 0, 0)),
                  pl.BlockSpec((1, de, dm), lambda i, be, n, jb: (be[i], 0, 0))],
        out_specs=pl.BlockSpec((EXPERT_ROWS, half), lambda i, be, n, jb: (i, 0)),
        scratch_shapes=[pltpu.VMEM((dm, 2 * de), BF16), pltpu.VMEM((de, dm), BF16),
                        pltpu.VMEM((EXPERT_ROWS, half), U32), pltpu.VMEM((EXPERT_ROWS, half), U32),
                        pltpu.SMEM((INDEX_WINDOW,), I32), pltpu.SMEM((INDEX_WINDOW,), I32),
                        pltpu.SemaphoreType.DMA((2,)), pltpu.SemaphoreType.DMA((2,))],
    )
    return pl.pallas_call(
        _experts_kernel,
        grid_spec=grid_spec,
        out_shape=jax.ShapeDtypeStruct((rows, half), U32),
        compiler_params=_cparams("arbitrary"),
        name="experts",
    )(blk_e, nb_used, jbase, tok, hp, w_gate, w_up, w_down)


def _combine_kernel(dcur_ref, dnext_ref, ys_ref, w_ref, xm_ref, gt2_ref, gfin_ref, o_ref, buf0, buf1, sem, *, tl):
    bufs = (buf0, buf1)
    half = buf0.shape[-1]
    s = pl.program_id(0)
    last = pl.num_programs(0) - 1
    group = SCAN_ROWS
    ngroups = tl // group

    def gather(dref, g, to):
        for tt in range(group):
            for k in range(TOP_K):
                row = dref[(g * group + tt) * TOP_K + k]
                pltpu.make_async_copy(ys_ref.at[pl.ds(row, 1)], bufs[to].at[k, g, pl.ds(tt, 1)], sem.at[to]).start()

    def drain(which):
        def body(t, c):
            for k in range(TOP_K):
                pltpu.make_async_copy(ys_ref.at[pl.ds(0, 1)], bufs[which].at[0, 0, pl.ds(0, 1)], sem.at[which]).wait()
            return c
        lax.fori_loop(0, tl, body, 0)

    @pl.when(s == 0)
    def _():
        def body(g, c):
            gather(dcur_ref, g, 0)
            return c
        lax.fori_loop(0, ngroups, body, 0)

    gt2 = gt2_ref[0]
    gfin = gfin_ref[...]

    def run(slot):
        drain(slot)

        def body(g, c):
            gather(dnext_ref, g, 1 - slot)
            t0 = pl.multiple_of(g * group, group)
            w = w_ref[pl.ds(t0, group), :]
            acc_a = jnp.zeros((group, half), F32)
            acc_b = jnp.zeros((group, half), F32)
            for k in range(TOP_K):
                ya, yb = _unpack_bf16_pair(bufs[slot][k, g])
                wk = w[:, k:k + 1]
                acc_a = acc_a + wk * ya
                acc_b = acc_b + wk * yb
            routed = jnp.concatenate([acc_a, acc_b], axis=-1)
            x2 = xm_ref[pl.ds(t0, group), :] + gt2 * routed
            o_ref[pl.ds(t0, group), :] = _rms(x2, gfin)
            return c

        lax.fori_loop(0, ngroups, body, 0)

        @pl.when(s == last)
        def _():
            drain(1 - slot)

    for slot in range(2):
        pl.when(s % 2 == slot)(functools.partial(run, slot))


def _combine(dest, ys, w, xm, gt2, gfin):
    b, l, d = xm.shape
    tl = min(l, COMBINE_ROWS)
    nl = l // tl
    n = b * nl
    half = ys.shape[1]
    out = pl.pallas_call(
        functools.partial(_combine_kernel, tl=tl),
        grid=(n,),
        in_specs=[pl.BlockSpec((tl * TOP_K,), lambda s: (s,), memory_space=pltpu.SMEM),
                  pl.BlockSpec((tl * TOP_K,), lambda s: (jnp.minimum(s + 1, n - 1),), memory_space=pltpu.SMEM),
                  pl.BlockSpec(memory_space=pl.ANY),
                  pl.BlockSpec((tl, TOP_K), lambda s: (s, 0)),
                  pl.BlockSpec((tl, d), lambda s: (s, 0)),
                  pl.BlockSpec((1, 1, d), lambda s: (s // nl, 0, 0)),
                  pl.BlockSpec(gfin.shape, lambda s: (0, 0))],
        out_specs=pl.BlockSpec((tl, d), lambda s: (s, 0)),
        out_shape=jax.ShapeDtypeStruct((b * l, d), F32),
        scratch_shapes=[pltpu.VMEM((TOP_K, tl // SCAN_ROWS, SCAN_ROWS, half), U32),
                        pltpu.VMEM((TOP_K, tl // SCAN_ROWS, SCAN_ROWS, half), U32),
                        pltpu.SemaphoreType.DMA((2,))],
        compiler_params=_cparams("arbitrary"),
        name=f"combine{l}",
    )(dest, dest, ys, w.reshape(b * l, TOP_K), xm.reshape(b * l, d), gt2, gfin)
    return out.reshape(b, l, d)


def _block_diag(m):
    g, a, b = m.shape
    eye = jnp.eye(g, dtype=m.dtype)
    return jnp.einsum("gab,gh->gahb", m, eye).reshape(g * a, g * b)


def kernel(x_prompt, x_sample, c_prompt, c_sample, cache_k, cache_v, cache_logf, state_ssm_re, state_ssm_im,
           w_ada, b_ada, g_mix, w_in, b_f, lam_re, lam_im, log_dt, ssm_b_re, ssm_b_im, ssm_c_re, ssm_c_im,
           ssm_d, w_glu, b_glu, g_attn_out, g_ssm_out, w_out, g_ffn, w_router, router_bias,
           w_gate, w_up, w_down, ws_gate, ws_up, ws_down, g_final):
    assert w_ada.shape[0] == 1, "single layer"
    bp, lp, d = x_prompt.shape
    bs, ls, _ = x_sample.shape
    past = cache_k.shape[2]

    nb = bp + bs
    nbp = -(-nb // 8) * 8
    c_all = jnp.concatenate([c_prompt, c_sample, jnp.zeros((nbp - nb, d), F32)], axis=0)
    mod = _prep(c_all, w_ada[0], b_ada[0]).reshape(nbp, 6, 1, d)
    mod_p = [mod[:bp, i] for i in range(6)]
    mod_s = [mod[bp:nb, i] for i in range(6)]

    a_re, a_im, bb_re, bb_im = _disc(lam_re[0], lam_im[0], log_dt[0],
                                     jnp.swapaxes(ssm_b_re[0], 1, 2), jnp.swapaxes(ssm_b_im[0], 1, 2))
    a_re = a_re.reshape(STATE_TILES, LANES)
    a_im = a_im.reshape(STATE_TILES, LANES)
    bd_re = _block_diag(bb_re).astype(BF16)
    bd_im = _block_diag(bb_im).astype(BF16)
    cc = jnp.concatenate([_block_diag(jnp.swapaxes(ssm_c_re[0], 1, 2)),
                          -_block_diag(jnp.swapaxes(ssm_c_im[0], 1, 2))], axis=0).astype(BF16)
    ssm_tail = (ssm_d[0].reshape(1, D_SSM), w_glu[0].astype(BF16), b_glu[0].reshape(1, D_SSM),
                g_ssm_out[0].reshape(1, D_SSM))

    ssm_consts = (bd_re, bd_im, cc, a_re, a_im) + ssm_tail

    inw = _inproj_weights(w_in[0], b_f[0])
    g_mix2 = g_mix[0].reshape(1, d)

    wo = w_out[0]
    wo_attn = jnp.pad(wo[:D_ATTN].reshape(N_HEADS, HEAD_DIM, d), ((0, 0), (0, HEAD_PAD - HEAD_DIM), (0, 0)))
    wo_pad = jnp.concatenate([wo_attn.reshape(N_HEADS * HEAD_PAD, d), wo[D_ATTN:]], axis=0).astype(BF16)
    gao_pad = jnp.pad(g_attn_out[0].reshape(N_HEADS, HEAD_DIM),
                      ((0, 0), (0, HEAD_PAD - HEAD_DIM))).reshape(1, N_HEADS * HEAD_PAD)
    wr_t = w_router[0].T
    wr_hi = wr_t.astype(BF16)
    wr_lo = (wr_t - wr_hi.astype(F32)).astype(BF16)
    post_consts = (gao_pad, wo_pad, g_ffn[0].reshape(1, d), wr_hi, wr_lo, router_bias[0].reshape(N_EXPERTS, 1),
                   jnp.concatenate([ws_gate[0], ws_up[0]], axis=1).astype(BF16), ws_down[0].astype(BF16))

    def mixer(x, modv, attn_fn, h0r, h0i):
        sh1, sc1, gt1, sh2, sc2, gt2 = modv
        q, k, v, kf, vf, lf, fb, u = _inproj(x, sh1, sc1, g_mix2, inw)
        attn = attn_fn(q, k, v, kf, vf, lf, fb)
        ssm, hr, hi = _ssm(u, h0r, h0i, ssm_consts)
        return attn, ssm, (kf, vf, lf, hr, hi), (gt1, sh2, sc2, gt2)

    def attn_prompt(q, k, v, kf, vf, lf, fb):
        return _attn(q, k, v, fb[:, :, 0, :N_HEADS].reshape(-1))

    zeros_p = jnp.zeros((bp, STATE_TILES, LANES), F32)
    attn_p, ssm_p, new_p, m_p = mixer(x_prompt, mod_p, attn_prompt, zeros_p, zeros_p)

    lk = -(-(past + ls) // LANES) * LANES
    padk = lk - past - ls

    def attn_sample(q, k, v, kf, vf, lf, fb):
        k_all = jnp.concatenate([cache_k[0].reshape(bs, past, D_ATTN), kf,
                                 jnp.zeros((bs, padk, D_ATTN), F32)], axis=1)
        v_all = jnp.concatenate([cache_v[0].reshape(bs, past, D_ATTN), vf,
                                 jnp.zeros((bs, padk, D_ATTN), F32)], axis=1)
        lf_all = jnp.concatenate([cache_logf[0], lf, jnp.zeros((bs, padk, N_HEADS), F32)], axis=1)
        lf_cols = jnp.pad(lf_all, ((0, 0), (0, 0), (0, LANES - N_HEADS)))
        lf_rows = jnp.swapaxes(lf_all, 1, 2)
        return _attn_dec(q, k_all, v_all, lf_cols, lf_rows, past, ls)

    attn_s, ssm_s, new_s, m_s = mixer(x_sample, mod_s, attn_sample,
                                      state_ssm_re[0].reshape(bs, STATE_TILES, LANES),
                                      state_ssm_im[0].reshape(bs, STATE_TILES, LANES))

    cnt0 = jnp.zeros((N_EXPERTS, 1), F32)
    xm_p, hp_p, idx_p, pos_p, w_p, cnt1 = _post(x_prompt, attn_p, ssm_p, m_p, cnt0, post_consts, True)
    xm_s, hp_s, idx_s, pos_s, w_s, cnt2 = _post(x_sample, attn_s, ssm_s, m_s, cnt1, post_consts, False)

    counts = cnt2[:, 0].astype(I32)
    padded = (counts + EXPERT_ROWS - 1) // EXPERT_ROWS * EXPERT_ROWS
    pend = jnp.cumsum(padded)
    pstart = pend - padded
    tp, ts = bp * lp, bs * ls
    n_blocks = -(-((tp + ts) * TOP_K) // EXPERT_ROWS) + N_EXPERTS
    rows = n_blocks * EXPERT_ROWS
    dest_p = _slots(pstart, idx_p, pos_p)
    dest_s = _slots(pstart, idx_s, pos_s)
    nb_used = (pend[-1] // EXPERT_ROWS).astype(I32).reshape(1)
    blk_row = jnp.minimum(jnp.arange(n_blocks, dtype=I32), nb_used[0] - 1) * EXPERT_ROWS
    blk_e = jnp.minimum(jnp.sum((pend[None, :] <= blk_row[:, None]).astype(I32), axis=1), N_EXPERTS - 1)

    half = d // 2
    dflat_p = jnp.swapaxes(dest_p, 1, 2).reshape(-1)
    dflat_s = jnp.swapaxes(dest_s, 1, 2).reshape(-1)
    n_pairs = (tp + ts) * TOP_K
    order = jnp.argsort(jnp.concatenate([dflat_p, dflat_s])).astype(I32)
    n_tok = ((n_pairs >> 10) + 3) << 10
    tok = jnp.concatenate([order // TOP_K, jnp.zeros((n_tok - n_pairs,), I32)])
    cstart = jnp.cumsum(counts) - counts
    blk_ids = jnp.arange(n_blocks, dtype=I32)
    jbase = jnp.where(blk_ids < nb_used[0], cstart[blk_e] + blk_ids * EXPERT_ROWS - pstart[blk_e], 0).astype(I32)
    hp_all = jnp.concatenate([hp_p.reshape(tp, half), hp_s.reshape(ts, half)], axis=0)
    ys = _experts(blk_e, nb_used, jbase, tok, hp_all, w_gate[0], w_up[0], w_down[0], rows)

    gfin = g_final.reshape(1, d)
    y_p = _combine(dflat_p, ys, w_p, xm_p, m_p[3], gfin)
    y_s = _combine(dflat_s, ys, w_s, xm_s, m_s[3], gfin)

    def pack(new, b, l):
        kf, vf, lf, hr, hi = new
        return (kf.reshape(1, b, l, N_HEADS, HEAD_DIM), vf.reshape(1, b, l, N_HEADS, HEAD_DIM),
                lf.reshape(1, b, l, N_HEADS),
                hr.reshape(1, b, N_SSM_GROUPS, SSM_STATE), hi.reshape(1, b, N_SSM_GROUPS, SSM_STATE))

    return (y_p, y_s) + pack(new_p, bp, lp) + pack(new_s, bs, ls)
```

```python
import functools
import math

import jax
import jax.numpy as jnp
from jax import lax
from jax.experimental import pallas as pl
from jax.experimental.pallas import tpu as pltpu

F32 = jnp.float32
BF16 = jnp.bfloat16
U32 = jnp.uint32
I32 = jnp.int32

N_HEADS = 8
HEAD_DIM = 64
HEAD_PAD = 128
D_ATTN = N_HEADS * HEAD_DIM
SSM_GROUP = 16
N_SSM_GROUPS = 32
SSM_STATE = 64
D_SSM = SSM_GROUP * N_SSM_GROUPS
D_STATE = N_SSM_GROUPS * SSM_STATE
N_EXPERTS = 256
TOP_K = 8
N_EXP_GROUPS = 8
TOPK_GROUPS = 4
ROUTED_SCALE = 2.5
EPS = 1e-6
LANES = 128

SEQ_BLOCK = 512
SSM_CHUNK = 512
EXPERT_ROWS = 512
ATTN_HEADS_PER_STEP = 2
ATTN_BLOCKS_PER_TRIP = 4
LOG2E = math.log2(math.e)
COMBINE_ROWS = 256
DISPATCH_ROWS = 512
VMEM_LIMIT = 56 * 1024 * 1024

_NT = (((1,), (1,)), ((), ()))


def _cparams(*sem):
    return pltpu.CompilerParams(dimension_semantics=sem, vmem_limit_bytes=VMEM_LIMIT)


def _split3(x):
    hi = x.astype(BF16)
    r = x - hi.astype(F32)
    mid = r.astype(BF16)
    lo = (r - mid.astype(F32)).astype(BF16)
    return hi, mid, lo


def _dot(a, b):
    return jnp.dot(a, b, preferred_element_type=F32)


def _dot_nt(a, b):
    return lax.dot_general(a, b, _NT, preferred_element_type=F32)


def _rms(x, g):
    return x * lax.rsqrt(jnp.mean(x * x, axis=-1, keepdims=True) + EPS) * g


def _sigmoid(x):
    return 1.0 / (1.0 + jnp.exp(-x))


def _pack_bf16_pair(a, b):
    ab = lax.bitcast_convert_type(a.astype(BF16).astype(F32), U32)
    bb = lax.bitcast_convert_type(b.astype(BF16).astype(F32), U32)
    return (ab >> 16) | (bb & jnp.uint32(0xFFFF0000))


def _unpack_bf16_pair(w):
    a = lax.bitcast_convert_type(w << 16, F32)
    b = lax.bitcast_convert_type(w & jnp.uint32(0xFFFF0000), F32)
    return a, b


def _store_slabs(ref, val):
    n, width = val.shape
    tiles = width // LANES
    for c in range(tiles):
        ref[pl.ds(c, n, stride=tiles), :] = val[:, c * LANES:(c + 1) * LANES]


def _load_slabs(ref, n):
    tiles = ref.shape[0] // n
    return jnp.concatenate([ref[pl.ds(c, n, stride=tiles), :] for c in range(tiles)], axis=-1)


def _prep_kernel(c_ref, w_ref, b_ref, o_ref):
    c = c_ref[...]
    a = c * _sigmoid(c)
    a_hi = a.astype(BF16)
    a_lo = (a - a_hi.astype(F32)).astype(BF16)
    w = w_ref[...]
    w_hi = w.astype(BF16)
    w_lo = (w - w_hi.astype(F32)).astype(BF16)
    o_ref[...] = _dot(a_hi, w_hi) + _dot(a_lo, w_hi) + _dot(a_hi, w_lo) + b_ref[...]


def _prep(c_all, w_ada, b_ada):
    n, d = c_all.shape
    nout = w_ada.shape[1]
    tn = 1024
    return pl.pallas_call(
        _prep_kernel,
        grid=(nout // tn,),
        in_specs=[pl.BlockSpec((n, d), lambda j: (0, 0)),
                  pl.BlockSpec((d, tn), lambda j: (0, j)),
                  pl.BlockSpec((1, tn), lambda j: (0, j))],
        out_specs=pl.BlockSpec((n, tn), lambda j: (0, j)),
        out_shape=jax.ShapeDtypeStruct((n, nout), F32),
        compiler_params=_cparams("arbitrary"),
        name="prep",
    )(c_all, w_ada, b_ada.reshape(1, nout))


def _disc_kernel(lr_ref, li_ref, ldt_ref, br_ref, bi_ref, ar_ref, ai_ref, bbr_ref, bbi_ref):
    lr = lr_ref[...]
    li = li_ref[...]
    dt = jnp.exp(ldt_ref[...])
    er = jnp.exp(lr * dt)
    ang = li * dt
    ar = er * jnp.cos(ang)
    ai = er * jnp.sin(ang)
    ar_ref[...] = ar
    ai_ref[...] = ai
    den = lr * lr + li * li
    nr = ((ar - 1.0) * lr + ai * li) / den
    ni = (ai * lr - (ar - 1.0) * li) / den
    nr3 = nr[:, None, :]
    ni3 = ni[:, None, :]
    br = br_ref[...]
    bi = bi_ref[...]
    bbr_ref[...] = nr3 * br - ni3 * bi
    bbi_ref[...] = nr3 * bi + ni3 * br


def _disc(lam_re, lam_im, log_dt, bt_re, bt_im):
    g, p = lam_re.shape
    c = bt_re.shape[1]
    return pl.pallas_call(
        _disc_kernel,
        out_shape=(jax.ShapeDtypeStruct((g, p), F32), jax.ShapeDtypeStruct((g, p), F32),
                   jax.ShapeDtypeStruct((g, c, p), F32), jax.ShapeDtypeStruct((g, c, p), F32)),
        name="disc",
    )(lam_re, lam_im, log_dt.reshape(g, 1), bt_re, bt_im)


SCAN_ROWS = 8
STATE_TILES = D_STATE // LANES


def _inproj_kernel(x_ref, sh_ref, sc_ref, g_ref, wm_ref, wf_ref, bf_ref, selq_ref, selk_ref,
                   cq_ref, ck_ref, cv_ref,
                   q_ref, k_ref, v_ref, kf_ref, vf_ref, lf_ref, fb_ref, u_ref, carry_ref, *, tl):
    @pl.when(pl.program_id(1) == 0)
    def _():
        carry_ref[...] = jnp.zeros_like(carry_ref)

    x = x_ref[0]
    h = _rms(x, g_ref[...]) * (1.0 + sc_ref[0]) + sh_ref[0]
    hb = h.astype(BF16)
    main = _dot(hb, wm_ref[...])
    hw = N_HEADS * HEAD_PAD
    qp = main[:, 0:hw] * (HEAD_DIM ** -0.5 * LOG2E)
    kp = main[:, hw:2 * hw]
    vp = main[:, 2 * hw:3 * hw]
    o = 3 * hw
    kf_ref[0] = main[:, o:o + D_ATTN]
    vf_ref[0] = main[:, o + D_ATTN:o + 2 * D_ATTN]
    u_ref[0] = main[:, o + 2 * D_ATTN:o + 2 * D_ATTN + D_SSM]

    fl = _dot(hb, wf_ref[...]) + bf_ref[...]
    lf = jnp.minimum(fl, 0.0) - jnp.log1p(jnp.exp(-jnp.abs(fl)))
    lf_ref[0] = lf[:, 0:N_HEADS]

    row = lax.broadcasted_iota(I32, (tl, tl), 0)
    col = lax.broadcasted_iota(I32, (tl, tl), 1)
    tri = jnp.where(row >= col, 1.0, 0.0).astype(BF16)
    hi, mid, lo = _split3(lf)
    frel = _dot(tri, hi) + _dot(tri, mid) + _dot(tri, lo)
    fb_ref[0, 0] = carry_ref[...] * LOG2E
    carry_ref[...] = carry_ref[...] + frel[tl - 1:tl, :]

    fcat = jnp.concatenate(_split3(frel * LOG2E), axis=-1)
    q_aug = (qp + _dot(fcat, selq_ref[...]) + cq_ref[...]).astype(BF16)
    k_aug = (kp + _dot(fcat, selk_ref[...]) + ck_ref[...]).astype(BF16)
    v_aug = (vp + cv_ref[...]).astype(BF16)
    for hd in range(N_HEADS):
        sl = slice(hd * HEAD_PAD, (hd + 1) * HEAD_PAD)
        q_ref[0, hd] = q_aug[:, sl]
        k_ref[0, hd] = k_aug[:, sl]
        v_ref[0, hd] = v_aug[:, sl]


def _inproj(x, sh, sc, g, wts):
    b, l, d = x.shape
    tl = min(l, SEQ_BLOCK)
    nl = l // tl
    wm, wf, bf, selq, selk, cq, ck, cv = wts
    hm = jax.ShapeDtypeStruct((b, N_HEADS, l, HEAD_PAD), BF16)
    hm_spec = pl.BlockSpec((1, N_HEADS, tl, HEAD_PAD), lambda i, j: (i, 0, j, 0))
    tok = lambda w: pl.BlockSpec((1, tl, w), lambda i, j: (i, j, 0))
    full = lambda a: pl.BlockSpec(a.shape, lambda i, j: (0,) * a.ndim)
    row = pl.BlockSpec((1, 1, d), lambda i, j: (i, 0, 0))
    return pl.pallas_call(
        functools.partial(_inproj_kernel, tl=tl),
        grid=(b, nl),
        in_specs=[tok(d), row, row, full(g), full(wm), full(wf), full(bf), full(selq), full(selk),
                  full(cq), full(ck), full(cv)],
        out_specs=[hm_spec, hm_spec, hm_spec, tok(D_ATTN), tok(D_ATTN), tok(N_HEADS),
                   pl.BlockSpec((1, 1, 1, LANES), lambda i, j: (i, j, 0, 0)), tok(D_SSM)],
        out_shape=[hm, hm, hm,
                   jax.ShapeDtypeStruct((b, l, D_ATTN), F32), jax.ShapeDtypeStruct((b, l, D_ATTN), F32),
                   jax.ShapeDtypeStruct((b, l, N_HEADS), F32),
                   jax.ShapeDtypeStruct((b, nl, 1, LANES), F32),
                   jax.ShapeDtypeStruct((b, l, D_SSM), F32)],
        scratch_shapes=[pltpu.VMEM((1, LANES), F32)],
        compiler_params=_cparams("arbitrary", "arbitrary"),
        name=f"inproj{l}",
    )(x, sh, sc, g, wm, wf, bf, selq, selk, cq, ck, cv)


def _inproj_weights(w_in, b_f):
    d = w_in.shape[0]
    wq, wk, wv = (w_in[:, i * D_ATTN:(i + 1) * D_ATTN] for i in range(3))
    wfl = w_in[:, 3 * D_ATTN:3 * D_ATTN + N_HEADS]
    wu = w_in[:, 3 * D_ATTN + N_HEADS:]

    def pad_heads(w):
        w = w.reshape(d, N_HEADS, HEAD_DIM)
        w = jnp.pad(w, ((0, 0), (0, 0), (0, HEAD_PAD - HEAD_DIM)))
        return w.reshape(d, N_HEADS * HEAD_PAD)

    wm = jnp.concatenate([pad_heads(wq), pad_heads(wk), pad_heads(wv), wk, wv, wu], axis=1).astype(BF16)
    wf = jnp.pad(wfl, ((0, 0), (0, LANES - N_HEADS))).astype(BF16)
    bf = jnp.pad(b_f, (0, LANES - N_HEADS)).reshape(1, LANES).astype(F32)

    hw = N_HEADS * HEAD_PAD
    hd = jnp.arange(N_HEADS)
    selq = jnp.zeros((3 * LANES, hw), F32)
    selk = jnp.zeros((3 * LANES, hw), F32)
    cq = jnp.zeros((1, hw), F32)
    ck = jnp.zeros((1, hw), F32)
    cv = jnp.zeros((1, hw), F32)
    for part in range(3):
        selq = selq.at[part * LANES + hd, hd * HEAD_PAD + HEAD_DIM + part].set(1.0)
        selk = selk.at[part * LANES + hd, hd * HEAD_PAD + HEAD_DIM + 3 + part].set(-1.0)
        cq = cq.at[0, hd * HEAD_PAD + HEAD_DIM + 3 + part].set(1.0)
        ck = ck.at[0, hd * HEAD_PAD + HEAD_DIM + part].set(1.0)
    cv = cv.at[0, hd * HEAD_PAD + HEAD_DIM].set(1.0)
    return wm, wf, bf, selq.astype(BF16), selk.astype(BF16), cq, ck, cv


def _attn_kernel(fb_ref, q_ref, k_ref, v_ref, o_ref, *, r, nblk, hp):
    b = pl.program_id(0)
    g = pl.program_id(1)
    i = pl.program_id(2)
    fbase = (b * nblk) * N_HEADS + g * hp

    def scores(j):
        start = pl.multiple_of(j * r, r)
        return tuple(_dot_nt(q_ref[0, hh], k_ref[0, hh, pl.ds(start, r), :]) for hh in range(hp))

    def absorb(j, s_all, state, masked):
        start = pl.multiple_of(j * r, r)
        out = []
        for hh in range(hp):
            m, acc = state[hh]
            s = s_all[hh]
            if masked:
                row = lax.broadcasted_iota(I32, (r, r), 0)
                col = lax.broadcasted_iota(I32, (r, r), 1)
                s = jnp.where(row >= col, s, -jnp.inf)
            dlt = fb_ref[fbase + hh + i * N_HEADS] - fb_ref[fbase + hh + j * N_HEADS]
            mnew = jnp.maximum(m, jnp.max(s, axis=-1, keepdims=True) + dlt)
            p = jnp.exp2(s - (mnew - dlt))
            acc = jnp.exp2(m - mnew) * acc + _dot(p.astype(BF16), v_ref[0, hh, pl.ds(start, r), :])
            out.append((mnew, acc))
        return tuple(out)

    nu = ATTN_BLOCKS_PER_TRIP

    def several(jj, state):
        js = [nu * jj + u for u in range(nu)]
        ss = [scores(j) for j in js]
        for j, s_all in zip(js, ss):
            state = absorb(j, s_all, state, False)
        return state

    def single(j, state):
        return absorb(j, scores(j), state, False)

    state = tuple((jnp.full((r, 1), -jnp.inf, F32), jnp.zeros((r, HEAD_PAD), F32)) for _ in range(hp))
    state = lax.fori_loop(0, i // nu, several, state)
    state = lax.fori_loop(nu * (i // nu), i, single, state)
    state = absorb(i, scores(i), state, True)
    lane = lax.broadcasted_iota(I32, (r, HEAD_PAD), 1)
    for hh in range(hp):
        acc = state[hh][1]
        out = acc / acc[:, HEAD_DIM:HEAD_DIM + 1]
        o_ref[0, hh] = jnp.where(lane < HEAD_DIM, out, 0.0).astype(BF16)


def _attn(q, k, v, fb):
    b, h, l, _ = q.shape
    r = min(l, SEQ_BLOCK)
    nblk = l // r
    hp = ATTN_HEADS_PER_STEP
    return pl.pallas_call(
        functools.partial(_attn_kernel, r=r, nblk=nblk, hp=hp),
        grid=(b, h // hp, nblk),
        in_specs=[pl.BlockSpec(memory_space=pltpu.SMEM),
                  pl.BlockSpec((1, hp, r, HEAD_PAD), lambda bi, gi, i: (bi, gi, i, 0)),
                  pl.BlockSpec((1, hp, l, HEAD_PAD), lambda bi, gi, i: (bi, gi, 0, 0)),
                  pl.BlockSpec((1, hp, l, HEAD_PAD), lambda bi, gi, i: (bi, gi, 0, 0))],
        out_specs=pl.BlockSpec((1, hp, r, HEAD_PAD), lambda bi, gi, i: (bi, gi, i, 0)),
        out_shape=jax.ShapeDtypeStruct((b, h, l, HEAD_PAD), BF16),
        compiler_params=_cparams("arbitrary", "arbitrary", "arbitrary"),
        name="attn",
    )(fb, q, k, v)


def _attn_dec_kernel(q_ref, k_ref, v_ref, lfc_ref, lfr_ref, o_ref, *, lq, past, lk):
    row = lax.broadcasted_iota(I32, (lk, lk), 0)
    col = lax.broadcasted_iota(I32, (lk, lk), 1)
    tri = jnp.where(row >= col, 1.0, 0.0).astype(BF16)
    upper = jnp.where(row <= col, 1.0, 0.0).astype(BF16)
    c_hi, c_mid, c_lo = _split3(lfc_ref[0])
    fcol = _dot(tri, c_hi) + _dot(tri, c_mid) + _dot(tri, c_lo)
    r_hi, r_mid, r_lo = _split3(lfr_ref[0])
    frow = _dot(r_hi, upper) + _dot(r_mid, upper) + _dot(r_lo, upper)
    fq = fcol[past:past + lq, :]
    qpos = past + lax.broadcasted_iota(I32, (lq, lk), 0)
    kpos = lax.broadcasted_iota(I32, (lq, lk), 1)
    ok = kpos <= qpos
    qlane = lax.broadcasted_iota(I32, (lq, HEAD_PAD), 1)
    klane = lax.broadcasted_iota(I32, (lk, HEAD_PAD), 1)
    for hd in range(N_HEADS):
        odd = hd % 2 == 1
        pair = slice((hd // 2) * HEAD_PAD, (hd // 2 + 1) * HEAD_PAD)
        mine = (klane >= HEAD_DIM) if odd else (klane < HEAD_DIM)
        kh = jnp.where(mine, k_ref[0, :, pair], 0.0).astype(BF16)
        vh = jnp.where(mine, v_ref[0, :, pair], 0.0).astype(BF16)
        qh = jnp.where(qlane < HEAD_DIM, q_ref[0, hd].astype(F32), 0.0)
        if odd:
            qh = pltpu.roll(qh, HEAD_DIM, axis=1)
        s = _dot_nt(qh.astype(BF16), kh) + LOG2E * (fq[:, hd:hd + 1] - frow[hd:hd + 1, :])
        s = jnp.where(ok, s, -jnp.inf)
        m = jnp.max(s, axis=-1, keepdims=True)
        pb = jnp.exp2(s - m).astype(BF16)
        den = jnp.sum(pb.astype(F32), axis=-1, keepdims=True)
        out = _dot(pb, vh) / den
        if odd:
            out = pltpu.roll(out, HEAD_DIM, axis=1)
        o_ref[0, hd] = out.astype(BF16)


def _attn_dec(q, k_all, v_all, lf_cols, lf_rows, past, lq):
    b = q.shape[0]
    lk = k_all.shape[1]
    return pl.pallas_call(
        functools.partial(_attn_dec_kernel, lq=lq, past=past, lk=lk),
        grid=(b,),
        in_specs=[pl.BlockSpec((1, N_HEADS, lq, HEAD_PAD), lambda i: (i, 0, 0, 0)),
                  pl.BlockSpec((1, lk, D_ATTN), lambda i: (i, 0, 0)),
                  pl.BlockSpec((1, lk, D_ATTN), lambda i: (i, 0, 0)),
                  pl.BlockSpec((1, lk, LANES), lambda i: (i, 0, 0)),
                  pl.BlockSpec((1, N_HEADS, lk), lambda i: (i, 0, 0))],
        out_specs=pl.BlockSpec((1, N_HEADS, lq, HEAD_PAD), lambda i: (i, 0, 0, 0)),
        out_shape=jax.ShapeDtypeStruct((b, N_HEADS, lq, HEAD_PAD), BF16),
        compiler_params=_cparams("arbitrary"),
        name="attn_dec",
    )(q, k_all, v_all, lf_cols, lf_rows)


def _ssm_kernel(u_ref, h0r_ref, h0i_ref, br_ref, bi_ref, cc_ref, ar_ref, ai_ref, d_ref, wg_ref, bg_ref,
                gso_ref, y_ref, hr_out, hi_out, cr_ref, ci_ref, xr_s, xi_s, or_s, oi_s, *, tc):
    @pl.when(pl.program_id(1) == 0)
    def _():
        cr_ref[...] = h0r_ref[0]
        ci_ref[...] = h0i_ref[0]

    u = u_ref[0]
    ub = u.astype(BF16)
    hc = D_SSM // 2
    hs = D_STATE // 2

    def drive(b_ref):
        return jnp.concatenate([_dot(ub[:, :hc], b_ref[0:hc, 0:hs]), _dot(ub[:, hc:], b_ref[hc:, hs:])], axis=-1)

    bur = drive(br_ref)
    bui = drive(bi_ref)
    nt = tc // SCAN_ROWS
    nc = STATE_TILES

    def tile_rows(tau, c):
        r0 = (tau * nc + c) * SCAN_ROWS
        return slice(r0, r0 + SCAN_ROWS)

    for tau in range(nt):
        rows = slice(tau * SCAN_ROWS, (tau + 1) * SCAN_ROWS)
        for c in range(nc):
            xr_s[tile_rows(tau, c), :] = bur[rows, c * LANES:(c + 1) * LANES]
            xi_s[tile_rows(tau, c), :] = bui[rows, c * LANES:(c + 1) * LANES]

    ar = ar_ref[...]
    ai = ai_ref[...]
    hr = cr_ref[...]
    hi = ci_ref[...]
    for t in range(tc):
        step = pl.ds((t // SCAN_ROWS) * nc * SCAN_ROWS + t % SCAN_ROWS, nc, stride=SCAN_ROWS)
        hr, hi = ar * hr - ai * hi + xr_s[step, :], ar * hi + ai * hr + xi_s[step, :]
        or_s[step, :] = hr
        oi_s[step, :] = hi
    cr_ref[...] = hr
    ci_ref[...] = hi
    hr_out[0] = hr
    hi_out[0] = hi

    def states(o_s, c0):
        return jnp.concatenate(
            [jnp.concatenate([o_s[tile_rows(tau, c), :] for c in range(c0, c0 + nc // 2)], axis=-1)
             for tau in range(nt)], axis=0).astype(BF16)

    y = jnp.concatenate(
        [_dot(states(or_s, 0), cc_ref[0:hs, 0:hc]) + _dot(states(oi_s, 0), cc_ref[D_STATE:D_STATE + hs, 0:hc]),
         _dot(states(or_s, nc // 2), cc_ref[hs:D_STATE, hc:]) + _dot(states(oi_s, nc // 2), cc_ref[D_STATE + hs:, hc:])],
        axis=-1) + d_ref[...] * u
    gl = 0.5 * y * (1.0 + jnp.tanh(math.sqrt(2.0 / math.pi) * (y + 0.044715 * (y * y * y))))
    z = _dot(gl.astype(BF16), wg_ref[...]) + bg_ref[...]
    out = y * _sigmoid(z)
    y_ref[0] = _rms(out, gso_ref[...]).astype(BF16)


def _ssm(u, h0r, h0i, consts):
    b, l, _ = u.shape
    br, bi, cc, ar, ai, dsk, wg, bg, gso = consts
    tc = min(l, SSM_CHUNK)
    full = lambda a: pl.BlockSpec(a.shape, lambda i, j: (0,) * a.ndim)
    st = pl.BlockSpec((1, STATE_TILES, LANES), lambda i, j: (i, 0, 0))
    state = jax.ShapeDtypeStruct((b, STATE_TILES, LANES), F32)
    carry = pltpu.VMEM((STATE_TILES, LANES), F32)
    tiles = pltpu.VMEM((tc * STATE_TILES, LANES), F32)
    return pl.pallas_call(
        functools.partial(_ssm_kernel, tc=tc),
        grid=(b, l // tc),
        in_specs=[pl.BlockSpec((1, tc, D_SSM), lambda i, j: (i, j, 0)), st, st,
                  full(br), full(bi), full(cc), full(ar), full(ai), full(dsk), full(wg), full(bg), full(gso)],
        out_specs=[pl.BlockSpec((1, tc, D_SSM), lambda i, j: (i, j, 0)), st, st],
        out_shape=[jax.ShapeDtypeStruct((b, l, D_SSM), BF16), state, state],
        scratch_shapes=[carry, carry, tiles, tiles, tiles, tiles],
        compiler_params=_cparams("arbitrary", "arbitrary"),
        name=f"ssm{l}",
    )(u, h0r, h0i, br, bi, cc, ar, ai, dsk, wg, bg, gso)


def _post_kernel(x_ref, a_ref, s_ref, gt1_ref, sh2_ref, sc2_ref, gt2_ref, cnt0_ref, gao_ref, wo_ref, gffn_ref,
                 wrh_ref, wrl_ref, rb_ref, wsgu_ref, wsd_ref,
                 xm_ref, hp_ref, idx_ref, pos_ref, w_ref, cnt_ref, carry_ref, *, tl, first):
    step = pl.program_id(0) * pl.num_programs(1) + pl.program_id(1)

    @pl.when(step == 0)
    def _():
        carry_ref[...] = cnt0_ref[...] if not first else jnp.zeros_like(carry_ref)

    x = x_ref[0]
    attn = jnp.concatenate([a_ref[0, hd] for hd in range(N_HEADS)], axis=-1).astype(F32)
    ms = jnp.sum(attn * attn, axis=-1, keepdims=True) * (1.0 / D_ATTN)
    attn_n = (attn * lax.rsqrt(ms + EPS) * gao_ref[...]).astype(BF16)
    merged = jnp.concatenate([attn_n, s_ref[0]], axis=-1)
    x1 = x + gt1_ref[0] * _dot(merged, wo_ref[...])
    h2 = _rms(x1, gffn_ref[...]) * (1.0 + sc2_ref[0]) + sh2_ref[0]
    d = h2.shape[-1]
    _store_slabs(hp_ref.at[0], _pack_bf16_pair(h2[:, :d // 2], h2[:, d // 2:]))
    h_hi = h2.astype(BF16)
    h_lo = (h2 - h_hi.astype(F32)).astype(BF16)

    gu = _dot(h_hi, wsgu_ref[...])
    ds = gu.shape[-1] // 2
    g = gu[:, :ds]
    act = (g * _sigmoid(g) * gu[:, ds:]).astype(BF16)
    xm_ref[0] = x1 + gt2_ref[0] * _dot(act, wsd_ref[...])

    wrh = wrh_ref[...]
    logits = _dot_nt(wrh, h_hi) + _dot_nt(wrh, h_lo) + _dot_nt(wrl_ref[...], h_hi)
    s = _sigmoid(logits)
    sc = s + rb_ref[...]
    ge = N_EXPERTS // N_EXP_GROUPS
    neg = -jnp.inf
    gi = lax.broadcasted_iota(I32, (ge, tl), 0)
    gsc = []
    for gidx in range(N_EXP_GROUPS):
        blk = sc[gidx * ge:(gidx + 1) * ge, :]
        m1 = jnp.max(blk, axis=0, keepdims=True)
        f1 = jnp.min(jnp.where(blk == m1, gi, ge), axis=0, keepdims=True)
        m2 = jnp.max(jnp.where(gi == f1, neg, blk), axis=0, keepdims=True)
        gsc.append(m1 + m2)
    gwork = jnp.concatenate(gsc, axis=0)
    ni = lax.broadcasted_iota(I32, (N_EXP_GROUPS, tl), 0)
    gsel = jnp.zeros((N_EXP_GROUPS, tl), F32)
    for _ in range(TOPK_GROUPS):
        mx = jnp.max(gwork, axis=0, keepdims=True)
        fi = jnp.min(jnp.where(gwork == mx, ni, N_EXP_GROUPS), axis=0, keepdims=True)
        hit = ni == fi
        gsel = jnp.where(hit, 1.0, gsel)
        gwork = jnp.where(hit, neg, gwork)
    work = jnp.concatenate(
        [jnp.where(gsel[gidx:gidx + 1, :] > 0.0, sc[gidx * ge:(gidx + 1) * ge, :], neg)
         for gidx in range(N_EXP_GROUPS)], axis=0)
    ei = lax.broadcasted_iota(I32, (N_EXPERTS, tl), 0)
    chosen = jnp.zeros((N_EXPERTS, tl), F32)
    idx_rows, w_rows = [], []
    for _ in range(TOP_K):
        mx = jnp.max(work, axis=0, keepdims=True)
        fi = jnp.min(jnp.where(work == mx, ei, N_EXPERTS), axis=0, keepdims=True)
        hit = ei == fi
        w_rows.append(jnp.sum(jnp.where(hit, s, 0.0), axis=0, keepdims=True))
        idx_rows.append(fi)
        chosen = jnp.where(hit, 1.0, chosen)
        work = jnp.where(hit, neg, work)
    wt = jnp.concatenate(w_rows, axis=0)
    wt = wt / jnp.sum(wt, axis=0, keepdims=True) * ROUTED_SCALE
    idx_ref[0] = jnp.concatenate(idx_rows, axis=0)

    trow = lax.broadcasted_iota(I32, (tl, tl), 0)
    tcol = lax.broadcasted_iota(I32, (tl, tl), 1)
    before = jnp.where(trow < tcol, 1.0, 0.0).astype(BF16)
    rank = _dot(chosen.astype(BF16), before) + carry_ref[...]
    pos_rows = [jnp.sum(jnp.where(ei == idx_rows[k], rank, 0.0), axis=0, keepdims=True) for k in range(TOP_K)]
    pos_ref[0] = jnp.concatenate(pos_rows, axis=0).astype(I32)
    carry_ref[...] = carry_ref[...] + jnp.sum(chosen, axis=1, keepdims=True)
    cnt_ref[...] = carry_ref[...]

    eye = jnp.where(trow == tcol, 1.0, 0.0).astype(BF16)
    t_hi, t_mid, t_lo = _split3(wt)
    w_ref[0] = _dot_nt(eye, t_hi) + _dot_nt(eye, t_mid) + _dot_nt(eye, t_lo)


def _post(x, attn, ssm, mods, cnt0, consts, first):
    b, l, d = x.shape
    tl = min(l, SEQ_BLOCK)
    gt1, sh2, sc2, gt2 = mods
    gao, wo, gffn, wrh, wrl, rb, wsgu, wsd = consts
    slab = d // 2 // LANES
    tok = lambda w: pl.BlockSpec((1, tl, w), lambda i, j: (i, j, 0))
    full = lambda a: pl.BlockSpec(a.shape, lambda i, j: (0,) * a.ndim)
    row = pl.BlockSpec((1, 1, d), lambda i, j: (i, 0, 0))
    tk = pl.BlockSpec((1, TOP_K, tl), lambda i, j: (i, 0, j))
    return pl.pallas_call(
        functools.partial(_post_kernel, tl=tl, first=first),
        grid=(b, l // tl),
        in_specs=[tok(d), pl.BlockSpec((1, N_HEADS, tl, HEAD_PAD), lambda i, j: (i, 0, j, 0)), tok(D_SSM),
                  row, row, row, row, full(cnt0), full(gao), full(wo), full(gffn), full(wrh), full(wrl),
                  full(rb), full(wsgu), full(wsd)],
        out_specs=[tok(d), pl.BlockSpec((1, tl * slab, LANES), lambda i, j: (i, j, 0)), tk, tk, tok(TOP_K),
                   full(cnt0)],
        out_shape=[jax.ShapeDtypeStruct((b, l, d), F32), jax.ShapeDtypeStruct((b, l * slab, LANES), U32),
                   jax.ShapeDtypeStruct((b, TOP_K, l), I32), jax.ShapeDtypeStruct((b, TOP_K, l), I32),
                   jax.ShapeDtypeStruct((b, l, TOP_K), F32), jax.ShapeDtypeStruct(cnt0.shape, F32)],
        scratch_shapes=[pltpu.VMEM(cnt0.shape, F32)],
        compiler_params=_cparams("arbitrary", "arbitrary"),
        name=f"post{l}",
    )(x, attn, ssm, gt1, sh2, sc2, gt2, cnt0, gao, wo, gffn, wrh, wrl, rb, wsgu, wsd)


def _slots_kernel(pstart_ref, idx_ref, pos_ref, dest_ref):
    idx = idx_ref[0]

    def body(e, base):
        return jnp.where(idx == e, pstart_ref[e], base)

    dest_ref[0] = lax.fori_loop(0, N_EXPERTS, body, jnp.zeros_like(idx)) + pos_ref[0]


def _slots(pstart, idx, pos):
    b, k, l = idx.shape
    tl = min(l, 2048)
    spec = pl.BlockSpec((1, k, tl), lambda i, j: (i, 0, j))
    return pl.pallas_call(
        _slots_kernel,
        grid=(b, l // tl),
        in_specs=[pl.BlockSpec(memory_space=pltpu.SMEM), spec, spec],
        out_specs=spec,
        out_shape=jax.ShapeDtypeStruct(idx.shape, I32),
        compiler_params=_cparams("arbitrary", "arbitrary"),
        name=f"slots{l}",
    )(pstart, idx, pos)


INDEX_WINDOW = 2048
INDEX_ALIGN = 1024


def _experts_kernel(be_ref, nb_ref, jb_ref, tok_ref, h_ref, wg_ref, wu_ref, wd_ref, y_ref,
                    wgu_s, wd_s, x0, x1, idx0, idx1, xsem, isem):
    i = pl.program_id(0)
    nbu = nb_ref[0]
    e = be_ref[i]
    prev = be_ref[jnp.maximum(i - 1, 0)]
    dm = wg_ref.shape[1]
    de = wg_ref.shape[2]
    xbufs = (x0, x1)
    ibufs = (idx0, idx1)

    def window(blk, slot):
        start = pl.multiple_of((jb_ref[blk] >> 10) << 10, INDEX_ALIGN)
        return pltpu.make_async_copy(tok_ref.at[pl.ds(start, INDEX_WINDOW)], ibufs[slot], isem.at[slot])

    slab = x0.shape[0] // EXPERT_ROWS

    def gather_rows(blk, slot):
        base = jb_ref[blk] & (INDEX_ALIGN - 1)
        for r in range(EXPERT_ROWS):
            src = pl.multiple_of(ibufs[slot][base + r], slab)
            pltpu.make_async_copy(h_ref.at[pl.ds(src, slab)], xbufs[slot].at[pl.ds(r * slab, slab)],
                                  xsem.at[slot]).start()

    def drain_rows(slot):
        def body(r, c):
            for _ in range(SCAN_ROWS):
                pltpu.make_async_copy(h_ref.at[pl.ds(0, slab)], xbufs[slot].at[pl.ds(0, slab)],
                                      xsem.at[slot]).wait()
            return c
        lax.fori_loop(0, EXPERT_ROWS // SCAN_ROWS, body, 0)

    @pl.when(i >= nbu)
    def _():
        y_ref[...] = jnp.zeros_like(y_ref)

    @pl.when(i < nbu)
    def _():
        last = nbu - 1

        @pl.when(i == 0)
        def _():
            first = window(0, 0)
            first.start()
            first.wait()
            gather_rows(0, 0)
            window(jnp.minimum(1, last), 1).start()

        @pl.when((i == 0) | (e != prev))
        def _():
            wgu_s[:, 0:de] = wg_ref[0].astype(BF16)
            wgu_s[:, de:2 * de] = wu_ref[0].astype(BF16)
            wd_s[...] = wd_ref[0].astype(BF16)

        def run(slot):
            nxt = jnp.minimum(i + 1, last)
            window(nxt, 1 - slot).wait()
            drain_rows(slot)
            gather_rows(nxt, 1 - slot)
            xa, xb = _unpack_bf16_pair(_load_slabs(xbufs[slot], EXPERT_ROWS))
            gu = _dot(xa.astype(BF16), wgu_s[0:dm // 2, :]) + _dot(xb.astype(BF16), wgu_s[dm // 2:dm, :])
            g = gu[:, :de]
            act = (g * _sigmoid(g) * gu[:, de:]).astype(BF16)
            y = _dot(act, wd_s[...])
            _store_slabs(y_ref, _pack_bf16_pair(y[:, :dm // 2], y[:, dm // 2:]))

            @pl.when(i < last)
            def _():
                window(jnp.minimum(i + 2, last), slot).start()

            @pl.when(i == last)
            def _():
                drain_rows(1 - slot)

        for slot in range(2):
            pl.when(i % 2 == slot)(functools.partial(run, slot))


def _experts(blk_e, nb_used, jbase, tok, hp, w_gate, w_up, w_down, rows):
    ne, dm, de = w_gate.shape
    slab = dm // 2 // LANES
    xrows = EXPERT_ROWS * slab
    nb = rows // EXPERT_ROWS
    grid_spec = pltpu.PrefetchScalarGridSpec(
        num_scalar_prefetch=3,
        grid=(nb,),
        in_specs=[pl.BlockSpec(memory_space=pl.ANY), pl.BlockSpec(memory_space=pl.ANY),
                  pl.BlockSpec((1, dm, de), lambda i, be, n, jb: (be[i], 0, 0)),
                  pl.BlockSpec((1, dm, de), lambda i, be, n, jb: (be[i], 0, 0)),
                  pl.BlockSpec((1, de, dm), lambda i, be, n, jb: (be[i], 0, 0))],
        out_specs=pl.BlockSpec((xrows, LANES), lambda i, be, n, jb: (i, 0)),
        scratch_shapes=[pltpu.VMEM((dm, 2 * de), BF16), pltpu.VMEM((de, dm), BF16),
                        pltpu.VMEM((xrows, LANES), U32), pltpu.VMEM((xrows, LANES), U32),
                        pltpu.SMEM((INDEX_WINDOW,), I32), pltpu.SMEM((INDEX_WINDOW,), I32),
                        pltpu.SemaphoreType.DMA((2,)), pltpu.SemaphoreType.DMA((2,))],
    )
    return pl.pallas_call(
        _experts_kernel,
        grid_spec=grid_spec,
        out_shape=jax.ShapeDtypeStruct((rows * slab, LANES), U32),
        compiler_params=_cparams("arbitrary"),
        name="experts",
    )(blk_e, nb_used, jbase, tok, hp, w_gate, w_up, w_down)


def _combine_kernel(dcur_ref, dnext_ref, ys_ref, w_ref, xm_ref, gt2_ref, gfin_ref, o_ref, buf0, buf1, sem, *, tl):
    bufs = (buf0, buf1)
    s = pl.program_id(0)
    last = pl.num_programs(0) - 1
    group = SCAN_ROWS
    ngroups = tl // group
    slab = buf0.shape[2] // group
    half = slab * LANES

    def gather(dref, g, to):
        for tt in range(group):
            for k in range(TOP_K):
                src = pl.multiple_of(dref[(g * group + tt) * TOP_K + k], slab)
                pltpu.make_async_copy(ys_ref.at[pl.ds(src, slab)], bufs[to].at[k, g, pl.ds(tt * slab, slab)],
                                      sem.at[to]).start()

    def drain(which):
        def body(t, c):
            for k in range(TOP_K):
                pltpu.make_async_copy(ys_ref.at[pl.ds(0, slab)], bufs[which].at[0, 0, pl.ds(0, slab)],
                                      sem.at[which]).wait()
            return c
        lax.fori_loop(0, tl, body, 0)

    @pl.when(s == 0)
    def _():
        def body(g, c):
            gather(dcur_ref, g, 0)
            return c
        lax.fori_loop(0, ngroups, body, 0)

    gt2 = gt2_ref[0]
    gfin = gfin_ref[...]

    def run(slot):
        drain(slot)

        def body(g, c):
            gather(dnext_ref, g, 1 - slot)
            t0 = pl.multiple_of(g * group, group)
            w = w_ref[pl.ds(t0, group), :]
            acc_a = jnp.zeros((group, half), F32)
            acc_b = jnp.zeros((group, half), F32)
            for k in range(TOP_K):
                ya, yb = _unpack_bf16_pair(_load_slabs(bufs[slot].at[k, g], group))
                wk = w[:, k:k + 1]
                acc_a = acc_a + wk * ya
                acc_b = acc_b + wk * yb
            routed = jnp.concatenate([acc_a, acc_b], axis=-1)
            x2 = xm_ref[pl.ds(t0, group), :] + gt2 * routed
            o_ref[pl.ds(t0, group), :] = _rms(x2, gfin)
            return c

        lax.fori_loop(0, ngroups, body, 0)

        @pl.when(s == last)
        def _():
            drain(1 - slot)

    for slot in range(2):
        pl.when(s % 2 == slot)(functools.partial(run, slot))


def _combine(dest, ys, w, xm, gt2, gfin):
    b, l, d = xm.shape
    tl = min(l, COMBINE_ROWS)
    nl = l // tl
    n = b * nl
    slab = d // 2 // LANES
    tile = pltpu.VMEM((TOP_K, tl // SCAN_ROWS, SCAN_ROWS * slab, LANES), U32)
    out = pl.pallas_call(
        functools.partial(_combine_kernel, tl=tl),
        grid=(n,),
        in_specs=[pl.BlockSpec((tl * TOP_K,), lambda s: (s,), memory_space=pltpu.SMEM),
                  pl.BlockSpec((tl * TOP_K,), lambda s: (jnp.minimum(s + 1, n - 1),), memory_space=pltpu.SMEM),
                  pl.BlockSpec(memory_space=pl.ANY),
                  pl.BlockSpec((tl, TOP_K), lambda s: (s, 0)),
                  pl.BlockSpec((tl, d), lambda s: (s, 0)),
                  pl.BlockSpec((1, 1, d), lambda s: (s // nl, 0, 0)),
                  pl.BlockSpec(gfin.shape, lambda s: (0, 0))],
        out_specs=pl.BlockSpec((tl, d), lambda s: (s, 0)),
        out_shape=jax.ShapeDtypeStruct((b * l, d), F32),
        scratch_shapes=[tile, tile, pltpu.SemaphoreType.DMA((2,))],
        compiler_params=_cparams("arbitrary"),
        name=f"combine{l}",
    )(dest, dest, ys, w.reshape(b * l, TOP_K), xm.reshape(b * l, d), gt2, gfin)
    return out.reshape(b, l, d)


def _block_diag(m):
    g, a, b = m.shape
    eye = jnp.eye(g, dtype=m.dtype)
    return jnp.einsum("gab,gh->gahb", m, eye).reshape(g * a, g * b)


def kernel(x_prompt, x_sample, c_prompt, c_sample, cache_k, cache_v, cache_logf, state_ssm_re, state_ssm_im,
           w_ada, b_ada, g_mix, w_in, b_f, lam_re, lam_im, log_dt, ssm_b_re, ssm_b_im, ssm_c_re, ssm_c_im,
           ssm_d, w_glu, b_glu, g_attn_out, g_ssm_out, w_out, g_ffn, w_router, router_bias,
           w_gate, w_up, w_down, ws_gate, ws_up, ws_down, g_final):
    assert w_ada.shape[0] == 1, "single layer"
    bp, lp, d = x_prompt.shape
    bs, ls, _ = x_sample.shape
    past = cache_k.shape[2]

    nb = bp + bs
    nbp = -(-nb // 8) * 8
    c_all = jnp.concatenate([c_prompt, c_sample, jnp.zeros((nbp - nb, d), F32)], axis=0)
    mod = _prep(c_all, w_ada[0], b_ada[0]).reshape(nbp, 6, 1, d)
    mod_p = [mod[:bp, i] for i in range(6)]
    mod_s = [mod[bp:nb, i] for i in range(6)]

    a_re, a_im, bb_re, bb_im = _disc(lam_re[0], lam_im[0], log_dt[0],
                                     jnp.swapaxes(ssm_b_re[0], 1, 2), jnp.swapaxes(ssm_b_im[0], 1, 2))
    a_re = a_re.reshape(STATE_TILES, LANES)
    a_im = a_im.reshape(STATE_TILES, LANES)
    bd_re = _block_diag(bb_re).astype(BF16)
    bd_im = _block_diag(bb_im).astype(BF16)
    cc = jnp.concatenate([_block_diag(jnp.swapaxes(ssm_c_re[0], 1, 2)),
                          -_block_diag(jnp.swapaxes(ssm_c_im[0], 1, 2))], axis=0).astype(BF16)
    ssm_tail = (ssm_d[0].reshape(1, D_SSM), w_glu[0].astype(BF16), b_glu[0].reshape(1, D_SSM),
                g_ssm_out[0].reshape(1, D_SSM))

    ssm_consts = (bd_re, bd_im, cc, a_re, a_im) + ssm_tail

    inw = _inproj_weights(w_in[0], b_f[0])
    g_mix2 = g_mix[0].reshape(1, d)

    wo = w_out[0]
    wo_attn = jnp.pad(wo[:D_ATTN].reshape(N_HEADS, HEAD_DIM, d), ((0, 0), (0, HEAD_PAD - HEAD_DIM), (0, 0)))
    wo_pad = jnp.concatenate([wo_attn.reshape(N_HEADS * HEAD_PAD, d), wo[D_ATTN:]], axis=0).astype(BF16)
    gao_pad = jnp.pad(g_attn_out[0].reshape(N_HEADS, HEAD_DIM),
                      ((0, 0), (0, HEAD_PAD - HEAD_DIM))).reshape(1, N_HEADS * HEAD_PAD)
    wr_t = w_router[0].T
    wr_hi = wr_t.astype(BF16)
    wr_lo = (wr_t - wr_hi.astype(F32)).astype(BF16)
    post_consts = (gao_pad, wo_pad, g_ffn[0].reshape(1, d), wr_hi, wr_lo, router_bias[0].reshape(N_EXPERTS, 1),
                   jnp.concatenate([ws_gate[0], ws_up[0]], axis=1).astype(BF16), ws_down[0].astype(BF16))

    def mixer(x, modv, attn_fn, h0r, h0i):
        sh1, sc1, gt1, sh2, sc2, gt2 = modv
        q, k, v, kf, vf, lf, fb, u = _inproj(x, sh1, sc1, g_mix2, inw)
        attn = attn_fn(q, k, v, kf, vf, lf, fb)
        ssm, hr, hi = _ssm(u, h0r, h0i, ssm_consts)
        return attn, ssm, (kf, vf, lf, hr, hi), (gt1, sh2, sc2, gt2)

    def attn_prompt(q, k, v, kf, vf, lf, fb):
        return _attn(q, k, v, fb[:, :, 0, :N_HEADS].reshape(-1))

    zeros_p = jnp.zeros((bp, STATE_TILES, LANES), F32)
    attn_p, ssm_p, new_p, m_p = mixer(x_prompt, mod_p, attn_prompt, zeros_p, zeros_p)

    lk = -(-(past + ls) // LANES) * LANES
    padk = lk - past - ls

    def attn_sample(q, k, v, kf, vf, lf, fb):
        k_all = jnp.concatenate([cache_k[0].reshape(bs, past, D_ATTN), kf,
                                 jnp.zeros((bs, padk, D_ATTN), F32)], axis=1)
        v_all = jnp.concatenate([cache_v[0].reshape(bs, past, D_ATTN), vf,
                                 jnp.zeros((bs, padk, D_ATTN), F32)], axis=1)
        lf_all = jnp.concatenate([cache_logf[0], lf, jnp.zeros((bs, padk, N_HEADS), F32)], axis=1)
        lf_cols = jnp.pad(lf_all, ((0, 0), (0, 0), (0, LANES - N_HEADS)))
        lf_rows = jnp.swapaxes(lf_all, 1, 2)
        return _attn_dec(q, k_all, v_all, lf_cols, lf_rows, past, ls)

    attn_s, ssm_s, new_s, m_s = mixer(x_sample, mod_s, attn_sample,
                                      state_ssm_re[0].reshape(bs, STATE_TILES, LANES),
                                      state_ssm_im[0].reshape(bs, STATE_TILES, LANES))

    cnt0 = jnp.zeros((N_EXPERTS, 1), F32)
    xm_p, hp_p, idx_p, pos_p, w_p, cnt1 = _post(x_prompt, attn_p, ssm_p, m_p, cnt0, post_consts, True)
    xm_s, hp_s, idx_s, pos_s, w_s, cnt2 = _post(x_sample, attn_s, ssm_s, m_s, cnt1, post_consts, False)

    counts = cnt2[:, 0].astype(I32)
    padded = (counts + EXPERT_ROWS - 1) // EXPERT_ROWS * EXPERT_ROWS
    pend = jnp.cumsum(padded)
    pstart = pend - padded
    tp, ts = bp * lp, bs * ls
    n_blocks = -(-((tp + ts) * TOP_K) // EXPERT_ROWS) + N_EXPERTS
    rows = n_blocks * EXPERT_ROWS
    dest_p = _slots(pstart, idx_p, pos_p)
    dest_s = _slots(pstart, idx_s, pos_s)
    nb_used = (pend[-1] // EXPERT_ROWS).astype(I32).reshape(1)
    blk_row = jnp.minimum(jnp.arange(n_blocks, dtype=I32), nb_used[0] - 1) * EXPERT_ROWS
    blk_e = jnp.minimum(jnp.sum((pend[None, :] <= blk_row[:, None]).astype(I32), axis=1), N_EXPERTS - 1)

    slab = d // 2 // LANES
    dflat_p = jnp.swapaxes(dest_p, 1, 2).reshape(-1)
    dflat_s = jnp.swapaxes(dest_s, 1, 2).reshape(-1)
    n_pairs = (tp + ts) * TOP_K
    order = jnp.argsort(jnp.concatenate([dflat_p, dflat_s])).astype(I32)
    n_tok = ((n_pairs >> 10) + 3) << 10
    tok = jnp.concatenate([order // TOP_K * slab, jnp.zeros((n_tok - n_pairs,), I32)])
    cstart = jnp.cumsum(counts) - counts
    blk_ids = jnp.arange(n_blocks, dtype=I32)
    jbase = jnp.where(blk_ids < nb_used[0], cstart[blk_e] + blk_ids * EXPERT_ROWS - pstart[blk_e], 0).astype(I32)
    hp_all = jnp.concatenate([hp_p.reshape(tp * slab, LANES), hp_s.reshape(ts * slab, LANES)], axis=0)
    ys = _experts(blk_e, nb_used, jbase, tok, hp_all, w_gate[0], w_up[0], w_down[0], rows)

    gfin = g_final.reshape(1, d)
    y_p = _combine(dflat_p * slab, ys, w_p, xm_p, m_p[3], gfin)
    y_s = _combine(dflat_s * slab, ys, w_s, xm_s, m_s[3], gfin)

    def pack(new, b, l):
        kf, vf, lf, hr, hi = new
        return (kf.reshape(1, b, l, N_HEADS, HEAD_DIM), vf.reshape(1, b, l, N_HEADS, HEAD_DIM),
                lf.reshape(1, b, l, N_HEADS),
                hr.reshape(1, b, N_SSM_GROUPS, SSM_STATE), hi.reshape(1, b, N_SSM_GROUPS, SSM_STATE))

    return (y_p, y_s) + pack(new_p, bp, lp) + pack(new_s, bs, ls)
```

```python
import functools
import math

import jax
import jax.numpy as jnp
from jax import lax
from jax.experimental import pallas as pl
from jax.experimental.pallas import tpu as pltpu

F32 = jnp.float32
BF16 = jnp.bfloat16
U32 = jnp.uint32
I32 = jnp.int32

N_HEADS = 8
HEAD_DIM = 64
HEAD_PAD = 128
D_ATTN = N_HEADS * HEAD_DIM
SSM_GROUP = 16
N_SSM_GROUPS = 32
SSM_STATE = 64
D_SSM = SSM_GROUP * N_SSM_GROUPS
D_STATE = N_SSM_GROUPS * SSM_STATE
N_EXPERTS = 256
TOP_K = 8
N_EXP_GROUPS = 8
TOPK_GROUPS = 4
ROUTED_SCALE = 2.5
EPS = 1e-6
LANES = 128

SEQ_BLOCK = 512
SSM_CHUNK = 512
EXPERT_ROWS = 512
ATTN_HEADS_PER_STEP = 2
ATTN_BLOCKS_PER_TRIP = 4
LOG2E = math.log2(math.e)
COMBINE_ROWS = 256
DISPATCH_ROWS = 512
VMEM_LIMIT = 56 * 1024 * 1024

_NT = (((1,), (1,)), ((), ()))


def _cparams(*sem):
    return pltpu.CompilerParams(dimension_semantics=sem, vmem_limit_bytes=VMEM_LIMIT)


def _split3(x):
    hi = x.astype(BF16)
    r = x - hi.astype(F32)
    mid = r.astype(BF16)
    lo = (r - mid.astype(F32)).astype(BF16)
    return hi, mid, lo


def _dot(a, b):
    return jnp.dot(a, b, preferred_element_type=F32)


def _dot_nt(a, b):
    return lax.dot_general(a, b, _NT, preferred_element_type=F32)


def _rms(x, g):
    return x * lax.rsqrt(jnp.mean(x * x, axis=-1, keepdims=True) + EPS) * g


def _sigmoid(x):
    return 1.0 / (1.0 + jnp.exp(-x))


def _pack_bf16_pair(a, b):
    ab = lax.bitcast_convert_type(a.astype(BF16).astype(F32), U32)
    bb = lax.bitcast_convert_type(b.astype(BF16).astype(F32), U32)
    return (ab >> 16) | (bb & jnp.uint32(0xFFFF0000))


def _unpack_bf16_pair(w):
    a = lax.bitcast_convert_type(w << 16, F32)
    b = lax.bitcast_convert_type(w & jnp.uint32(0xFFFF0000), F32)
    return a, b


def _store_slabs(ref, val):
    n, width = val.shape
    tiles = width // LANES
    for c in range(tiles):
        ref[pl.ds(c, n, stride=tiles), :] = val[:, c * LANES:(c + 1) * LANES]


def _load_slabs(ref, n):
    tiles = ref.shape[0] // n
    return jnp.concatenate([ref[pl.ds(c, n, stride=tiles), :] for c in range(tiles)], axis=-1)


def _prep_kernel(c_ref, w_ref, b_ref, o_ref):
    c = c_ref[...]
    a = c * _sigmoid(c)
    a_hi = a.astype(BF16)
    a_lo = (a - a_hi.astype(F32)).astype(BF16)
    w = w_ref[...]
    w_hi = w.astype(BF16)
    w_lo = (w - w_hi.astype(F32)).astype(BF16)
    o_ref[...] = _dot(a_hi, w_hi) + _dot(a_lo, w_hi) + _dot(a_hi, w_lo) + b_ref[...]


def _prep(c_all, w_ada, b_ada):
    n, d = c_all.shape
    nout = w_ada.shape[1]
    tn = 1024
    return pl.pallas_call(
        _prep_kernel,
        grid=(nout // tn,),
        in_specs=[pl.BlockSpec((n, d), lambda j: (0, 0)),
                  pl.BlockSpec((d, tn), lambda j: (0, j)),
                  pl.BlockSpec((1, tn), lambda j: (0, j))],
        out_specs=pl.BlockSpec((n, tn), lambda j: (0, j)),
        out_shape=jax.ShapeDtypeStruct((n, nout), F32),
        compiler_params=_cparams("arbitrary"),
        name="prep",
    )(c_all, w_ada, b_ada.reshape(1, nout))


def _disc_kernel(lr_ref, li_ref, ldt_ref, br_ref, bi_ref, ar_ref, ai_ref, bbr_ref, bbi_ref):
    lr = lr_ref[...]
    li = li_ref[...]
    dt = jnp.exp(ldt_ref[...])
    er = jnp.exp(lr * dt)
    ang = li * dt
    ar = er * jnp.cos(ang)
    ai = er * jnp.sin(ang)
    ar_ref[...] = ar
    ai_ref[...] = ai
    den = lr * lr + li * li
    nr = ((ar - 1.0) * lr + ai * li) / den
    ni = (ai * lr - (ar - 1.0) * li) / den
    nr3 = nr[:, None, :]
    ni3 = ni[:, None, :]
    br = br_ref[...]
    bi = bi_ref[...]
    bbr_ref[...] = nr3 * br - ni3 * bi
    bbi_ref[...] = nr3 * bi + ni3 * br


def _disc(lam_re, lam_im, log_dt, bt_re, bt_im):
    g, p = lam_re.shape
    c = bt_re.shape[1]
    return pl.pallas_call(
        _disc_kernel,
        out_shape=(jax.ShapeDtypeStruct((g, p), F32), jax.ShapeDtypeStruct((g, p), F32),
                   jax.ShapeDtypeStruct((g, c, p), F32), jax.ShapeDtypeStruct((g, c, p), F32)),
        name="disc",
    )(lam_re, lam_im, log_dt.reshape(g, 1), bt_re, bt_im)


SCAN_ROWS = 8
STATE_TILES = D_STATE // LANES


def _inproj_kernel(x_ref, sh_ref, sc_ref, g_ref, wm_ref, wf_ref, bf_ref, selq_ref, selk_ref,
                   cq_ref, ck_ref, cv_ref,
                   q_ref, k_ref, v_ref, kf_ref, vf_ref, lf_ref, fb_ref, u_ref, carry_ref, *, tl):
    @pl.when(pl.program_id(1) == 0)
    def _():
        carry_ref[...] = jnp.zeros_like(carry_ref)

    x = x_ref[0]
    h = _rms(x, g_ref[...]) * (1.0 + sc_ref[0]) + sh_ref[0]
    hb = h.astype(BF16)
    main = _dot(hb, wm_ref[...])
    hw = N_HEADS * HEAD_PAD
    qp = main[:, 0:hw] * (HEAD_DIM ** -0.5 * LOG2E)
    kp = main[:, hw:2 * hw]
    vp = main[:, 2 * hw:3 * hw]
    o = 3 * hw
    kf_ref[0] = main[:, o:o + D_ATTN]
    vf_ref[0] = main[:, o + D_ATTN:o + 2 * D_ATTN]
    u_ref[0] = main[:, o + 2 * D_ATTN:o + 2 * D_ATTN + D_SSM]

    fl = _dot(hb, wf_ref[...]) + bf_ref[...]
    lf = jnp.minimum(fl, 0.0) - jnp.log1p(jnp.exp(-jnp.abs(fl)))
    lf_ref[0] = lf[:, 0:N_HEADS]

    row = lax.broadcasted_iota(I32, (tl, tl), 0)
    col = lax.broadcasted_iota(I32, (tl, tl), 1)
    tri = jnp.where(row >= col, 1.0, 0.0).astype(BF16)
    hi, mid, lo = _split3(lf)
    frel = _dot(tri, hi) + _dot(tri, mid) + _dot(tri, lo)
    fb_ref[0, 0] = carry_ref[...] * LOG2E
    carry_ref[...] = carry_ref[...] + frel[tl - 1:tl, :]

    fcat = jnp.concatenate(_split3(frel * LOG2E), axis=-1)
    q_aug = (qp + _dot(fcat, selq_ref[...]) + cq_ref[...]).astype(BF16)
    k_aug = (kp + _dot(fcat, selk_ref[...]) + ck_ref[...]).astype(BF16)
    v_aug = (vp + cv_ref[...]).astype(BF16)
    for hd in range(N_HEADS):
        sl = slice(hd * HEAD_PAD, (hd + 1) * HEAD_PAD)
        q_ref[0, hd] = q_aug[:, sl]
        k_ref[0, hd] = k_aug[:, sl]
        v_ref[0, hd] = v_aug[:, sl]


def _inproj(x, sh, sc, g, wts):
    b, l, d = x.shape
    tl = min(l, SEQ_BLOCK)
    nl = l // tl
    wm, wf, bf, selq, selk, cq, ck, cv = wts
    hm = jax.ShapeDtypeStruct((b, N_HEADS, l, HEAD_PAD), BF16)
    hm_spec = pl.BlockSpec((1, N_HEADS, tl, HEAD_PAD), lambda i, j: (i, 0, j, 0))
    tok = lambda w: pl.BlockSpec((1, tl, w), lambda i, j: (i, j, 0))
    full = lambda a: pl.BlockSpec(a.shape, lambda i, j: (0,) * a.ndim)
    row = pl.BlockSpec((1, 1, d), lambda i, j: (i, 0, 0))
    return pl.pallas_call(
        functools.partial(_inproj_kernel, tl=tl),
        grid=(b, nl),
        in_specs=[tok(d), row, row, full(g), full(wm), full(wf), full(bf), full(selq), full(selk),
                  full(cq), full(ck), full(cv)],
        out_specs=[hm_spec, hm_spec, hm_spec, tok(D_ATTN), tok(D_ATTN), tok(N_HEADS),
                   pl.BlockSpec((1, 1, 1, LANES), lambda i, j: (i, j, 0, 0)), tok(D_SSM)],
        out_shape=[hm, hm, hm,
                   jax.ShapeDtypeStruct((b, l, D_ATTN), F32), jax.ShapeDtypeStruct((b, l, D_ATTN), F32),
                   jax.ShapeDtypeStruct((b, l, N_HEADS), F32),
                   jax.ShapeDtypeStruct((b, nl, 1, LANES), F32),
                   jax.ShapeDtypeStruct((b, l, D_SSM), F32)],
        scratch_shapes=[pltpu.VMEM((1, LANES), F32)],
        compiler_params=_cparams("arbitrary", "arbitrary"),
        name=f"inproj{l}",
    )(x, sh, sc, g, wm, wf, bf, selq, selk, cq, ck, cv)


def _inproj_weights(w_in, b_f):
    d = w_in.shape[0]
    wq, wk, wv = (w_in[:, i * D_ATTN:(i + 1) * D_ATTN] for i in range(3))
    wfl = w_in[:, 3 * D_ATTN:3 * D_ATTN + N_HEADS]
    wu = w_in[:, 3 * D_ATTN + N_HEADS:]

    def pad_heads(w):
        w = w.reshape(d, N_HEADS, HEAD_DIM)
        w = jnp.pad(w, ((0, 0), (0, 0), (0, HEAD_PAD - HEAD_DIM)))
        return w.reshape(d, N_HEADS * HEAD_PAD)

    wm = jnp.concatenate([pad_heads(wq), pad_heads(wk), pad_heads(wv), wk, wv, wu], axis=1).astype(BF16)
    wf = jnp.pad(wfl, ((0, 0), (0, LANES - N_HEADS))).astype(BF16)
    bf = jnp.pad(b_f, (0, LANES - N_HEADS)).reshape(1, LANES).astype(F32)

    hw = N_HEADS * HEAD_PAD
    hd = jnp.arange(N_HEADS)
    selq = jnp.zeros((3 * LANES, hw), F32)
    selk = jnp.zeros((3 * LANES, hw), F32)
    cq = jnp.zeros((1, hw), F32)
    ck = jnp.zeros((1, hw), F32)
    cv = jnp.zeros((1, hw), F32)
    for part in range(3):
        selq = selq.at[part * LANES + hd, hd * HEAD_PAD + HEAD_DIM + part].set(1.0)
        selk = selk.at[part * LANES + hd, hd * HEAD_PAD + HEAD_DIM + 3 + part].set(-1.0)
        cq = cq.at[0, hd * HEAD_PAD + HEAD_DIM + 3 + part].set(1.0)
        ck = ck.at[0, hd * HEAD_PAD + HEAD_DIM + part].set(1.0)
    cv = cv.at[0, hd * HEAD_PAD + HEAD_DIM].set(1.0)
    return wm, wf, bf, selq.astype(BF16), selk.astype(BF16), cq, ck, cv


def _attn_kernel(fb_ref, q_ref, k_ref, v_ref, o_ref, *, r, nblk, hp):
    b = pl.program_id(0)
    g = pl.program_id(1)
    i = pl.program_id(2)
    fbase = (b * nblk) * N_HEADS + g * hp

    def scores(j):
        start = pl.multiple_of(j * r, r)
        return tuple(_dot_nt(q_ref[0, hh], k_ref[0, hh, pl.ds(start, r), :]) for hh in range(hp))

    def absorb(j, s_all, state, masked):
        start = pl.multiple_of(j * r, r)
        out = []
        for hh in range(hp):
            m, acc = state[hh]
            s = s_all[hh]
            if masked:
                row = lax.broadcasted_iota(I32, (r, r), 0)
                col = lax.broadcasted_iota(I32, (r, r), 1)
                s = jnp.where(row >= col, s, -jnp.inf)
            dlt = fb_ref[fbase + hh + i * N_HEADS] - fb_ref[fbase + hh + j * N_HEADS]
            mnew = jnp.maximum(m, jnp.max(s, axis=-1, keepdims=True) + dlt)
            p = jnp.exp2(s - (mnew - dlt))
            acc = jnp.exp2(m - mnew) * acc + _dot(p.astype(BF16), v_ref[0, hh, pl.ds(start, r), :])
            out.append((mnew, acc))
        return tuple(out)

    nu = ATTN_BLOCKS_PER_TRIP

    def several(jj, state):
        js = [nu * jj + u for u in range(nu)]
        ss = [scores(j) for j in js]
        for j, s_all in zip(js, ss):
            state = absorb(j, s_all, state, False)
        return state

    def single(j, state):
        return absorb(j, scores(j), state, False)

    state = tuple((jnp.full((r, 1), -jnp.inf, F32), jnp.zeros((r, HEAD_PAD), F32)) for _ in range(hp))
    state = lax.fori_loop(0, i // nu, several, state)
    state = lax.fori_loop(nu * (i // nu), i, single, state)
    state = absorb(i, scores(i), state, True)
    lane = lax.broadcasted_iota(I32, (r, HEAD_PAD), 1)
    for hh in range(hp):
        acc = state[hh][1]
        out = acc / acc[:, HEAD_DIM:HEAD_DIM + 1]
        o_ref[0, hh] = jnp.where(lane < HEAD_DIM, out, 0.0).astype(BF16)


def _attn(q, k, v, fb):
    b, h, l, _ = q.shape
    r = min(l, SEQ_BLOCK)
    nblk = l // r
    hp = ATTN_HEADS_PER_STEP
    return pl.pallas_call(
        functools.partial(_attn_kernel, r=r, nblk=nblk, hp=hp),
        grid=(b, h // hp, nblk),
        in_specs=[pl.BlockSpec(memory_space=pltpu.SMEM),
                  pl.BlockSpec((1, hp, r, HEAD_PAD), lambda bi, gi, i: (bi, gi, i, 0)),
                  pl.BlockSpec((1, hp, l, HEAD_PAD), lambda bi, gi, i: (bi, gi, 0, 0)),
                  pl.BlockSpec((1, hp, l, HEAD_PAD), lambda bi, gi, i: (bi, gi, 0, 0))],
        out_specs=pl.BlockSpec((1, hp, r, HEAD_PAD), lambda bi, gi, i: (bi, gi, i, 0)),
        out_shape=jax.ShapeDtypeStruct((b, h, l, HEAD_PAD), BF16),
        compiler_params=_cparams("arbitrary", "arbitrary", "arbitrary"),
        name="attn",
    )(fb, q, k, v)


def _attn_dec_kernel(q_ref, k_ref, v_ref, lfc_ref, lfr_ref, o_ref, *, lq, past, lk):
    row = lax.broadcasted_iota(I32, (lk, lk), 0)
    col = lax.broadcasted_iota(I32, (lk, lk), 1)
    tri = jnp.where(row >= col, 1.0, 0.0).astype(BF16)
    upper = jnp.where(row <= col, 1.0, 0.0).astype(BF16)
    c_hi, c_mid, c_lo = _split3(lfc_ref[0])
    fcol = _dot(tri, c_hi) + _dot(tri, c_mid) + _dot(tri, c_lo)
    r_hi, r_mid, r_lo = _split3(lfr_ref[0])
    frow = _dot(r_hi, upper) + _dot(r_mid, upper) + _dot(r_lo, upper)
    fq = fcol[past:past + lq, :]
    qpos = past + lax.broadcasted_iota(I32, (lq, lk), 0)
    kpos = lax.broadcasted_iota(I32, (lq, lk), 1)
    ok = kpos <= qpos
    qlane = lax.broadcasted_iota(I32, (lq, HEAD_PAD), 1)
    klane = lax.broadcasted_iota(I32, (lk, HEAD_PAD), 1)
    for hd in range(N_HEADS):
        odd = hd % 2 == 1
        pair = slice((hd // 2) * HEAD_PAD, (hd // 2 + 1) * HEAD_PAD)
        mine = (klane >= HEAD_DIM) if odd else (klane < HEAD_DIM)
        kh = jnp.where(mine, k_ref[0, :, pair], 0.0).astype(BF16)
        vh = jnp.where(mine, v_ref[0, :, pair], 0.0).astype(BF16)
        qh = jnp.where(qlane < HEAD_DIM, q_ref[0, hd].astype(F32), 0.0)
        if odd:
            qh = pltpu.roll(qh, HEAD_DIM, axis=1)
        s = _dot_nt(qh.astype(BF16), kh) + LOG2E * (fq[:, hd:hd + 1] - frow[hd:hd + 1, :])
        s = jnp.where(ok, s, -jnp.inf)
        m = jnp.max(s, axis=-1, keepdims=True)
        pb = jnp.exp2(s - m).astype(BF16)
        den = jnp.sum(pb.astype(F32), axis=-1, keepdims=True)
        out = _dot(pb, vh) / den
        if odd:
            out = pltpu.roll(out, HEAD_DIM, axis=1)
        o_ref[0, hd] = out.astype(BF16)


def _attn_dec(q, k_all, v_all, lf_cols, lf_rows, past, lq):
    b = q.shape[0]
    lk = k_all.shape[1]
    return pl.pallas_call(
        functools.partial(_attn_dec_kernel, lq=lq, past=past, lk=lk),
        grid=(b,),
        in_specs=[pl.BlockSpec((1, N_HEADS, lq, HEAD_PAD), lambda i: (i, 0, 0, 0)),
                  pl.BlockSpec((1, lk, D_ATTN), lambda i: (i, 0, 0)),
                  pl.BlockSpec((1, lk, D_ATTN), lambda i: (i, 0, 0)),
                  pl.BlockSpec((1, lk, LANES), lambda i: (i, 0, 0)),
                  pl.BlockSpec((1, N_HEADS, lk), lambda i: (i, 0, 0))],
        out_specs=pl.BlockSpec((1, N_HEADS, lq, HEAD_PAD), lambda i: (i, 0, 0, 0)),
        out_shape=jax.ShapeDtypeStruct((b, N_HEADS, lq, HEAD_PAD), BF16),
        compiler_params=_cparams("arbitrary"),
        name="attn_dec",
    )(q, k_all, v_all, lf_cols, lf_rows)


def _ssm_kernel(u_ref, h0r_ref, h0i_ref, br_ref, bi_ref, cc_ref, ar_ref, ai_ref, d_ref, wg_ref, bg_ref,
                gso_ref, y_ref, hr_out, hi_out, cr_ref, ci_ref, xr_s, xi_s, or_s, oi_s, *, tc):
    @pl.when(pl.program_id(1) == 0)
    def _():
        cr_ref[...] = h0r_ref[0]
        ci_ref[...] = h0i_ref[0]

    u = u_ref[0]
    ub = u.astype(BF16)
    hc = D_SSM // 2
    hs = D_STATE // 2

    def drive(b_ref):
        return jnp.concatenate([_dot(ub[:, :hc], b_ref[0:hc, 0:hs]), _dot(ub[:, hc:], b_ref[hc:, hs:])], axis=-1)

    bur = drive(br_ref)
    bui = drive(bi_ref)
    nt = tc // SCAN_ROWS
    nc = STATE_TILES

    def tile_rows(tau, c):
        r0 = (tau * nc + c) * SCAN_ROWS
        return slice(r0, r0 + SCAN_ROWS)

    for tau in range(nt):
        rows = slice(tau * SCAN_ROWS, (tau + 1) * SCAN_ROWS)
        for c in range(nc):
            xr_s[tile_rows(tau, c), :] = bur[rows, c * LANES:(c + 1) * LANES]
            xi_s[tile_rows(tau, c), :] = bui[rows, c * LANES:(c + 1) * LANES]

    ar = ar_ref[...]
    ai = ai_ref[...]
    hr = cr_ref[...]
    hi = ci_ref[...]
    for t in range(tc):
        step = pl.ds((t // SCAN_ROWS) * nc * SCAN_ROWS + t % SCAN_ROWS, nc, stride=SCAN_ROWS)
        hr, hi = ar * hr - ai * hi + xr_s[step, :], ar * hi + ai * hr + xi_s[step, :]
        or_s[step, :] = hr
        oi_s[step, :] = hi
    cr_ref[...] = hr
    ci_ref[...] = hi
    hr_out[0] = hr
    hi_out[0] = hi

    def states(o_s, c0):
        return jnp.concatenate(
            [jnp.concatenate([o_s[tile_rows(tau, c), :] for c in range(c0, c0 + nc // 2)], axis=-1)
             for tau in range(nt)], axis=0).astype(BF16)

    y = jnp.concatenate(
        [_dot(states(or_s, 0), cc_ref[0:hs, 0:hc]) + _dot(states(oi_s, 0), cc_ref[D_STATE:D_STATE + hs, 0:hc]),
         _dot(states(or_s, nc // 2), cc_ref[hs:D_STATE, hc:]) + _dot(states(oi_s, nc // 2), cc_ref[D_STATE + hs:, hc:])],
        axis=-1) + d_ref[...] * u
    gl = 0.5 * y * (1.0 + jnp.tanh(math.sqrt(2.0 / math.pi) * (y + 0.044715 * (y * y * y))))
    z = _dot(gl.astype(BF16), wg_ref[...]) + bg_ref[...]
    out = y * _sigmoid(z)
    y_ref[0] = _rms(out, gso_ref[...]).astype(BF16)


def _ssm(u, h0r, h0i, consts):
    b, l, _ = u.shape
    br, bi, cc, ar, ai, dsk, wg, bg, gso = consts
    tc = min(l, SSM_CHUNK)
    full = lambda a: pl.BlockSpec(a.shape, lambda i, j: (0,) * a.ndim)
    st = pl.BlockSpec((1, STATE_TILES, LANES), lambda i, j: (i, 0, 0))
    state = jax.ShapeDtypeStruct((b, STATE_TILES, LANES), F32)
    carry = pltpu.VMEM((STATE_TILES, LANES), F32)
    tiles = pltpu.VMEM((tc * STATE_TILES, LANES), F32)
    return pl.pallas_call(
        functools.partial(_ssm_kernel, tc=tc),
        grid=(b, l // tc),
        in_specs=[pl.BlockSpec((1, tc, D_SSM), lambda i, j: (i, j, 0)), st, st,
                  full(br), full(bi), full(cc), full(ar), full(ai), full(dsk), full(wg), full(bg), full(gso)],
        out_specs=[pl.BlockSpec((1, tc, D_SSM), lambda i, j: (i, j, 0)), st, st],
        out_shape=[jax.ShapeDtypeStruct((b, l, D_SSM), BF16), state, state],
        scratch_shapes=[carry, carry, tiles, tiles, tiles, tiles],
        compiler_params=_cparams("arbitrary", "arbitrary"),
        name=f"ssm{l}",
    )(u, h0r, h0i, br, bi, cc, ar, ai, dsk, wg, bg, gso)


def _post_kernel(x_ref, a_ref, s_ref, gt1_ref, sh2_ref, sc2_ref, gt2_ref, cnt0_ref, gao_ref, wo_ref, gffn_ref,
                 wrh_ref, wrl_ref, rb_ref, wsgu_ref, wsd_ref,
                 xm_ref, hp_ref, idx_ref, pos_ref, w_ref, cnt_ref, carry_ref, *, tl, first):
    step = pl.program_id(0) * pl.num_programs(1) + pl.program_id(1)

    @pl.when(step == 0)
    def _():
        carry_ref[...] = cnt0_ref[...] if not first else jnp.zeros_like(carry_ref)

    x = x_ref[0]
    attn = jnp.concatenate([a_ref[0, hd] for hd in range(N_HEADS)], axis=-1).astype(F32)
    ms = jnp.sum(attn * attn, axis=-1, keepdims=True) * (1.0 / D_ATTN)
    attn_n = (attn * lax.rsqrt(ms + EPS) * gao_ref[...]).astype(BF16)
    merged = jnp.concatenate([attn_n, s_ref[0]], axis=-1)
    x1 = x + gt1_ref[0] * _dot(merged, wo_ref[...])
    h2 = _rms(x1, gffn_ref[...]) * (1.0 + sc2_ref[0]) + sh2_ref[0]
    d = h2.shape[-1]
    _store_slabs(hp_ref.at[0], _pack_bf16_pair(h2[:, :d // 2], h2[:, d // 2:]))
    h_hi = h2.astype(BF16)
    h_lo = (h2 - h_hi.astype(F32)).astype(BF16)

    gu = _dot(h_hi, wsgu_ref[...])
    ds = gu.shape[-1] // 2
    g = gu[:, :ds]
    act = (g * _sigmoid(g) * gu[:, ds:]).astype(BF16)
    xm_ref[0] = x1 + gt2_ref[0] * _dot(act, wsd_ref[...])

    wrh = wrh_ref[...]
    logits = _dot_nt(wrh, h_hi) + _dot_nt(wrh, h_lo) + _dot_nt(wrl_ref[...], h_hi)
    s = _sigmoid(logits)
    sc = s + rb_ref[...]
    ge = N_EXPERTS // N_EXP_GROUPS
    neg = -jnp.inf
    gi = lax.broadcasted_iota(I32, (ge, tl), 0)
    gsc = []
    for gidx in range(N_EXP_GROUPS):
        blk = sc[gidx * ge:(gidx + 1) * ge, :]
        m1 = jnp.max(blk, axis=0, keepdims=True)
        f1 = jnp.min(jnp.where(blk == m1, gi, ge), axis=0, keepdims=True)
        m2 = jnp.max(jnp.where(gi == f1, neg, blk), axis=0, keepdims=True)
        gsc.append(m1 + m2)
    gwork = jnp.concatenate(gsc, axis=0)
    ni = lax.broadcasted_iota(I32, (N_EXP_GROUPS, tl), 0)
    gsel = jnp.zeros((N_EXP_GROUPS, tl), F32)
    for _ in range(TOPK_GROUPS):
        mx = jnp.max(gwork, axis=0, keepdims=True)
        fi = jnp.min(jnp.where(gwork == mx, ni, N_EXP_GROUPS), axis=0, keepdims=True)
        hit = ni == fi
        gsel = jnp.where(hit, 1.0, gsel)
        gwork = jnp.where(hit, neg, gwork)
    work = jnp.concatenate(
        [jnp.where(gsel[gidx:gidx + 1, :] > 0.0, sc[gidx * ge:(gidx + 1) * ge, :], neg)
         for gidx in range(N_EXP_GROUPS)], axis=0)
    ei = lax.broadcasted_iota(I32, (N_EXPERTS, tl), 0)
    chosen = jnp.zeros((N_EXPERTS, tl), F32)
    idx_rows, w_rows = [], []
    for _ in range(TOP_K):
        mx = jnp.max(work, axis=0, keepdims=True)
        fi = jnp.min(jnp.where(work == mx, ei, N_EXPERTS), axis=0, keepdims=True)
        hit = ei == fi
        w_rows.append(jnp.sum(jnp.where(hit, s, 0.0), axis=0, keepdims=True))
        idx_rows.append(fi)
        chosen = jnp.where(hit, 1.0, chosen)
        work = jnp.where(hit, neg, work)
    wt = jnp.concatenate(w_rows, axis=0)
    wt = wt / jnp.sum(wt, axis=0, keepdims=True) * ROUTED_SCALE
    idx_ref[0] = jnp.concatenate(idx_rows, axis=0)

    trow = lax.broadcasted_iota(I32, (tl, tl), 0)
    tcol = lax.broadcasted_iota(I32, (tl, tl), 1)
    before = jnp.where(trow < tcol, 1.0, 0.0).astype(BF16)
    rank = _dot(chosen.astype(BF16), before) + carry_ref[...]
    pos_rows = [jnp.sum(jnp.where(ei == idx_rows[k], rank, 0.0), axis=0, keepdims=True) for k in range(TOP_K)]
    pos_ref[0] = jnp.concatenate(pos_rows, axis=0).astype(I32)
    carry_ref[...] = carry_ref[...] + jnp.sum(chosen, axis=1, keepdims=True)
    cnt_ref[...] = carry_ref[...]

    eye = jnp.where(trow == tcol, 1.0, 0.0).astype(BF16)
    t_hi, t_mid, t_lo = _split3(wt)
    w_ref[0] = _dot_nt(eye, t_hi) + _dot_nt(eye, t_mid) + _dot_nt(eye, t_lo)


def _post(x, attn, ssm, mods, cnt0, consts, first):
    b, l, d = x.shape
    tl = min(l, SEQ_BLOCK)
    gt1, sh2, sc2, gt2 = mods
    gao, wo, gffn, wrh, wrl, rb, wsgu, wsd = consts
    slab = d // 2 // LANES
    tok = lambda w: pl.BlockSpec((1, tl, w), lambda i, j: (i, j, 0))
    full = lambda a: pl.BlockSpec(a.shape, lambda i, j: (0,) * a.ndim)
    row = pl.BlockSpec((1, 1, d), lambda i, j: (i, 0, 0))
    tk = pl.BlockSpec((1, TOP_K, tl), lambda i, j: (i, 0, j))
    return pl.pallas_call(
        functools.partial(_post_kernel, tl=tl, first=first),
        grid=(b, l // tl),
        in_specs=[tok(d), pl.BlockSpec((1, N_HEADS, tl, HEAD_PAD), lambda i, j: (i, 0, j, 0)), tok(D_SSM),
                  row, row, row, row, full(cnt0), full(gao), full(wo), full(gffn), full(wrh), full(wrl),
                  full(rb), full(wsgu), full(wsd)],
        out_specs=[tok(d), pl.BlockSpec((1, tl * slab, LANES), lambda i, j: (i, j, 0)), tk, tk, tok(TOP_K),
                   full(cnt0)],
        out_shape=[jax.ShapeDtypeStruct((b, l, d), F32), jax.ShapeDtypeStruct((b, l * slab, LANES), U32),
                   jax.ShapeDtypeStruct((b, TOP_K, l), I32), jax.ShapeDtypeStruct((b, TOP_K, l), I32),
                   jax.ShapeDtypeStruct((b, l, TOP_K), F32), jax.ShapeDtypeStruct(cnt0.shape, F32)],
        scratch_shapes=[pltpu.VMEM(cnt0.shape, F32)],
        compiler_params=_cparams("arbitrary", "arbitrary"),
        name=f"post{l}",
    )(x, attn, ssm, gt1, sh2, sc2, gt2, cnt0, gao, wo, gffn, wrh, wrl, rb, wsgu, wsd)


def _slots_kernel(pstart_ref, idx_ref, pos_ref, dest_ref):
    idx = idx_ref[0]

    def body(e, base):
        return jnp.where(idx == e, pstart_ref[e], base)

    dest_ref[0] = lax.fori_loop(0, N_EXPERTS, body, jnp.zeros_like(idx)) + pos_ref[0]


def _slots(pstart, idx, pos):
    b, k, l = idx.shape
    tl = min(l, 2048)
    spec = pl.BlockSpec((1, k, tl), lambda i, j: (i, 0, j))
    return pl.pallas_call(
        _slots_kernel,
        grid=(b, l // tl),
        in_specs=[pl.BlockSpec(memory_space=pltpu.SMEM), spec, spec],
        out_specs=spec,
        out_shape=jax.ShapeDtypeStruct(idx.shape, I32),
        compiler_params=_cparams("arbitrary", "arbitrary"),
        name=f"slots{l}",
    )(pstart, idx, pos)


INDEX_WINDOW = 2048
INDEX_ALIGN = 1024


def _experts_kernel(be_ref, nb_ref, jb_ref, ng_ref, tok_ref, h_ref, wg_ref, wu_ref, wd_ref, y_ref,
                    wgu_s, wd_s, x0, x1, idx0, idx1, xsem, isem):
    i = pl.program_id(0)
    nbu = nb_ref[0]
    e = be_ref[i]
    prev = be_ref[jnp.maximum(i - 1, 0)]
    dm = wg_ref.shape[1]
    de = wg_ref.shape[2]
    xbufs = (x0, x1)
    ibufs = (idx0, idx1)

    def window(blk, slot):
        start = pl.multiple_of((jb_ref[blk] >> 10) << 10, INDEX_ALIGN)
        return pltpu.make_async_copy(tok_ref.at[pl.ds(start, INDEX_WINDOW)], ibufs[slot], isem.at[slot])

    slab = x0.shape[0] // EXPERT_ROWS

    def gather_rows(blk, slot):
        base = jb_ref[blk] & (INDEX_ALIGN - 1)

        def body(g, c):
            for tt in range(SCAN_ROWS):
                r = g * SCAN_ROWS + tt
                src = pl.multiple_of(ibufs[slot][base + r], slab)
                dst = pl.multiple_of(r * slab, slab)
                pltpu.make_async_copy(h_ref.at[pl.ds(src, slab)], xbufs[slot].at[pl.ds(dst, slab)],
                                      xsem.at[slot]).start()
            return c
        lax.fori_loop(0, ng_ref[blk], body, 0)

    def drain_rows(blk, slot):
        def body(g, c):
            for _ in range(SCAN_ROWS):
                pltpu.make_async_copy(h_ref.at[pl.ds(0, slab)], xbufs[slot].at[pl.ds(0, slab)],
                                      xsem.at[slot]).wait()
            return c
        lax.fori_loop(0, ng_ref[blk], body, 0)

    @pl.when(i >= nbu)
    def _():
        y_ref[...] = jnp.zeros_like(y_ref)

    @pl.when(i < nbu)
    def _():
        last = nbu - 1

        @pl.when(i == 0)
        def _():
            x0[...] = jnp.zeros_like(x0)
            x1[...] = jnp.zeros_like(x1)
            first = window(0, 0)
            first.start()
            first.wait()
            gather_rows(0, 0)
            window(jnp.minimum(1, last), 1).start()

        @pl.when((i == 0) | (e != prev))
        def _():
            wgu_s[:, 0:de] = wg_ref[0].astype(BF16)
            wgu_s[:, de:2 * de] = wu_ref[0].astype(BF16)
            wd_s[...] = wd_ref[0].astype(BF16)

        def run(slot):
            nxt = jnp.minimum(i + 1, last)
            window(nxt, 1 - slot).wait()
            drain_rows(i, slot)
            gather_rows(nxt, 1 - slot)
            xa, xb = _unpack_bf16_pair(_load_slabs(xbufs[slot], EXPERT_ROWS))
            gu = _dot(xa.astype(BF16), wgu_s[0:dm // 2, :]) + _dot(xb.astype(BF16), wgu_s[dm // 2:dm, :])
            g = gu[:, :de]
            act = (g * _sigmoid(g) * gu[:, de:]).astype(BF16)
            y = _dot(act, wd_s[...])
            _store_slabs(y_ref, _pack_bf16_pair(y[:, :dm // 2], y[:, dm // 2:]))

            @pl.when(i < last)
            def _():
                window(jnp.minimum(i + 2, last), slot).start()

            @pl.when(i == last)
            def _():
                drain_rows(last, 1 - slot)

        for slot in range(2):
            pl.when(i % 2 == slot)(functools.partial(run, slot))


def _experts(blk_e, nb_used, jbase, ngroups, tok, hp, w_gate, w_up, w_down, rows):
    ne, dm, de = w_gate.shape
    slab = dm // 2 // LANES
    xrows = EXPERT_ROWS * slab
    nb = rows // EXPERT_ROWS
    grid_spec = pltpu.PrefetchScalarGridSpec(
        num_scalar_prefetch=4,
        grid=(nb,),
        in_specs=[pl.BlockSpec(memory_space=pl.ANY), pl.BlockSpec(memory_space=pl.ANY),
                  pl.BlockSpec((1, dm, de), lambda i, be, n, jb, ng: (be[i], 0, 0)),
                  pl.BlockSpec((1, dm, de), lambda i, be, n, jb, ng: (be[i], 0, 0)),
                  pl.BlockSpec((1, de, dm), lambda i, be, n, jb, ng: (be[i], 0, 0))],
        out_specs=pl.BlockSpec((xrows, LANES), lambda i, be, n, jb, ng: (i, 0)),
        scratch_shapes=[pltpu.VMEM((dm, 2 * de), BF16), pltpu.VMEM((de, dm), BF16),
                        pltpu.VMEM((xrows, LANES), U32), pltpu.VMEM((xrows, LANES), U32),
                        pltpu.SMEM((INDEX_WINDOW,), I32), pltpu.SMEM((INDEX_WINDOW,), I32),
                        pltpu.SemaphoreType.DMA((2,)), pltpu.SemaphoreType.DMA((2,))],
    )
    return pl.pallas_call(
        _experts_kernel,
        grid_spec=grid_spec,
        out_shape=jax.ShapeDtypeStruct((rows * slab, LANES), U32),
        compiler_params=_cparams("arbitrary"),
        name="experts",
    )(blk_e, nb_used, jbase, ngroups, tok, hp, w_gate, w_up, w_down)


def _combine_kernel(dcur_ref, dnext_ref, ys_ref, w_ref, xm_ref, gt2_ref, gfin_ref, o_ref, buf0, buf1, sem, *, tl):
    bufs = (buf0, buf1)
    s = pl.program_id(0)
    last = pl.num_programs(0) - 1
    group = SCAN_ROWS
    ngroups = tl // group
    slab = buf0.shape[2] // group
    half = slab * LANES

    def gather(dref, g, to):
        for tt in range(group):
            for k in range(TOP_K):
                src = pl.multiple_of(dref[(g * group + tt) * TOP_K + k], slab)
                pltpu.make_async_copy(ys_ref.at[pl.ds(src, slab)], bufs[to].at[k, g, pl.ds(tt * slab, slab)],
                                      sem.at[to]).start()

    def drain(which):
        def body(t, c):
            for k in range(TOP_K):
                pltpu.make_async_copy(ys_ref.at[pl.ds(0, slab)], bufs[which].at[0, 0, pl.ds(0, slab)],
                                      sem.at[which]).wait()
            return c
        lax.fori_loop(0, tl, body, 0)

    @pl.when(s == 0)
    def _():
        def body(g, c):
            gather(dcur_ref, g, 0)
            return c
        lax.fori_loop(0, ngroups, body, 0)

    gt2 = gt2_ref[0]
    gfin = gfin_ref[...]

    def run(slot):
        drain(slot)

        def body(g, c):
            gather(dnext_ref, g, 1 - slot)
            t0 = pl.multiple_of(g * group, group)
            w = w_ref[pl.ds(t0, group), :]
            acc_a = jnp.zeros((group, half), F32)
            acc_b = jnp.zeros((group, half), F32)
            for k in range(TOP_K):
                ya, yb = _unpack_bf16_pair(_load_slabs(bufs[slot].at[k, g], group))
                wk = w[:, k:k + 1]
                acc_a = acc_a + wk * ya
                acc_b = acc_b + wk * yb
            routed = jnp.concatenate([acc_a, acc_b], axis=-1)
            x2 = xm_ref[pl.ds(t0, group), :] + gt2 * routed
            o_ref[pl.ds(t0, group), :] = _rms(x2, gfin)
            return c

        lax.fori_loop(0, ngroups, body, 0)

        @pl.when(s == last)
        def _():
            drain(1 - slot)

    for slot in range(2):
        pl.when(s % 2 == slot)(functools.partial(run, slot))


def _combine(dest, ys, w, xm, gt2, gfin):
    b, l, d = xm.shape
    tl = min(l, COMBINE_ROWS)
    nl = l // tl
    n = b * nl
    slab = d // 2 // LANES
    tile = pltpu.VMEM((TOP_K, tl // SCAN_ROWS, SCAN_ROWS * slab, LANES), U32)
    out = pl.pallas_call(
        functools.partial(_combine_kernel, tl=tl),
        grid=(n,),
        in_specs=[pl.BlockSpec((tl * TOP_K,), lambda s: (s,), memory_space=pltpu.SMEM),
                  pl.BlockSpec((tl * TOP_K,), lambda s: (jnp.minimum(s + 1, n - 1),), memory_space=pltpu.SMEM),
                  pl.BlockSpec(memory_space=pl.ANY),
                  pl.BlockSpec((tl, TOP_K), lambda s: (s, 0)),
                  pl.BlockSpec((tl, d), lambda s: (s, 0)),
                  pl.BlockSpec((1, 1, d), lambda s: (s // nl, 0, 0)),
                  pl.BlockSpec(gfin.shape, lambda s: (0, 0))],
        out_specs=pl.BlockSpec((tl, d), lambda s: (s, 0)),
        out_shape=jax.ShapeDtypeStruct((b * l, d), F32),
        scratch_shapes=[tile, tile, pltpu.SemaphoreType.DMA((2,))],
        compiler_params=_cparams("arbitrary"),
        name=f"combine{l}",
    )(dest, dest, ys, w.reshape(b * l, TOP_K), xm.reshape(b * l, d), gt2, gfin)
    return out.reshape(b, l, d)


def _block_diag(m):
    g, a, b = m.shape
    eye = jnp.eye(g, dtype=m.dtype)
    return jnp.einsum("gab,gh->gahb", m, eye).reshape(g * a, g * b)


def kernel(x_prompt, x_sample, c_prompt, c_sample, cache_k, cache_v, cache_logf, state_ssm_re, state_ssm_im,
           w_ada, b_ada, g_mix, w_in, b_f, lam_re, lam_im, log_dt, ssm_b_re, ssm_b_im, ssm_c_re, ssm_c_im,
           ssm_d, w_glu, b_glu, g_attn_out, g_ssm_out, w_out, g_ffn, w_router, router_bias,
           w_gate, w_up, w_down, ws_gate, ws_up, ws_down, g_final):
    assert w_ada.shape[0] == 1, "single layer"
    bp, lp, d = x_prompt.shape
    bs, ls, _ = x_sample.shape
    past = cache_k.shape[2]

    nb = bp + bs
    nbp = -(-nb // 8) * 8
    c_all = jnp.concatenate([c_prompt, c_sample, jnp.zeros((nbp - nb, d), F32)], axis=0)
    mod = _prep(c_all, w_ada[0], b_ada[0]).reshape(nbp, 6, 1, d)
    mod_p = [mod[:bp, i] for i in range(6)]
    mod_s = [mod[bp:nb, i] for i in range(6)]

    a_re, a_im, bb_re, bb_im = _disc(lam_re[0], lam_im[0], log_dt[0],
                                     jnp.swapaxes(ssm_b_re[0], 1, 2), jnp.swapaxes(ssm_b_im[0], 1, 2))
    a_re = a_re.reshape(STATE_TILES, LANES)
    a_im = a_im.reshape(STATE_TILES, LANES)
    bd_re = _block_diag(bb_re).astype(BF16)
    bd_im = _block_diag(bb_im).astype(BF16)
    cc = jnp.concatenate([_block_diag(jnp.swapaxes(ssm_c_re[0], 1, 2)),
                          -_block_diag(jnp.swapaxes(ssm_c_im[0], 1, 2))], axis=0).astype(BF16)
    ssm_tail = (ssm_d[0].reshape(1, D_SSM), w_glu[0].astype(BF16), b_glu[0].reshape(1, D_SSM),
                g_ssm_out[0].reshape(1, D_SSM))

    ssm_consts = (bd_re, bd_im, cc, a_re, a_im) + ssm_tail

    inw = _inproj_weights(w_in[0], b_f[0])
    g_mix2 = g_mix[0].reshape(1, d)

    wo = w_out[0]
    wo_attn = jnp.pad(wo[:D_ATTN].reshape(N_HEADS, HEAD_DIM, d), ((0, 0), (0, HEAD_PAD - HEAD_DIM), (0, 0)))
    wo_pad = jnp.concatenate([wo_attn.reshape(N_HEADS * HEAD_PAD, d), wo[D_ATTN:]], axis=0).astype(BF16)
    gao_pad = jnp.pad(g_attn_out[0].reshape(N_HEADS, HEAD_DIM),
                      ((0, 0), (0, HEAD_PAD - HEAD_DIM))).reshape(1, N_HEADS * HEAD_PAD)
    wr_t = w_router[0].T
    wr_hi = wr_t.astype(BF16)
    wr_lo = (wr_t - wr_hi.astype(F32)).astype(BF16)
    post_consts = (gao_pad, wo_pad, g_ffn[0].reshape(1, d), wr_hi, wr_lo, router_bias[0].reshape(N_EXPERTS, 1),
                   jnp.concatenate([ws_gate[0], ws_up[0]], axis=1).astype(BF16), ws_down[0].astype(BF16))

    def mixer(x, modv, attn_fn, h0r, h0i):
        sh1, sc1, gt1, sh2, sc2, gt2 = modv
        q, k, v, kf, vf, lf, fb, u = _inproj(x, sh1, sc1, g_mix2, inw)
        attn = attn_fn(q, k, v, kf, vf, lf, fb)
        ssm, hr, hi = _ssm(u, h0r, h0i, ssm_consts)
        return attn, ssm, (kf, vf, lf, hr, hi), (gt1, sh2, sc2, gt2)

    def attn_prompt(q, k, v, kf, vf, lf, fb):
        return _attn(q, k, v, fb[:, :, 0, :N_HEADS].reshape(-1))

    zeros_p = jnp.zeros((bp, STATE_TILES, LANES), F32)
    attn_p, ssm_p, new_p, m_p = mixer(x_prompt, mod_p, attn_prompt, zeros_p, zeros_p)

    lk = -(-(past + ls) // LANES) * LANES
    padk = lk - past - ls

    def attn_sample(q, k, v, kf, vf, lf, fb):
        k_all = jnp.concatenate([cache_k[0].reshape(bs, past, D_ATTN), kf,
                                 jnp.zeros((bs, padk, D_ATTN), F32)], axis=1)
        v_all = jnp.concatenate([cache_v[0].reshape(bs, past, D_ATTN), vf,
                                 jnp.zeros((bs, padk, D_ATTN), F32)], axis=1)
        lf_all = jnp.concatenate([cache_logf[0], lf, jnp.zeros((bs, padk, N_HEADS), F32)], axis=1)
        lf_cols = jnp.pad(lf_all, ((0, 0), (0, 0), (0, LANES - N_HEADS)))
        lf_rows = jnp.swapaxes(lf_all, 1, 2)
        return _attn_dec(q, k_all, v_all, lf_cols, lf_rows, past, ls)

    attn_s, ssm_s, new_s, m_s = mixer(x_sample, mod_s, attn_sample,
                                      state_ssm_re[0].reshape(bs, STATE_TILES, LANES),
                                      state_ssm_im[0].reshape(bs, STATE_TILES, LANES))

    cnt0 = jnp.zeros((N_EXPERTS, 1), F32)
    xm_p, hp_p, idx_p, pos_p, w_p, cnt1 = _post(x_prompt, attn_p, ssm_p, m_p, cnt0, post_consts, True)
    xm_s, hp_s, idx_s, pos_s, w_s, cnt2 = _post(x_sample, attn_s, ssm_s, m_s, cnt1, post_consts, False)

    counts = cnt2[:, 0].astype(I32)
    padded = (counts + EXPERT_ROWS - 1) // EXPERT_ROWS * EXPERT_ROWS
    pend = jnp.cumsum(padded)
    pstart = pend - padded
    tp, ts = bp * lp, bs * ls
    n_blocks = -(-((tp + ts) * TOP_K) // EXPERT_ROWS) + N_EXPERTS
    rows = n_blocks * EXPERT_ROWS
    dest_p = _slots(pstart, idx_p, pos_p)
    dest_s = _slots(pstart, idx_s, pos_s)
    nb_used = (pend[-1] // EXPERT_ROWS).astype(I32).reshape(1)
    blk_row = jnp.minimum(jnp.arange(n_blocks, dtype=I32), nb_used[0] - 1) * EXPERT_ROWS
    blk_e = jnp.minimum(jnp.sum((pend[None, :] <= blk_row[:, None]).astype(I32), axis=1), N_EXPERTS - 1)

    slab = d // 2 // LANES
    dflat_p = jnp.swapaxes(dest_p, 1, 2).reshape(-1)
    dflat_s = jnp.swapaxes(dest_s, 1, 2).reshape(-1)
    n_pairs = (tp + ts) * TOP_K
    order = jnp.argsort(jnp.concatenate([dflat_p, dflat_s])).astype(I32)
    n_tok = ((n_pairs >> 10) + 3) << 10
    tok = jnp.concatenate([order // TOP_K * slab, jnp.zeros((n_tok - n_pairs,), I32)])
    cstart = jnp.cumsum(counts) - counts
    blk_ids = jnp.arange(n_blocks, dtype=I32)
    blk_off = blk_ids * EXPERT_ROWS - pstart[blk_e]
    used = blk_ids < nb_used[0]
    jbase = jnp.where(used, cstart[blk_e] + blk_off, 0).astype(I32)
    real = jnp.clip(counts[blk_e] - blk_off, 0, EXPERT_ROWS)
    ngroups = jnp.where(used, (real + SCAN_ROWS - 1) // SCAN_ROWS, 0).astype(I32)
    hp_all = jnp.concatenate([hp_p.reshape(tp * slab, LANES), hp_s.reshape(ts * slab, LANES)], axis=0)
    ys = _experts(blk_e, nb_used, jbase, ngroups, tok, hp_all, w_gate[0], w_up[0], w_down[0], rows)

    gfin = g_final.reshape(1, d)
    y_p = _combine(dflat_p * slab, ys, w_p, xm_p, m_p[3], gfin)
    y_s = _combine(dflat_s * slab, ys, w_s, xm_s, m_s[3], gfin)

    def pack(new, b, l):
        kf, vf, lf, hr, hi = new
        return (kf.reshape(1, b, l, N_HEADS, HEAD_DIM), vf.reshape(1, b, l, N_HEADS, HEAD_DIM),
                lf.reshape(1, b, l, N_HEADS),
                hr.reshape(1, b, N_SSM_GROUPS, SSM_STATE), hi.reshape(1, b, N_SSM_GROUPS, SSM_STATE))

    return (y_p, y_s) + pack(new_p, bp, lp) + pack(new_s, bs, ls)
```

```python
import functools
import math

import jax
import jax.numpy as jnp
from jax import lax
from jax.experimental import pallas as pl
from jax.experimental.pallas import tpu as pltpu

F32 = jnp.float32
BF16 = jnp.bfloat16
U32 = jnp.uint32
I32 = jnp.int32

N_HEADS = 8
HEAD_DIM = 64
HEAD_PAD = 128
D_ATTN = N_HEADS * HEAD_DIM
SSM_GROUP = 16
N_SSM_GROUPS = 32
SSM_STATE = 64
D_SSM = SSM_GROUP * N_SSM_GROUPS
D_STATE = N_SSM_GROUPS * SSM_STATE
N_EXPERTS = 256
TOP_K = 8
N_EXP_GROUPS = 8
TOPK_GROUPS = 4
ROUTED_SCALE = 2.5
EPS = 1e-6
LANES = 128

SEQ_BLOCK = 512
SSM_CHUNK = 512
EXPERT_ROWS = 1024
ATTN_HEADS_PER_STEP = 2
ATTN_BLOCKS_PER_TRIP = 4
LOG2E = math.log2(math.e)
COMBINE_ROWS = 256
DISPATCH_ROWS = 512
VMEM_LIMIT = 56 * 1024 * 1024

_NT = (((1,), (1,)), ((), ()))


def _cparams(*sem):
    return pltpu.CompilerParams(dimension_semantics=sem, vmem_limit_bytes=VMEM_LIMIT)


def _split3(x):
    hi = x.astype(BF16)
    r = x - hi.astype(F32)
    mid = r.astype(BF16)
    lo = (r - mid.astype(F32)).astype(BF16)
    return hi, mid, lo


def _dot(a, b):
    return jnp.dot(a, b, preferred_element_type=F32)


def _dot_nt(a, b):
    return lax.dot_general(a, b, _NT, preferred_element_type=F32)


def _rms(x, g):
    return x * lax.rsqrt(jnp.mean(x * x, axis=-1, keepdims=True) + EPS) * g


def _sigmoid(x):
    return 1.0 / (1.0 + jnp.exp(-x))


def _pack_bf16_pair(a, b):
    ab = lax.bitcast_convert_type(a.astype(BF16).astype(F32), U32)
    bb = lax.bitcast_convert_type(b.astype(BF16).astype(F32), U32)
    return (ab >> 16) | (bb & jnp.uint32(0xFFFF0000))


def _unpack_bf16_pair(w):
    a = lax.bitcast_convert_type(w << 16, F32)
    b = lax.bitcast_convert_type(w & jnp.uint32(0xFFFF0000), F32)
    return a, b


def _store_slabs(ref, val):
    n, width = val.shape
    tiles = width // LANES
    for c in range(tiles):
        ref[pl.ds(c, n, stride=tiles), :] = val[:, c * LANES:(c + 1) * LANES]


def _load_slabs(ref, n):
    tiles = ref.shape[0] // n
    return jnp.concatenate([ref[pl.ds(c, n, stride=tiles), :] for c in range(tiles)], axis=-1)


def _prep_kernel(c_ref, w_ref, b_ref, o_ref):
    c = c_ref[...]
    a = c * _sigmoid(c)
    a_hi = a.astype(BF16)
    a_lo = (a - a_hi.astype(F32)).astype(BF16)
    w = w_ref[...]
    w_hi = w.astype(BF16)
    w_lo = (w - w_hi.astype(F32)).astype(BF16)
    o_ref[...] = _dot(a_hi, w_hi) + _dot(a_lo, w_hi) + _dot(a_hi, w_lo) + b_ref[...]


def _prep(c_all, w_ada, b_ada):
    n, d = c_all.shape
    nout = w_ada.shape[1]
    tn = 1024
    return pl.pallas_call(
        _prep_kernel,
        grid=(nout // tn,),
        in_specs=[pl.BlockSpec((n, d), lambda j: (0, 0)),
                  pl.BlockSpec((d, tn), lambda j: (0, j)),
                  pl.BlockSpec((1, tn), lambda j: (0, j))],
        out_specs=pl.BlockSpec((n, tn), lambda j: (0, j)),
        out_shape=jax.ShapeDtypeStruct((n, nout), F32),
        compiler_params=_cparams("arbitrary"),
        name="prep",
    )(c_all, w_ada, b_ada.reshape(1, nout))


def _disc_kernel(lr_ref, li_ref, ldt_ref, br_ref, bi_ref, ar_ref, ai_ref, bbr_ref, bbi_ref):
    lr = lr_ref[...]
    li = li_ref[...]
    dt = jnp.exp(ldt_ref[...])
    er = jnp.exp(lr * dt)
    ang = li * dt
    ar = er * jnp.cos(ang)
    ai = er * jnp.sin(ang)
    ar_ref[...] = ar
    ai_ref[...] = ai
    den = lr * lr + li * li
    nr = ((ar - 1.0) * lr + ai * li) / den
    ni = (ai * lr - (ar - 1.0) * li) / den
    nr3 = nr[:, None, :]
    ni3 = ni[:, None, :]
    br = br_ref[...]
    bi = bi_ref[...]
    bbr_ref[...] = nr3 * br - ni3 * bi
    bbi_ref[...] = nr3 * bi + ni3 * br


def _disc(lam_re, lam_im, log_dt, bt_re, bt_im):
    g, p = lam_re.shape
    c = bt_re.shape[1]
    return pl.pallas_call(
        _disc_kernel,
        out_shape=(jax.ShapeDtypeStruct((g, p), F32), jax.ShapeDtypeStruct((g, p), F32),
                   jax.ShapeDtypeStruct((g, c, p), F32), jax.ShapeDtypeStruct((g, c, p), F32)),
        name="disc",
    )(lam_re, lam_im, log_dt.reshape(g, 1), bt_re, bt_im)


SCAN_ROWS = 8
STATE_TILES = D_STATE // LANES


def _inproj_kernel(x_ref, sh_ref, sc_ref, g_ref, wm_ref, wf_ref, bf_ref, selq_ref, selk_ref,
                   cq_ref, ck_ref, cv_ref,
                   q_ref, k_ref, v_ref, kf_ref, vf_ref, lf_ref, fb_ref, u_ref, carry_ref, *, tl):
    @pl.when(pl.program_id(1) == 0)
    def _():
        carry_ref[...] = jnp.zeros_like(carry_ref)

    x = x_ref[0]
    h = _rms(x, g_ref[...]) * (1.0 + sc_ref[0]) + sh_ref[0]
    hb = h.astype(BF16)
    main = _dot(hb, wm_ref[...])
    hw = N_HEADS * HEAD_PAD
    qp = main[:, 0:hw] * (HEAD_DIM ** -0.5 * LOG2E)
    kp = main[:, hw:2 * hw]
    vp = main[:, 2 * hw:3 * hw]
    u_ref[0] = main[:, 3 * hw:3 * hw + D_SSM]
    for hd in range(N_HEADS):
        src = slice(hd * HEAD_PAD, hd * HEAD_PAD + HEAD_DIM)
        dst = slice(hd * HEAD_DIM, (hd + 1) * HEAD_DIM)
        kf_ref[0, :, dst] = kp[:, src]
        vf_ref[0, :, dst] = vp[:, src]

    fl = _dot(hb, wf_ref[...]) + bf_ref[...]
    lf = jnp.minimum(fl, 0.0) - jnp.log1p(jnp.exp(-jnp.abs(fl)))
    lf_ref[0] = lf[:, 0:N_HEADS]

    row = lax.broadcasted_iota(I32, (tl, tl), 0)
    col = lax.broadcasted_iota(I32, (tl, tl), 1)
    tri = jnp.where(row >= col, 1.0, 0.0).astype(BF16)
    hi, mid, lo = _split3(lf)
    frel = _dot(tri, hi) + _dot(tri, mid) + _dot(tri, lo)
    fb_ref[0, 0] = carry_ref[...] * LOG2E
    carry_ref[...] = carry_ref[...] + frel[tl - 1:tl, :]

    fcat = jnp.concatenate(_split3(frel * LOG2E), axis=-1)
    q_aug = (qp + _dot(fcat, selq_ref[...]) + cq_ref[...]).astype(BF16)
    k_aug = (kp + _dot(fcat, selk_ref[...]) + ck_ref[...]).astype(BF16)
    v_aug = (vp + cv_ref[...]).astype(BF16)
    for hd in range(N_HEADS):
        sl = slice(hd * HEAD_PAD, (hd + 1) * HEAD_PAD)
        q_ref[0, hd] = q_aug[:, sl]
        k_ref[0, hd] = k_aug[:, sl]
        v_ref[0, hd] = v_aug[:, sl]


def _inproj(x, sh, sc, g, wts):
    b, l, d = x.shape
    tl = min(l, SEQ_BLOCK)
    nl = l // tl
    wm, wf, bf, selq, selk, cq, ck, cv = wts
    hm = jax.ShapeDtypeStruct((b, N_HEADS, l, HEAD_PAD), BF16)
    hm_spec = pl.BlockSpec((1, N_HEADS, tl, HEAD_PAD), lambda i, j: (i, 0, j, 0))
    tok = lambda w: pl.BlockSpec((1, tl, w), lambda i, j: (i, j, 0))
    full = lambda a: pl.BlockSpec(a.shape, lambda i, j: (0,) * a.ndim)
    row = pl.BlockSpec((1, 1, d), lambda i, j: (i, 0, 0))
    return pl.pallas_call(
        functools.partial(_inproj_kernel, tl=tl),
        grid=(b, nl),
        in_specs=[tok(d), row, row, full(g), full(wm), full(wf), full(bf), full(selq), full(selk),
                  full(cq), full(ck), full(cv)],
        out_specs=[hm_spec, hm_spec, hm_spec, tok(D_ATTN), tok(D_ATTN), tok(N_HEADS),
                   pl.BlockSpec((1, 1, 1, LANES), lambda i, j: (i, j, 0, 0)), tok(D_SSM)],
        out_shape=[hm, hm, hm,
                   jax.ShapeDtypeStruct((b, l, D_ATTN), F32), jax.ShapeDtypeStruct((b, l, D_ATTN), F32),
                   jax.ShapeDtypeStruct((b, l, N_HEADS), F32),
                   jax.ShapeDtypeStruct((b, nl, 1, LANES), F32),
                   jax.ShapeDtypeStruct((b, l, D_SSM), F32)],
        scratch_shapes=[pltpu.VMEM((1, LANES), F32)],
        compiler_params=_cparams("arbitrary", "arbitrary"),
        name=f"inproj{l}",
    )(x, sh, sc, g, wm, wf, bf, selq, selk, cq, ck, cv)


def _inproj_weights(w_in, b_f):
    d = w_in.shape[0]
    wq, wk, wv = (w_in[:, i * D_ATTN:(i + 1) * D_ATTN] for i in range(3))
    wfl = w_in[:, 3 * D_ATTN:3 * D_ATTN + N_HEADS]
    wu = w_in[:, 3 * D_ATTN + N_HEADS:]

    def pad_heads(w):
        w = w.reshape(d, N_HEADS, HEAD_DIM)
        w = jnp.pad(w, ((0, 0), (0, 0), (0, HEAD_PAD - HEAD_DIM)))
        return w.reshape(d, N_HEADS * HEAD_PAD)

    wm = jnp.concatenate([pad_heads(wq), pad_heads(wk), pad_heads(wv), wu], axis=1).astype(BF16)
    wf = jnp.pad(wfl, ((0, 0), (0, LANES - N_HEADS))).astype(BF16)
    bf = jnp.pad(b_f, (0, LANES - N_HEADS)).reshape(1, LANES).astype(F32)

    hw = N_HEADS * HEAD_PAD
    hd = jnp.arange(N_HEADS)
    selq = jnp.zeros((3 * LANES, hw), F32)
    selk = jnp.zeros((3 * LANES, hw), F32)
    cq = jnp.zeros((1, hw), F32)
    ck = jnp.zeros((1, hw), F32)
    cv = jnp.zeros((1, hw), F32)
    for part in range(3):
        selq = selq.at[part * LANES + hd, hd * HEAD_PAD + HEAD_DIM + part].set(1.0)
        selk = selk.at[part * LANES + hd, hd * HEAD_PAD + HEAD_DIM + 3 + part].set(-1.0)
        cq = cq.at[0, hd * HEAD_PAD + HEAD_DIM + 3 + part].set(1.0)
        ck = ck.at[0, hd * HEAD_PAD + HEAD_DIM + part].set(1.0)
    cv = cv.at[0, hd * HEAD_PAD + HEAD_DIM].set(1.0)
    return wm, wf, bf, selq.astype(BF16), selk.astype(BF16), cq, ck, cv


def _attn_kernel(fb_ref, q_ref, k_ref, v_ref, o_ref, *, r, nblk, hp):
    b = pl.program_id(0)
    g = pl.program_id(1)
    i = pl.program_id(2)
    fbase = (b * nblk) * N_HEADS + g * hp

    def scores(j):
        start = pl.multiple_of(j * r, r)
        return tuple(_dot_nt(q_ref[0, hh], k_ref[0, hh, pl.ds(start, r), :]) for hh in range(hp))

    def absorb(j, s_all, state, masked):
        start = pl.multiple_of(j * r, r)
        out = []
        for hh in range(hp):
            m, acc = state[hh]
            s = s_all[hh]
            if masked:
                row = lax.broadcasted_iota(I32, (r, r), 0)
                col = lax.broadcasted_iota(I32, (r, r), 1)
                s = jnp.where(row >= col, s, -jnp.inf)
            dlt = fb_ref[fbase + hh + i * N_HEADS] - fb_ref[fbase + hh + j * N_HEADS]
            mnew = jnp.maximum(m, jnp.max(s, axis=-1, keepdims=True) + dlt)
            p = jnp.exp2(s - (mnew - dlt))
            acc = jnp.exp2(m - mnew) * acc + _dot(p.astype(BF16), v_ref[0, hh, pl.ds(start, r), :])
            out.append((mnew, acc))
        return tuple(out)

    nu = ATTN_BLOCKS_PER_TRIP

    def several(jj, state):
        js = [nu * jj + u for u in range(nu)]
        ss = [scores(j) for j in js]
        for j, s_all in zip(js, ss):
            state = absorb(j, s_all, state, False)
        return state

    def single(j, state):
        return absorb(j, scores(j), state, False)

    state = tuple((jnp.full((r, 1), -jnp.inf, F32), jnp.zeros((r, HEAD_PAD), F32)) for _ in range(hp))
    state = lax.fori_loop(0, i // nu, several, state)
    state = lax.fori_loop(nu * (i // nu), i, single, state)
    state = absorb(i, scores(i), state, True)
    lane = lax.broadcasted_iota(I32, (r, HEAD_PAD), 1)
    for hh in range(hp):
        acc = state[hh][1]
        out = acc / acc[:, HEAD_DIM:HEAD_DIM + 1]
        o_ref[0, hh] = jnp.where(lane < HEAD_DIM, out, 0.0).astype(BF16)


def _attn(q, k, v, fb):
    b, h, l, _ = q.shape
    r = min(l, SEQ_BLOCK)
    nblk = l // r
    hp = ATTN_HEADS_PER_STEP
    return pl.pallas_call(
        functools.partial(_attn_kernel, r=r, nblk=nblk, hp=hp),
        grid=(b, h // hp, nblk),
        in_specs=[pl.BlockSpec(memory_space=pltpu.SMEM),
                  pl.BlockSpec((1, hp, r, HEAD_PAD), lambda bi, gi, i: (bi, gi, i, 0)),
                  pl.BlockSpec((1, hp, l, HEAD_PAD), lambda bi, gi, i: (bi, gi, 0, 0)),
                  pl.BlockSpec((1, hp, l, HEAD_PAD), lambda bi, gi, i: (bi, gi, 0, 0))],
        out_specs=pl.BlockSpec((1, hp, r, HEAD_PAD), lambda bi, gi, i: (bi, gi, i, 0)),
        out_shape=jax.ShapeDtypeStruct((b, h, l, HEAD_PAD), BF16),
        compiler_params=_cparams("arbitrary", "arbitrary", "arbitrary"),
        name="attn",
    )(fb, q, k, v)


def _attn_dec_kernel(q_ref, k_ref, v_ref, lfc_ref, lfr_ref, o_ref, *, lq, past, lk):
    row = lax.broadcasted_iota(I32, (lk, lk), 0)
    col = lax.broadcasted_iota(I32, (lk, lk), 1)
    tri = jnp.where(row >= col, 1.0, 0.0).astype(BF16)
    upper = jnp.where(row <= col, 1.0, 0.0).astype(BF16)
    c_hi, c_mid, c_lo = _split3(lfc_ref[0])
    fcol = _dot(tri, c_hi) + _dot(tri, c_mid) + _dot(tri, c_lo)
    r_hi, r_mid, r_lo = _split3(lfr_ref[0])
    frow = _dot(r_hi, upper) + _dot(r_mid, upper) + _dot(r_lo, upper)
    fq = fcol[past:past + lq, :]
    qpos = past + lax.broadcasted_iota(I32, (lq, lk), 0)
    kpos = lax.broadcasted_iota(I32, (lq, lk), 1)
    ok = kpos <= qpos
    qlane = lax.broadcasted_iota(I32, (lq, HEAD_PAD), 1)
    klane = lax.broadcasted_iota(I32, (lk, HEAD_PAD), 1)
    for hd in range(N_HEADS):
        odd = hd % 2 == 1
        pair = slice((hd // 2) * HEAD_PAD, (hd // 2 + 1) * HEAD_PAD)
        mine = (klane >= HEAD_DIM) if odd else (klane < HEAD_DIM)
        kh = jnp.where(mine, k_ref[0, :, pair], 0.0).astype(BF16)
        vh = jnp.where(mine, v_ref[0, :, pair], 0.0).astype(BF16)
        qh = jnp.where(qlane < HEAD_DIM, q_ref[0, hd].astype(F32), 0.0)
        if odd:
            qh = pltpu.roll(qh, HEAD_DIM, axis=1)
        s = _dot_nt(qh.astype(BF16), kh) + LOG2E * (fq[:, hd:hd + 1] - frow[hd:hd + 1, :])
        s = jnp.where(ok, s, -jnp.inf)
        m = jnp.max(s, axis=-1, keepdims=True)
        pb = jnp.exp2(s - m).astype(BF16)
        den = jnp.sum(pb.astype(F32), axis=-1, keepdims=True)
        out = _dot(pb, vh) / den
        if odd:
            out = pltpu.roll(out, HEAD_DIM, axis=1)
        o_ref[0, hd] = out.astype(BF16)


def _attn_dec(q, k_all, v_all, lf_cols, lf_rows, past, lq):
    b = q.shape[0]
    lk = k_all.shape[1]
    return pl.pallas_call(
        functools.partial(_attn_dec_kernel, lq=lq, past=past, lk=lk),
        grid=(b,),
        in_specs=[pl.BlockSpec((1, N_HEADS, lq, HEAD_PAD), lambda i: (i, 0, 0, 0)),
                  pl.BlockSpec((1, lk, D_ATTN), lambda i: (i, 0, 0)),
                  pl.BlockSpec((1, lk, D_ATTN), lambda i: (i, 0, 0)),
                  pl.BlockSpec((1, lk, LANES), lambda i: (i, 0, 0)),
                  pl.BlockSpec((1, N_HEADS, lk), lambda i: (i, 0, 0))],
        out_specs=pl.BlockSpec((1, N_HEADS, lq, HEAD_PAD), lambda i: (i, 0, 0, 0)),
        out_shape=jax.ShapeDtypeStruct((b, N_HEADS, lq, HEAD_PAD), BF16),
        compiler_params=_cparams("arbitrary"),
        name="attn_dec",
    )(q, k_all, v_all, lf_cols, lf_rows)


def _ssm_kernel(u_ref, h0r_ref, h0i_ref, br_ref, bi_ref, cc_ref, ar_ref, ai_ref, d_ref, wg_ref, bg_ref,
                gso_ref, y_ref, hr_out, hi_out, cr_ref, ci_ref, xr_s, xi_s, or_s, oi_s, *, tc):
    @pl.when(pl.program_id(1) == 0)
    def _():
        cr_ref[...] = h0r_ref[0]
        ci_ref[...] = h0i_ref[0]

    u = u_ref[0]
    ub = u.astype(BF16)
    hc = D_SSM // 2
    hs = D_STATE // 2

    def drive(b_ref):
        return jnp.concatenate([_dot(ub[:, :hc], b_ref[0:hc, 0:hs]), _dot(ub[:, hc:], b_ref[hc:, hs:])], axis=-1)

    bur = drive(br_ref)
    bui = drive(bi_ref)
    nt = tc // SCAN_ROWS
    nc = STATE_TILES

    def tile_rows(tau, c):
        r0 = (tau * nc + c) * SCAN_ROWS
        return slice(r0, r0 + SCAN_ROWS)

    for tau in range(nt):
        rows = slice(tau * SCAN_ROWS, (tau + 1) * SCAN_ROWS)
        for c in range(nc):
            xr_s[tile_rows(tau, c), :] = bur[rows, c * LANES:(c + 1) * LANES]
            xi_s[tile_rows(tau, c), :] = bui[rows, c * LANES:(c + 1) * LANES]

    ar = ar_ref[...]
    ai = ai_ref[...]
    hr = cr_ref[...]
    hi = ci_ref[...]
    for t in range(tc):
        step = pl.ds((t // SCAN_ROWS) * nc * SCAN_ROWS + t % SCAN_ROWS, nc, stride=SCAN_ROWS)
        hr, hi = ar * hr - ai * hi + xr_s[step, :], ar * hi + ai * hr + xi_s[step, :]
        or_s[step, :] = hr
        oi_s[step, :] = hi
    cr_ref[...] = hr
    ci_ref[...] = hi
    hr_out[0] = hr
    hi_out[0] = hi

    def states(o_s, c0):
        return jnp.concatenate(
            [jnp.concatenate([o_s[tile_rows(tau, c), :] for c in range(c0, c0 + nc // 2)], axis=-1)
             for tau in range(nt)], axis=0).astype(BF16)

    y = jnp.concatenate(
        [_dot(states(or_s, 0), cc_ref[0:hs, 0:hc]) + _dot(states(oi_s, 0), cc_ref[D_STATE:D_STATE + hs, 0:hc]),
         _dot(states(or_s, nc // 2), cc_ref[hs:D_STATE, hc:]) + _dot(states(oi_s, nc // 2), cc_ref[D_STATE + hs:, hc:])],
        axis=-1) + d_ref[...] * u
    gl = 0.5 * y * (1.0 + jnp.tanh(math.sqrt(2.0 / math.pi) * (y + 0.044715 * (y * y * y))))
    z = _dot(gl.astype(BF16), wg_ref[...]) + bg_ref[...]
    out = y * _sigmoid(z)
    y_ref[0] = _rms(out, gso_ref[...]).astype(BF16)


def _ssm(u, h0r, h0i, consts):
    b, l, _ = u.shape
    br, bi, cc, ar, ai, dsk, wg, bg, gso = consts
    tc = min(l, SSM_CHUNK)
    full = lambda a: pl.BlockSpec(a.shape, lambda i, j: (0,) * a.ndim)
    st = pl.BlockSpec((1, STATE_TILES, LANES), lambda i, j: (i, 0, 0))
    state = jax.ShapeDtypeStruct((b, STATE_TILES, LANES), F32)
    carry = pltpu.VMEM((STATE_TILES, LANES), F32)
    tiles = pltpu.VMEM((tc * STATE_TILES, LANES), F32)
    return pl.pallas_call(
        functools.partial(_ssm_kernel, tc=tc),
        grid=(b, l // tc),
        in_specs=[pl.BlockSpec((1, tc, D_SSM), lambda i, j: (i, j, 0)), st, st,
                  full(br), full(bi), full(cc), full(ar), full(ai), full(dsk), full(wg), full(bg), full(gso)],
        out_specs=[pl.BlockSpec((1, tc, D_SSM), lambda i, j: (i, j, 0)), st, st],
        out_shape=[jax.ShapeDtypeStruct((b, l, D_SSM), BF16), state, state],
        scratch_shapes=[carry, carry, tiles, tiles, tiles, tiles],
        compiler_params=_cparams("arbitrary", "arbitrary"),
        name=f"ssm{l}",
    )(u, h0r, h0i, br, bi, cc, ar, ai, dsk, wg, bg, gso)


def _post_kernel(x_ref, a_ref, s_ref, gt1_ref, sh2_ref, sc2_ref, gt2_ref, cnt0_ref, gao_ref, wo_ref, gffn_ref,
                 wrh_ref, wrl_ref, rb_ref, wsgu_ref, wsd_ref,
                 xm_ref, hp_ref, idx_ref, pos_ref, w_ref, cnt_ref, carry_ref, *, tl, first):
    step = pl.program_id(0) * pl.num_programs(1) + pl.program_id(1)

    @pl.when(step == 0)
    def _():
        carry_ref[...] = cnt0_ref[...] if not first else jnp.zeros_like(carry_ref)

    x = x_ref[0]
    attn = jnp.concatenate([a_ref[0, hd] for hd in range(N_HEADS)], axis=-1).astype(F32)
    ms = jnp.sum(attn * attn, axis=-1, keepdims=True) * (1.0 / D_ATTN)
    attn_n = (attn * lax.rsqrt(ms + EPS) * gao_ref[...]).astype(BF16)
    merged = jnp.concatenate([attn_n, s_ref[0]], axis=-1)
    x1 = x + gt1_ref[0] * _dot(merged, wo_ref[...])
    h2 = _rms(x1, gffn_ref[...]) * (1.0 + sc2_ref[0]) + sh2_ref[0]
    d = h2.shape[-1]
    _store_slabs(hp_ref.at[0], _pack_bf16_pair(h2[:, :d // 2], h2[:, d // 2:]))
    h_hi = h2.astype(BF16)
    h_lo = (h2 - h_hi.astype(F32)).astype(BF16)

    gu = _dot(h_hi, wsgu_ref[...])
    ds = gu.shape[-1] // 2
    g = gu[:, :ds]
    act = (g * _sigmoid(g) * gu[:, ds:]).astype(BF16)
    xm_ref[0] = x1 + gt2_ref[0] * _dot(act, wsd_ref[...])

    wrh = wrh_ref[...]
    logits = _dot_nt(wrh, h_hi) + _dot_nt(wrh, h_lo) + _dot_nt(wrl_ref[...], h_hi)
    s = _sigmoid(logits)
    sc = s + rb_ref[...]
    ge = N_EXPERTS // N_EXP_GROUPS
    neg = -jnp.inf
    gi = lax.broadcasted_iota(I32, (ge, tl), 0)
    gsc = []
    for gidx in range(N_EXP_GROUPS):
        blk = sc[gidx * ge:(gidx + 1) * ge, :]
        m1 = jnp.max(blk, axis=0, keepdims=True)
        f1 = jnp.min(jnp.where(blk == m1, gi, ge), axis=0, keepdims=True)
        m2 = jnp.max(jnp.where(gi == f1, neg, blk), axis=0, keepdims=True)
        gsc.append(m1 + m2)
    gwork = jnp.concatenate(gsc, axis=0)
    ni = lax.broadcasted_iota(I32, (N_EXP_GROUPS, tl), 0)
    gsel = jnp.zeros((N_EXP_GROUPS, tl), F32)
    for _ in range(TOPK_GROUPS):
        mx = jnp.max(gwork, axis=0, keepdims=True)
        fi = jnp.min(jnp.where(gwork == mx, ni, N_EXP_GROUPS), axis=0, keepdims=True)
        hit = ni == fi
        gsel = jnp.where(hit, 1.0, gsel)
        gwork = jnp.where(hit, neg, gwork)
    work = jnp.concatenate(
        [jnp.where(gsel[gidx:gidx + 1, :] > 0.0, sc[gidx * ge:(gidx + 1) * ge, :], neg)
         for gidx in range(N_EXP_GROUPS)], axis=0)
    ei = lax.broadcasted_iota(I32, (N_EXPERTS, tl), 0)
    chosen = jnp.zeros((N_EXPERTS, tl), F32)
    idx_rows, w_rows = [], []
    for _ in range(TOP_K):
        mx = jnp.max(work, axis=0, keepdims=True)
        fi = jnp.min(jnp.where(work == mx, ei, N_EXPERTS), axis=0, keepdims=True)
        hit = ei == fi
        w_rows.append(jnp.sum(jnp.where(hit, s, 0.0), axis=0, keepdims=True))
        idx_rows.append(fi)
        chosen = jnp.where(hit, 1.0, chosen)
        work = jnp.where(hit, neg, work)
    wt = jnp.concatenate(w_rows, axis=0)
    wt = wt / jnp.sum(wt, axis=0, keepdims=True) * ROUTED_SCALE
    idx_ref[0] = jnp.concatenate(idx_rows, axis=0)

    trow = lax.broadcasted_iota(I32, (tl, tl), 0)
    tcol = lax.broadcasted_iota(I32, (tl, tl), 1)
    before = jnp.where(trow < tcol, 1.0, 0.0).astype(BF16)
    rank = _dot(chosen.astype(BF16), before) + carry_ref[...]
    pos_rows = [jnp.sum(jnp.where(ei == idx_rows[k], rank, 0.0), axis=0, keepdims=True) for k in range(TOP_K)]
    pos_ref[0] = jnp.concatenate(pos_rows, axis=0).astype(I32)
    carry_ref[...] = carry_ref[...] + jnp.sum(chosen, axis=1, keepdims=True)
    cnt_ref[...] = carry_ref[...]

    eye = jnp.where(trow == tcol, 1.0, 0.0).astype(BF16)
    t_hi, t_mid, t_lo = _split3(wt)
    w_ref[0] = _dot_nt(eye, t_hi) + _dot_nt(eye, t_mid) + _dot_nt(eye, t_lo)


def _post(x, attn, ssm, mods, cnt0, consts, first):
    b, l, d = x.shape
    tl = min(l, SEQ_BLOCK)
    gt1, sh2, sc2, gt2 = mods
    gao, wo, gffn, wrh, wrl, rb, wsgu, wsd = consts
    slab = d // 2 // LANES
    tok = lambda w: pl.BlockSpec((1, tl, w), lambda i, j: (i, j, 0))
    full = lambda a: pl.BlockSpec(a.shape, lambda i, j: (0,) * a.ndim)
    row = pl.BlockSpec((1, 1, d), lambda i, j: (i, 0, 0))
    tk = pl.BlockSpec((1, TOP_K, tl), lambda i, j: (i, 0, j))
    return pl.pallas_call(
        functools.partial(_post_kernel, tl=tl, first=first),
        grid=(b, l // tl),
        in_specs=[tok(d), pl.BlockSpec((1, N_HEADS, tl, HEAD_PAD), lambda i, j: (i, 0, j, 0)), tok(D_SSM),
                  row, row, row, row, full(cnt0), full(gao), full(wo), full(gffn), full(wrh), full(wrl),
                  full(rb), full(wsgu), full(wsd)],
        out_specs=[tok(d), pl.BlockSpec((1, tl * slab, LANES), lambda i, j: (i, j, 0)), tk, tk, tok(TOP_K),
                   full(cnt0)],
        out_shape=[jax.ShapeDtypeStruct((b, l, d), F32), jax.ShapeDtypeStruct((b, l * slab, LANES), U32),
                   jax.ShapeDtypeStruct((b, TOP_K, l), I32), jax.ShapeDtypeStruct((b, TOP_K, l), I32),
                   jax.ShapeDtypeStruct((b, l, TOP_K), F32), jax.ShapeDtypeStruct(cnt0.shape, F32)],
        scratch_shapes=[pltpu.VMEM(cnt0.shape, F32)],
        compiler_params=_cparams("arbitrary", "arbitrary"),
        name=f"post{l}",
    )(x, attn, ssm, gt1, sh2, sc2, gt2, cnt0, gao, wo, gffn, wrh, wrl, rb, wsgu, wsd)


def _slots_kernel(pstart_ref, idx_ref, pos_ref, dest_ref):
    idx = idx_ref[0]

    def body(e, base):
        return jnp.where(idx == e, pstart_ref[e], base)

    dest_ref[0] = lax.fori_loop(0, N_EXPERTS, body, jnp.zeros_like(idx)) + pos_ref[0]


def _slots(pstart, idx, pos):
    b, k, l = idx.shape
    tl = min(l, 2048)
    spec = pl.BlockSpec((1, k, tl), lambda i, j: (i, 0, j))
    return pl.pallas_call(
        _slots_kernel,
        grid=(b, l // tl),
        in_specs=[pl.BlockSpec(memory_space=pltpu.SMEM), spec, spec],
        out_specs=spec,
        out_shape=jax.ShapeDtypeStruct(idx.shape, I32),
        compiler_params=_cparams("arbitrary", "arbitrary"),
        name=f"slots{l}",
    )(pstart, idx, pos)


INDEX_WINDOW = 2048
INDEX_ALIGN = 1024


def _experts_kernel(be_ref, nb_ref, jb_ref, ng_ref, tok_ref, h_ref, wg_ref, wu_ref, wd_ref, y_ref,
                    wgu_s, wd_s, x0, x1, idx0, idx1, xsem, isem):
    i = pl.program_id(0)
    nbu = nb_ref[0]
    e = be_ref[i]
    prev = be_ref[jnp.maximum(i - 1, 0)]
    dm = wg_ref.shape[1]
    de = wg_ref.shape[2]
    xbufs = (x0, x1)
    ibufs = (idx0, idx1)

    def window(blk, slot):
        start = pl.multiple_of((jb_ref[blk] >> 10) << 10, INDEX_ALIGN)
        return pltpu.make_async_copy(tok_ref.at[pl.ds(start, INDEX_WINDOW)], ibufs[slot], isem.at[slot])

    slab = x0.shape[0] // EXPERT_ROWS

    def gather_rows(blk, slot):
        base = jb_ref[blk] & (INDEX_ALIGN - 1)

        def body(g, c):
            for tt in range(SCAN_ROWS):
                r = g * SCAN_ROWS + tt
                src = pl.multiple_of(ibufs[slot][base + r], slab)
                dst = pl.multiple_of(r * slab, slab)
                pltpu.make_async_copy(h_ref.at[pl.ds(src, slab)], xbufs[slot].at[pl.ds(dst, slab)],
                                      xsem.at[slot]).start()
            return c
        lax.fori_loop(0, ng_ref[blk], body, 0)

    def drain_rows(blk, slot):
        def body(g, c):
            for _ in range(SCAN_ROWS):
                pltpu.make_async_copy(h_ref.at[pl.ds(0, slab)], xbufs[slot].at[pl.ds(0, slab)],
                                      xsem.at[slot]).wait()
            return c
        lax.fori_loop(0, ng_ref[blk], body, 0)

    @pl.when(i >= nbu)
    def _():
        y_ref[...] = jnp.zeros_like(y_ref)

    @pl.when(i < nbu)
    def _():
        last = nbu - 1

        @pl.when(i == 0)
        def _():
            x0[...] = jnp.zeros_like(x0)
            x1[...] = jnp.zeros_like(x1)
            first = window(0, 0)
            first.start()
            first.wait()
            gather_rows(0, 0)
            window(jnp.minimum(1, last), 1).start()

        @pl.when((i == 0) | (e != prev))
        def _():
            wgu_s[:, 0:de] = wg_ref[0].astype(BF16)
            wgu_s[:, de:2 * de] = wu_ref[0].astype(BF16)
            wd_s[...] = wd_ref[0].astype(BF16)

        def run(slot):
            nxt = jnp.minimum(i + 1, last)
            window(nxt, 1 - slot).wait()
            drain_rows(i, slot)
            gather_rows(nxt, 1 - slot)
            xa, xb = _unpack_bf16_pair(_load_slabs(xbufs[slot], EXPERT_ROWS))
            gu = _dot(xa.astype(BF16), wgu_s[0:dm // 2, :]) + _dot(xb.astype(BF16), wgu_s[dm // 2:dm, :])
            g = gu[:, :de]
            act = (g * _sigmoid(g) * gu[:, de:]).astype(BF16)
            y = _dot(act, wd_s[...])
            _store_slabs(y_ref, _pack_bf16_pair(y[:, :dm // 2], y[:, dm // 2:]))

            @pl.when(i < last)
            def _():
                window(jnp.minimum(i + 2, last), slot).start()

            @pl.when(i == last)
            def _():
                drain_rows(last, 1 - slot)

        for slot in range(2):
            pl.when(i % 2 == slot)(functools.partial(run, slot))


def _experts(blk_e, nb_used, jbase, ngroups, tok, hp, w_gate, w_up, w_down, rows):
    ne, dm, de = w_gate.shape
    slab = dm // 2 // LANES
    xrows = EXPERT_ROWS * slab
    nb = rows // EXPERT_ROWS
    grid_spec = pltpu.PrefetchScalarGridSpec(
        num_scalar_prefetch=4,
        grid=(nb,),
        in_specs=[pl.BlockSpec(memory_space=pl.ANY), pl.BlockSpec(memory_space=pl.ANY),
                  pl.BlockSpec((1, dm, de), lambda i, be, n, jb, ng: (be[i], 0, 0)),
                  pl.BlockSpec((1, dm, de), lambda i, be, n, jb, ng: (be[i], 0, 0)),
                  pl.BlockSpec((1, de, dm), lambda i, be, n, jb, ng: (be[i], 0, 0))],
        out_specs=pl.BlockSpec((xrows, LANES), lambda i, be, n, jb, ng: (i, 0)),
        scratch_shapes=[pltpu.VMEM((dm, 2 * de), BF16), pltpu.VMEM((de, dm), BF16),
                        pltpu.VMEM((xrows, LANES), U32), pltpu.VMEM((xrows, LANES), U32),
                        pltpu.SMEM((INDEX_WINDOW,), I32), pltpu.SMEM((INDEX_WINDOW,), I32),
                        pltpu.SemaphoreType.DMA((2,)), pltpu.SemaphoreType.DMA((2,))],
    )
    return pl.pallas_call(
        _experts_kernel,
        grid_spec=grid_spec,
        out_shape=jax.ShapeDtypeStruct((rows * slab, LANES), U32),
        compiler_params=_cparams("arbitrary"),
        name="experts",
    )(blk_e, nb_used, jbase, ngroups, tok, hp, w_gate, w_up, w_down)


def _combine_kernel(dcur_ref, dnext_ref, ys_ref, w_ref, xm_ref, gt2_ref, gfin_ref, o_ref, buf0, buf1, sem, *, tl):
    bufs = (buf0, buf1)
    s = pl.program_id(0)
    last = pl.num_programs(0) - 1
    group = SCAN_ROWS
    ngroups = tl // group
    slab = buf0.shape[2] // group
    half = slab * LANES

    def gather(dref, g, to):
        for tt in range(group):
            for k in range(TOP_K):
                src = pl.multiple_of(dref[(g * group + tt) * TOP_K + k], slab)
                pltpu.make_async_copy(ys_ref.at[pl.ds(src, slab)], bufs[to].at[k, g, pl.ds(tt * slab, slab)],
                                      sem.at[to]).start()

    def drain(which):
        def body(t, c):
            for k in range(TOP_K):
                pltpu.make_async_copy(ys_ref.at[pl.ds(0, slab)], bufs[which].at[0, 0, pl.ds(0, slab)],
                                      sem.at[which]).wait()
            return c
        lax.fori_loop(0, tl, body, 0)

    @pl.when(s == 0)
    def _():
        def body(g, c):
            gather(dcur_ref, g, 0)
            return c
        lax.fori_loop(0, ngroups, body, 0)

    gt2 = gt2_ref[0]
    gfin = gfin_ref[...]

    def run(slot):
        drain(slot)

        def body(g, c):
            gather(dnext_ref, g, 1 - slot)
            t0 = pl.multiple_of(g * group, group)
            w = w_ref[pl.ds(t0, group), :]
            acc_a = jnp.zeros((group, half), F32)
            acc_b = jnp.zeros((group, half), F32)
            for k in range(TOP_K):
                ya, yb = _unpack_bf16_pair(_load_slabs(bufs[slot].at[k, g], group))
                wk = w[:, k:k + 1]
                acc_a = acc_a + wk * ya
                acc_b = acc_b + wk * yb
            routed = jnp.concatenate([acc_a, acc_b], axis=-1)
            x2 = xm_ref[pl.ds(t0, group), :] + gt2 * routed
            o_ref[pl.ds(t0, group), :] = _rms(x2, gfin)
            return c

        lax.fori_loop(0, ngroups, body, 0)

        @pl.when(s == last)
        def _():
            drain(1 - slot)

    for slot in range(2):
        pl.when(s % 2 == slot)(functools.partial(run, slot))


def _combine(dest, ys, w, xm, gt2, gfin):
    b, l, d = xm.shape
    tl = min(l, COMBINE_ROWS)
    nl = l // tl
    n = b * nl
    slab = d // 2 // LANES
    tile = pltpu.VMEM((TOP_K, tl // SCAN_ROWS, SCAN_ROWS * slab, LANES), U32)
    out = pl.pallas_call(
        functools.partial(_combine_kernel, tl=tl),
        grid=(n,),
        in_specs=[pl.BlockSpec((tl * TOP_K,), lambda s: (s,), memory_space=pltpu.SMEM),
                  pl.BlockSpec((tl * TOP_K,), lambda s: (jnp.minimum(s + 1, n - 1),), memory_space=pltpu.SMEM),
                  pl.BlockSpec(memory_space=pl.ANY),
                  pl.BlockSpec((tl, TOP_K), lambda s: (s, 0)),
                  pl.BlockSpec((tl, d), lambda s: (s, 0)),
                  pl.BlockSpec((1, 1, d), lambda s: (s // nl, 0, 0)),
                  pl.BlockSpec(gfin.shape, lambda s: (0, 0))],
        out_specs=pl.BlockSpec((tl, d), lambda s: (s, 0)),
        out_shape=jax.ShapeDtypeStruct((b * l, d), F32),
        scratch_shapes=[tile, tile, pltpu.SemaphoreType.DMA((2,))],
        compiler_params=_cparams("arbitrary"),
        name=f"combine{l}",
    )(dest, dest, ys, w.reshape(b * l, TOP_K), xm.reshape(b * l, d), gt2, gfin)
    return out.reshape(b, l, d)


def _block_diag(m):
    g, a, b = m.shape
    eye = jnp.eye(g, dtype=m.dtype)
    return jnp.einsum("gab,gh->gahb", m, eye).reshape(g * a, g * b)


def kernel(x_prompt, x_sample, c_prompt, c_sample, cache_k, cache_v, cache_logf, state_ssm_re, state_ssm_im,
           w_ada, b_ada, g_mix, w_in, b_f, lam_re, lam_im, log_dt, ssm_b_re, ssm_b_im, ssm_c_re, ssm_c_im,
           ssm_d, w_glu, b_glu, g_attn_out, g_ssm_out, w_out, g_ffn, w_router, router_bias,
           w_gate, w_up, w_down, ws_gate, ws_up, ws_down, g_final):
    assert w_ada.shape[0] == 1, "single layer"
    bp, lp, d = x_prompt.shape
    bs, ls, _ = x_sample.shape
    past = cache_k.shape[2]

    nb = bp + bs
    nbp = -(-nb // 8) * 8
    c_all = jnp.concatenate([c_prompt, c_sample, jnp.zeros((nbp - nb, d), F32)], axis=0)
    mod = _prep(c_all, w_ada[0], b_ada[0]).reshape(nbp, 6, 1, d)
    mod_p = [mod[:bp, i] for i in range(6)]
    mod_s = [mod[bp:nb, i] for i in range(6)]

    a_re, a_im, bb_re, bb_im = _disc(lam_re[0], lam_im[0], log_dt[0],
                                     jnp.swapaxes(ssm_b_re[0], 1, 2), jnp.swapaxes(ssm_b_im[0], 1, 2))
    a_re = a_re.reshape(STATE_TILES, LANES)
    a_im = a_im.reshape(STATE_TILES, LANES)
    bd_re = _block_diag(bb_re).astype(BF16)
    bd_im = _block_diag(bb_im).astype(BF16)
    cc = jnp.concatenate([_block_diag(jnp.swapaxes(ssm_c_re[0], 1, 2)),
                          -_block_diag(jnp.swapaxes(ssm_c_im[0], 1, 2))], axis=0).astype(BF16)
    ssm_tail = (ssm_d[0].reshape(1, D_SSM), w_glu[0].astype(BF16), b_glu[0].reshape(1, D_SSM),
                g_ssm_out[0].reshape(1, D_SSM))

    ssm_consts = (bd_re, bd_im, cc, a_re, a_im) + ssm_tail

    inw = _inproj_weights(w_in[0], b_f[0])
    g_mix2 = g_mix[0].reshape(1, d)

    wo = w_out[0]
    wo_attn = jnp.pad(wo[:D_ATTN].reshape(N_HEADS, HEAD_DIM, d), ((0, 0), (0, HEAD_PAD - HEAD_DIM), (0, 0)))
    wo_pad = jnp.concatenate([wo_attn.reshape(N_HEADS * HEAD_PAD, d), wo[D_ATTN:]], axis=0).astype(BF16)
    gao_pad = jnp.pad(g_attn_out[0].reshape(N_HEADS, HEAD_DIM),
                      ((0, 0), (0, HEAD_PAD - HEAD_DIM))).reshape(1, N_HEADS * HEAD_PAD)
    wr_t = w_router[0].T
    wr_hi = wr_t.astype(BF16)
    wr_lo = (wr_t - wr_hi.astype(F32)).astype(BF16)
    post_consts = (gao_pad, wo_pad, g_ffn[0].reshape(1, d), wr_hi, wr_lo, router_bias[0].reshape(N_EXPERTS, 1),
                   jnp.concatenate([ws_gate[0], ws_up[0]], axis=1).astype(BF16), ws_down[0].astype(BF16))

    def mixer(x, modv, attn_fn, h0r, h0i):
        sh1, sc1, gt1, sh2, sc2, gt2 = modv
        q, k, v, kf, vf, lf, fb, u = _inproj(x, sh1, sc1, g_mix2, inw)
        attn = attn_fn(q, k, v, kf, vf, lf, fb)
        ssm, hr, hi = _ssm(u, h0r, h0i, ssm_consts)
        return attn, ssm, (kf, vf, lf, hr, hi), (gt1, sh2, sc2, gt2)

    def attn_prompt(q, k, v, kf, vf, lf, fb):
        return _attn(q, k, v, fb[:, :, 0, :N_HEADS].reshape(-1))

    zeros_p = jnp.zeros((bp, STATE_TILES, LANES), F32)
    attn_p, ssm_p, new_p, m_p = mixer(x_prompt, mod_p, attn_prompt, zeros_p, zeros_p)

    lk = -(-(past + ls) // LANES) * LANES
    padk = lk - past - ls

    def attn_sample(q, k, v, kf, vf, lf, fb):
        k_all = jnp.concatenate([cache_k[0].reshape(bs, past, D_ATTN), kf,
                                 jnp.zeros((bs, padk, D_ATTN), F32)], axis=1)
        v_all = jnp.concatenate([cache_v[0].reshape(bs, past, D_ATTN), vf,
                                 jnp.zeros((bs, padk, D_ATTN), F32)], axis=1)
        lf_all = jnp.concatenate([cache_logf[0], lf, jnp.zeros((bs, padk, N_HEADS), F32)], axis=1)
        lf_cols = jnp.pad(lf_all, ((0, 0), (0, 0), (0, LANES - N_HEADS)))
        lf_rows = jnp.swapaxes(lf_all, 1, 2)
        return _attn_dec(q, k_all, v_all, lf_cols, lf_rows, past, ls)

    attn_s, ssm_s, new_s, m_s = mixer(x_sample, mod_s, attn_sample,
                                      state_ssm_re[0].reshape(bs, STATE_TILES, LANES),
                                      state_ssm_im[0].reshape(bs, STATE_TILES, LANES))

    cnt0 = jnp.zeros((N_EXPERTS, 1), F32)
    xm_p, hp_p, idx_p, pos_p, w_p, cnt1 = _post(x_prompt, attn_p, ssm_p, m_p, cnt0, post_consts, True)
    xm_s, hp_s, idx_s, pos_s, w_s, cnt2 = _post(x_sample, attn_s, ssm_s, m_s, cnt1, post_consts, False)

    counts = cnt2[:, 0].astype(I32)
    padded = (counts + EXPERT_ROWS - 1) // EXPERT_ROWS * EXPERT_ROWS
    pend = jnp.cumsum(padded)
    pstart = pend - padded
    tp, ts = bp * lp, bs * ls
    n_blocks = -(-((tp + ts) * TOP_K) // EXPERT_ROWS) + N_EXPERTS
    rows = n_blocks * EXPERT_ROWS
    dest_p = _slots(pstart, idx_p, pos_p)
    dest_s = _slots(pstart, idx_s, pos_s)
    nb_used = (pend[-1] // EXPERT_ROWS).astype(I32).reshape(1)
    blk_row = jnp.minimum(jnp.arange(n_blocks, dtype=I32), nb_used[0] - 1) * EXPERT_ROWS
    blk_e = jnp.minimum(jnp.sum((pend[None, :] <= blk_row[:, None]).astype(I32), axis=1), N_EXPERTS - 1)

    slab = d // 2 // LANES
    dflat_p = jnp.swapaxes(dest_p, 1, 2).reshape(-1)
    dflat_s = jnp.swapaxes(dest_s, 1, 2).reshape(-1)
    n_pairs = (tp + ts) * TOP_K
    order = jnp.argsort(jnp.concatenate([dflat_p, dflat_s])).astype(I32)
    n_tok = ((n_pairs >> 10) + 3) << 10
    tok = jnp.concatenate([order // TOP_K * slab, jnp.zeros((n_tok - n_pairs,), I32)])
    cstart = jnp.cumsum(counts) - counts
    blk_ids = jnp.arange(n_blocks, dtype=I32)
    blk_off = blk_ids * EXPERT_ROWS - pstart[blk_e]
    used = blk_ids < nb_used[0]
    jbase = jnp.where(used, cstart[blk_e] + blk_off, 0).astype(I32)
    real = jnp.clip(counts[blk_e] - blk_off, 0, EXPERT_ROWS)
    ngroups = jnp.where(used, (real + SCAN_ROWS - 1) // SCAN_ROWS, 0).astype(I32)
    hp_all = jnp.concatenate([hp_p.reshape(tp * slab, LANES), hp_s.reshape(ts * slab, LANES)], axis=0)
    ys = _experts(blk_e, nb_used, jbase, ngroups, tok, hp_all, w_gate[0], w_up[0], w_down[0], rows)

    gfin = g_final.reshape(1, d)
    y_p = _combine(dflat_p * slab, ys, w_p, xm_p, m_p[3], gfin)
    y_s = _combine(dflat_s * slab, ys, w_s, xm_s, m_s[3], gfin)

    def pack(new, b, l):
        kf, vf, lf, hr, hi = new
        return (kf.reshape(1, b, l, N_HEADS, HEAD_DIM), vf.reshape(1, b, l, N_HEADS, HEAD_DIM),
                lf.reshape(1, b, l, N_HEADS),
                hr.reshape(1, b, N_SSM_GROUPS, SSM_STATE), hi.reshape(1, b, N_SSM_GROUPS, SSM_STATE))

    return (y_p, y_s) + pack(new_p, bp, lp) + pack(new_s, bs, ls)
```

```python
import functools
import math

import jax
import jax.numpy as jnp
from jax import lax
from jax.experimental import pallas as pl
from jax.experimental.pallas import tpu as pltpu

F32 = jnp.float32
BF16 = jnp.bfloat16
U32 = jnp.uint32
I32 = jnp.int32

N_HEADS = 8
HEAD_DIM = 64
HEAD_PAD = 128
D_ATTN = N_HEADS * HEAD_DIM
SSM_GROUP = 16
N_SSM_GROUPS = 32
SSM_STATE = 64
D_SSM = SSM_GROUP * N_SSM_GROUPS
D_STATE = N_SSM_GROUPS * SSM_STATE
N_EXPERTS = 256
TOP_K = 8
N_EXP_GROUPS = 8
TOPK_GROUPS = 4
ROUTED_SCALE = 2.5
EPS = 1e-6
LANES = 128

SEQ_BLOCK = 512
SSM_CHUNK = 512
EXPERT_ROWS = 512
ATTN_HEADS_PER_STEP = 2
ATTN_BLOCKS_PER_TRIP = 4
LOG2E = math.log2(math.e)
COMBINE_ROWS = 256
DISPATCH_ROWS = 512
VMEM_LIMIT = 56 * 1024 * 1024

_NT = (((1,), (1,)), ((), ()))


def _cparams(*sem):
    return pltpu.CompilerParams(dimension_semantics=sem, vmem_limit_bytes=VMEM_LIMIT)


def _split3(x):
    hi = x.astype(BF16)
    r = x - hi.astype(F32)
    mid = r.astype(BF16)
    lo = (r - mid.astype(F32)).astype(BF16)
    return hi, mid, lo


def _dot(a, b):
    return jnp.dot(a, b, preferred_element_type=F32)


def _dot_nt(a, b):
    return lax.dot_general(a, b, _NT, preferred_element_type=F32)


def _rms(x, g):
    return x * lax.rsqrt(jnp.mean(x * x, axis=-1, keepdims=True) + EPS) * g


def _sigmoid(x):
    return 1.0 / (1.0 + jnp.exp(-x))


def _pack_bf16_pair(a, b):
    ab = lax.bitcast_convert_type(a.astype(BF16).astype(F32), U32)
    bb = lax.bitcast_convert_type(b.astype(BF16).astype(F32), U32)
    return (ab >> 16) | (bb & jnp.uint32(0xFFFF0000))


def _unpack_bf16_pair(w):
    a = lax.bitcast_convert_type(w << 16, F32)
    b = lax.bitcast_convert_type(w & jnp.uint32(0xFFFF0000), F32)
    return a, b


def _store_slabs(ref, val):
    n, width = val.shape
    tiles = width // LANES
    for c in range(tiles):
        ref[pl.ds(c, n, stride=tiles), :] = val[:, c * LANES:(c + 1) * LANES]


def _load_slabs(ref, n):
    tiles = ref.shape[0] // n
    return jnp.concatenate([ref[pl.ds(c, n, stride=tiles), :] for c in range(tiles)], axis=-1)


def _prep_kernel(c_ref, w_ref, b_ref, o_ref):
    c = c_ref[...]
    a = c * _sigmoid(c)
    a_hi = a.astype(BF16)
    a_lo = (a - a_hi.astype(F32)).astype(BF16)
    w = w_ref[...]
    w_hi = w.astype(BF16)
    w_lo = (w - w_hi.astype(F32)).astype(BF16)
    o_ref[...] = _dot(a_hi, w_hi) + _dot(a_lo, w_hi) + _dot(a_hi, w_lo) + b_ref[...]


def _prep(c_all, w_ada, b_ada):
    n, d = c_all.shape
    nout = w_ada.shape[1]
    tn = 1024
    return pl.pallas_call(
        _prep_kernel,
        grid=(nout // tn,),
        in_specs=[pl.BlockSpec((n, d), lambda j: (0, 0)),
                  pl.BlockSpec((d, tn), lambda j: (0, j)),
                  pl.BlockSpec((1, tn), lambda j: (0, j))],
        out_specs=pl.BlockSpec((n, tn), lambda j: (0, j)),
        out_shape=jax.ShapeDtypeStruct((n, nout), F32),
        compiler_params=_cparams("arbitrary"),
        name="prep",
    )(c_all, w_ada, b_ada.reshape(1, nout))


def _disc_kernel(lr_ref, li_ref, ldt_ref, br_ref, bi_ref, ar_ref, ai_ref, bbr_ref, bbi_ref):
    lr = lr_ref[...]
    li = li_ref[...]
    dt = jnp.exp(ldt_ref[...])
    er = jnp.exp(lr * dt)
    ang = li * dt
    ar = er * jnp.cos(ang)
    ai = er * jnp.sin(ang)
    ar_ref[...] = ar
    ai_ref[...] = ai
    den = lr * lr + li * li
    nr = ((ar - 1.0) * lr + ai * li) / den
    ni = (ai * lr - (ar - 1.0) * li) / den
    nr3 = nr[:, None, :]
    ni3 = ni[:, None, :]
    br = br_ref[...]
    bi = bi_ref[...]
    bbr_ref[...] = nr3 * br - ni3 * bi
    bbi_ref[...] = nr3 * bi + ni3 * br


def _disc(lam_re, lam_im, log_dt, bt_re, bt_im):
    g, p = lam_re.shape
    c = bt_re.shape[1]
    return pl.pallas_call(
        _disc_kernel,
        out_shape=(jax.ShapeDtypeStruct((g, p), F32), jax.ShapeDtypeStruct((g, p), F32),
                   jax.ShapeDtypeStruct((g, c, p), F32), jax.ShapeDtypeStruct((g, c, p), F32)),
        name="disc",
    )(lam_re, lam_im, log_dt.reshape(g, 1), bt_re, bt_im)


SCAN_ROWS = 8
STATE_TILES = D_STATE // LANES


def _inproj_kernel(x_ref, sh_ref, sc_ref, g_ref, wm_ref, wf_ref, bf_ref, selq_ref, selk_ref,
                   cq_ref, ck_ref, cv_ref,
                   q_ref, k_ref, v_ref, kf_ref, vf_ref, lf_ref, fb_ref, u_ref, carry_ref, *, tl):
    @pl.when(pl.program_id(1) == 0)
    def _():
        carry_ref[...] = jnp.zeros_like(carry_ref)

    x = x_ref[0]
    h = _rms(x, g_ref[...]) * (1.0 + sc_ref[0]) + sh_ref[0]
    hb = h.astype(BF16)
    main = _dot(hb, wm_ref[...])
    hw = N_HEADS * HEAD_PAD
    qp = main[:, 0:hw] * (HEAD_DIM ** -0.5 * LOG2E)
    kp = main[:, hw:2 * hw]
    vp = main[:, 2 * hw:3 * hw]
    u_ref[0] = main[:, 3 * hw:3 * hw + D_SSM]
    for hd in range(N_HEADS):
        src = slice(hd * HEAD_PAD, hd * HEAD_PAD + HEAD_DIM)
        dst = slice(hd * HEAD_DIM, (hd + 1) * HEAD_DIM)
        kf_ref[0, :, dst] = kp[:, src]
        vf_ref[0, :, dst] = vp[:, src]

    fl = _dot(hb, wf_ref[...]) + bf_ref[...]
    lf = jnp.minimum(fl, 0.0) - jnp.log1p(jnp.exp(-jnp.abs(fl)))
    lf_ref[0] = lf[:, 0:N_HEADS]

    row = lax.broadcasted_iota(I32, (tl, tl), 0)
    col = lax.broadcasted_iota(I32, (tl, tl), 1)
    tri = jnp.where(row >= col, 1.0, 0.0).astype(BF16)
    hi, mid, lo = _split3(lf)
    frel = _dot(tri, hi) + _dot(tri, mid) + _dot(tri, lo)
    fb_ref[0, 0] = carry_ref[...] * LOG2E
    carry_ref[...] = carry_ref[...] + frel[tl - 1:tl, :]

    fcat = jnp.concatenate(_split3(frel * LOG2E), axis=-1)
    q_aug = (qp + _dot(fcat, selq_ref[...]) + cq_ref[...]).astype(BF16)
    k_aug = (kp + _dot(fcat, selk_ref[...]) + ck_ref[...]).astype(BF16)
    v_aug = (vp + cv_ref[...]).astype(BF16)
    for hd in range(N_HEADS):
        sl = slice(hd * HEAD_PAD, (hd + 1) * HEAD_PAD)
        q_ref[0, hd] = q_aug[:, sl]
        k_ref[0, hd] = k_aug[:, sl]
        v_ref[0, hd] = v_aug[:, sl]


def _inproj(x, sh, sc, g, wts):
    b, l, d = x.shape
    tl = min(l, SEQ_BLOCK)
    nl = l // tl
    wm, wf, bf, selq, selk, cq, ck, cv = wts
    hm = jax.ShapeDtypeStruct((b, N_HEADS, l, HEAD_PAD), BF16)
    hm_spec = pl.BlockSpec((1, N_HEADS, tl, HEAD_PAD), lambda i, j: (i, 0, j, 0))
    tok = lambda w: pl.BlockSpec((1, tl, w), lambda i, j: (i, j, 0))
    full = lambda a: pl.BlockSpec(a.shape, lambda i, j: (0,) * a.ndim)
    row = pl.BlockSpec((1, 1, d), lambda i, j: (i, 0, 0))
    return pl.pallas_call(
        functools.partial(_inproj_kernel, tl=tl),
        grid=(b, nl),
        in_specs=[tok(d), row, row, full(g), full(wm), full(wf), full(bf), full(selq), full(selk),
                  full(cq), full(ck), full(cv)],
        out_specs=[hm_spec, hm_spec, hm_spec, tok(D_ATTN), tok(D_ATTN), tok(N_HEADS),
                   pl.BlockSpec((1, 1, 1, LANES), lambda i, j: (i, j, 0, 0)), tok(D_SSM)],
        out_shape=[hm, hm, hm,
                   jax.ShapeDtypeStruct((b, l, D_ATTN), F32), jax.ShapeDtypeStruct((b, l, D_ATTN), F32),
                   jax.ShapeDtypeStruct((b, l, N_HEADS), F32),
                   jax.ShapeDtypeStruct((b, nl, 1, LANES), F32),
                   jax.ShapeDtypeStruct((b, l, D_SSM), F32)],
        scratch_shapes=[pltpu.VMEM((1, LANES), F32)],
        compiler_params=_cparams("arbitrary", "arbitrary"),
        name=f"inproj{l}",
    )(x, sh, sc, g, wm, wf, bf, selq, selk, cq, ck, cv)


def _inproj_weights(w_in, b_f):
    d = w_in.shape[0]
    wq, wk, wv = (w_in[:, i * D_ATTN:(i + 1) * D_ATTN] for i in range(3))
    wfl = w_in[:, 3 * D_ATTN:3 * D_ATTN + N_HEADS]
    wu = w_in[:, 3 * D_ATTN + N_HEADS:]

    def pad_heads(w):
        w = w.reshape(d, N_HEADS, HEAD_DIM)
        w = jnp.pad(w, ((0, 0), (0, 0), (0, HEAD_PAD - HEAD_DIM)))
        return w.reshape(d, N_HEADS * HEAD_PAD)

    wm = jnp.concatenate([pad_heads(wq), pad_heads(wk), pad_heads(wv), wu], axis=1).astype(BF16)
    wf = jnp.pad(wfl, ((0, 0), (0, LANES - N_HEADS))).astype(BF16)
    bf = jnp.pad(b_f, (0, LANES - N_HEADS)).reshape(1, LANES).astype(F32)

    hw = N_HEADS * HEAD_PAD
    hd = jnp.arange(N_HEADS)
    selq = jnp.zeros((3 * LANES, hw), F32)
    selk = jnp.zeros((3 * LANES, hw), F32)
    cq = jnp.zeros((1, hw), F32)
    ck = jnp.zeros((1, hw), F32)
    cv = jnp.zeros((1, hw), F32)
    for part in range(3):
        selq = selq.at[part * LANES + hd, hd * HEAD_PAD + HEAD_DIM + part].set(1.0)
        selk = selk.at[part * LANES + hd, hd * HEAD_PAD + HEAD_DIM + 3 + part].set(-1.0)
        cq = cq.at[0, hd * HEAD_PAD + HEAD_DIM + 3 + part].set(1.0)
        ck = ck.at[0, hd * HEAD_PAD + HEAD_DIM + part].set(1.0)
    cv = cv.at[0, hd * HEAD_PAD + HEAD_DIM].set(1.0)
    return wm, wf, bf, selq.astype(BF16), selk.astype(BF16), cq, ck, cv


def _attn_kernel(fb_ref, q_ref, k_ref, v_ref, o_ref, *, r, nblk, hp):
    b = pl.program_id(0)
    g = pl.program_id(1)
    i = pl.program_id(2)
    fbase = (b * nblk) * N_HEADS + g * hp

    def scores(j):
        start = pl.multiple_of(j * r, r)
        return tuple(_dot_nt(q_ref[0, hh], k_ref[0, hh, pl.ds(start, r), :]) for hh in range(hp))

    def absorb(j, s_all, state, masked):
        start = pl.multiple_of(j * r, r)
        out = []
        for hh in range(hp):
            m, acc = state[hh]
            s = s_all[hh]
            if masked:
                row = lax.broadcasted_iota(I32, (r, r), 0)
                col = lax.broadcasted_iota(I32, (r, r), 1)
                s = jnp.where(row >= col, s, -jnp.inf)
            dlt = fb_ref[fbase + hh + i * N_HEADS] - fb_ref[fbase + hh + j * N_HEADS]
            mnew = jnp.maximum(m, jnp.max(s, axis=-1, keepdims=True) + dlt)
            p = jnp.exp2(s - (mnew - dlt))
            acc = jnp.exp2(m - mnew) * acc + _dot(p.astype(BF16), v_ref[0, hh, pl.ds(start, r), :])
            out.append((mnew, acc))
        return tuple(out)

    nu = ATTN_BLOCKS_PER_TRIP

    def several(jj, state):
        js = [nu * jj + u for u in range(nu)]
        ss = [scores(j) for j in js]
        for j, s_all in zip(js, ss):
            state = absorb(j, s_all, state, False)
        return state

    def single(j, state):
        return absorb(j, scores(j), state, False)

    state = tuple((jnp.full((r, 1), -jnp.inf, F32), jnp.zeros((r, HEAD_PAD), F32)) for _ in range(hp))
    state = lax.fori_loop(0, i // nu, several, state)
    state = lax.fori_loop(nu * (i // nu), i, single, state)
    state = absorb(i, scores(i), state, True)
    lane = lax.broadcasted_iota(I32, (r, HEAD_PAD), 1)
    for hh in range(hp):
        acc = state[hh][1]
        out = acc / acc[:, HEAD_DIM:HEAD_DIM + 1]
        o_ref[0, hh] = jnp.where(lane < HEAD_DIM, out, 0.0).astype(BF16)


def _attn(q, k, v, fb):
    b, h, l, _ = q.shape
    r = min(l, SEQ_BLOCK)
    nblk = l // r
    hp = ATTN_HEADS_PER_STEP
    return pl.pallas_call(
        functools.partial(_attn_kernel, r=r, nblk=nblk, hp=hp),
        grid=(b, h // hp, nblk),
        in_specs=[pl.BlockSpec(memory_space=pltpu.SMEM),
                  pl.BlockSpec((1, hp, r, HEAD_PAD), lambda bi, gi, i: (bi, gi, i, 0)),
                  pl.BlockSpec((1, hp, l, HEAD_PAD), lambda bi, gi, i: (bi, gi, 0, 0)),
                  pl.BlockSpec((1, hp, l, HEAD_PAD), lambda bi, gi, i: (bi, gi, 0, 0))],
        out_specs=pl.BlockSpec((1, hp, r, HEAD_PAD), lambda bi, gi, i: (bi, gi, i, 0)),
        out_shape=jax.ShapeDtypeStruct((b, h, l, HEAD_PAD), BF16),
        compiler_params=_cparams("arbitrary", "arbitrary", "arbitrary"),
        name="attn",
    )(fb, q, k, v)


def _attn_dec_kernel(q_ref, k_ref, v_ref, lfc_ref, lfr_ref, o_ref, *, lq, past, lk):
    row = lax.broadcasted_iota(I32, (lk, lk), 0)
    col = lax.broadcasted_iota(I32, (lk, lk), 1)
    tri = jnp.where(row >= col, 1.0, 0.0).astype(BF16)
    upper = jnp.where(row <= col, 1.0, 0.0).astype(BF16)
    c_hi, c_mid, c_lo = _split3(lfc_ref[0])
    fcol = _dot(tri, c_hi) + _dot(tri, c_mid) + _dot(tri, c_lo)
    r_hi, r_mid, r_lo = _split3(lfr_ref[0])
    frow = _dot(r_hi, upper) + _dot(r_mid, upper) + _dot(r_lo, upper)
    fq = fcol[past:past + lq, :]
    qpos = past + lax.broadcasted_iota(I32, (lq, lk), 0)
    kpos = lax.broadcasted_iota(I32, (lq, lk), 1)
    ok = kpos <= qpos
    qlane = lax.broadcasted_iota(I32, (lq, HEAD_PAD), 1)
    klane = lax.broadcasted_iota(I32, (lk, HEAD_PAD), 1)
    for hd in range(N_HEADS):
        odd = hd % 2 == 1
        pair = slice((hd // 2) * HEAD_PAD, (hd // 2 + 1) * HEAD_PAD)
        mine = (klane >= HEAD_DIM) if odd else (klane < HEAD_DIM)
        kh = jnp.where(mine, k_ref[0, :, pair], 0.0).astype(BF16)
        vh = jnp.where(mine, v_ref[0, :, pair], 0.0).astype(BF16)
        qh = jnp.where(qlane < HEAD_DIM, q_ref[0, hd].astype(F32), 0.0)
        if odd:
            qh = pltpu.roll(qh, HEAD_DIM, axis=1)
        s = _dot_nt(qh.astype(BF16), kh) + LOG2E * (fq[:, hd:hd + 1] - frow[hd:hd + 1, :])
        s = jnp.where(ok, s, -jnp.inf)
        m = jnp.max(s, axis=-1, keepdims=True)
        pb = jnp.exp2(s - m).astype(BF16)
        den = jnp.sum(pb.astype(F32), axis=-1, keepdims=True)
        out = _dot(pb, vh) / den
        if odd:
            out = pltpu.roll(out, HEAD_DIM, axis=1)
        o_ref[0, hd] = out.astype(BF16)


def _attn_dec(q, k_all, v_all, lf_cols, lf_rows, past, lq):
    b = q.shape[0]
    lk = k_all.shape[1]
    return pl.pallas_call(
        functools.partial(_attn_dec_kernel, lq=lq, past=past, lk=lk),
        grid=(b,),
        in_specs=[pl.BlockSpec((1, N_HEADS, lq, HEAD_PAD), lambda i: (i, 0, 0, 0)),
                  pl.BlockSpec((1, lk, D_ATTN), lambda i: (i, 0, 0)),
                  pl.BlockSpec((1, lk, D_ATTN), lambda i: (i, 0, 0)),
                  pl.BlockSpec((1, lk, LANES), lambda i: (i, 0, 0)),
                  pl.BlockSpec((1, N_HEADS, lk), lambda i: (i, 0, 0))],
        out_specs=pl.BlockSpec((1, N_HEADS, lq, HEAD_PAD), lambda i: (i, 0, 0, 0)),
        out_shape=jax.ShapeDtypeStruct((b, N_HEADS, lq, HEAD_PAD), BF16),
        compiler_params=_cparams("arbitrary"),
        name="attn_dec",
    )(q, k_all, v_all, lf_cols, lf_rows)


def _ssm_kernel(u_ref, h0r_ref, h0i_ref, br_ref, bi_ref, cc_ref, ar_ref, ai_ref, d_ref, wg_ref, bg_ref,
                gso_ref, y_ref, hr_out, hi_out, cr_ref, ci_ref, xr_s, xi_s, or_s, oi_s, *, tc):
    @pl.when(pl.program_id(1) == 0)
    def _():
        cr_ref[...] = h0r_ref[0]
        ci_ref[...] = h0i_ref[0]

    u = u_ref[0]
    ub = u.astype(BF16)
    hc = D_SSM // 2
    hs = D_STATE // 2

    def drive(b_ref):
        return jnp.concatenate([_dot(ub[:, :hc], b_ref[0:hc, 0:hs]), _dot(ub[:, hc:], b_ref[hc:, hs:])], axis=-1)

    bur = drive(br_ref)
    bui = drive(bi_ref)
    nt = tc // SCAN_ROWS
    nc = STATE_TILES

    def tile_rows(tau, c):
        r0 = (tau * nc + c) * SCAN_ROWS
        return slice(r0, r0 + SCAN_ROWS)

    for tau in range(nt):
        rows = slice(tau * SCAN_ROWS, (tau + 1) * SCAN_ROWS)
        for c in range(nc):
            xr_s[tile_rows(tau, c), :] = bur[rows, c * LANES:(c + 1) * LANES]
            xi_s[tile_rows(tau, c), :] = bui[rows, c * LANES:(c + 1) * LANES]

    ar = ar_ref[...]
    ai = ai_ref[...]
    hr = cr_ref[...]
    hi = ci_ref[...]
    for t in range(tc):
        step = pl.ds((t // SCAN_ROWS) * nc * SCAN_ROWS + t % SCAN_ROWS, nc, stride=SCAN_ROWS)
        hr, hi = ar * hr - ai * hi + xr_s[step, :], ar * hi + ai * hr + xi_s[step, :]
        or_s[step, :] = hr
        oi_s[step, :] = hi
    cr_ref[...] = hr
    ci_ref[...] = hi
    hr_out[0] = hr
    hi_out[0] = hi

    def states(o_s, c0):
        return jnp.concatenate(
            [jnp.concatenate([o_s[tile_rows(tau, c), :] for c in range(c0, c0 + nc // 2)], axis=-1)
             for tau in range(nt)], axis=0).astype(BF16)

    y = jnp.concatenate(
        [_dot(states(or_s, 0), cc_ref[0:hs, 0:hc]) + _dot(states(oi_s, 0), cc_ref[D_STATE:D_STATE + hs, 0:hc]),
         _dot(states(or_s, nc // 2), cc_ref[hs:D_STATE, hc:]) + _dot(states(oi_s, nc // 2), cc_ref[D_STATE + hs:, hc:])],
        axis=-1) + d_ref[...] * u
    gl = 0.5 * y * (1.0 + jnp.tanh(math.sqrt(2.0 / math.pi) * (y + 0.044715 * (y * y * y))))
    z = _dot(gl.astype(BF16), wg_ref[...]) + bg_ref[...]
    out = y * _sigmoid(z)
    y_ref[0] = _rms(out, gso_ref[...]).astype(BF16)


def _ssm(u, h0r, h0i, consts):
    b, l, _ = u.shape
    br, bi, cc, ar, ai, dsk, wg, bg, gso = consts
    tc = min(l, SSM_CHUNK)
    full = lambda a: pl.BlockSpec(a.shape, lambda i, j: (0,) * a.ndim)
    st = pl.BlockSpec((1, STATE_TILES, LANES), lambda i, j: (i, 0, 0))
    state = jax.ShapeDtypeStruct((b, STATE_TILES, LANES), F32)
    carry = pltpu.VMEM((STATE_TILES, LANES), F32)
    tiles = pltpu.VMEM((tc * STATE_TILES, LANES), F32)
    return pl.pallas_call(
        functools.partial(_ssm_kernel, tc=tc),
        grid=(b, l // tc),
        in_specs=[pl.BlockSpec((1, tc, D_SSM), lambda i, j: (i, j, 0)), st, st,
                  full(br), full(bi), full(cc), full(ar), full(ai), full(dsk), full(wg), full(bg), full(gso)],
        out_specs=[pl.BlockSpec((1, tc, D_SSM), lambda i, j: (i, j, 0)), st, st],
        out_shape=[jax.ShapeDtypeStruct((b, l, D_SSM), BF16), state, state],
        scratch_shapes=[carry, carry, tiles, tiles, tiles, tiles],
        compiler_params=_cparams("arbitrary", "arbitrary"),
        name=f"ssm{l}",
    )(u, h0r, h0i, br, bi, cc, ar, ai, dsk, wg, bg, gso)


def _post_kernel(x_ref, a_ref, s_ref, gt1_ref, sh2_ref, sc2_ref, gt2_ref, cnt0_ref, gao_ref, wo_ref, gffn_ref,
                 wrh_ref, wrl_ref, rb_ref, wsgu_ref, wsd_ref,
                 xm_ref, hp_ref, idx_ref, pos_ref, w_ref, cnt_ref, carry_ref, *, tl, first):
    step = pl.program_id(0) * pl.num_programs(1) + pl.program_id(1)

    @pl.when(step == 0)
    def _():
        carry_ref[...] = cnt0_ref[...] if not first else jnp.zeros_like(carry_ref)

    x = x_ref[0]
    attn = jnp.concatenate([a_ref[0, hd] for hd in range(N_HEADS)], axis=-1).astype(F32)
    ms = jnp.sum(attn * attn, axis=-1, keepdims=True) * (1.0 / D_ATTN)
    attn_n = (attn * lax.rsqrt(ms + EPS) * gao_ref[...]).astype(BF16)
    merged = jnp.concatenate([attn_n, s_ref[0]], axis=-1)
    x1 = x + gt1_ref[0] * _dot(merged, wo_ref[...])
    h2 = _rms(x1, gffn_ref[...]) * (1.0 + sc2_ref[0]) + sh2_ref[0]
    d = h2.shape[-1]
    _store_slabs(hp_ref.at[0], _pack_bf16_pair(h2[:, :d // 2], h2[:, d // 2:]))
    h_hi = h2.astype(BF16)
    h_lo = (h2 - h_hi.astype(F32)).astype(BF16)

    gu = _dot(h_hi, wsgu_ref[...])
    ds = gu.shape[-1] // 2
    g = gu[:, :ds]
    act = (g * _sigmoid(g) * gu[:, ds:]).astype(BF16)
    xm_ref[0] = x1 + gt2_ref[0] * _dot(act, wsd_ref[...])

    wrh = wrh_ref[...]
    logits = _dot_nt(wrh, h_hi) + _dot_nt(wrh, h_lo) + _dot_nt(wrl_ref[...], h_hi)
    s = _sigmoid(logits)
    sc = s + rb_ref[...]
    ge = N_EXPERTS // N_EXP_GROUPS
    neg = -jnp.inf
    gi = lax.broadcasted_iota(I32, (ge, tl), 0)
    gsc = []
    for gidx in range(N_EXP_GROUPS):
        blk = sc[gidx * ge:(gidx + 1) * ge, :]
        m1 = jnp.max(blk, axis=0, keepdims=True)
        f1 = jnp.min(jnp.where(blk == m1, gi, ge), axis=0, keepdims=True)
        m2 = jnp.max(jnp.where(gi == f1, neg, blk), axis=0, keepdims=True)
        gsc.append(m1 + m2)
    gwork = jnp.concatenate(gsc, axis=0)
    ni = lax.broadcasted_iota(I32, (N_EXP_GROUPS, tl), 0)
    gsel = jnp.zeros((N_EXP_GROUPS, tl), F32)
    for _ in range(TOPK_GROUPS):
        mx = jnp.max(gwork, axis=0, keepdims=True)
        fi = jnp.min(jnp.where(gwork == mx, ni, N_EXP_GROUPS), axis=0, keepdims=True)
        hit = ni == fi
        gsel = jnp.where(hit, 1.0, gsel)
        gwork = jnp.where(hit, neg, gwork)
    work = jnp.concatenate(
        [jnp.where(gsel[gidx:gidx + 1, :] > 0.0, sc[gidx * ge:(gidx + 1) * ge, :], neg)
         for gidx in range(N_EXP_GROUPS)], axis=0)
    ei = lax.broadcasted_iota(I32, (N_EXPERTS, tl), 0)
    chosen = jnp.zeros((N_EXPERTS, tl), F32)
    idx_rows, w_rows = [], []
    for _ in range(TOP_K):
        mx = jnp.max(work, axis=0, keepdims=True)
        fi = jnp.min(jnp.where(work == mx, ei, N_EXPERTS), axis=0, keepdims=True)
        hit = ei == fi
        w_rows.append(jnp.sum(jnp.where(hit, s, 0.0), axis=0, keepdims=True))
        idx_rows.append(fi)
        chosen = jnp.where(hit, 1.0, chosen)
        work = jnp.where(hit, neg, work)
    wt = jnp.concatenate(w_rows, axis=0)
    wt = wt / jnp.sum(wt, axis=0, keepdims=True) * ROUTED_SCALE
    idx_ref[0] = jnp.concatenate(idx_rows, axis=0)

    trow = lax.broadcasted_iota(I32, (tl, tl), 0)
    tcol = lax.broadcasted_iota(I32, (tl, tl), 1)
    before = jnp.where(trow < tcol, 1.0, 0.0).astype(BF16)
    rank = _dot(chosen.astype(BF16), before) + carry_ref[...]
    pos_rows = [jnp.sum(jnp.where(ei == idx_rows[k], rank, 0.0), axis=0, keepdims=True) for k in range(TOP_K)]
    pos_ref[0] = jnp.concatenate(pos_rows, axis=0).astype(I32)
    carry_ref[...] = carry_ref[...] + jnp.sum(chosen, axis=1, keepdims=True)
    cnt_ref[...] = carry_ref[...]

    eye = jnp.where(trow == tcol, 1.0, 0.0).astype(BF16)
    t_hi, t_mid, t_lo = _split3(wt)
    w_ref[0] = _dot_nt(eye, t_hi) + _dot_nt(eye, t_mid) + _dot_nt(eye, t_lo)


def _post(x, attn, ssm, mods, cnt0, consts, first):
    b, l, d = x.shape
    tl = min(l, SEQ_BLOCK)
    gt1, sh2, sc2, gt2 = mods
    gao, wo, gffn, wrh, wrl, rb, wsgu, wsd = consts
    slab = d // 2 // LANES
    tok = lambda w: pl.BlockSpec((1, tl, w), lambda i, j: (i, j, 0))
    full = lambda a: pl.BlockSpec(a.shape, lambda i, j: (0,) * a.ndim)
    row = pl.BlockSpec((1, 1, d), lambda i, j: (i, 0, 0))
    tk = pl.BlockSpec((1, TOP_K, tl), lambda i, j: (i, 0, j))
    return pl.pallas_call(
        functools.partial(_post_kernel, tl=tl, first=first),
        grid=(b, l // tl),
        in_specs=[tok(d), pl.BlockSpec((1, N_HEADS, tl, HEAD_PAD), lambda i, j: (i, 0, j, 0)), tok(D_SSM),
                  row, row, row, row, full(cnt0), full(gao), full(wo), full(gffn), full(wrh), full(wrl),
                  full(rb), full(wsgu), full(wsd)],
        out_specs=[tok(d), pl.BlockSpec((1, tl * slab, LANES), lambda i, j: (i, j, 0)), tk, tk, tok(TOP_K),
                   full(cnt0)],
        out_shape=[jax.ShapeDtypeStruct((b, l, d), F32), jax.ShapeDtypeStruct((b, l * slab, LANES), U32),
                   jax.ShapeDtypeStruct((b, TOP_K, l), I32), jax.ShapeDtypeStruct((b, TOP_K, l), I32),
                   jax.ShapeDtypeStruct((b, l, TOP_K), F32), jax.ShapeDtypeStruct(cnt0.shape, F32)],
        scratch_shapes=[pltpu.VMEM(cnt0.shape, F32)],
        compiler_params=_cparams("arbitrary", "arbitrary"),
        name=f"post{l}",
    )(x, attn, ssm, gt1, sh2, sc2, gt2, cnt0, gao, wo, gffn, wrh, wrl, rb, wsgu, wsd)


def _slots_kernel(pstart_ref, idx_ref, pos_ref, dest_ref):
    idx = idx_ref[0]

    def body(e, base):
        return jnp.where(idx == e, pstart_ref[e], base)

    dest_ref[0] = lax.fori_loop(0, N_EXPERTS, body, jnp.zeros_like(idx)) + pos_ref[0]


def _slots(pstart, idx, pos):
    b, k, l = idx.shape
    tl = min(l, 2048)
    spec = pl.BlockSpec((1, k, tl), lambda i, j: (i, 0, j))
    return pl.pallas_call(
        _slots_kernel,
        grid=(b, l // tl),
        in_specs=[pl.BlockSpec(memory_space=pltpu.SMEM), spec, spec],
        out_specs=spec,
        out_shape=jax.ShapeDtypeStruct(idx.shape, I32),
        compiler_params=_cparams("arbitrary", "arbitrary"),
        name=f"slots{l}",
    )(pstart, idx, pos)


INDEX_WINDOW = 2048
INDEX_ALIGN = 1024


def _experts_kernel(be_ref, nb_ref, jb_ref, ng_ref, tok_ref, h_ref, wg_ref, wu_ref, wd_ref, y_ref,
                    wgu_s, wd_s, x0, x1, idx0, idx1, xsem, isem):
    i = pl.program_id(0)
    nbu = nb_ref[0]
    e = be_ref[i]
    prev = be_ref[jnp.maximum(i - 1, 0)]
    dm = wg_ref.shape[1]
    de = wg_ref.shape[2]
    xbufs = (x0, x1)
    ibufs = (idx0, idx1)

    def window(blk, slot):
        start = pl.multiple_of((jb_ref[blk] >> 10) << 10, INDEX_ALIGN)
        return pltpu.make_async_copy(tok_ref.at[pl.ds(start, INDEX_WINDOW)], ibufs[slot], isem.at[slot])

    slab = x0.shape[0] // EXPERT_ROWS

    def gather_rows(blk, slot):
        base = jb_ref[blk] & (INDEX_ALIGN - 1)

        def body(g, c):
            for tt in range(SCAN_ROWS):
                r = g * SCAN_ROWS + tt
                src = pl.multiple_of(ibufs[slot][base + r], slab)
                dst = pl.multiple_of(r * slab, slab)
                pltpu.make_async_copy(h_ref.at[pl.ds(src, slab)], xbufs[slot].at[pl.ds(dst, slab)],
                                      xsem.at[slot]).start()
            return c
        lax.fori_loop(0, ng_ref[blk], body, 0)

    def gather_full_block(blk, slot):
        base = jb_ref[blk] & (INDEX_ALIGN - 1)
        for r in range(EXPERT_ROWS):
            src = pl.multiple_of(ibufs[slot][base + r], slab)
            pltpu.make_async_copy(h_ref.at[pl.ds(src, slab)], xbufs[slot].at[pl.ds(r * slab, slab)],
                                  xsem.at[slot]).start()

    def drain_rows(blk, slot):
        def body(g, c):
            for _ in range(SCAN_ROWS):
                pltpu.make_async_copy(h_ref.at[pl.ds(0, slab)], xbufs[slot].at[pl.ds(0, slab)],
                                      xsem.at[slot]).wait()
            return c
        lax.fori_loop(0, ng_ref[blk], body, 0)

    @pl.when(i >= nbu)
    def _():
        y_ref[...] = jnp.zeros_like(y_ref)

    @pl.when(i < nbu)
    def _():
        last = nbu - 1

        @pl.when(i == 0)
        def _():
            x0[...] = jnp.zeros_like(x0)
            x1[...] = jnp.zeros_like(x1)
            first = window(0, 0)
            first.start()
            first.wait()
            gather_rows(0, 0)
            window(jnp.minimum(1, last), 1).start()

        @pl.when((i == 0) | (e != prev))
        def _():
            wgu_s[:, 0:de] = wg_ref[0].astype(BF16)
            wgu_s[:, de:2 * de] = wu_ref[0].astype(BF16)
            wd_s[...] = wd_ref[0].astype(BF16)

        def run(slot):
            nxt = jnp.minimum(i + 1, last)
            window(nxt, 1 - slot).wait()
            drain_rows(i, slot)

            def multiply():
                xa, xb = _unpack_bf16_pair(_load_slabs(xbufs[slot], EXPERT_ROWS))
                gu = _dot(xa.astype(BF16), wgu_s[0:dm // 2, :]) + _dot(xb.astype(BF16), wgu_s[dm // 2:dm, :])
                g = gu[:, :de]
                act = (g * _sigmoid(g) * gu[:, de:]).astype(BF16)
                y = _dot(act, wd_s[...])
                _store_slabs(y_ref, _pack_bf16_pair(y[:, :dm // 2], y[:, dm // 2:]))

            next_full = ng_ref[nxt] == EXPERT_ROWS // SCAN_ROWS

            @pl.when(next_full)
            def _():
                gather_full_block(nxt, 1 - slot)
                multiply()

            @pl.when(jnp.logical_not(next_full))
            def _():
                gather_rows(nxt, 1 - slot)
                multiply()

            @pl.when(i < last)
            def _():
                window(jnp.minimum(i + 2, last), slot).start()

            @pl.when(i == last)
            def _():
                drain_rows(last, 1 - slot)

        for slot in range(2):
            pl.when(i % 2 == slot)(functools.partial(run, slot))


def _experts(blk_e, nb_used, jbase, ngroups, tok, hp, w_gate, w_up, w_down, rows):
    ne, dm, de = w_gate.shape
    slab = dm // 2 // LANES
    xrows = EXPERT_ROWS * slab
    nb = rows // EXPERT_ROWS
    grid_spec = pltpu.PrefetchScalarGridSpec(
        num_scalar_prefetch=4,
        grid=(nb,),
        in_specs=[pl.BlockSpec(memory_space=pl.ANY), pl.BlockSpec(memory_space=pl.ANY),
                  pl.BlockSpec((1, dm, de), lambda i, be, n, jb, ng: (be[i], 0, 0)),
                  pl.BlockSpec((1, dm, de), lambda i, be, n, jb, ng: (be[i], 0, 0)),
                  pl.BlockSpec((1, de, dm), lambda i, be, n, jb, ng: (be[i], 0, 0))],
        out_specs=pl.BlockSpec((xrows, LANES), lambda i, be, n, jb, ng: (i, 0)),
        scratch_shapes=[pltpu.VMEM((dm, 2 * de), BF16), pltpu.VMEM((de, dm), BF16),
                        pltpu.VMEM((xrows, LANES), U32), pltpu.VMEM((xrows, LANES), U32),
                        pltpu.SMEM((INDEX_WINDOW,), I32), pltpu.SMEM((INDEX_WINDOW,), I32),
                        pltpu.SemaphoreType.DMA((2,)), pltpu.SemaphoreType.DMA((2,))],
    )
    return pl.pallas_call(
        _experts_kernel,
        grid_spec=grid_spec,
        out_shape=jax.ShapeDtypeStruct((rows * slab, LANES), U32),
        compiler_params=_cparams("arbitrary"),
        name="experts",
    )(blk_e, nb_used, jbase, ngroups, tok, hp, w_gate, w_up, w_down)


def _combine_kernel(dcur_ref, dnext_ref, ys_ref, w_ref, xm_ref, gt2_ref, gfin_ref, o_ref, buf0, buf1, sem, *, tl):
    bufs = (buf0, buf1)
    s = pl.program_id(0)
    last = pl.num_programs(0) - 1
    group = SCAN_ROWS
    ngroups = tl // group
    slab = buf0.shape[2] // group
    half = slab * LANES

    def gather(dref, g, to):
        for tt in range(group):
            for k in range(TOP_K):
                src = pl.multiple_of(dref[(g * group + tt) * TOP_K + k], slab)
                pltpu.make_async_copy(ys_ref.at[pl.ds(src, slab)], bufs[to].at[k, g, pl.ds(tt * slab, slab)],
                                      sem.at[to]).start()

    def drain(which):
        def body(t, c):
            for k in range(TOP_K):
                pltpu.make_async_copy(ys_ref.at[pl.ds(0, slab)], bufs[which].at[0, 0, pl.ds(0, slab)],
                                      sem.at[which]).wait()
            return c
        lax.fori_loop(0, tl, body, 0)

    @pl.when(s == 0)
    def _():
        def body(g, c):
            gather(dcur_ref, g, 0)
            return c
        lax.fori_loop(0, ngroups, body, 0)

    gt2 = gt2_ref[0]
    gfin = gfin_ref[...]

    def run(slot):
        drain(slot)

        def body(g, c):
            gather(dnext_ref, g, 1 - slot)
            t0 = pl.multiple_of(g * group, group)
            w = w_ref[pl.ds(t0, group), :]
            acc_a = jnp.zeros((group, half), F32)
            acc_b = jnp.zeros((group, half), F32)
            for k in range(TOP_K):
                ya, yb = _unpack_bf16_pair(_load_slabs(bufs[slot].at[k, g], group))
                wk = w[:, k:k + 1]
                acc_a = acc_a + wk * ya
                acc_b = acc_b + wk * yb
            routed = jnp.concatenate([acc_a, acc_b], axis=-1)
            x2 = xm_ref[pl.ds(t0, group), :] + gt2 * routed
            o_ref[pl.ds(t0, group), :] = _rms(x2, gfin)
            return c

        lax.fori_loop(0, ngroups, body, 0)

        @pl.when(s == last)
        def _():
            drain(1 - slot)

    for slot in range(2):
        pl.when(s % 2 == slot)(functools.partial(run, slot))


def _combine(dest, ys, w, xm, gt2, gfin):
    b, l, d = xm.shape
    tl = min(l, COMBINE_ROWS)
    nl = l // tl
    n = b * nl
    slab = d // 2 // LANES
    tile = pltpu.VMEM((TOP_K, tl // SCAN_ROWS, SCAN_ROWS * slab, LANES), U32)
    out = pl.pallas_call(
        functools.partial(_combine_kernel, tl=tl),
        grid=(n,),
        in_specs=[pl.BlockSpec((tl * TOP_K,), lambda s: (s,), memory_space=pltpu.SMEM),
                  pl.BlockSpec((tl * TOP_K,), lambda s: (jnp.minimum(s + 1, n - 1),), memory_space=pltpu.SMEM),
                  pl.BlockSpec(memory_space=pl.ANY),
                  pl.BlockSpec((tl, TOP_K), lambda s: (s, 0)),
                  pl.BlockSpec((tl, d), lambda s: (s, 0)),
                  pl.BlockSpec((1, 1, d), lambda s: (s // nl, 0, 0)),
                  pl.BlockSpec(gfin.shape, lambda s: (0, 0))],
        out_specs=pl.BlockSpec((tl, d), lambda s: (s, 0)),
        out_shape=jax.ShapeDtypeStruct((b * l, d), F32),
        scratch_shapes=[tile, tile, pltpu.SemaphoreType.DMA((2,))],
        compiler_params=_cparams("arbitrary"),
        name=f"combine{l}",
    )(dest, dest, ys, w.reshape(b * l, TOP_K), xm.reshape(b * l, d), gt2, gfin)
    return out.reshape(b, l, d)


def _block_diag(m):
    g, a, b = m.shape
    eye = jnp.eye(g, dtype=m.dtype)
    return jnp.einsum("gab,gh->gahb", m, eye).reshape(g * a, g * b)


def kernel(x_prompt, x_sample, c_prompt, c_sample, cache_k, cache_v, cache_logf, state_ssm_re, state_ssm_im,
           w_ada, b_ada, g_mix, w_in, b_f, lam_re, lam_im, log_dt, ssm_b_re, ssm_b_im, ssm_c_re, ssm_c_im,
           ssm_d, w_glu, b_glu, g_attn_out, g_ssm_out, w_out, g_ffn, w_router, router_bias,
           w_gate, w_up, w_down, ws_gate, ws_up, ws_down, g_final):
    assert w_ada.shape[0] == 1, "single layer"
    bp, lp, d = x_prompt.shape
    bs, ls, _ = x_sample.shape
    past = cache_k.shape[2]

    nb = bp + bs
    nbp = -(-nb // 8) * 8
    c_all = jnp.concatenate([c_prompt, c_sample, jnp.zeros((nbp - nb, d), F32)], axis=0)
    mod = _prep(c_all, w_ada[0], b_ada[0]).reshape(nbp, 6, 1, d)
    mod_p = [mod[:bp, i] for i in range(6)]
    mod_s = [mod[bp:nb, i] for i in range(6)]

    a_re, a_im, bb_re, bb_im = _disc(lam_re[0], lam_im[0], log_dt[0],
                                     jnp.swapaxes(ssm_b_re[0], 1, 2), jnp.swapaxes(ssm_b_im[0], 1, 2))
    a_re = a_re.reshape(STATE_TILES, LANES)
    a_im = a_im.reshape(STATE_TILES, LANES)
    bd_re = _block_diag(bb_re).astype(BF16)
    bd_im = _block_diag(bb_im).astype(BF16)
    cc = jnp.concatenate([_block_diag(jnp.swapaxes(ssm_c_re[0], 1, 2)),
                          -_block_diag(jnp.swapaxes(ssm_c_im[0], 1, 2))], axis=0).astype(BF16)
    ssm_tail = (ssm_d[0].reshape(1, D_SSM), w_glu[0].astype(BF16), b_glu[0].reshape(1, D_SSM),
                g_ssm_out[0].reshape(1, D_SSM))

    ssm_consts = (bd_re, bd_im, cc, a_re, a_im) + ssm_tail

    inw = _inproj_weights(w_in[0], b_f[0])
    g_mix2 = g_mix[0].reshape(1, d)

    wo = w_out[0]
    wo_attn = jnp.pad(wo[:D_ATTN].reshape(N_HEADS, HEAD_DIM, d), ((0, 0), (0, HEAD_PAD - HEAD_DIM), (0, 0)))
    wo_pad = jnp.concatenate([wo_attn.reshape(N_HEADS * HEAD_PAD, d), wo[D_ATTN:]], axis=0).astype(BF16)
    gao_pad = jnp.pad(g_attn_out[0].reshape(N_HEADS, HEAD_DIM),
                      ((0, 0), (0, HEAD_PAD - HEAD_DIM))).reshape(1, N_HEADS * HEAD_PAD)
    wr_t = w_router[0].T
    wr_hi = wr_t.astype(BF16)
    wr_lo = (wr_t - wr_hi.astype(F32)).astype(BF16)
    post_consts = (gao_pad, wo_pad, g_ffn[0].reshape(1, d), wr_hi, wr_lo, router_bias[0].reshape(N_EXPERTS, 1),
                   jnp.concatenate([ws_gate[0], ws_up[0]], axis=1).astype(BF16), ws_down[0].astype(BF16))

    def mixer(x, modv, attn_fn, h0r, h0i):
        sh1, sc1, gt1, sh2, sc2, gt2 = modv
        q, k, v, kf, vf, lf, fb, u = _inproj(x, sh1, sc1, g_mix2, inw)
        attn = attn_fn(q, k, v, kf, vf, lf, fb)
        ssm, hr, hi = _ssm(u, h0r, h0i, ssm_consts)
        return attn, ssm, (kf, vf, lf, hr, hi), (gt1, sh2, sc2, gt2)

    def attn_prompt(q, k, v, kf, vf, lf, fb):
        return _attn(q, k, v, fb[:, :, 0, :N_HEADS].reshape(-1))

    zeros_p = jnp.zeros((bp, STATE_TILES, LANES), F32)
    attn_p, ssm_p, new_p, m_p = mixer(x_prompt, mod_p, attn_prompt, zeros_p, zeros_p)

    lk = -(-(past + ls) // LANES) * LANES
    padk = lk - past - ls

    def attn_sample(q, k, v, kf, vf, lf, fb):
        k_all = jnp.concatenate([cache_k[0].reshape(bs, past, D_ATTN), kf,
                                 jnp.zeros((bs, padk, D_ATTN), F32)], axis=1)
        v_all = jnp.concatenate([cache_v[0].reshape(bs, past, D_ATTN), vf,
                                 jnp.zeros((bs, padk, D_ATTN), F32)], axis=1)
        lf_all = jnp.concatenate([cache_logf[0], lf, jnp.zeros((bs, padk, N_HEADS), F32)], axis=1)
        lf_cols = jnp.pad(lf_all, ((0, 0), (0, 0), (0, LANES - N_HEADS)))
        lf_rows = jnp.swapaxes(lf_all, 1, 2)
        return _attn_dec(q, k_all, v_all, lf_cols, lf_rows, past, ls)

    attn_s, ssm_s, new_s, m_s = mixer(x_sample, mod_s, attn_sample,
                                      state_ssm_re[0].reshape(bs, STATE_TILES, LANES),
                                      state_ssm_im[0].reshape(bs, STATE_TILES, LANES))

    cnt0 = jnp.zeros((N_EXPERTS, 1), F32)
    xm_p, hp_p, idx_p, pos_p, w_p, cnt1 = _post(x_prompt, attn_p, ssm_p, m_p, cnt0, post_consts, True)
    xm_s, hp_s, idx_s, pos_s, w_s, cnt2 = _post(x_sample, attn_s, ssm_s, m_s, cnt1, post_consts, False)

    counts = cnt2[:, 0].astype(I32)
    padded = (counts + EXPERT_ROWS - 1) // EXPERT_ROWS * EXPERT_ROWS
    pend = jnp.cumsum(padded)
    pstart = pend - padded
    tp, ts = bp * lp, bs * ls
    n_blocks = -(-((tp + ts) * TOP_K) // EXPERT_ROWS) + N_EXPERTS
    rows = n_blocks * EXPERT_ROWS
    dest_p = _slots(pstart, idx_p, pos_p)
    dest_s = _slots(pstart, idx_s, pos_s)
    nb_used = (pend[-1] // EXPERT_ROWS).astype(I32).reshape(1)
    blk_row = jnp.minimum(jnp.arange(n_blocks, dtype=I32), nb_used[0] - 1) * EXPERT_ROWS
    blk_e = jnp.minimum(jnp.sum((pend[None, :] <= blk_row[:, None]).astype(I32), axis=1), N_EXPERTS - 1)

    slab = d // 2 // LANES
    dflat_p = jnp.swapaxes(dest_p, 1, 2).reshape(-1)
    dflat_s = jnp.swapaxes(dest_s, 1, 2).reshape(-1)
    n_pairs = (tp + ts) * TOP_K
    order = jnp.argsort(jnp.concatenate([dflat_p, dflat_s])).astype(I32)
    n_tok = ((n_pairs >> 10) + 3) << 10
    tok = jnp.concatenate([order // TOP_K * slab, jnp.zeros((n_tok - n_pairs,), I32)])
    cstart = jnp.cumsum(counts) - counts
    blk_ids = jnp.arange(n_blocks, dtype=I32)
    blk_off = blk_ids * EXPERT_ROWS - pstart[blk_e]
    used = blk_ids < nb_used[0]
    jbase = jnp.where(used, cstart[blk_e] + blk_off, 0).astype(I32)
    real = jnp.clip(counts[blk_e] - blk_off, 0, EXPERT_ROWS)
    ngroups = jnp.where(used, (real + SCAN_ROWS - 1) // SCAN_ROWS, 0).astype(I32)
    hp_all = jnp.concatenate([hp_p.reshape(tp * slab, LANES), hp_s.reshape(ts * slab, LANES)], axis=0)
    ys = _experts(blk_e, nb_used, jbase, ngroups, tok, hp_all, w_gate[0], w_up[0], w_down[0], rows)

    gfin = g_final.reshape(1, d)
    y_p = _combine(dflat_p * slab, ys, w_p, xm_p, m_p[3], gfin)
    y_s = _combine(dflat_s * slab, ys, w_s, xm_s, m_s[3], gfin)

    def pack(new, b, l):
        kf, vf, lf, hr, hi = new
        return (kf.reshape(1, b, l, N_HEADS, HEAD_DIM), vf.reshape(1, b, l, N_HEADS, HEAD_DIM),
                lf.reshape(1, b, l, N_HEADS),
                hr.reshape(1, b, N_SSM_GROUPS, SSM_STATE), hi.reshape(1, b, N_SSM_GROUPS, SSM_STATE))

    return (y_p, y_s) + pack(new_p, bp, lp) + pack(new_s, bs, ls)
```

```python
import functools
import math

import jax
import jax.numpy as jnp
from jax import lax
from jax.experimental import pallas as pl
from jax.experimental.pallas import tpu as pltpu

F32 = jnp.float32
BF16 = jnp.bfloat16
U32 = jnp.uint32
I32 = jnp.int32

N_HEADS = 8
HEAD_DIM = 64
HEAD_PAD = 128
D_ATTN = N_HEADS * HEAD_DIM
SSM_GROUP = 16
N_SSM_GROUPS = 32
SSM_STATE = 64
D_SSM = SSM_GROUP * N_SSM_GROUPS
D_STATE = N_SSM_GROUPS * SSM_STATE
N_EXPERTS = 256
TOP_K = 8
N_EXP_GROUPS = 8
TOPK_GROUPS = 4
ROUTED_SCALE = 2.5
EPS = 1e-6
LANES = 128

SEQ_BLOCK = 512
SSM_CHUNK = 512
EXPERT_ROWS = 1024
ATTN_HEADS_PER_STEP = 2
ATTN_BLOCKS_PER_TRIP = 4
LOG2E = math.log2(math.e)
COMBINE_ROWS = 256
DISPATCH_ROWS = 512
VMEM_LIMIT = 56 * 1024 * 1024

_NT = (((1,), (1,)), ((), ()))


def _cparams(*sem):
    return pltpu.CompilerParams(dimension_semantics=sem, vmem_limit_bytes=VMEM_LIMIT)


def _split3(x):
    hi = x.astype(BF16)
    r = x - hi.astype(F32)
    mid = r.astype(BF16)
    lo = (r - mid.astype(F32)).astype(BF16)
    return hi, mid, lo


def _dot(a, b):
    return jnp.dot(a, b, preferred_element_type=F32)


def _dot_nt(a, b):
    return lax.dot_general(a, b, _NT, preferred_element_type=F32)


def _rms(x, g):
    return x * lax.rsqrt(jnp.mean(x * x, axis=-1, keepdims=True) + EPS) * g


def _sigmoid(x):
    return 1.0 / (1.0 + jnp.exp(-x))


def _pack_bf16_pair(a, b):
    ab = lax.bitcast_convert_type(a.astype(BF16).astype(F32), U32)
    bb = lax.bitcast_convert_type(b.astype(BF16).astype(F32), U32)
    return (ab >> 16) | (bb & jnp.uint32(0xFFFF0000))


def _unpack_bf16_pair(w):
    a = lax.bitcast_convert_type(w << 16, F32)
    b = lax.bitcast_convert_type(w & jnp.uint32(0xFFFF0000), F32)
    return a, b


def _store_slabs(ref, val):
    n, width = val.shape
    tiles = width // LANES
    for c in range(tiles):
        ref[pl.ds(c, n, stride=tiles), :] = val[:, c * LANES:(c + 1) * LANES]


def _load_slabs(ref, n):
    tiles = ref.shape[0] // n
    return jnp.concatenate([ref[pl.ds(c, n, stride=tiles), :] for c in range(tiles)], axis=-1)


def _prep_kernel(c_ref, w_ref, b_ref, o_ref):
    c = c_ref[...]
    a = c * _sigmoid(c)
    a_hi = a.astype(BF16)
    a_lo = (a - a_hi.astype(F32)).astype(BF16)
    w = w_ref[...]
    w_hi = w.astype(BF16)
    w_lo = (w - w_hi.astype(F32)).astype(BF16)
    o_ref[...] = _dot(a_hi, w_hi) + _dot(a_lo, w_hi) + _dot(a_hi, w_lo) + b_ref[...]


def _prep(c_all, w_ada, b_ada):
    n, d = c_all.shape
    nout = w_ada.shape[1]
    tn = 1024
    return pl.pallas_call(
        _prep_kernel,
        grid=(nout // tn,),
        in_specs=[pl.BlockSpec((n, d), lambda j: (0, 0)),
                  pl.BlockSpec((d, tn), lambda j: (0, j)),
                  pl.BlockSpec((1, tn), lambda j: (0, j))],
        out_specs=pl.BlockSpec((n, tn), lambda j: (0, j)),
        out_shape=jax.ShapeDtypeStruct((n, nout), F32),
        compiler_params=_cparams("arbitrary"),
        name="prep",
    )(c_all, w_ada, b_ada.reshape(1, nout))


def _disc_kernel(lr_ref, li_ref, ldt_ref, br_ref, bi_ref, ar_ref, ai_ref, bbr_ref, bbi_ref):
    lr = lr_ref[...]
    li = li_ref[...]
    dt = jnp.exp(ldt_ref[...])
    er = jnp.exp(lr * dt)
    ang = li * dt
    ar = er * jnp.cos(ang)
    ai = er * jnp.sin(ang)
    ar_ref[...] = ar
    ai_ref[...] = ai
    den = lr * lr + li * li
    nr = ((ar - 1.0) * lr + ai * li) / den
    ni = (ai * lr - (ar - 1.0) * li) / den
    nr3 = nr[:, None, :]
    ni3 = ni[:, None, :]
    br = br_ref[...]
    bi = bi_ref[...]
    bbr_ref[...] = nr3 * br - ni3 * bi
    bbi_ref[...] = nr3 * bi + ni3 * br


def _disc(lam_re, lam_im, log_dt, bt_re, bt_im):
    g, p = lam_re.shape
    c = bt_re.shape[1]
    return pl.pallas_call(
        _disc_kernel,
        out_shape=(jax.ShapeDtypeStruct((g, p), F32), jax.ShapeDtypeStruct((g, p), F32),
                   jax.ShapeDtypeStruct((g, c, p), F32), jax.ShapeDtypeStruct((g, c, p), F32)),
        name="disc",
    )(lam_re, lam_im, log_dt.reshape(g, 1), bt_re, bt_im)


SCAN_ROWS = 8
STATE_TILES = D_STATE // LANES


def _inproj_kernel(x_ref, sh_ref, sc_ref, g_ref, wm_ref, wf_ref, bf_ref, selq_ref, selk_ref,
                   cq_ref, ck_ref, cv_ref,
                   q_ref, k_ref, v_ref, kf_ref, vf_ref, lf_ref, fb_ref, u_ref, carry_ref, *, tl):
    @pl.when(pl.program_id(1) == 0)
    def _():
        carry_ref[...] = jnp.zeros_like(carry_ref)

    x = x_ref[0]
    h = _rms(x, g_ref[...]) * (1.0 + sc_ref[0]) + sh_ref[0]
    hb = h.astype(BF16)
    main = _dot(hb, wm_ref[...])
    hw = N_HEADS * HEAD_PAD
    qp = main[:, 0:hw] * (HEAD_DIM ** -0.5 * LOG2E)
    kp = main[:, hw:2 * hw]
    vp = main[:, 2 * hw:3 * hw]
    u_ref[0] = main[:, 3 * hw:3 * hw + D_SSM]
    for hd in range(N_HEADS):
        src = slice(hd * HEAD_PAD, hd * HEAD_PAD + HEAD_DIM)
        dst = slice(hd * HEAD_DIM, (hd + 1) * HEAD_DIM)
        kf_ref[0, :, dst] = kp[:, src]
        vf_ref[0, :, dst] = vp[:, src]

    fl = _dot(hb, wf_ref[...]) + bf_ref[...]
    lf = jnp.minimum(fl, 0.0) - jnp.log1p(jnp.exp(-jnp.abs(fl)))
    lf_ref[0] = lf[:, 0:N_HEADS]

    row = lax.broadcasted_iota(I32, (tl, tl), 0)
    col = lax.broadcasted_iota(I32, (tl, tl), 1)
    tri = jnp.where(row >= col, 1.0, 0.0).astype(BF16)
    hi, mid, lo = _split3(lf)
    frel = _dot(tri, hi) + _dot(tri, mid) + _dot(tri, lo)
    fb_ref[0, 0] = carry_ref[...] * LOG2E
    carry_ref[...] = carry_ref[...] + frel[tl - 1:tl, :]

    fcat = jnp.concatenate(_split3(frel * LOG2E), axis=-1)
    q_aug = (qp + _dot(fcat, selq_ref[...]) + cq_ref[...]).astype(BF16)
    k_aug = (kp + _dot(fcat, selk_ref[...]) + ck_ref[...]).astype(BF16)
    v_aug = (vp + cv_ref[...]).astype(BF16)
    for hd in range(N_HEADS):
        sl = slice(hd * HEAD_PAD, (hd + 1) * HEAD_PAD)
        q_ref[0, hd] = q_aug[:, sl]
        k_ref[0, hd] = k_aug[:, sl]
        v_ref[0, hd] = v_aug[:, sl]


def _inproj(x, sh, sc, g, wts):
    b, l, d = x.shape
    tl = min(l, SEQ_BLOCK)
    nl = l // tl
    wm, wf, bf, selq, selk, cq, ck, cv = wts
    hm = jax.ShapeDtypeStruct((b, N_HEADS, l, HEAD_PAD), BF16)
    hm_spec = pl.BlockSpec((1, N_HEADS, tl, HEAD_PAD), lambda i, j: (i, 0, j, 0))
    tok = lambda w: pl.BlockSpec((1, tl, w), lambda i, j: (i, j, 0))
    full = lambda a: pl.BlockSpec(a.shape, lambda i, j: (0,) * a.ndim)
    row = pl.BlockSpec((1, 1, d), lambda i, j: (i, 0, 0))
    return pl.pallas_call(
        functools.partial(_inproj_kernel, tl=tl),
        grid=(b, nl),
        in_specs=[tok(d), row, row, full(g), full(wm), full(wf), full(bf), full(selq), full(selk),
                  full(cq), full(ck), full(cv)],
        out_specs=[hm_spec, hm_spec, hm_spec, tok(D_ATTN), tok(D_ATTN), tok(N_HEADS),
                   pl.BlockSpec((1, 1, 1, LANES), lambda i, j: (i, j, 0, 0)), tok(D_SSM)],
        out_shape=[hm, hm, hm,
                   jax.ShapeDtypeStruct((b, l, D_ATTN), F32), jax.ShapeDtypeStruct((b, l, D_ATTN), F32),
                   jax.ShapeDtypeStruct((b, l, N_HEADS), F32),
                   jax.ShapeDtypeStruct((b, nl, 1, LANES), F32),
                   jax.ShapeDtypeStruct((b, l, D_SSM), F32)],
        scratch_shapes=[pltpu.VMEM((1, LANES), F32)],
        compiler_params=_cparams("arbitrary", "arbitrary"),
        name=f"inproj{l}",
    )(x, sh, sc, g, wm, wf, bf, selq, selk, cq, ck, cv)


def _inproj_weights(w_in, b_f):
    d = w_in.shape[0]
    wq, wk, wv = (w_in[:, i * D_ATTN:(i + 1) * D_ATTN] for i in range(3))
    wfl = w_in[:, 3 * D_ATTN:3 * D_ATTN + N_HEADS]
    wu = w_in[:, 3 * D_ATTN + N_HEADS:]

    def pad_heads(w):
        w = w.reshape(d, N_HEADS, HEAD_DIM)
        w = jnp.pad(w, ((0, 0), (0, 0), (0, HEAD_PAD - HEAD_DIM)))
        return w.reshape(d, N_HEADS * HEAD_PAD)

    wm = jnp.concatenate([pad_heads(wq), pad_heads(wk), pad_heads(wv), wu], axis=1).astype(BF16)
    wf = jnp.pad(wfl, ((0, 0), (0, LANES - N_HEADS))).astype(BF16)
    bf = jnp.pad(b_f, (0, LANES - N_HEADS)).reshape(1, LANES).astype(F32)

    hw = N_HEADS * HEAD_PAD
    hd = jnp.arange(N_HEADS)
    selq = jnp.zeros((3 * LANES, hw), F32)
    selk = jnp.zeros((3 * LANES, hw), F32)
    cq = jnp.zeros((1, hw), F32)
    ck = jnp.zeros((1, hw), F32)
    cv = jnp.zeros((1, hw), F32)
    for part in range(3):
        selq = selq.at[part * LANES + hd, hd * HEAD_PAD + HEAD_DIM + part].set(1.0)
        selk = selk.at[part * LANES + hd, hd * HEAD_PAD + HEAD_DIM + 3 + part].set(-1.0)
        cq = cq.at[0, hd * HEAD_PAD + HEAD_DIM + 3 + part].set(1.0)
        ck = ck.at[0, hd * HEAD_PAD + HEAD_DIM + part].set(1.0)
    cv = cv.at[0, hd * HEAD_PAD + HEAD_DIM].set(1.0)
    return wm, wf, bf, selq.astype(BF16), selk.astype(BF16), cq, ck, cv


def _attn_kernel(fb_ref, q_ref, k_ref, v_ref, o_ref, *, r, nblk, hp):
    b = pl.program_id(0)
    g = pl.program_id(1)
    i = pl.program_id(2)
    fbase = (b * nblk) * N_HEADS + g * hp

    def scores(j):
        start = pl.multiple_of(j * r, r)
        return tuple(_dot_nt(q_ref[0, hh], k_ref[0, hh, pl.ds(start, r), :]) for hh in range(hp))

    def absorb(j, s_all, state, masked):
        start = pl.multiple_of(j * r, r)
        out = []
        for hh in range(hp):
            m, acc = state[hh]
            s = s_all[hh]
            if masked:
                row = lax.broadcasted_iota(I32, (r, r), 0)
                col = lax.broadcasted_iota(I32, (r, r), 1)
                s = jnp.where(row >= col, s, -jnp.inf)
            dlt = fb_ref[fbase + hh + i * N_HEADS] - fb_ref[fbase + hh + j * N_HEADS]
            mnew = jnp.maximum(m, jnp.max(s, axis=-1, keepdims=True) + dlt)
            p = jnp.exp2(s - (mnew - dlt))
            acc = jnp.exp2(m - mnew) * acc + _dot(p.astype(BF16), v_ref[0, hh, pl.ds(start, r), :])
            out.append((mnew, acc))
        return tuple(out)

    nu = ATTN_BLOCKS_PER_TRIP

    def several(jj, state):
        js = [nu * jj + u for u in range(nu)]
        ss = [scores(j) for j in js]
        for j, s_all in zip(js, ss):
            state = absorb(j, s_all, state, False)
        return state

    def single(j, state):
        return absorb(j, scores(j), state, False)

    state = tuple((jnp.full((r, 1), -jnp.inf, F32), jnp.zeros((r, HEAD_PAD), F32)) for _ in range(hp))
    state = lax.fori_loop(0, i // nu, several, state)
    state = lax.fori_loop(nu * (i // nu), i, single, state)
    state = absorb(i, scores(i), state, True)
    lane = lax.broadcasted_iota(I32, (r, HEAD_PAD), 1)
    for hh in range(hp):
        acc = state[hh][1]
        out = acc / acc[:, HEAD_DIM:HEAD_DIM + 1]
        o_ref[0, hh] = jnp.where(lane < HEAD_DIM, out, 0.0).astype(BF16)


def _attn(q, k, v, fb):
    b, h, l, _ = q.shape
    r = min(l, SEQ_BLOCK)
    nblk = l // r
    hp = ATTN_HEADS_PER_STEP
    return pl.pallas_call(
        functools.partial(_attn_kernel, r=r, nblk=nblk, hp=hp),
        grid=(b, h // hp, nblk),
        in_specs=[pl.BlockSpec(memory_space=pltpu.SMEM),
                  pl.BlockSpec((1, hp, r, HEAD_PAD), lambda bi, gi, i: (bi, gi, i, 0)),
                  pl.BlockSpec((1, hp, l, HEAD_PAD), lambda bi, gi, i: (bi, gi, 0, 0)),
                  pl.BlockSpec((1, hp, l, HEAD_PAD), lambda bi, gi, i: (bi, gi, 0, 0))],
        out_specs=pl.BlockSpec((1, hp, r, HEAD_PAD), lambda bi, gi, i: (bi, gi, i, 0)),
        out_shape=jax.ShapeDtypeStruct((b, h, l, HEAD_PAD), BF16),
        compiler_params=_cparams("arbitrary", "arbitrary", "arbitrary"),
        name="attn",
    )(fb, q, k, v)


def _attn_dec_kernel(q_ref, k_ref, v_ref, lfc_ref, lfr_ref, o_ref, *, lq, past, lk):
    row = lax.broadcasted_iota(I32, (lk, lk), 0)
    col = lax.broadcasted_iota(I32, (lk, lk), 1)
    tri = jnp.where(row >= col, 1.0, 0.0).astype(BF16)
    upper = jnp.where(row <= col, 1.0, 0.0).astype(BF16)
    c_hi, c_mid, c_lo = _split3(lfc_ref[0])
    fcol = _dot(tri, c_hi) + _dot(tri, c_mid) + _dot(tri, c_lo)
    r_hi, r_mid, r_lo = _split3(lfr_ref[0])
    frow = _dot(r_hi, upper) + _dot(r_mid, upper) + _dot(r_lo, upper)
    fq = fcol[past:past + lq, :]
    qpos = past + lax.broadcasted_iota(I32, (lq, lk), 0)
    kpos = lax.broadcasted_iota(I32, (lq, lk), 1)
    ok = kpos <= qpos
    qlane = lax.broadcasted_iota(I32, (lq, HEAD_PAD), 1)
    klane = lax.broadcasted_iota(I32, (lk, HEAD_PAD), 1)
    for hd in range(N_HEADS):
        odd = hd % 2 == 1
        pair = slice((hd // 2) * HEAD_PAD, (hd // 2 + 1) * HEAD_PAD)
        mine = (klane >= HEAD_DIM) if odd else (klane < HEAD_DIM)
        kh = jnp.where(mine, k_ref[0, :, pair], 0.0).astype(BF16)
        vh = jnp.where(mine, v_ref[0, :, pair], 0.0).astype(BF16)
        qh = jnp.where(qlane < HEAD_DIM, q_ref[0, hd].astype(F32), 0.0)
        if odd:
            qh = pltpu.roll(qh, HEAD_DIM, axis=1)
        s = _dot_nt(qh.astype(BF16), kh) + LOG2E * (fq[:, hd:hd + 1] - frow[hd:hd + 1, :])
        s = jnp.where(ok, s, -jnp.inf)
        m = jnp.max(s, axis=-1, keepdims=True)
        pb = jnp.exp2(s - m).astype(BF16)
        den = jnp.sum(pb.astype(F32), axis=-1, keepdims=True)
        out = _dot(pb, vh) / den
        if odd:
            out = pltpu.roll(out, HEAD_DIM, axis=1)
        o_ref[0, hd] = out.astype(BF16)


def _attn_dec(q, k_all, v_all, lf_cols, lf_rows, past, lq):
    b = q.shape[0]
    lk = k_all.shape[1]
    return pl.pallas_call(
        functools.partial(_attn_dec_kernel, lq=lq, past=past, lk=lk),
        grid=(b,),
        in_specs=[pl.BlockSpec((1, N_HEADS, lq, HEAD_PAD), lambda i: (i, 0, 0, 0)),
                  pl.BlockSpec((1, lk, D_ATTN), lambda i: (i, 0, 0)),
                  pl.BlockSpec((1, lk, D_ATTN), lambda i: (i, 0, 0)),
                  pl.BlockSpec((1, lk, LANES), lambda i: (i, 0, 0)),
                  pl.BlockSpec((1, N_HEADS, lk), lambda i: (i, 0, 0))],
        out_specs=pl.BlockSpec((1, N_HEADS, lq, HEAD_PAD), lambda i: (i, 0, 0, 0)),
        out_shape=jax.ShapeDtypeStruct((b, N_HEADS, lq, HEAD_PAD), BF16),
        compiler_params=_cparams("arbitrary"),
        name="attn_dec",
    )(q, k_all, v_all, lf_cols, lf_rows)


def _ssm_kernel(u_ref, h0r_ref, h0i_ref, br_ref, bi_ref, cc_ref, ar_ref, ai_ref, d_ref, wg_ref, bg_ref,
                gso_ref, y_ref, hr_out, hi_out, cr_ref, ci_ref, xr_s, xi_s, or_s, oi_s, *, tc):
    @pl.when(pl.program_id(1) == 0)
    def _():
        cr_ref[...] = h0r_ref[0]
        ci_ref[...] = h0i_ref[0]

    u = u_ref[0]
    ub = u.astype(BF16)
    hc = D_SSM // 2
    hs = D_STATE // 2

    def drive(b_ref):
        return jnp.concatenate([_dot(ub[:, :hc], b_ref[0:hc, 0:hs]), _dot(ub[:, hc:], b_ref[hc:, hs:])], axis=-1)

    bur = drive(br_ref)
    bui = drive(bi_ref)
    nt = tc // SCAN_ROWS
    nc = STATE_TILES

    def tile_rows(tau, c):
        r0 = (tau * nc + c) * SCAN_ROWS
        return slice(r0, r0 + SCAN_ROWS)

    for tau in range(nt):
        rows = slice(tau * SCAN_ROWS, (tau + 1) * SCAN_ROWS)
        for c in range(nc):
            xr_s[tile_rows(tau, c), :] = bur[rows, c * LANES:(c + 1) * LANES]
            xi_s[tile_rows(tau, c), :] = bui[rows, c * LANES:(c + 1) * LANES]

    ar = ar_ref[...]
    ai = ai_ref[...]
    hr = cr_ref[...]
    hi = ci_ref[...]
    for t in range(tc):
        step = pl.ds((t // SCAN_ROWS) * nc * SCAN_ROWS + t % SCAN_ROWS, nc, stride=SCAN_ROWS)
        hr, hi = ar * hr - ai * hi + xr_s[step, :], ar * hi + ai * hr + xi_s[step, :]
        or_s[step, :] = hr
        oi_s[step, :] = hi
    cr_ref[...] = hr
    ci_ref[...] = hi
    hr_out[0] = hr
    hi_out[0] = hi

    def states(o_s, c0):
        return jnp.concatenate(
            [jnp.concatenate([o_s[tile_rows(tau, c), :] for c in range(c0, c0 + nc // 2)], axis=-1)
             for tau in range(nt)], axis=0).astype(BF16)

    y = jnp.concatenate(
        [_dot(states(or_s, 0), cc_ref[0:hs, 0:hc]) + _dot(states(oi_s, 0), cc_ref[D_STATE:D_STATE + hs, 0:hc]),
         _dot(states(or_s, nc // 2), cc_ref[hs:D_STATE, hc:]) + _dot(states(oi_s, nc // 2), cc_ref[D_STATE + hs:, hc:])],
        axis=-1) + d_ref[...] * u
    gl = 0.5 * y * (1.0 + jnp.tanh(math.sqrt(2.0 / math.pi) * (y + 0.044715 * (y * y * y))))
    z = _dot(gl.astype(BF16), wg_ref[...]) + bg_ref[...]
    out = y * _sigmoid(z)
    y_ref[0] = _rms(out, gso_ref[...]).astype(BF16)


def _ssm(u, h0r, h0i, consts):
    b, l, _ = u.shape
    br, bi, cc, ar, ai, dsk, wg, bg, gso = consts
    tc = min(l, SSM_CHUNK)
    full = lambda a: pl.BlockSpec(a.shape, lambda i, j: (0,) * a.ndim)
    st = pl.BlockSpec((1, STATE_TILES, LANES), lambda i, j: (i, 0, 0))
    state = jax.ShapeDtypeStruct((b, STATE_TILES, LANES), F32)
    carry = pltpu.VMEM((STATE_TILES, LANES), F32)
    tiles = pltpu.VMEM((tc * STATE_TILES, LANES), F32)
    return pl.pallas_call(
        functools.partial(_ssm_kernel, tc=tc),
        grid=(b, l // tc),
        in_specs=[pl.BlockSpec((1, tc, D_SSM), lambda i, j: (i, j, 0)), st, st,
                  full(br), full(bi), full(cc), full(ar), full(ai), full(dsk), full(wg), full(bg), full(gso)],
        out_specs=[pl.BlockSpec((1, tc, D_SSM), lambda i, j: (i, j, 0)), st, st],
        out_shape=[jax.ShapeDtypeStruct((b, l, D_SSM), BF16), state, state],
        scratch_shapes=[carry, carry, tiles, tiles, tiles, tiles],
        compiler_params=_cparams("arbitrary", "arbitrary"),
        name=f"ssm{l}",
    )(u, h0r, h0i, br, bi, cc, ar, ai, dsk, wg, bg, gso)


def _post_kernel(x_ref, a_ref, s_ref, gt1_ref, sh2_ref, sc2_ref, gt2_ref, cnt0_ref, gao_ref, wo_ref, gffn_ref,
                 wrh_ref, wrl_ref, rb_ref, wsgu_ref, wsd_ref,
                 xm_ref, hp_ref, idx_ref, pos_ref, w_ref, cnt_ref, carry_ref, *, tl, first):
    step = pl.program_id(0) * pl.num_programs(1) + pl.program_id(1)

    @pl.when(step == 0)
    def _():
        carry_ref[...] = cnt0_ref[...] if not first else jnp.zeros_like(carry_ref)

    x = x_ref[0]
    attn = jnp.concatenate([a_ref[0, hd] for hd in range(N_HEADS)], axis=-1).astype(F32)
    ms = jnp.sum(attn * attn, axis=-1, keepdims=True) * (1.0 / D_ATTN)
    attn_n = (attn * lax.rsqrt(ms + EPS) * gao_ref[...]).astype(BF16)
    merged = jnp.concatenate([attn_n, s_ref[0]], axis=-1)
    x1 = x + gt1_ref[0] * _dot(merged, wo_ref[...])
    h2 = _rms(x1, gffn_ref[...]) * (1.0 + sc2_ref[0]) + sh2_ref[0]
    d = h2.shape[-1]
    _store_slabs(hp_ref.at[0], _pack_bf16_pair(h2[:, :d // 2], h2[:, d // 2:]))
    h_hi = h2.astype(BF16)
    h_lo = (h2 - h_hi.astype(F32)).astype(BF16)

    gu = _dot(h_hi, wsgu_ref[...])
    ds = gu.shape[-1] // 2
    g = gu[:, :ds]
    act = (g * _sigmoid(g) * gu[:, ds:]).astype(BF16)
    xm_ref[0] = x1 + gt2_ref[0] * _dot(act, wsd_ref[...])

    wrh = wrh_ref[...]
    logits = _dot_nt(wrh, h_hi) + _dot_nt(wrh, h_lo) + _dot_nt(wrl_ref[...], h_hi)
    s = _sigmoid(logits)
    sc = s + rb_ref[...]
    ge = N_EXPERTS // N_EXP_GROUPS
    neg = -jnp.inf
    gi = lax.broadcasted_iota(I32, (ge, tl), 0)
    gsc = []
    for gidx in range(N_EXP_GROUPS):
        blk = sc[gidx * ge:(gidx + 1) * ge, :]
        m1 = jnp.max(blk, axis=0, keepdims=True)
        f1 = jnp.min(jnp.where(blk == m1, gi, ge), axis=0, keepdims=True)
        m2 = jnp.max(jnp.where(gi == f1, neg, blk), axis=0, keepdims=True)
        gsc.append(m1 + m2)
    gwork = jnp.concatenate(gsc, axis=0)
    ni = lax.broadcasted_iota(I32, (N_EXP_GROUPS, tl), 0)
    gsel = jnp.zeros((N_EXP_GROUPS, tl), F32)
    for _ in range(TOPK_GROUPS):
        mx = jnp.max(gwork, axis=0, keepdims=True)
        fi = jnp.min(jnp.where(gwork == mx, ni, N_EXP_GROUPS), axis=0, keepdims=True)
        hit = ni == fi
        gsel = jnp.where(hit, 1.0, gsel)
        gwork = jnp.where(hit, neg, gwork)
    work = jnp.concatenate(
        [jnp.where(gsel[gidx:gidx + 1, :] > 0.0, sc[gidx * ge:(gidx + 1) * ge, :], neg)
         for gidx in range(N_EXP_GROUPS)], axis=0)
    ei = lax.broadcasted_iota(I32, (N_EXPERTS, tl), 0)
    chosen = jnp.zeros((N_EXPERTS, tl), F32)
    idx_rows, w_rows = [], []
    for _ in range(TOP_K):
        mx = jnp.max(work, axis=0, keepdims=True)
        fi = jnp.min(jnp.where(work == mx, ei, N_EXPERTS), axis=0, keepdims=True)
        hit = ei == fi
        w_rows.append(jnp.sum(jnp.where(hit, s, 0.0), axis=0, keepdims=True))
        idx_rows.append(fi)
        chosen = jnp.where(hit, 1.0, chosen)
        work = jnp.where(hit, neg, work)
    wt = jnp.concatenate(w_rows, axis=0)
    wt = wt / jnp.sum(wt, axis=0, keepdims=True) * ROUTED_SCALE
    idx_ref[0] = jnp.concatenate(idx_rows, axis=0)

    trow = lax.broadcasted_iota(I32, (tl, tl), 0)
    tcol = lax.broadcasted_iota(I32, (tl, tl), 1)
    before = jnp.where(trow < tcol, 1.0, 0.0).astype(BF16)
    rank = _dot(chosen.astype(BF16), before) + carry_ref[...]
    pos_rows = [jnp.sum(jnp.where(ei == idx_rows[k], rank, 0.0), axis=0, keepdims=True) for k in range(TOP_K)]
    pos_ref[0] = jnp.concatenate(pos_rows, axis=0).astype(I32)
    carry_ref[...] = carry_ref[...] + jnp.sum(chosen, axis=1, keepdims=True)
    cnt_ref[...] = carry_ref[...]

    eye = jnp.where(trow == tcol, 1.0, 0.0).astype(BF16)
    t_hi, t_mid, t_lo = _split3(wt)
    w_ref[0] = _dot_nt(eye, t_hi) + _dot_nt(eye, t_mid) + _dot_nt(eye, t_lo)


def _post(x, attn, ssm, mods, cnt0, consts, first):
    b, l, d = x.shape
    tl = min(l, SEQ_BLOCK)
    gt1, sh2, sc2, gt2 = mods
    gao, wo, gffn, wrh, wrl, rb, wsgu, wsd = consts
    slab = d // 2 // LANES
    tok = lambda w: pl.BlockSpec((1, tl, w), lambda i, j: (i, j, 0))
    full = lambda a: pl.BlockSpec(a.shape, lambda i, j: (0,) * a.ndim)
    row = pl.BlockSpec((1, 1, d), lambda i, j: (i, 0, 0))
    tk = pl.BlockSpec((1, TOP_K, tl), lambda i, j: (i, 0, j))
    return pl.pallas_call(
        functools.partial(_post_kernel, tl=tl, first=first),
        grid=(b, l // tl),
        in_specs=[tok(d), pl.BlockSpec((1, N_HEADS, tl, HEAD_PAD), lambda i, j: (i, 0, j, 0)), tok(D_SSM),
                  row, row, row, row, full(cnt0), full(gao), full(wo), full(gffn), full(wrh), full(wrl),
                  full(rb), full(wsgu), full(wsd)],
        out_specs=[tok(d), pl.BlockSpec((1, tl * slab, LANES), lambda i, j: (i, j, 0)), tk, tk, tok(TOP_K),
                   full(cnt0)],
        out_shape=[jax.ShapeDtypeStruct((b, l, d), F32), jax.ShapeDtypeStruct((b, l * slab, LANES), U32),
                   jax.ShapeDtypeStruct((b, TOP_K, l), I32), jax.ShapeDtypeStruct((b, TOP_K, l), I32),
                   jax.ShapeDtypeStruct((b, l, TOP_K), F32), jax.ShapeDtypeStruct(cnt0.shape, F32)],
        scratch_shapes=[pltpu.VMEM(cnt0.shape, F32)],
        compiler_params=_cparams("arbitrary", "arbitrary"),
        name=f"post{l}",
    )(x, attn, ssm, gt1, sh2, sc2, gt2, cnt0, gao, wo, gffn, wrh, wrl, rb, wsgu, wsd)


def _slots_kernel(pstart_ref, idx_ref, pos_ref, dest_ref):
    idx = idx_ref[0]

    def body(e, base):
        return jnp.where(idx == e, pstart_ref[e], base)

    dest_ref[0] = lax.fori_loop(0, N_EXPERTS, body, jnp.zeros_like(idx)) + pos_ref[0]


def _slots(pstart, idx, pos):
    b, k, l = idx.shape
    tl = min(l, 2048)
    spec = pl.BlockSpec((1, k, tl), lambda i, j: (i, 0, j))
    return pl.pallas_call(
        _slots_kernel,
        grid=(b, l // tl),
        in_specs=[pl.BlockSpec(memory_space=pltpu.SMEM), spec, spec],
        out_specs=spec,
        out_shape=jax.ShapeDtypeStruct(idx.shape, I32),
        compiler_params=_cparams("arbitrary", "arbitrary"),
        name=f"slots{l}",
    )(pstart, idx, pos)


INDEX_WINDOW = 2048
INDEX_ALIGN = 1024


def _experts_kernel(be_ref, nb_ref, jb_ref, ng_ref, tok_ref, h_ref, wg_ref, wu_ref, wd_ref, y_ref,
                    wgu_s, wd_s, x0, x1, idx0, idx1, xsem, isem):
    i = pl.program_id(0)
    nbu = nb_ref[0]
    e = be_ref[i]
    prev = be_ref[jnp.maximum(i - 1, 0)]
    dm = wg_ref.shape[1]
    de = wg_ref.shape[2]
    xbufs = (x0, x1)
    ibufs = (idx0, idx1)

    def window(blk, slot):
        start = pl.multiple_of((jb_ref[blk] >> 10) << 10, INDEX_ALIGN)
        return pltpu.make_async_copy(tok_ref.at[pl.ds(start, INDEX_WINDOW)], ibufs[slot], isem.at[slot])

    slab = x0.shape[0] // EXPERT_ROWS

    def gather_rows(blk, slot):
        base = jb_ref[blk] & (INDEX_ALIGN - 1)

        def body(g, c):
            for tt in range(SCAN_ROWS):
                r = g * SCAN_ROWS + tt
                src = pl.multiple_of(ibufs[slot][base + r], slab)
                dst = pl.multiple_of(r * slab, slab)
                pltpu.make_async_copy(h_ref.at[pl.ds(src, slab)], xbufs[slot].at[pl.ds(dst, slab)],
                                      xsem.at[slot]).start()
            return c
        lax.fori_loop(0, ng_ref[blk], body, 0)

    def drain_rows(blk, slot):
        def body(g, c):
            pltpu.make_async_copy(h_ref.at[pl.ds(0, SCAN_ROWS * slab)], xbufs[slot].at[pl.ds(0, SCAN_ROWS * slab)],
                                  xsem.at[slot]).wait()
            return c
        lax.fori_loop(0, ng_ref[blk], body, 0)

    @pl.when(i >= nbu)
    def _():
        y_ref[...] = jnp.zeros_like(y_ref)

    @pl.when(i < nbu)
    def _():
        last = nbu - 1

        @pl.when(i == 0)
        def _():
            x0[...] = jnp.zeros_like(x0)
            x1[...] = jnp.zeros_like(x1)
            first = window(0, 0)
            first.start()
            first.wait()
            gather_rows(0, 0)
            window(jnp.minimum(1, last), 1).start()

        @pl.when((i == 0) | (e != prev))
        def _():
            wgu_s[:, 0:de] = wg_ref[0].astype(BF16)
            wgu_s[:, de:2 * de] = wu_ref[0].astype(BF16)
            wd_s[...] = wd_ref[0].astype(BF16)

        def run(slot):
            nxt = jnp.minimum(i + 1, last)
            window(nxt, 1 - slot).wait()
            drain_rows(i, slot)
            gather_rows(nxt, 1 - slot)
            xa, xb = _unpack_bf16_pair(_load_slabs(xbufs[slot], EXPERT_ROWS))
            gu = _dot(xa.astype(BF16), wgu_s[0:dm // 2, :]) + _dot(xb.astype(BF16), wgu_s[dm // 2:dm, :])
            g = gu[:, :de]
            act = (g * _sigmoid(g) * gu[:, de:]).astype(BF16)
            y = _dot(act, wd_s[...])
            _store_slabs(y_ref, _pack_bf16_pair(y[:, :dm // 2], y[:, dm // 2:]))

            @pl.when(i < last)
            def _():
                window(jnp.minimum(i + 2, last), slot).start()

            @pl.when(i == last)
            def _():
                drain_rows(last, 1 - slot)

        for slot in range(2):
            pl.when(i % 2 == slot)(functools.partial(run, slot))


def _experts(blk_e, nb_used, jbase, ngroups, tok, hp, w_gate, w_up, w_down, rows):
    ne, dm, de = w_gate.shape
    slab = dm // 2 // LANES
    xrows = EXPERT_ROWS * slab
    nb = rows // EXPERT_ROWS
    grid_spec = pltpu.PrefetchScalarGridSpec(
        num_scalar_prefetch=4,
        grid=(nb,),
        in_specs=[pl.BlockSpec(memory_space=pl.ANY), pl.BlockSpec(memory_space=pl.ANY),
                  pl.BlockSpec((1, dm, de), lambda i, be, n, jb, ng: (be[i], 0, 0)),
                  pl.BlockSpec((1, dm, de), lambda i, be, n, jb, ng: (be[i], 0, 0)),
                  pl.BlockSpec((1, de, dm), lambda i, be, n, jb, ng: (be[i], 0, 0))],
        out_specs=pl.BlockSpec((xrows, LANES), lambda i, be, n, jb, ng: (i, 0)),
        scratch_shapes=[pltpu.VMEM((dm, 2 * de), BF16), pltpu.VMEM((de, dm), BF16),
                        pltpu.VMEM((xrows, LANES), U32), pltpu.VMEM((xrows, LANES), U32),
                        pltpu.SMEM((INDEX_WINDOW,), I32), pltpu.SMEM((INDEX_WINDOW,), I32),
                        pltpu.SemaphoreType.DMA((2,)), pltpu.SemaphoreType.DMA((2,))],
    )
    return pl.pallas_call(
        _experts_kernel,
        grid_spec=grid_spec,
        out_shape=jax.ShapeDtypeStruct((rows * slab, LANES), U32),
        compiler_params=_cparams("arbitrary"),
        name="experts",
    )(blk_e, nb_used, jbase, ngroups, tok, hp, w_gate, w_up, w_down)


def _combine_kernel(dcur_ref, dnext_ref, ys_ref, w_ref, xm_ref, gt2_ref, gfin_ref, o_ref, buf0, buf1, sem, *, tl):
    bufs = (buf0, buf1)
    s = pl.program_id(0)
    last = pl.num_programs(0) - 1
    group = SCAN_ROWS
    ngroups = tl // group
    slab = buf0.shape[2] // group
    half = slab * LANES

    def gather(dref, g, to):
        for tt in range(group):
            for k in range(TOP_K):
                src = pl.multiple_of(dref[(g * group + tt) * TOP_K + k], slab)
                pltpu.make_async_copy(ys_ref.at[pl.ds(src, slab)], bufs[to].at[k, g, pl.ds(tt * slab, slab)],
                                      sem.at[to]).start()

    def drain(which):
        def body(g, c):
            for k in range(TOP_K):
                pltpu.make_async_copy(ys_ref.at[pl.ds(0, group * slab)], bufs[which].at[0, 0], sem.at[which]).wait()
            return c
        lax.fori_loop(0, ngroups, body, 0)

    @pl.when(s == 0)
    def _():
        def body(g, c):
            gather(dcur_ref, g, 0)
            return c
        lax.fori_loop(0, ngroups, body, 0)

    gt2 = gt2_ref[0]
    gfin = gfin_ref[...]

    def run(slot):
        drain(slot)

        def body(g, c):
            gather(dnext_ref, g, 1 - slot)
            t0 = pl.multiple_of(g * group, group)
            w = w_ref[pl.ds(t0, group), :]
            acc_a = jnp.zeros((group, half), F32)
            acc_b = jnp.zeros((group, half), F32)
            for k in range(TOP_K):
                ya, yb = _unpack_bf16_pair(_load_slabs(bufs[slot].at[k, g], group))
                wk = w[:, k:k + 1]
                acc_a = acc_a + wk * ya
                acc_b = acc_b + wk * yb
            routed = jnp.concatenate([acc_a, acc_b], axis=-1)
            x2 = xm_ref[pl.ds(t0, group), :] + gt2 * routed
            o_ref[pl.ds(t0, group), :] = _rms(x2, gfin)
            return c

        lax.fori_loop(0, ngroups, body, 0)

        @pl.when(s == last)
        def _():
            drain(1 - slot)

    for slot in range(2):
        pl.when(s % 2 == slot)(functools.partial(run, slot))


def _combine(dest, ys, w, xm, gt2, gfin):
    b, l, d = xm.shape
    tl = min(l, COMBINE_ROWS)
    nl = l // tl
    n = b * nl
    slab = d // 2 // LANES
    tile = pltpu.VMEM((TOP_K, tl // SCAN_ROWS, SCAN_ROWS * slab, LANES), U32)
    out = pl.pallas_call(
        functools.partial(_combine_kernel, tl=tl),
        grid=(n,),
        in_specs=[pl.BlockSpec((tl * TOP_K,), lambda s: (s,), memory_space=pltpu.SMEM),
                  pl.BlockSpec((tl * TOP_K,), lambda s: (jnp.minimum(s + 1, n - 1),), memory_space=pltpu.SMEM),
                  pl.BlockSpec(memory_space=pl.ANY),
                  pl.BlockSpec((tl, TOP_K), lambda s: (s, 0)),
                  pl.BlockSpec((tl, d), lambda s: (s, 0)),
                  pl.BlockSpec((1, 1, d), lambda s: (s // nl, 0, 0)),
                  pl.BlockSpec(gfin.shape, lambda s: (0, 0))],
        out_specs=pl.BlockSpec((tl, d), lambda s: (s, 0)),
        out_shape=jax.ShapeDtypeStruct((b * l, d), F32),
        scratch_shapes=[tile, tile, pltpu.SemaphoreType.DMA((2,))],
        compiler_params=_cparams("arbitrary"),
        name=f"combine{l}",
    )(dest, dest, ys, w.reshape(b * l, TOP_K), xm.reshape(b * l, d), gt2, gfin)
    return out.reshape(b, l, d)


def _block_diag(m):
    g, a, b = m.shape
    eye = jnp.eye(g, dtype=m.dtype)
    return jnp.einsum("gab,gh->gahb", m, eye).reshape(g * a, g * b)


def kernel(x_prompt, x_sample, c_prompt, c_sample, cache_k, cache_v, cache_logf, state_ssm_re, state_ssm_im,
           w_ada, b_ada, g_mix, w_in, b_f, lam_re, lam_im, log_dt, ssm_b_re, ssm_b_im, ssm_c_re, ssm_c_im,
           ssm_d, w_glu, b_glu, g_attn_out, g_ssm_out, w_out, g_ffn, w_router, router_bias,
           w_gate, w_up, w_down, ws_gate, ws_up, ws_down, g_final):
    assert w_ada.shape[0] == 1, "single layer"
    bp, lp, d = x_prompt.shape
    bs, ls, _ = x_sample.shape
    past = cache_k.shape[2]

    nb = bp + bs
    nbp = -(-nb // 8) * 8
    c_all = jnp.concatenate([c_prompt, c_sample, jnp.zeros((nbp - nb, d), F32)], axis=0)
    mod = _prep(c_all, w_ada[0], b_ada[0]).reshape(nbp, 6, 1, d)
    mod_p = [mod[:bp, i] for i in range(6)]
    mod_s = [mod[bp:nb, i] for i in range(6)]

    a_re, a_im, bb_re, bb_im = _disc(lam_re[0], lam_im[0], log_dt[0],
                                     jnp.swapaxes(ssm_b_re[0], 1, 2), jnp.swapaxes(ssm_b_im[0], 1, 2))
    a_re = a_re.reshape(STATE_TILES, LANES)
    a_im = a_im.reshape(STATE_TILES, LANES)
    bd_re = _block_diag(bb_re).astype(BF16)
    bd_im = _block_diag(bb_im).astype(BF16)
    cc = jnp.concatenate([_block_diag(jnp.swapaxes(ssm_c_re[0], 1, 2)),
                          -_block_diag(jnp.swapaxes(ssm_c_im[0], 1, 2))], axis=0).astype(BF16)
    ssm_tail = (ssm_d[0].reshape(1, D_SSM), w_glu[0].astype(BF16), b_glu[0].reshape(1, D_SSM),
                g_ssm_out[0].reshape(1, D_SSM))

    ssm_consts = (bd_re, bd_im, cc, a_re, a_im) + ssm_tail

    inw = _inproj_weights(w_in[0], b_f[0])
    g_mix2 = g_mix[0].reshape(1, d)

    wo = w_out[0]
    wo_attn = jnp.pad(wo[:D_ATTN].reshape(N_HEADS, HEAD_DIM, d), ((0, 0), (0, HEAD_PAD - HEAD_DIM), (0, 0)))
    wo_pad = jnp.concatenate([wo_attn.reshape(N_HEADS * HEAD_PAD, d), wo[D_ATTN:]], axis=0).astype(BF16)
    gao_pad = jnp.pad(g_attn_out[0].reshape(N_HEADS, HEAD_DIM),
                      ((0, 0), (0, HEAD_PAD - HEAD_DIM))).reshape(1, N_HEADS * HEAD_PAD)
    wr_t = w_router[0].T
    wr_hi = wr_t.astype(BF16)
    wr_lo = (wr_t - wr_hi.astype(F32)).astype(BF16)
    post_consts = (gao_pad, wo_pad, g_ffn[0].reshape(1, d), wr_hi, wr_lo, router_bias[0].reshape(N_EXPERTS, 1),
                   jnp.concatenate([ws_gate[0], ws_up[0]], axis=1).astype(BF16), ws_down[0].astype(BF16))

    def mixer(x, modv, attn_fn, h0r, h0i):
        sh1, sc1, gt1, sh2, sc2, gt2 = modv
        q, k, v, kf, vf, lf, fb, u = _inproj(x, sh1, sc1, g_mix2, inw)
        attn = attn_fn(q, k, v, kf, vf, lf, fb)
        ssm, hr, hi = _ssm(u, h0r, h0i, ssm_consts)
        return attn, ssm, (kf, vf, lf, hr, hi), (gt1, sh2, sc2, gt2)

    def attn_prompt(q, k, v, kf, vf, lf, fb):
        return _attn(q, k, v, fb[:, :, 0, :N_HEADS].reshape(-1))

    zeros_p = jnp.zeros((bp, STATE_TILES, LANES), F32)
    attn_p, ssm_p, new_p, m_p = mixer(x_prompt, mod_p, attn_prompt, zeros_p, zeros_p)

    lk = -(-(past + ls) // LANES) * LANES
    padk = lk - past - ls

    def attn_sample(q, k, v, kf, vf, lf, fb):
        k_all = jnp.concatenate([cache_k[0].reshape(bs, past, D_ATTN), kf,
                                 jnp.zeros((bs, padk, D_ATTN), F32)], axis=1)
        v_all = jnp.concatenate([cache_v[0].reshape(bs, past, D_ATTN), vf,
                                 jnp.zeros((bs, padk, D_ATTN), F32)], axis=1)
        lf_all = jnp.concatenate([cache_logf[0], lf, jnp.zeros((bs, padk, N_HEADS), F32)], axis=1)
        lf_cols = jnp.pad(lf_all, ((0, 0), (0, 0), (0, LANES - N_HEADS)))
        lf_rows = jnp.swapaxes(lf_all, 1, 2)
        return _attn_dec(q, k_all, v_all, lf_cols, lf_rows, past, ls)

    attn_s, ssm_s, new_s, m_s = mixer(x_sample, mod_s, attn_sample,
                                      state_ssm_re[0].reshape(bs, STATE_TILES, LANES),
                                      state_ssm_im[0].reshape(bs, STATE_TILES, LANES))

    cnt0 = jnp.zeros((N_EXPERTS, 1), F32)
    xm_p, hp_p, idx_p, pos_p, w_p, cnt1 = _post(x_prompt, attn_p, ssm_p, m_p, cnt0, post_consts, True)
    xm_s, hp_s, idx_s, pos_s, w_s, cnt2 = _post(x_sample, attn_s, ssm_s, m_s, cnt1, post_consts, False)

    counts = cnt2[:, 0].astype(I32)
    padded = (counts + EXPERT_ROWS - 1) // EXPERT_ROWS * EXPERT_ROWS
    pend = jnp.cumsum(padded)
    pstart = pend - padded
    tp, ts = bp * lp, bs * ls
    n_blocks = -(-((tp + ts) * TOP_K) // EXPERT_ROWS) + N_EXPERTS
    rows = n_blocks * EXPERT_ROWS
    dest_p = _slots(pstart, idx_p, pos_p)
    dest_s = _slots(pstart, idx_s, pos_s)
    nb_used = (pend[-1] // EXPERT_ROWS).astype(I32).reshape(1)
    blk_row = jnp.minimum(jnp.arange(n_blocks, dtype=I32), nb_used[0] - 1) * EXPERT_ROWS
    blk_e = jnp.minimum(jnp.sum((pend[None, :] <= blk_row[:, None]).astype(I32), axis=1), N_EXPERTS - 1)

    slab = d // 2 // LANES
    dflat_p = jnp.swapaxes(dest_p, 1, 2).reshape(-1)
    dflat_s = jnp.swapaxes(dest_s, 1, 2).reshape(-1)
    n_pairs = (tp + ts) * TOP_K
    order = jnp.argsort(jnp.concatenate([dflat_p, dflat_s])).astype(I32)
    n_tok = ((n_pairs >> 10) + 3) << 10
    tok = jnp.concatenate([order // TOP_K * slab, jnp.zeros((n_tok - n_pairs,), I32)])
    cstart = jnp.cumsum(counts) - counts
    blk_ids = jnp.arange(n_blocks, dtype=I32)
    blk_off = blk_ids * EXPERT_ROWS - pstart[blk_e]
    used = blk_ids < nb_used[0]
    jbase = jnp.where(used, cstart[blk_e] + blk_off, 0).astype(I32)
    real = jnp.clip(counts[blk_e] - blk_off, 0, EXPERT_ROWS)
    ngroups = jnp.where(used, (real + SCAN_ROWS - 1) // SCAN_ROWS, 0).astype(I32)
    hp_all = jnp.concatenate([hp_p.reshape(tp * slab, LANES), hp_s.reshape(ts * slab, LANES)], axis=0)
    ys = _experts(blk_e, nb_used, jbase, ngroups, tok, hp_all, w_gate[0], w_up[0], w_down[0], rows)

    gfin = g_final.reshape(1, d)
    y_p = _combine(dflat_p * slab, ys, w_p, xm_p, m_p[3], gfin)
    y_s = _combine(dflat_s * slab, ys, w_s, xm_s, m_s[3], gfin)

    def pack(new, b, l):
        kf, vf, lf, hr, hi = new
        return (kf.reshape(1, b, l, N_HEADS, HEAD_DIM), vf.reshape(1, b, l, N_HEADS, HEAD_DIM),
                lf.reshape(1, b, l, N_HEADS),
                hr.reshape(1, b, N_SSM_GROUPS, SSM_STATE), hi.reshape(1, b, N_SSM_GROUPS, SSM_STATE))

    return (y_p, y_s) + pack(new_p, bp, lp) + pack(new_s, bs, ls)
```

```python
import functools
import math

import jax
import jax.numpy as jnp
from jax import lax
from jax.experimental import pallas as pl
from jax.experimental.pallas import tpu as pltpu

F32 = jnp.float32
BF16 = jnp.bfloat16
U32 = jnp.uint32
I32 = jnp.int32

N_HEADS = 8
HEAD_DIM = 64
HEAD_PAD = 128
D_ATTN = N_HEADS * HEAD_DIM
SSM_GROUP = 16
N_SSM_GROUPS = 32
SSM_STATE = 64
D_SSM = SSM_GROUP * N_SSM_GROUPS
D_STATE = N_SSM_GROUPS * SSM_STATE
N_EXPERTS = 256
TOP_K = 8
N_EXP_GROUPS = 8
TOPK_GROUPS = 4
ROUTED_SCALE = 2.5
EPS = 1e-6
LANES = 128

SEQ_BLOCK = 512
SSM_CHUNK = 512
EXPERT_ROWS = 1024
ATTN_HEADS_PER_STEP = 2
ATTN_BLOCKS_PER_TRIP = 4
LOG2E = math.log2(math.e)
COMBINE_ROWS = 512
VMEM_LIMIT = 56 * 1024 * 1024

_NT = (((1,), (1,)), ((), ()))


def _cparams(*sem):
    return pltpu.CompilerParams(dimension_semantics=sem, vmem_limit_bytes=VMEM_LIMIT)


def _split3(x):
    hi = x.astype(BF16)
    r = x - hi.astype(F32)
    mid = r.astype(BF16)
    lo = (r - mid.astype(F32)).astype(BF16)
    return hi, mid, lo


def _dot(a, b):
    return jnp.dot(a, b, preferred_element_type=F32)


def _dot_nt(a, b):
    return lax.dot_general(a, b, _NT, preferred_element_type=F32)


def _rms(x, g):
    return x * lax.rsqrt(jnp.mean(x * x, axis=-1, keepdims=True) + EPS) * g


def _sigmoid(x):
    return 1.0 / (1.0 + jnp.exp(-x))


def _pack_bf16_pair(a, b):
    ab = lax.bitcast_convert_type(a.astype(BF16).astype(F32), U32)
    bb = lax.bitcast_convert_type(b.astype(BF16).astype(F32), U32)
    return (ab >> 16) | (bb & jnp.uint32(0xFFFF0000))


def _unpack_bf16_pair(w):
    a = lax.bitcast_convert_type(w << 16, F32)
    b = lax.bitcast_convert_type(w & jnp.uint32(0xFFFF0000), F32)
    return a, b


def _store_slabs(ref, val):
    n, width = val.shape
    tiles = width // LANES
    for c in range(tiles):
        ref[pl.ds(c, n, stride=tiles), :] = val[:, c * LANES:(c + 1) * LANES]


def _load_slabs(ref, n):
    tiles = ref.shape[0] // n
    return jnp.concatenate([ref[pl.ds(c, n, stride=tiles), :] for c in range(tiles)], axis=-1)


def _prep_kernel(c_ref, w_ref, b_ref, o_ref):
    c = c_ref[...]
    a = c * _sigmoid(c)
    a_hi = a.astype(BF16)
    a_lo = (a - a_hi.astype(F32)).astype(BF16)
    w = w_ref[...]
    w_hi = w.astype(BF16)
    w_lo = (w - w_hi.astype(F32)).astype(BF16)
    o_ref[...] = _dot(a_hi, w_hi) + _dot(a_lo, w_hi) + _dot(a_hi, w_lo) + b_ref[...]


def _prep(c_all, w_ada, b_ada):
    n, d = c_all.shape
    nout = w_ada.shape[1]
    tn = 1024
    return pl.pallas_call(
        _prep_kernel,
        grid=(nout // tn,),
        in_specs=[pl.BlockSpec((n, d), lambda j: (0, 0)),
                  pl.BlockSpec((d, tn), lambda j: (0, j)),
                  pl.BlockSpec((1, tn), lambda j: (0, j))],
        out_specs=pl.BlockSpec((n, tn), lambda j: (0, j)),
        out_shape=jax.ShapeDtypeStruct((n, nout), F32),
        compiler_params=_cparams("arbitrary"),
        name="prep",
    )(c_all, w_ada, b_ada.reshape(1, nout))


def _disc_kernel(lr_ref, li_ref, ldt_ref, br_ref, bi_ref, ar_ref, ai_ref, bbr_ref, bbi_ref):
    lr = lr_ref[...]
    li = li_ref[...]
    dt = jnp.exp(ldt_ref[...])
    er = jnp.exp(lr * dt)
    ang = li * dt
    ar = er * jnp.cos(ang)
    ai = er * jnp.sin(ang)
    ar_ref[...] = ar
    ai_ref[...] = ai
    den = lr * lr + li * li
    nr = ((ar - 1.0) * lr + ai * li) / den
    ni = (ai * lr - (ar - 1.0) * li) / den
    nr3 = nr[:, None, :]
    ni3 = ni[:, None, :]
    br = br_ref[...]
    bi = bi_ref[...]
    bbr_ref[...] = nr3 * br - ni3 * bi
    bbi_ref[...] = nr3 * bi + ni3 * br


def _disc(lam_re, lam_im, log_dt, bt_re, bt_im):
    g, p = lam_re.shape
    c = bt_re.shape[1]
    return pl.pallas_call(
        _disc_kernel,
        out_shape=(jax.ShapeDtypeStruct((g, p), F32), jax.ShapeDtypeStruct((g, p), F32),
                   jax.ShapeDtypeStruct((g, c, p), F32), jax.ShapeDtypeStruct((g, c, p), F32)),
        name="disc",
    )(lam_re, lam_im, log_dt.reshape(g, 1), bt_re, bt_im)


SCAN_ROWS = 8
STATE_TILES = D_STATE // LANES


def _inproj_kernel(x_ref, sh_ref, sc_ref, g_ref, wm_ref, wf_ref, bf_ref, selq_ref, selk_ref,
                   cq_ref, ck_ref, cv_ref,
                   q_ref, k_ref, v_ref, kf_ref, vf_ref, lf_ref, fb_ref, u_ref, carry_ref, *, tl):
    @pl.when(pl.program_id(1) == 0)
    def _():
        carry_ref[...] = jnp.zeros_like(carry_ref)

    x = x_ref[0]
    h = _rms(x, g_ref[...]) * (1.0 + sc_ref[0]) + sh_ref[0]
    hb = h.astype(BF16)
    main = _dot(hb, wm_ref[...])
    hw = N_HEADS * HEAD_PAD
    qp = main[:, 0:hw] * (HEAD_DIM ** -0.5 * LOG2E)
    kp = main[:, hw:2 * hw]
    vp = main[:, 2 * hw:3 * hw]
    u_ref[0] = main[:, 3 * hw:3 * hw + D_SSM]
    for hd in range(N_HEADS):
        src = slice(hd * HEAD_PAD, hd * HEAD_PAD + HEAD_DIM)
        dst = slice(hd * HEAD_DIM, (hd + 1) * HEAD_DIM)
        kf_ref[0, :, dst] = kp[:, src]
        vf_ref[0, :, dst] = vp[:, src]

    fl = _dot(hb, wf_ref[...]) + bf_ref[...]
    lf = jnp.minimum(fl, 0.0) - jnp.log1p(jnp.exp(-jnp.abs(fl)))
    lf_ref[0] = lf[:, 0:N_HEADS]

    row = lax.broadcasted_iota(I32, (tl, tl), 0)
    col = lax.broadcasted_iota(I32, (tl, tl), 1)
    tri = jnp.where(row >= col, 1.0, 0.0).astype(BF16)
    hi, mid, lo = _split3(lf)
    frel = _dot(tri, hi) + _dot(tri, mid) + _dot(tri, lo)
    fb_ref[0, 0] = carry_ref[...] * LOG2E
    carry_ref[...] = carry_ref[...] + frel[tl - 1:tl, :]

    fcat = jnp.concatenate(_split3(frel * LOG2E), axis=-1)
    q_aug = (qp + _dot(fcat, selq_ref[...]) + cq_ref[...]).astype(BF16)
    k_aug = (kp + _dot(fcat, selk_ref[...]) + ck_ref[...]).astype(BF16)
    v_aug = (vp + cv_ref[...]).astype(BF16)
    for hd in range(N_HEADS):
        sl = slice(hd * HEAD_PAD, (hd + 1) * HEAD_PAD)
        q_ref[0, hd] = q_aug[:, sl]
        k_ref[0, hd] = k_aug[:, sl]
        v_ref[0, hd] = v_aug[:, sl]


def _inproj(x, sh, sc, g, wts):
    b, l, d = x.shape
    tl = min(l, SEQ_BLOCK)
    nl = l // tl
    wm, wf, bf, selq, selk, cq, ck, cv = wts
    hm = jax.ShapeDtypeStruct((b, N_HEADS, l, HEAD_PAD), BF16)
    hm_spec = pl.BlockSpec((1, N_HEADS, tl, HEAD_PAD), lambda i, j: (i, 0, j, 0))
    tok = lambda w: pl.BlockSpec((1, tl, w), lambda i, j: (i, j, 0))
    full = lambda a: pl.BlockSpec(a.shape, lambda i, j: (0,) * a.ndim)
    row = pl.BlockSpec((1, 1, d), lambda i, j: (i, 0, 0))
    return pl.pallas_call(
        functools.partial(_inproj_kernel, tl=tl),
        grid=(b, nl),
        in_specs=[tok(d), row, row, full(g), full(wm), full(wf), full(bf), full(selq), full(selk),
                  full(cq), full(ck), full(cv)],
        out_specs=[hm_spec, hm_spec, hm_spec, tok(D_ATTN), tok(D_ATTN), tok(N_HEADS),
                   pl.BlockSpec((1, 1, 1, LANES), lambda i, j: (i, j, 0, 0)), tok(D_SSM)],
        out_shape=[hm, hm, hm,
                   jax.ShapeDtypeStruct((b, l, D_ATTN), F32), jax.ShapeDtypeStruct((b, l, D_ATTN), F32),
                   jax.ShapeDtypeStruct((b, l, N_HEADS), F32),
                   jax.ShapeDtypeStruct((b, nl, 1, LANES), F32),
                   jax.ShapeDtypeStruct((b, l, D_SSM), F32)],
        scratch_shapes=[pltpu.VMEM((1, LANES), F32)],
        compiler_params=_cparams("arbitrary", "arbitrary"),
        name=f"inproj{l}",
    )(x, sh, sc, g, wm, wf, bf, selq, selk, cq, ck, cv)


def _inproj_weights(w_in, b_f):
    d = w_in.shape[0]
    wq, wk, wv = (w_in[:, i * D_ATTN:(i + 1) * D_ATTN] for i in range(3))
    wfl = w_in[:, 3 * D_ATTN:3 * D_ATTN + N_HEADS]
    wu = w_in[:, 3 * D_ATTN + N_HEADS:]

    def pad_heads(w):
        w = w.reshape(d, N_HEADS, HEAD_DIM)
        w = jnp.pad(w, ((0, 0), (0, 0), (0, HEAD_PAD - HEAD_DIM)))
        return w.reshape(d, N_HEADS * HEAD_PAD)

    wm = jnp.concatenate([pad_heads(wq), pad_heads(wk), pad_heads(wv), wu], axis=1).astype(BF16)
    wf = jnp.pad(wfl, ((0, 0), (0, LANES - N_HEADS))).astype(BF16)
    bf = jnp.pad(b_f, (0, LANES - N_HEADS)).reshape(1, LANES).astype(F32)

    hw = N_HEADS * HEAD_PAD
    hd = jnp.arange(N_HEADS)
    selq = jnp.zeros((3 * LANES, hw), F32)
    selk = jnp.zeros((3 * LANES, hw), F32)
    cq = jnp.zeros((1, hw), F32)
    ck = jnp.zeros((1, hw), F32)
    cv = jnp.zeros((1, hw), F32)
    for part in range(3):
        selq = selq.at[part * LANES + hd, hd * HEAD_PAD + HEAD_DIM + part].set(1.0)
        selk = selk.at[part * LANES + hd, hd * HEAD_PAD + HEAD_DIM + 3 + part].set(-1.0)
        cq = cq.at[0, hd * HEAD_PAD + HEAD_DIM + 3 + part].set(1.0)
        ck = ck.at[0, hd * HEAD_PAD + HEAD_DIM + part].set(1.0)
    cv = cv.at[0, hd * HEAD_PAD + HEAD_DIM].set(1.0)
    return wm, wf, bf, selq.astype(BF16), selk.astype(BF16), cq, ck, cv


def _attn_kernel(fb_ref, q_ref, k_ref, v_ref, o_ref, *, r, nblk, hp):
    b = pl.program_id(0)
    g = pl.program_id(1)
    i = pl.program_id(2)
    fbase = (b * nblk) * N_HEADS + g * hp

    def scores(j):
        start = pl.multiple_of(j * r, r)
        return tuple(_dot_nt(q_ref[0, hh], k_ref[0, hh, pl.ds(start, r), :]) for hh in range(hp))

    def absorb(j, s_all, state, masked):
        start = pl.multiple_of(j * r, r)
        out = []
        for hh in range(hp):
            m, acc = state[hh]
            s = s_all[hh]
            if masked:
                row = lax.broadcasted_iota(I32, (r, r), 0)
                col = lax.broadcasted_iota(I32, (r, r), 1)
                s = jnp.where(row >= col, s, -jnp.inf)
            dlt = fb_ref[fbase + hh + i * N_HEADS] - fb_ref[fbase + hh + j * N_HEADS]
            mnew = jnp.maximum(m, jnp.max(s, axis=-1, keepdims=True) + dlt)
            p = jnp.exp2(s - (mnew - dlt))
            acc = jnp.exp2(m - mnew) * acc + _dot(p.astype(BF16), v_ref[0, hh, pl.ds(start, r), :])
            out.append((mnew, acc))
        return tuple(out)

    nu = ATTN_BLOCKS_PER_TRIP

    def several(jj, state):
        js = [nu * jj + u for u in range(nu)]
        ss = [scores(j) for j in js]
        for j, s_all in zip(js, ss):
            state = absorb(j, s_all, state, False)
        return state

    def single(j, state):
        return absorb(j, scores(j), state, False)

    state = tuple((jnp.full((r, 1), -jnp.inf, F32), jnp.zeros((r, HEAD_PAD), F32)) for _ in range(hp))
    state = lax.fori_loop(0, i // nu, several, state)
    state = lax.fori_loop(nu * (i // nu), i, single, state)
    state = absorb(i, scores(i), state, True)
    lane = lax.broadcasted_iota(I32, (r, HEAD_PAD), 1)
    for hh in range(hp):
        acc = state[hh][1]
        out = acc / acc[:, HEAD_DIM:HEAD_DIM + 1]
        o_ref[0, hh] = jnp.where(lane < HEAD_DIM, out, 0.0).astype(BF16)


def _attn(q, k, v, fb):
    b, h, l, _ = q.shape
    r = min(l, SEQ_BLOCK)
    nblk = l // r
    hp = ATTN_HEADS_PER_STEP
    return pl.pallas_call(
        functools.partial(_attn_kernel, r=r, nblk=nblk, hp=hp),
        grid=(b, h // hp, nblk),
        in_specs=[pl.BlockSpec(memory_space=pltpu.SMEM),
                  pl.BlockSpec((1, hp, r, HEAD_PAD), lambda bi, gi, i: (bi, gi, i, 0)),
                  pl.BlockSpec((1, hp, l, HEAD_PAD), lambda bi, gi, i: (bi, gi, 0, 0)),
                  pl.BlockSpec((1, hp, l, HEAD_PAD), lambda bi, gi, i: (bi, gi, 0, 0))],
        out_specs=pl.BlockSpec((1, hp, r, HEAD_PAD), lambda bi, gi, i: (bi, gi, i, 0)),
        out_shape=jax.ShapeDtypeStruct((b, h, l, HEAD_PAD), BF16),
        compiler_params=_cparams("arbitrary", "arbitrary", "arbitrary"),
        name="attn",
    )(fb, q, k, v)


def _attn_dec_kernel(q_ref, k_ref, v_ref, lfc_ref, lfr_ref, o_ref, *, lq, past, lk):
    row = lax.broadcasted_iota(I32, (lk, lk), 0)
    col = lax.broadcasted_iota(I32, (lk, lk), 1)
    tri = jnp.where(row >= col, 1.0, 0.0).astype(BF16)
    upper = jnp.where(row <= col, 1.0, 0.0).astype(BF16)
    c_hi, c_mid, c_lo = _split3(lfc_ref[0])
    fcol = _dot(tri, c_hi) + _dot(tri, c_mid) + _dot(tri, c_lo)
    r_hi, r_mid, r_lo = _split3(lfr_ref[0])
    frow = _dot(r_hi, upper) + _dot(r_mid, upper) + _dot(r_lo, upper)
    fq = fcol[past:past + lq, :]
    qpos = past + lax.broadcasted_iota(I32, (lq, lk), 0)
    kpos = lax.broadcasted_iota(I32, (lq, lk), 1)
    ok = kpos <= qpos
    qlane = lax.broadcasted_iota(I32, (lq, HEAD_PAD), 1)
    klane = lax.broadcasted_iota(I32, (lk, HEAD_PAD), 1)
    for hd in range(N_HEADS):
        odd = hd % 2 == 1
        pair = slice((hd // 2) * HEAD_PAD, (hd // 2 + 1) * HEAD_PAD)
        mine = (klane >= HEAD_DIM) if odd else (klane < HEAD_DIM)
        kh = jnp.where(mine, k_ref[0, :, pair], 0.0).astype(BF16)
        vh = jnp.where(mine, v_ref[0, :, pair], 0.0).astype(BF16)
        qh = jnp.where(qlane < HEAD_DIM, q_ref[0, hd].astype(F32), 0.0)
        if odd:
            qh = pltpu.roll(qh, HEAD_DIM, axis=1)
        s = _dot_nt(qh.astype(BF16), kh) + LOG2E * (fq[:, hd:hd + 1] - frow[hd:hd + 1, :])
        s = jnp.where(ok, s, -jnp.inf)
        m = jnp.max(s, axis=-1, keepdims=True)
        pb = jnp.exp2(s - m).astype(BF16)
        den = jnp.sum(pb.astype(F32), axis=-1, keepdims=True)
        out = _dot(pb, vh) / den
        if odd:
            out = pltpu.roll(out, HEAD_DIM, axis=1)
        o_ref[0, hd] = out.astype(BF16)


def _attn_dec(q, k_all, v_all, lf_cols, lf_rows, past, lq):
    b = q.shape[0]
    lk = k_all.shape[1]
    return pl.pallas_call(
        functools.partial(_attn_dec_kernel, lq=lq, past=past, lk=lk),
        grid=(b,),
        in_specs=[pl.BlockSpec((1, N_HEADS, lq, HEAD_PAD), lambda i: (i, 0, 0, 0)),
                  pl.BlockSpec((1, lk, D_ATTN), lambda i: (i, 0, 0)),
                  pl.BlockSpec((1, lk, D_ATTN), lambda i: (i, 0, 0)),
                  pl.BlockSpec((1, lk, LANES), lambda i: (i, 0, 0)),
                  pl.BlockSpec((1, N_HEADS, lk), lambda i: (i, 0, 0))],
        out_specs=pl.BlockSpec((1, N_HEADS, lq, HEAD_PAD), lambda i: (i, 0, 0, 0)),
        out_shape=jax.ShapeDtypeStruct((b, N_HEADS, lq, HEAD_PAD), BF16),
        compiler_params=_cparams("arbitrary"),
        name="attn_dec",
    )(q, k_all, v_all, lf_cols, lf_rows)


def _ssm_kernel(u_ref, h0r_ref, h0i_ref, br_ref, bi_ref, cc_ref, ar_ref, ai_ref, d_ref, wg_ref, bg_ref,
                gso_ref, y_ref, hr_out, hi_out, cr_ref, ci_ref, xr_s, xi_s, or_s, oi_s, *, tc):
    @pl.when(pl.program_id(1) == 0)
    def _():
        cr_ref[...] = h0r_ref[0]
        ci_ref[...] = h0i_ref[0]

    u = u_ref[0]
    ub = u.astype(BF16)
    hc = D_SSM // 2
    hs = D_STATE // 2

    def drive(b_ref):
        return jnp.concatenate([_dot(ub[:, :hc], b_ref[0:hc, 0:hs]), _dot(ub[:, hc:], b_ref[hc:, hs:])], axis=-1)

    bur = drive(br_ref)
    bui = drive(bi_ref)
    nt = tc // SCAN_ROWS
    nc = STATE_TILES

    def tile_rows(tau, c):
        r0 = (tau * nc + c) * SCAN_ROWS
        return slice(r0, r0 + SCAN_ROWS)

    for tau in range(nt):
        rows = slice(tau * SCAN_ROWS, (tau + 1) * SCAN_ROWS)
        for c in range(nc):
            xr_s[tile_rows(tau, c), :] = bur[rows, c * LANES:(c + 1) * LANES]
            xi_s[tile_rows(tau, c), :] = bui[rows, c * LANES:(c + 1) * LANES]

    ar = ar_ref[...]
    ai = ai_ref[...]
    hr = cr_ref[...]
    hi = ci_ref[...]
    for t in range(tc):
        step = pl.ds((t // SCAN_ROWS) * nc * SCAN_ROWS + t % SCAN_ROWS, nc, stride=SCAN_ROWS)
        hr, hi = ar * hr - ai * hi + xr_s[step, :], ar * hi + ai * hr + xi_s[step, :]
        or_s[step, :] = hr
        oi_s[step, :] = hi
    cr_ref[...] = hr
    ci_ref[...] = hi
    hr_out[0] = hr
    hi_out[0] = hi

    def states(o_s, c0):
        return jnp.concatenate(
            [jnp.concatenate([o_s[tile_rows(tau, c), :] for c in range(c0, c0 + nc // 2)], axis=-1)
             for tau in range(nt)], axis=0).astype(BF16)

    y = jnp.concatenate(
        [_dot(states(or_s, 0), cc_ref[0:hs, 0:hc]) + _dot(states(oi_s, 0), cc_ref[D_STATE:D_STATE + hs, 0:hc]),
         _dot(states(or_s, nc // 2), cc_ref[hs:D_STATE, hc:]) + _dot(states(oi_s, nc // 2), cc_ref[D_STATE + hs:, hc:])],
        axis=-1) + d_ref[...] * u
    gl = 0.5 * y * (1.0 + jnp.tanh(math.sqrt(2.0 / math.pi) * (y + 0.044715 * (y * y * y))))
    z = _dot(gl.astype(BF16), wg_ref[...]) + bg_ref[...]
    out = y * _sigmoid(z)
    y_ref[0] = _rms(out, gso_ref[...]).astype(BF16)


def _ssm(u, h0r, h0i, consts):
    b, l, _ = u.shape
    br, bi, cc, ar, ai, dsk, wg, bg, gso = consts
    tc = min(l, SSM_CHUNK)
    full = lambda a: pl.BlockSpec(a.shape, lambda i, j: (0,) * a.ndim)
    st = pl.BlockSpec((1, STATE_TILES, LANES), lambda i, j: (i, 0, 0))
    state = jax.ShapeDtypeStruct((b, STATE_TILES, LANES), F32)
    carry = pltpu.VMEM((STATE_TILES, LANES), F32)
    tiles = pltpu.VMEM((tc * STATE_TILES, LANES), F32)
    return pl.pallas_call(
        functools.partial(_ssm_kernel, tc=tc),
        grid=(b, l // tc),
        in_specs=[pl.BlockSpec((1, tc, D_SSM), lambda i, j: (i, j, 0)), st, st,
                  full(br), full(bi), full(cc), full(ar), full(ai), full(dsk), full(wg), full(bg), full(gso)],
        out_specs=[pl.BlockSpec((1, tc, D_SSM), lambda i, j: (i, j, 0)), st, st],
        out_shape=[jax.ShapeDtypeStruct((b, l, D_SSM), BF16), state, state],
        scratch_shapes=[carry, carry, tiles, tiles, tiles, tiles],
        compiler_params=_cparams("arbitrary", "arbitrary"),
        name=f"ssm{l}",
    )(u, h0r, h0i, br, bi, cc, ar, ai, dsk, wg, bg, gso)


def _post_kernel(x_ref, a_ref, s_ref, gt1_ref, sh2_ref, sc2_ref, gt2_ref, cnt0_ref, gao_ref, wo_ref, gffn_ref,
                 wrh_ref, wrl_ref, rb_ref, wsgu_ref, wsd_ref,
                 xm_ref, hp_ref, idx_ref, pos_ref, w_ref, cnt_ref, carry_ref, *, tl, first):
    step = pl.program_id(0) * pl.num_programs(1) + pl.program_id(1)

    @pl.when(step == 0)
    def _():
        carry_ref[...] = cnt0_ref[...] if not first else jnp.zeros_like(carry_ref)

    x = x_ref[0]
    attn = jnp.concatenate([a_ref[0, hd] for hd in range(N_HEADS)], axis=-1).astype(F32)
    ms = jnp.sum(attn * attn, axis=-1, keepdims=True) * (1.0 / D_ATTN)
    attn_n = (attn * lax.rsqrt(ms + EPS) * gao_ref[...]).astype(BF16)
    merged = jnp.concatenate([attn_n, s_ref[0]], axis=-1)
    x1 = x + gt1_ref[0] * _dot(merged, wo_ref[...])
    h2 = _rms(x1, gffn_ref[...]) * (1.0 + sc2_ref[0]) + sh2_ref[0]
    d = h2.shape[-1]
    _store_slabs(hp_ref.at[0], _pack_bf16_pair(h2[:, :d // 2], h2[:, d // 2:]))
    h_hi = h2.astype(BF16)
    h_lo = (h2 - h_hi.astype(F32)).astype(BF16)

    gu = _dot(h_hi, wsgu_ref[...])
    ds = gu.shape[-1] // 2
    g = gu[:, :ds]
    act = (g * _sigmoid(g) * gu[:, ds:]).astype(BF16)
    xm_ref[0] = x1 + gt2_ref[0] * _dot(act, wsd_ref[...])

    wrh = wrh_ref[...]
    logits = _dot_nt(wrh, h_hi) + _dot_nt(wrh, h_lo) + _dot_nt(wrl_ref[...], h_hi)
    s = _sigmoid(logits)
    sc = s + rb_ref[...]
    ge = N_EXPERTS // N_EXP_GROUPS
    neg = -jnp.inf
    gi = lax.broadcasted_iota(I32, (ge, tl), 0)
    gsc = []
    for gidx in range(N_EXP_GROUPS):
        blk = sc[gidx * ge:(gidx + 1) * ge, :]
        m1 = jnp.max(blk, axis=0, keepdims=True)
        f1 = jnp.min(jnp.where(blk == m1, gi, ge), axis=0, keepdims=True)
        m2 = jnp.max(jnp.where(gi == f1, neg, blk), axis=0, keepdims=True)
        gsc.append(m1 + m2)
    gwork = jnp.concatenate(gsc, axis=0)
    ni = lax.broadcasted_iota(I32, (N_EXP_GROUPS, tl), 0)
    gsel = jnp.zeros((N_EXP_GROUPS, tl), F32)
    for _ in range(TOPK_GROUPS):
        mx = jnp.max(gwork, axis=0, keepdims=True)
        fi = jnp.min(jnp.where(gwork == mx, ni, N_EXP_GROUPS), axis=0, keepdims=True)
        hit = ni == fi
        gsel = jnp.where(hit, 1.0, gsel)
        gwork = jnp.where(hit, neg, gwork)
    work = jnp.concatenate(
        [jnp.where(gsel[gidx:gidx + 1, :] > 0.0, sc[gidx * ge:(gidx + 1) * ge, :], neg)
         for gidx in range(N_EXP_GROUPS)], axis=0)
    ei = lax.broadcasted_iota(I32, (N_EXPERTS, tl), 0)
    chosen = jnp.zeros((N_EXPERTS, tl), F32)
    idx_rows, w_rows = [], []
    for _ in range(TOP_K):
        mx = jnp.max(work, axis=0, keepdims=True)
        fi = jnp.min(jnp.where(work == mx, ei, N_EXPERTS), axis=0, keepdims=True)
        hit = ei == fi
        w_rows.append(jnp.sum(jnp.where(hit, s, 0.0), axis=0, keepdims=True))
        idx_rows.append(fi)
        chosen = jnp.where(hit, 1.0, chosen)
        work = jnp.where(hit, neg, work)
    wt = jnp.concatenate(w_rows, axis=0)
    wt = wt / jnp.sum(wt, axis=0, keepdims=True) * ROUTED_SCALE
    idx_ref[0] = jnp.concatenate(idx_rows, axis=0)

    trow = lax.broadcasted_iota(I32, (tl, tl), 0)
    tcol = lax.broadcasted_iota(I32, (tl, tl), 1)
    before = jnp.where(trow < tcol, 1.0, 0.0).astype(BF16)
    rank = _dot(chosen.astype(BF16), before) + carry_ref[...]
    pos_rows = [jnp.sum(jnp.where(ei == idx_rows[k], rank, 0.0), axis=0, keepdims=True) for k in range(TOP_K)]
    pos_ref[0] = jnp.concatenate(pos_rows, axis=0).astype(I32)
    carry_ref[...] = carry_ref[...] + jnp.sum(chosen, axis=1, keepdims=True)
    cnt_ref[...] = carry_ref[...]

    eye = jnp.where(trow == tcol, 1.0, 0.0).astype(BF16)
    t_hi, t_mid, t_lo = _split3(wt)
    w_ref[0] = _dot_nt(eye, t_hi) + _dot_nt(eye, t_mid) + _dot_nt(eye, t_lo)


def _post(x, attn, ssm, mods, cnt0, consts, first):
    b, l, d = x.shape
    tl = min(l, SEQ_BLOCK)
    gt1, sh2, sc2, gt2 = mods
    gao, wo, gffn, wrh, wrl, rb, wsgu, wsd = consts
    slab = d // 2 // LANES
    tok = lambda w: pl.BlockSpec((1, tl, w), lambda i, j: (i, j, 0))
    full = lambda a: pl.BlockSpec(a.shape, lambda i, j: (0,) * a.ndim)
    row = pl.BlockSpec((1, 1, d), lambda i, j: (i, 0, 0))
    tk = pl.BlockSpec((1, TOP_K, tl), lambda i, j: (i, 0, j))
    return pl.pallas_call(
        functools.partial(_post_kernel, tl=tl, first=first),
        grid=(b, l // tl),
        in_specs=[tok(d), pl.BlockSpec((1, N_HEADS, tl, HEAD_PAD), lambda i, j: (i, 0, j, 0)), tok(D_SSM),
                  row, row, row, row, full(cnt0), full(gao), full(wo), full(gffn), full(wrh), full(wrl),
                  full(rb), full(wsgu), full(wsd)],
        out_specs=[tok(d), pl.BlockSpec((1, tl * slab, LANES), lambda i, j: (i, j, 0)), tk, tk, tok(TOP_K),
                   full(cnt0)],
        out_shape=[jax.ShapeDtypeStruct((b, l, d), F32), jax.ShapeDtypeStruct((b, l * slab, LANES), U32),
                   jax.ShapeDtypeStruct((b, TOP_K, l), I32), jax.ShapeDtypeStruct((b, TOP_K, l), I32),
                   jax.ShapeDtypeStruct((b, l, TOP_K), F32), jax.ShapeDtypeStruct(cnt0.shape, F32)],
        scratch_shapes=[pltpu.VMEM(cnt0.shape, F32)],
        compiler_params=_cparams("arbitrary", "arbitrary"),
        name=f"post{l}",
    )(x, attn, ssm, gt1, sh2, sc2, gt2, cnt0, gao, wo, gffn, wrh, wrl, rb, wsgu, wsd)


def _slots_kernel(pstart_ref, idx_ref, pos_ref, dest_ref):
    idx = idx_ref[0]

    def body(e, base):
        return jnp.where(idx == e, pstart_ref[e], base)

    dest_ref[0] = lax.fori_loop(0, N_EXPERTS, body, jnp.zeros_like(idx)) + pos_ref[0]


def _slots(pstart, idx, pos):
    b, k, l = idx.shape
    tl = min(l, 2048)
    spec = pl.BlockSpec((1, k, tl), lambda i, j: (i, 0, j))
    return pl.pallas_call(
        _slots_kernel,
        grid=(b, l // tl),
        in_specs=[pl.BlockSpec(memory_space=pltpu.SMEM), spec, spec],
        out_specs=spec,
        out_shape=jax.ShapeDtypeStruct(idx.shape, I32),
        compiler_params=_cparams("arbitrary", "arbitrary"),
        name=f"slots{l}",
    )(pstart, idx, pos)


INDEX_WINDOW = 2048
INDEX_ALIGN = 1024


def _experts_kernel(be_ref, nb_ref, jb_ref, ng_ref, tok_ref, h_ref, wg_ref, wu_ref, wd_ref, y_ref,
                    wgu_s, wd_s, x0, x1, idx0, idx1, xsem, isem):
    i = pl.program_id(0)
    nbu = nb_ref[0]
    e = be_ref[i]
    prev = be_ref[jnp.maximum(i - 1, 0)]
    dm = wg_ref.shape[1]
    de = wg_ref.shape[2]
    xbufs = (x0, x1)
    ibufs = (idx0, idx1)

    def window(blk, slot):
        start = pl.multiple_of((jb_ref[blk] >> 10) << 10, INDEX_ALIGN)
        return pltpu.make_async_copy(tok_ref.at[pl.ds(start, INDEX_WINDOW)], ibufs[slot], isem.at[slot])

    slab = x0.shape[0] // EXPERT_ROWS

    def gather_rows(blk, slot):
        base = jb_ref[blk] & (INDEX_ALIGN - 1)

        def body(g, c):
            for tt in range(SCAN_ROWS):
                r = g * SCAN_ROWS + tt
                src = pl.multiple_of(ibufs[slot][base + r], slab)
                dst = pl.multiple_of(r * slab, slab)
                pltpu.make_async_copy(h_ref.at[pl.ds(src, slab)], xbufs[slot].at[pl.ds(dst, slab)],
                                      xsem.at[slot]).start()
            return c
        lax.fori_loop(0, ng_ref[blk], body, 0)

    def drain_rows(blk, slot):
        def body(g, c):
            for _ in range(SCAN_ROWS):
                pltpu.make_async_copy(h_ref.at[pl.ds(0, slab)], xbufs[slot].at[pl.ds(0, slab)],
                                      xsem.at[slot]).wait()
            return c
        lax.fori_loop(0, ng_ref[blk], body, 0)

    @pl.when(i >= nbu)
    def _():
        y_ref[...] = jnp.zeros_like(y_ref)

    @pl.when(i < nbu)
    def _():
        last = nbu - 1

        @pl.when(i == 0)
        def _():
            x0[...] = jnp.zeros_like(x0)
            x1[...] = jnp.zeros_like(x1)
            first = window(0, 0)
            first.start()
            first.wait()
            gather_rows(0, 0)
            window(jnp.minimum(1, last), 1).start()

        @pl.when((i == 0) | (e != prev))
        def _():
            wgu_s[:, 0:de] = wg_ref[0].astype(BF16)
            wgu_s[:, de:2 * de] = wu_ref[0].astype(BF16)
            wd_s[...] = wd_ref[0].astype(BF16)

        def run(slot):
            nxt = jnp.minimum(i + 1, last)
            window(nxt, 1 - slot).wait()
            drain_rows(i, slot)
            gather_rows(nxt, 1 - slot)
            xa, xb = _unpack_bf16_pair(_load_slabs(xbufs[slot], EXPERT_ROWS))
            gu = _dot(xa.astype(BF16), wgu_s[0:dm // 2, :]) + _dot(xb.astype(BF16), wgu_s[dm // 2:dm, :])
            g = gu[:, :de]
            act = (g * _sigmoid(g) * gu[:, de:]).astype(BF16)
            y = _dot(act, wd_s[...])
            _store_slabs(y_ref, _pack_bf16_pair(y[:, :dm // 2], y[:, dm // 2:]))

            @pl.when(i < last)
            def _():
                window(jnp.minimum(i + 2, last), slot).start()

            @pl.when(i == last)
            def _():
                drain_rows(last, 1 - slot)

        for slot in range(2):
            pl.when(i % 2 == slot)(functools.partial(run, slot))


def _experts(blk_e, nb_used, jbase, ngroups, tok, hp, w_gate, w_up, w_down, rows):
    ne, dm, de = w_gate.shape
    slab = dm // 2 // LANES
    xrows = EXPERT_ROWS * slab
    nb = rows // EXPERT_ROWS
    grid_spec = pltpu.PrefetchScalarGridSpec(
        num_scalar_prefetch=4,
        grid=(nb,),
        in_specs=[pl.BlockSpec(memory_space=pl.ANY), pl.BlockSpec(memory_space=pl.ANY),
                  pl.BlockSpec((1, dm, de), lambda i, be, n, jb, ng: (be[i], 0, 0)),
                  pl.BlockSpec((1, dm, de), lambda i, be, n, jb, ng: (be[i], 0, 0)),
                  pl.BlockSpec((1, de, dm), lambda i, be, n, jb, ng: (be[i], 0, 0))],
        out_specs=pl.BlockSpec((xrows, LANES), lambda i, be, n, jb, ng: (i, 0)),
        scratch_shapes=[pltpu.VMEM((dm, 2 * de), BF16), pltpu.VMEM((de, dm), BF16),
                        pltpu.VMEM((xrows, LANES), U32), pltpu.VMEM((xrows, LANES), U32),
                        pltpu.SMEM((INDEX_WINDOW,), I32), pltpu.SMEM((INDEX_WINDOW,), I32),
                        pltpu.SemaphoreType.DMA((2,)), pltpu.SemaphoreType.DMA((2,))],
    )
    return pl.pallas_call(
        _experts_kernel,
        grid_spec=grid_spec,
        out_shape=jax.ShapeDtypeStruct((rows * slab, LANES), U32),
        compiler_params=_cparams("arbitrary"),
        name="experts",
    )(blk_e, nb_used, jbase, ngroups, tok, hp, w_gate, w_up, w_down)


def _combine_kernel(dcur_ref, dnext_ref, ys_ref, w_ref, xm_ref, gt2_ref, gfin_ref, o_ref, buf0, buf1, sem, *, tl):
    bufs = (buf0, buf1)
    s = pl.program_id(0)
    last = pl.num_programs(0) - 1
    group = SCAN_ROWS
    ngroups = tl // group
    slab = buf0.shape[2] // group
    half = slab * LANES

    def gather(dref, g, to):
        for tt in range(group):
            for k in range(TOP_K):
                src = pl.multiple_of(dref[(g * group + tt) * TOP_K + k], slab)
                pltpu.make_async_copy(ys_ref.at[pl.ds(src, slab)], bufs[to].at[k, g, pl.ds(tt * slab, slab)],
                                      sem.at[to]).start()

    def drain(which):
        def body(t, c):
            for k in range(TOP_K):
                pltpu.make_async_copy(ys_ref.at[pl.ds(0, slab)], bufs[which].at[0, 0, pl.ds(0, slab)],
                                      sem.at[which]).wait()
            return c
        lax.fori_loop(0, tl, body, 0)

    @pl.when(s == 0)
    def _():
        def body(g, c):
            gather(dcur_ref, g, 0)
            return c
        lax.fori_loop(0, ngroups, body, 0)

    gt2 = gt2_ref[0]
    gfin = gfin_ref[...]

    def run(slot):
        drain(slot)

        def body(g, c):
            gather(dnext_ref, g, 1 - slot)
            t0 = pl.multiple_of(g * group, group)
            w = w_ref[pl.ds(t0, group), :]
            acc_a = jnp.zeros((group, half), F32)
            acc_b = jnp.zeros((group, half), F32)
            for k in range(TOP_K):
                ya, yb = _unpack_bf16_pair(_load_slabs(bufs[slot].at[k, g], group))
                wk = w[:, k:k + 1]
                acc_a = acc_a + wk * ya
                acc_b = acc_b + wk * yb
            routed = jnp.concatenate([acc_a, acc_b], axis=-1)
            x2 = xm_ref[pl.ds(t0, group), :] + gt2 * routed
            o_ref[pl.ds(t0, group), :] = _rms(x2, gfin)
            return c

        lax.fori_loop(0, ngroups, body, 0)

        @pl.when(s == last)
        def _():
            drain(1 - slot)

    for slot in range(2):
        pl.when(s % 2 == slot)(functools.partial(run, slot))


def _combine(dest, ys, w, xm, gt2, gfin):
    b, l, d = xm.shape
    tl = min(l, COMBINE_ROWS)
    nl = l // tl
    n = b * nl
    slab = d // 2 // LANES
    tile = pltpu.VMEM((TOP_K, tl // SCAN_ROWS, SCAN_ROWS * slab, LANES), U32)
    out = pl.pallas_call(
        functools.partial(_combine_kernel, tl=tl),
        grid=(n,),
        in_specs=[pl.BlockSpec((tl * TOP_K,), lambda s: (s,), memory_space=pltpu.SMEM),
                  pl.BlockSpec((tl * TOP_K,), lambda s: (jnp.minimum(s + 1, n - 1),), memory_space=pltpu.SMEM),
                  pl.BlockSpec(memory_space=pl.ANY),
                  pl.BlockSpec((tl, TOP_K), lambda s: (s, 0)),
                  pl.BlockSpec((tl, d), lambda s: (s, 0)),
                  pl.BlockSpec((1, 1, d), lambda s: (s // nl, 0, 0)),
                  pl.BlockSpec(gfin.shape, lambda s: (0, 0))],
        out_specs=pl.BlockSpec((tl, d), lambda s: (s, 0)),
        out_shape=jax.ShapeDtypeStruct((b * l, d), F32),
        scratch_shapes=[tile, tile, pltpu.SemaphoreType.DMA((2,))],
        compiler_params=_cparams("arbitrary"),
        name=f"combine{l}",
    )(dest, dest, ys, w.reshape(b * l, TOP_K), xm.reshape(b * l, d), gt2, gfin)
    return out.reshape(b, l, d)


def _block_diag(m):
    g, a, b = m.shape
    eye = jnp.eye(g, dtype=m.dtype)
    return jnp.einsum("gab,gh->gahb", m, eye).reshape(g * a, g * b)


def kernel(x_prompt, x_sample, c_prompt, c_sample, cache_k, cache_v, cache_logf, state_ssm_re, state_ssm_im,
           w_ada, b_ada, g_mix, w_in, b_f, lam_re, lam_im, log_dt, ssm_b_re, ssm_b_im, ssm_c_re, ssm_c_im,
           ssm_d, w_glu, b_glu, g_attn_out, g_ssm_out, w_out, g_ffn, w_router, router_bias,
           w_gate, w_up, w_down, ws_gate, ws_up, ws_down, g_final):
    assert w_ada.shape[0] == 1, "single layer"
    bp, lp, d = x_prompt.shape
    bs, ls, _ = x_sample.shape
    past = cache_k.shape[2]

    nb = bp + bs
    nbp = -(-nb // 8) * 8
    c_all = jnp.concatenate([c_prompt, c_sample, jnp.zeros((nbp - nb, d), F32)], axis=0)
    mod = _prep(c_all, w_ada[0], b_ada[0]).reshape(nbp, 6, 1, d)
    mod_p = [mod[:bp, i] for i in range(6)]
    mod_s = [mod[bp:nb, i] for i in range(6)]

    a_re, a_im, bb_re, bb_im = _disc(lam_re[0], lam_im[0], log_dt[0],
                                     jnp.swapaxes(ssm_b_re[0], 1, 2), jnp.swapaxes(ssm_b_im[0], 1, 2))
    a_re = a_re.reshape(STATE_TILES, LANES)
    a_im = a_im.reshape(STATE_TILES, LANES)
    bd_re = _block_diag(bb_re).astype(BF16)
    bd_im = _block_diag(bb_im).astype(BF16)
    cc = jnp.concatenate([_block_diag(jnp.swapaxes(ssm_c_re[0], 1, 2)),
                          -_block_diag(jnp.swapaxes(ssm_c_im[0], 1, 2))], axis=0).astype(BF16)
    ssm_tail = (ssm_d[0].reshape(1, D_SSM), w_glu[0].astype(BF16), b_glu[0].reshape(1, D_SSM),
                g_ssm_out[0].reshape(1, D_SSM))

    ssm_consts = (bd_re, bd_im, cc, a_re, a_im) + ssm_tail

    inw = _inproj_weights(w_in[0], b_f[0])
    g_mix2 = g_mix[0].reshape(1, d)

    wo = w_out[0]
    wo_attn = jnp.pad(wo[:D_ATTN].reshape(N_HEADS, HEAD_DIM, d), ((0, 0), (0, HEAD_PAD - HEAD_DIM), (0, 0)))
    wo_pad = jnp.concatenate([wo_attn.reshape(N_HEADS * HEAD_PAD, d), wo[D_ATTN:]], axis=0).astype(BF16)
    gao_pad = jnp.pad(g_attn_out[0].reshape(N_HEADS, HEAD_DIM),
                      ((0, 0), (0, HEAD_PAD - HEAD_DIM))).reshape(1, N_HEADS * HEAD_PAD)
    wr_t = w_router[0].T
    wr_hi = wr_t.astype(BF16)
    wr_lo = (wr_t - wr_hi.astype(F32)).astype(BF16)
    post_consts = (gao_pad, wo_pad, g_ffn[0].reshape(1, d), wr_hi, wr_lo, router_bias[0].reshape(N_EXPERTS, 1),
                   jnp.concatenate([ws_gate[0], ws_up[0]], axis=1).astype(BF16), ws_down[0].astype(BF16))

    def mixer(x, modv, attn_fn, h0r, h0i):
        sh1, sc1, gt1, sh2, sc2, gt2 = modv
        q, k, v, kf, vf, lf, fb, u = _inproj(x, sh1, sc1, g_mix2, inw)
        attn = attn_fn(q, k, v, kf, vf, lf, fb)
        ssm, hr, hi = _ssm(u, h0r, h0i, ssm_consts)
        return attn, ssm, (kf, vf, lf, hr, hi), (gt1, sh2, sc2, gt2)

    def attn_prompt(q, k, v, kf, vf, lf, fb):
        return _attn(q, k, v, fb[:, :, 0, :N_HEADS].reshape(-1))

    zeros_p = jnp.zeros((bp, STATE_TILES, LANES), F32)
    attn_p, ssm_p, new_p, m_p = mixer(x_prompt, mod_p, attn_prompt, zeros_p, zeros_p)

    lk = -(-(past + ls) // LANES) * LANES
    padk = lk - past - ls

    def attn_sample(q, k, v, kf, vf, lf, fb):
        k_all = jnp.concatenate([cache_k[0].reshape(bs, past, D_ATTN), kf,
                                 jnp.zeros((bs, padk, D_ATTN), F32)], axis=1)
        v_all = jnp.concatenate([cache_v[0].reshape(bs, past, D_ATTN), vf,
                                 jnp.zeros((bs, padk, D_ATTN), F32)], axis=1)
        lf_all = jnp.concatenate([cache_logf[0], lf, jnp.zeros((bs, padk, N_HEADS), F32)], axis=1)
        lf_cols = jnp.pad(lf_all, ((0, 0), (0, 0), (0, LANES - N_HEADS)))
        lf_rows = jnp.swapaxes(lf_all, 1, 2)
        return _attn_dec(q, k_all, v_all, lf_cols, lf_rows, past, ls)

    attn_s, ssm_s, new_s, m_s = mixer(x_sample, mod_s, attn_sample,
                                      state_ssm_re[0].reshape(bs, STATE_TILES, LANES),
                                      state_ssm_im[0].reshape(bs, STATE_TILES, LANES))

    cnt0 = jnp.zeros((N_EXPERTS, 1), F32)
    xm_p, hp_p, idx_p, pos_p, w_p, cnt1 = _post(x_prompt, attn_p, ssm_p, m_p, cnt0, post_consts, True)
    xm_s, hp_s, idx_s, pos_s, w_s, cnt2 = _post(x_sample, attn_s, ssm_s, m_s, cnt1, post_consts, False)

    counts = cnt2[:, 0].astype(I32)
    padded = (counts + EXPERT_ROWS - 1) // EXPERT_ROWS * EXPERT_ROWS
    pend = jnp.cumsum(padded)
    pstart = pend - padded
    tp, ts = bp * lp, bs * ls
    n_blocks = -(-((tp + ts) * TOP_K) // EXPERT_ROWS) + N_EXPERTS
    rows = n_blocks * EXPERT_ROWS
    dest_p = _slots(pstart, idx_p, pos_p)
    dest_s = _slots(pstart, idx_s, pos_s)
    nb_used = (pend[-1] // EXPERT_ROWS).astype(I32).reshape(1)
    blk_row = jnp.minimum(jnp.arange(n_blocks, dtype=I32), nb_used[0] - 1) * EXPERT_ROWS
    blk_e = jnp.minimum(jnp.sum((pend[None, :] <= blk_row[:, None]).astype(I32), axis=1), N_EXPERTS - 1)

    slab = d // 2 // LANES
    dflat_p = jnp.swapaxes(dest_p, 1, 2).reshape(-1)
    dflat_s = jnp.swapaxes(dest_s, 1, 2).reshape(-1)
    n_pairs = (tp + ts) * TOP_K
    order = jnp.argsort(jnp.concatenate([dflat_p, dflat_s])).astype(I32)
    n_tok = ((n_pairs >> 10) + 3) << 10
    tok = jnp.concatenate([order // TOP_K * slab, jnp.zeros((n_tok - n_pairs,), I32)])
    cstart = jnp.cumsum(counts) - counts
    blk_ids = jnp.arange(n_blocks, dtype=I32)
    blk_off = blk_ids * EXPERT_ROWS - pstart[blk_e]
    used = blk_ids < nb_used[0]
    jbase = jnp.where(used, cstart[blk_e] + blk_off, 0).astype(I32)
    real = jnp.clip(counts[blk_e] - blk_off, 0, EXPERT_ROWS)
    ngroups = jnp.where(used, (real + SCAN_ROWS - 1) // SCAN_ROWS, 0).astype(I32)
    hp_all = jnp.concatenate([hp_p.reshape(tp * slab, LANES), hp_s.reshape(ts * slab, LANES)], axis=0)
    ys = _experts(blk_e, nb_used, jbase, ngroups, tok, hp_all, w_gate[0], w_up[0], w_down[0], rows)

    gfin = g_final.reshape(1, d)
    y_p = _combine(dflat_p * slab, ys, w_p, xm_p, m_p[3], gfin)
    y_s = _combine(dflat_s * slab, ys, w_s, xm_s, m_s[3], gfin)

    def pack(new, b, l):
        kf, vf, lf, hr, hi = new
        return (kf.reshape(1, b, l, N_HEADS, HEAD_DIM), vf.reshape(1, b, l, N_HEADS, HEAD_DIM),
                lf.reshape(1, b, l, N_HEADS),
                hr.reshape(1, b, N_SSM_GROUPS, SSM_STATE), hi.reshape(1, b, N_SSM_GROUPS, SSM_STATE))

    return (y_p, y_s) + pack(new_p, bp, lp) + pack(new_s, bs, ls)
```
